```python
import math
import jax, jax.numpy as jnp
from jax import lax
import numpy as np

D_MODEL = 2048
BATCH = 4
SEQ = 2048
DEPTH = 1
DEC_BATCH = 128
DEC_SEQ = 1
PAST_LEN = 16384
PAGE_SIZE = 128

GLA_HEADS = 4
GLA_DK = D_MODEL // 2 // GLA_HEADS
GLA_DV = D_MODEL // GLA_HEADS
GLA_KW = GLA_HEADS * GLA_DK
GLA_VW = GLA_HEADS * GLA_DV
GATE_RANK = 16
GATE_TEMP = 16.0
GLA_CHUNK = 64
POOL_WINDOWS = (2, 4, 8, 16)
POOL_GROUPS = 4
POOL_GW = D_MODEL // 8
POOL_W = POOL_GROUPS * POOL_GW
POOL_BUF = 15
D_FF = 128 * ((8 * D_MODEL // 3 + 127) // 128)
CONV_W = 3
PLE_DIM = 256
EPS = 1e-6
IN_SIZES = (GLA_KW, GLA_KW, GLA_VW, GLA_VW, GATE_RANK, POOL_W, D_MODEL, D_MODEL)
IN_WIDTH = GLA_KW * 2 + GLA_VW * 2 + GATE_RANK + POOL_W + D_MODEL * 2

kernel_name = 'gla_pool_convffn_hybrid_step'


def rmsnorm(x, g):
    xf = x.astype(jnp.float32)
    y = xf * lax.rsqrt(jnp.mean(xf * xf, axis=-1, keepdims=True) + EPS)
    return (y * g.astype(jnp.float32)).astype(x.dtype)


def split_cols(z):
    idx = [int(i) for i in np.cumsum(IN_SIZES)[:-1]]
    return jnp.split(z, idx, axis=-1)


def gla_scan(q, k, v, glog, s0):
    B, T, H, DK = q.shape
    DV = v.shape[-1]
    C = math.gcd(T, GLA_CHUNK)
    NC = T // C

    def to_chunks(a):
        return a.reshape(B, NC, C, H, a.shape[-1]).transpose(1, 0, 3, 2, 4)

    qc, kc, vc, gc = to_chunks(q), to_chunks(k), to_chunks(v), to_chunks(glog)
    causal = jnp.tril(jnp.ones((C, C), dtype=bool))

    def step(S, inp):
        qi, ki, vi, gi = inp
        b = jnp.cumsum(gi, axis=2)
        b_last = b[:, :, -1:, :]
        qb = qi * jnp.exp(b)
        kb = ki * jnp.exp(-b)
        att = jnp.where(causal, jnp.einsum('bhtd,bhsd->bhts', qb, kb), 0.0)
        o = jnp.einsum('bhtd,bhdv->bhtv', qb, S) + jnp.einsum('bhts,bhsv->bhtv', att, vi)
        kd = ki * jnp.exp(b_last - b)
        S = jnp.exp(b_last[:, :, 0, :])[..., None] * S + jnp.einsum('bhsd,bhsv->bhdv', kd, vi)
        return S, o

    S, oc = lax.scan(step, s0, (qc, kc, vc, gc))
    o = oc.transpose(1, 0, 3, 2, 4).reshape(B, T, H, DV)
    return o, S


def pool_mix(u, buf, start_pos):
    B, T, _ = u.shape
    uf = u.astype(jnp.float32)
    ext = jnp.concatenate([buf.astype(jnp.float32), uf], axis=1)
    cs = jnp.concatenate([jnp.zeros((B, 1, POOL_W), jnp.float32), jnp.cumsum(ext, axis=1)], axis=1)
    pos = start_pos + jnp.arange(T)
    means = []
    for g, w in enumerate(POOL_WINDOWS):
        lo, hi = g * POOL_GW, (g + 1) * POOL_GW
        s = cs[:, POOL_BUF + 1:POOL_BUF + 1 + T, lo:hi] - cs[:, POOL_BUF + 1 - w:POOL_BUF + 1 - w + T, lo:hi]
        cnt = jnp.minimum(pos + 1, w).astype(jnp.float32)[None, :, None]
        means.append(s / cnt)
    mix = jnp.concatenate(means, axis=-1) - uf
    return mix.astype(u.dtype), ext[:, -POOL_BUF:].astype(buf.dtype)


def causal_dwconv(u, buf, w, b):
    T = u.shape[1]
    ext = jnp.concatenate([buf.astype(u.dtype), u], axis=1)
    y = b
    for j in range(CONV_W):
        y = y + w[j] * ext[:, j:j + T]
    return y, ext[:, -(CONV_W - 1):].astype(buf.dtype)


def trunk_layer(h, pl, s_gla, s_pool, s_conv, start_pos, lw):
    B, T, _ = h.shape
    dt = h.dtype
    f32 = jnp.float32
    a = rmsnorm(h, lw['norm_mix'])
    z = a @ lw['w_in']
    q, k, v, r, glr, u, ga, gb = split_cols(z)
    zg = glr.astype(f32) @ lw['w_gate_up'].astype(f32) + lw['b_gate'].astype(f32)
    glog = (jax.nn.log_sigmoid(zg) / GATE_TEMP).reshape(B, T, GLA_HEADS, GLA_DK)
    qh = q.astype(f32).reshape(B, T, GLA_HEADS, GLA_DK) * (GLA_DK ** -0.5)
    kh = k.astype(f32).reshape(B, T, GLA_HEADS, GLA_DK)
    vh = v.astype(f32).reshape(B, T, GLA_HEADS, GLA_DV)
    o, s_gla_new = gla_scan(qh, kh, vh, glog, s_gla.astype(f32))
    o = o * lax.rsqrt(jnp.mean(o * o, axis=-1, keepdims=True) + EPS)
    o = o * lw['gla_norm'].astype(f32).reshape(GLA_HEADS, GLA_DV)
    o = o.reshape(B, T, GLA_VW).astype(dt) * jax.nn.silu(r)
    y_a = o @ lw['w_branch_a']
    pm, s_pool_new = pool_mix(u, s_pool, start_pos)
    pm = jnp.einsum('btgc,gcd->btgd', pm.reshape(B, T, POOL_GROUPS, POOL_GW), lw['w_pool'])
    pm = pm.reshape(B, T, POOL_W) * lw['pool_scale']
    y_b = pm @ lw['w_branch_b']
    m = jax.nn.sigmoid(ga) * y_a + jax.nn.sigmoid(gb) * y_b
    h = h + m @ lw['w_out']
    c = rmsnorm(h, lw['norm_ffn'])
    gate, up = jnp.split(c @ lw['w_up'], 2, axis=-1)
    gconv, s_conv_new = causal_dwconv(gate, s_conv, lw['conv_w'], lw['conv_b'])
    h = h + (jax.nn.silu(gconv) * up) @ lw['w_down']
    pg = jax.nn.sigmoid(rmsnorm(h, lw['norm_ple']) @ lw['w_ple_gate'])
    h = h + pg * (pl @ lw['w_ple'])
    return h, s_gla_new.astype(s_gla.dtype), s_pool_new, s_conv_new


def setup_inputs(seed: int = 0) -> dict:
    key = jax.random.key(seed)
    ks = jax.random.split(key, 32)

    def nrm(k, shape, scale):
        return jax.random.normal(k, shape, jnp.float32) * scale

    L = DEPTH
    return {
        'x_prompt': nrm(ks[0], (BATCH, SEQ, D_MODEL), 1.0),
        'x_sample': nrm(ks[1], (DEC_BATCH, DEC_SEQ, D_MODEL), 1.0),
        'state_gla': nrm(ks[2], (L, DEC_BATCH, GLA_HEADS, GLA_DK, GLA_DV), 1.0),
        'state_pool': nrm(ks[3], (L, DEC_BATCH, POOL_BUF, POOL_W), 1.0),
        'state_conv': nrm(ks[4], (L, DEC_BATCH, CONV_W - 1, D_FF), 1.0),
        'p_prompt': nrm(ks[5], (L, BATCH, SEQ, PLE_DIM), 1.0),
        'p_sample': nrm(ks[6], (L, DEC_BATCH, DEC_SEQ, PLE_DIM), 1.0),
        'norm_mix': 1.0 + nrm(ks[7], (L, D_MODEL), 0.02),
        'w_in': nrm(ks[8], (L, D_MODEL, IN_WIDTH), D_MODEL ** -0.5),
        'w_gate_up': nrm(ks[9], (L, GATE_RANK, GLA_KW), GATE_RANK ** -0.5),
        'b_gate': nrm(ks[10], (L, GLA_KW), 0.01),
        'gla_norm': 1.0 + nrm(ks[11], (L, GLA_VW), 0.02),
        'w_branch_a': nrm(ks[12], (L, GLA_VW, D_MODEL), GLA_VW ** -0.5),
        'w_pool': nrm(ks[13], (L, POOL_GROUPS, POOL_GW, POOL_GW), POOL_GW ** -0.5),
        'pool_scale': 1.0 + nrm(ks[14], (L, POOL_W), 0.02),
        'w_branch_b': nrm(ks[15], (L, POOL_W, D_MODEL), POOL_W ** -0.5),
        'w_out': nrm(ks[16], (L, D_MODEL, D_MODEL), D_MODEL ** -0.5),
        'norm_ffn': 1.0 + nrm(ks[17], (L, D_MODEL), 0.02),
        'w_up': nrm(ks[18], (L, D_MODEL, 2 * D_FF), D_MODEL ** -0.5),
        'conv_w': nrm(ks[19], (L, CONV_W, D_FF), CONV_W ** -0.5),
        'conv_b': nrm(ks[20], (L, D_FF), 0.01),
        'w_down': nrm(ks[21], (L, D_FF, D_MODEL), D_FF ** -0.5),
        'norm_ple': 1.0 + nrm(ks[22], (L, D_MODEL), 0.02),
        'w_ple_gate': nrm(ks[23], (L, D_MODEL, D_MODEL), D_MODEL ** -0.5),
        'w_ple': nrm(ks[24], (L, PLE_DIM, D_MODEL), PLE_DIM ** -0.5),
        'norm_final': 1.0 + nrm(ks[25], (D_MODEL,), 0.02),
    }


def reference(x_prompt, x_sample, state_gla, state_pool, state_conv, p_prompt, p_sample,
              norm_mix, w_in, w_gate_up, b_gate, gla_norm, w_branch_a, w_pool, pool_scale,
              w_branch_b, w_out, norm_ffn, w_up, conv_w, conv_b, w_down, norm_ple, w_ple_gate,
              w_ple, norm_final):
    hp, hs = x_prompt, x_sample
    bp = x_prompt.shape[0]
    gla_p, pool_p, conv_p, gla_s, pool_s, conv_s = [], [], [], [], [], []
    for i in range(DEPTH):
        lw = {
            'norm_mix': norm_mix[i], 'w_in': w_in[i], 'w_gate_up': w_gate_up[i], 'b_gate': b_gate[i],
            'gla_norm': gla_norm[i], 'w_branch_a': w_branch_a[i], 'w_pool': w_pool[i],
            'pool_scale': pool_scale[i], 'w_branch_b': w_branch_b[i], 'w_out': w_out[i],
            'norm_ffn': norm_ffn[i], 'w_up': w_up[i], 'conv_w': conv_w[i], 'conv_b': conv_b[i],
            'w_down': w_down[i], 'norm_ple': norm_ple[i], 'w_ple_gate': w_ple_gate[i], 'w_ple': w_ple[i],
        }
        z_gla = jnp.zeros((bp, GLA_HEADS, GLA_DK, GLA_DV), state_gla.dtype)
        z_pool = jnp.zeros((bp, POOL_BUF, POOL_W), state_pool.dtype)
        z_conv = jnp.zeros((bp, CONV_W - 1, D_FF), state_conv.dtype)
        hp, g1, p1, c1 = trunk_layer(hp, p_prompt[i], z_gla, z_pool, z_conv, 0, lw)
        hs, g2, p2, c2 = trunk_layer(hs, p_sample[i], state_gla[i], state_pool[i], state_conv[i], PAST_LEN, lw)
        gla_p.append(g1); pool_p.append(p1); conv_p.append(c1)
        gla_s.append(g2); pool_s.append(p2); conv_s.append(c2)
    y_prompt = rmsnorm(hp, norm_final)
    y_sample = rmsnorm(hs, norm_final)
    new_gla_prompt = jnp.stack(gla_p, axis=0)
    new_pool_prompt = jnp.stack(pool_p, axis=0)
    new_conv_prompt = jnp.stack(conv_p, axis=0)
    new_gla_sample = jnp.stack(gla_s, axis=0)
    new_pool_sample = jnp.stack(pool_s, axis=0)
    new_conv_sample = jnp.stack(conv_s, axis=0)
    return (y_prompt, y_sample, new_gla_prompt, new_pool_prompt, new_conv_prompt, new_gla_sample, new_pool_sample, new_conv_sample)
```

```python
import functools

import jax
import jax.numpy as jnp
from jax import lax
from jax.experimental import pallas as pl
from jax.experimental.pallas import tpu as pltpu

F32 = jnp.float32
BF16 = jnp.bfloat16
HIGHEST = lax.Precision.HIGHEST

D_MODEL = 2048
GLA_HEADS = 4
GLA_DK = 256
GLA_DV = 512
GLA_KW = GLA_HEADS * GLA_DK
GLA_VW = GLA_HEADS * GLA_DV
GATE_RANK = 16
GATE_TEMP = 16.0
GLA_CHUNK = 64
POOL_WINDOWS = (2, 4, 8, 16)
POOL_GW = 256
POOL_W = 1024
POOL_BUF = 15
D_FF = 5504
CONV_W = 3
PLE_DIM = 256
EPS = 1e-6

LANE = 128
Z_Q, Z_K, Z_V, Z_R, Z_U, Z_GA, Z_GB, Z_WIDTH = 0, 1024, 2048, 4096, 6144, 7168, 9216, 11264
GLR_SRC = 6144
D_FF_PAD = 5632

VMEM_LIMIT = 56 * 1024 * 1024


def _cparams(n_axes):
    return pltpu.CompilerParams(dimension_semantics=("arbitrary",) * n_axes, vmem_limit_bytes=VMEM_LIMIT)


def _sigmoid(x):
    return 1.0 / (1.0 + jnp.exp(-x))


def _silu(x):
    return x * _sigmoid(x)


def _log_sigmoid(x):
    return jnp.minimum(x, 0.0) - jnp.log1p(jnp.exp(-jnp.abs(x)))


def _rms_kernel(x_ref, g_ref, o_ref):
    x = x_ref[...]
    ms = jnp.mean(x * x, axis=-1, keepdims=True)
    o_ref[...] = (x * lax.rsqrt(ms + EPS) * g_ref[...]).astype(o_ref.dtype)


def _rmsnorm(x, g, out_dtype, tm):
    m, d = x.shape
    return pl.pallas_call(
        _rms_kernel,
        grid=(m // tm,),
        in_specs=[pl.BlockSpec((tm, d), lambda i: (i, 0)), pl.BlockSpec((1, d), lambda i: (0, 0))],
        out_specs=pl.BlockSpec((tm, d), lambda i: (i, 0)),
        out_shape=jax.ShapeDtypeStruct((m, d), out_dtype),
        compiler_params=_cparams(1),
        name="rmsnorm",
    )(x, g.reshape(1, d))


def _mm_kernel(*refs, has_res, cast_w):
    a_ref, w_ref = refs[0], refs[1]
    pos = 2
    res_ref = None
    if has_res:
        res_ref = refs[pos]
        pos += 1
    o_ref = refs[pos]
    pos += 1
    if cast_w:
        wbf_ref = refs[pos]

        @pl.when(pl.program_id(1) == 0)
        def _():
            wbf_ref[...] = w_ref[...].astype(BF16)

        w = wbf_ref[...]
    else:
        w = w_ref[...]
    acc = jnp.dot(a_ref[...], w, preferred_element_type=F32)
    if has_res:
        acc = acc + res_ref[...]
    o_ref[...] = acc.astype(o_ref.dtype)


def _matmul(a, w, *, tm, tn, out_dtype, res=None, name):
    m, k = a.shape
    n = w.shape[1]
    cast_w = w.dtype != BF16
    in_specs = [pl.BlockSpec((tm, k), lambda j, i: (i, 0)), pl.BlockSpec((k, tn), lambda j, i: (0, j))]
    args = [a, w]
    if res is not None:
        in_specs.append(pl.BlockSpec((tm, tn), lambda j, i: (i, j)))
        args.append(res)
    return pl.pallas_call(
        functools.partial(_mm_kernel, has_res=res is not None, cast_w=cast_w),
        grid=(pl.cdiv(n, tn), m // tm),
        in_specs=in_specs,
        out_specs=pl.BlockSpec((tm, tn), lambda j, i: (i, j)),
        out_shape=jax.ShapeDtypeStruct((m, n), out_dtype),
        scratch_shapes=[pltpu.VMEM((k, tn), BF16)] if cast_w else [],
        compiler_params=_cparams(2),
        name=name,
    )(*args)


def _gla_prompt_kernel(q_ref, k_ref, v_ref, r_ref, glr_ref, wg_ref, bg_ref, gn_ref,
                       og_ref, s_out_ref, glog_ref, st_ref):
    t = q_ref.shape[0]
    c = GLA_CHUNK
    zg = jnp.dot(glr_ref[...], wg_ref[...], precision=HIGHEST, preferred_element_type=F32) + bg_ref[...]
    glog_ref[...] = _log_sigmoid(zg) * (1.0 / GATE_TEMP)
    st_ref[...] = jnp.zeros_like(st_ref)
    row = lax.broadcasted_iota(jnp.int32, (c, c), 0)
    col = lax.broadcasted_iota(jnp.int32, (c, c), 1)
    causal = row >= col
    tri = causal.astype(F32)
    gn = gn_ref[...]
    nt = (((1,), (1,)), ((), ()))
    tn = (((0,), (0,)), ((), ()))

    def body(ci, carry):
        r0 = pl.multiple_of(ci * c, c)
        gl = glog_ref[pl.ds(r0, c), :]
        b = jnp.dot(tri, gl, precision=HIGHEST, preferred_element_type=F32)
        bl = b[c - 1:c, :]
        q = q_ref[pl.ds(r0, c), :] * (GLA_DK ** -0.5)
        k = k_ref[pl.ds(r0, c), :]
        v = v_ref[pl.ds(r0, c), :].astype(BF16)
        qb = (q * jnp.exp(b)).astype(BF16)
        kb = (k * jnp.exp(-b)).astype(BF16)
        kd = (k * jnp.exp(bl - b)).astype(BF16)
        att = lax.dot_general(qb, kb, nt, preferred_element_type=F32)
        att = jnp.where(causal, att, 0.0).astype(BF16)
        st = st_ref[...]
        o = lax.dot_general(qb, st.astype(BF16), nt, preferred_element_type=F32)
        o = o + jnp.dot(att, v, preferred_element_type=F32)
        dst = lax.dot_general(v, kd, tn, preferred_element_type=F32)
        st_ref[...] = st * jnp.exp(bl) + dst
        ms = jnp.mean(o * o, axis=-1, keepdims=True)
        on = o * lax.rsqrt(ms + EPS) * gn
        og_ref[pl.ds(r0, c), :] = (on * _silu(r_ref[pl.ds(r0, c), :])).astype(og_ref.dtype)
        return carry

    lax.fori_loop(0, t // c, body, 0)
    s_out_ref[0, 0] = st_ref[...].T


def _gla_prompt(z, glr, wg, bg, gn, nb, t):
    m = z.shape[0]
    hh = GLA_HEADS
    og, s_new = pl.pallas_call(
        _gla_prompt_kernel,
        grid=(nb, hh),
        in_specs=[
            pl.BlockSpec((t, GLA_DK), lambda b, h: (b, Z_Q // GLA_DK + h)),
            pl.BlockSpec((t, GLA_DK), lambda b, h: (b, Z_K // GLA_DK + h)),
            pl.BlockSpec((t, GLA_DV), lambda b, h: (b, Z_V // GLA_DV + h)),
            pl.BlockSpec((t, GLA_DV), lambda b, h: (b, Z_R // GLA_DV + h)),
            pl.BlockSpec((t, LANE), lambda b, h: (b, 0)),
            pl.BlockSpec((LANE, GLA_DK), lambda b, h: (0, h)),
            pl.BlockSpec((1, GLA_DK), lambda b, h: (0, h)),
            pl.BlockSpec((1, GLA_DV), lambda b, h: (0, h)),
        ],
        out_specs=[
            pl.BlockSpec((t, GLA_DV), lambda b, h: (b, h)),
            pl.BlockSpec((1, 1, GLA_DK, GLA_DV), lambda b, h: (b, h, 0, 0)),
        ],
        out_shape=[
            jax.ShapeDtypeStruct((m, GLA_VW), BF16),
            jax.ShapeDtypeStruct((nb, hh, GLA_DK, GLA_DV), F32),
        ],
        scratch_shapes=[pltpu.VMEM((t, GLA_DK), F32), pltpu.VMEM((GLA_DV, GLA_DK), F32)],
        compiler_params=_cparams(2),
        name="gla_prompt",
    )(z, z, z, z, glr, wg, bg, gn)
    return og, s_new


def _gla_sample_kernel(q_ref, k_ref, v_ref, r_ref, glr_ref, wg_ref, bg_ref, gn_ref, s_ref,
                       og_ref, s_out_ref):
    rows = 8
    tn = (((0,), (0,)), ((), ()))
    glr8 = jnp.broadcast_to(glr_ref[0], (rows, LANE))
    zg = jnp.dot(glr8, wg_ref[...], precision=HIGHEST, preferred_element_type=F32) + bg_ref[...]
    g = _log_sigmoid(zg) * (1.0 / GATE_TEMP)
    row0 = lax.broadcasted_iota(jnp.int32, (rows, GLA_DV), 0) == 0
    ones0 = jnp.where(row0, 1.0, 0.0).astype(F32)
    q_all = jnp.broadcast_to(q_ref[0], (rows, GLA_KW)) * (GLA_DK ** -0.5)
    k_all = jnp.broadcast_to(k_ref[0], (rows, GLA_KW))
    v_all = jnp.broadcast_to(v_ref[0], (rows, GLA_VW))
    r_all = r_ref[0]
    gn_all = gn_ref[...]
    for h in range(GLA_HEADS):
        ks = slice(h * GLA_DK, (h + 1) * GLA_DK)
        vs = slice(h * GLA_DV, (h + 1) * GLA_DV)
        gh = g[:, ks]
        q, k, v = q_all[:, ks], k_all[:, ks], v_all[:, vs]
        s_old = s_ref[0, h]
        decay = lax.dot_general(jnp.exp(gh), ones0, tn, precision=HIGHEST, preferred_element_type=F32)
        v0 = jnp.where(row0, v, 0.0)
        ds = lax.dot_general(k.astype(BF16), v0.astype(BF16), tn, preferred_element_type=F32)
        s_out_ref[0, h] = decay * s_old + ds
        qb = q * jnp.exp(gh)
        kb = k * jnp.exp(-gh)
        att = jnp.sum(qb * kb, axis=-1, keepdims=True)
        o = jnp.dot(qb.astype(BF16), s_old.astype(BF16), preferred_element_type=F32) + att * v
        o = o[0:1, :]
        ms = jnp.mean(o * o, axis=-1, keepdims=True)
        on = o * lax.rsqrt(ms + EPS) * gn_all[:, vs]
        og_ref[0, :, vs] = (on * _silu(r_all[:, vs])).astype(og_ref.dtype)


def _gla_sample(z, glr, wg, bg, gn, state):
    n = z.shape[0]
    z3 = z.reshape(n, 1, Z_WIDTH)
    glr3 = glr.reshape(n, 1, LANE)
    og, s_new = pl.pallas_call(
        _gla_sample_kernel,
        grid=(n,),
        in_specs=[
            pl.BlockSpec((1, 1, GLA_KW), lambda i: (i, 0, Z_Q // GLA_KW)),
            pl.BlockSpec((1, 1, GLA_KW), lambda i: (i, 0, Z_K // GLA_KW)),
            pl.BlockSpec((1, 1, GLA_VW), lambda i: (i, 0, Z_V // GLA_VW)),
            pl.BlockSpec((1, 1, GLA_VW), lambda i: (i, 0, Z_R // GLA_VW)),
            pl.BlockSpec((1, 1, LANE), lambda i: (i, 0, 0)),
            pl.BlockSpec((LANE, GLA_KW), lambda i: (0, 0)),
            pl.BlockSpec((1, GLA_KW), lambda i: (0, 0)),
            pl.BlockSpec((1, GLA_VW), lambda i: (0, 0)),
            pl.BlockSpec((1, GLA_HEADS, GLA_DK, GLA_DV), lambda i: (i, 0, 0, 0)),
        ],
        out_specs=[
            pl.BlockSpec((1, 1, GLA_VW), lambda i: (i, 0, 0)),
            pl.BlockSpec((1, GLA_HEADS, GLA_DK, GLA_DV), lambda i: (i, 0, 0, 0)),
        ],
        out_shape=[
            jax.ShapeDtypeStruct((n, 1, GLA_VW), BF16),
            jax.ShapeDtypeStruct((n, GLA_HEADS, GLA_DK, GLA_DV), F32),
        ],
        compiler_params=_cparams(1),
        name="gla_sample",
    )(z3, z3, z3, z3, glr3, wg, bg, gn, state)
    return og.reshape(n, GLA_VW), s_new


def _pool_group_out(mix, g, wp_ref, ps_ref):
    cs = slice(g * POOL_GW, (g + 1) * POOL_GW)
    pm = jnp.dot(mix.astype(BF16), wp_ref[g].astype(BF16), preferred_element_type=F32)
    return pm * ps_ref[:, cs]


def _pool_prompt_kernel(u_ref, wp_ref, ps_ref, o_ref, ext_ref):
    i = pl.program_id(1)
    tt = u_ref.shape[0]
    hist = POOL_BUF + 1

    @pl.when(i == 0)
    def _():
        ext_ref[0:hist, :] = jnp.zeros((hist, POOL_W), F32)

    @pl.when(i > 0)
    def _():
        ext_ref[0:hist, :] = ext_ref[tt:tt + hist, :]

    ext_ref[hist:hist + tt, :] = u_ref[...]
    pos = i * tt + lax.broadcasted_iota(jnp.int32, (tt, 1), 0)
    for g, w in enumerate(POOL_WINDOWS):
        cs = slice(g * POOL_GW, (g + 1) * POOL_GW)
        u = ext_ref[hist:hist + tt, cs]
        s = u
        for d in range(1, w):
            s = s + ext_ref[hist - d:hist - d + tt, cs]
        cnt = jnp.minimum(pos + 1, w).astype(F32)
        mix = s / cnt - u
        o_ref[:, cs] = _pool_group_out(mix, g, wp_ref, ps_ref).astype(o_ref.dtype)


def _pool_prompt(z, w_pool, pool_scale, nb, t, tt):
    m = z.shape[0]
    nt = t // tt
    return pl.pallas_call(
        _pool_prompt_kernel,
        grid=(nb, nt),
        in_specs=[
            pl.BlockSpec((tt, POOL_W), lambda b, i: (b * nt + i, Z_U // POOL_W)),
            pl.BlockSpec((4, POOL_GW, POOL_GW), lambda b, i: (0, 0, 0)),
            pl.BlockSpec((1, POOL_W), lambda b, i: (0, 0)),
        ],
        out_specs=pl.BlockSpec((tt, POOL_W), lambda b, i: (b * nt + i, 0)),
        out_shape=jax.ShapeDtypeStruct((m, POOL_W), BF16),
        scratch_shapes=[pltpu.VMEM((tt + POOL_BUF + 1, POOL_W), F32)],
        compiler_params=_cparams(2),
        name="pool_prompt",
    )(z, w_pool, pool_scale)


def _pool_sample_kernel(u_ref, buf_ref, wp_ref, ps_ref, o_ref):
    for g, w in enumerate(POOL_WINDOWS):
        cs = slice(g * POOL_GW, (g + 1) * POOL_GW)
        u = u_ref[:, cs]
        s = u
        for d in range(1, w):
            s = s + buf_ref[POOL_BUF - d, :, cs]
        mix = s / float(w) - u
        o_ref[:, cs] = _pool_group_out(mix, g, wp_ref, ps_ref).astype(o_ref.dtype)


def _pool_sample(z, buf_t, w_pool, pool_scale):
    n = z.shape[0]
    return pl.pallas_call(
        _pool_sample_kernel,
        grid=(1,),
        in_specs=[
            pl.BlockSpec((n, POOL_W), lambda i: (0, Z_U // POOL_W)),
            pl.BlockSpec((POOL_BUF, n, POOL_W), lambda i: (0, 0, 0)),
            pl.BlockSpec((4, POOL_GW, POOL_GW), lambda i: (0, 0, 0)),
            pl.BlockSpec((1, POOL_W), lambda i: (0, 0)),
        ],
        out_specs=pl.BlockSpec((n, POOL_W), lambda i: (0, 0)),
        out_shape=jax.ShapeDtypeStruct((n, POOL_W), BF16),
        compiler_params=_cparams(1),
        name="pool_sample",
    )(z, buf_t, w_pool, pool_scale)


def _merge_kernel(og_ref, pm_ref, wa_ref, wb_ref, ga_ref, gb_ref, o_ref, wabf_ref, wbbf_ref):
    @pl.when(pl.program_id(1) == 0)
    def _():
        wabf_ref[...] = wa_ref[...].astype(BF16)
        wbbf_ref[...] = wb_ref[...].astype(BF16)

    ya = jnp.dot(og_ref[...], wabf_ref[...], preferred_element_type=F32)
    yb = jnp.dot(pm_ref[...], wbbf_ref[...], preferred_element_type=F32)
    o_ref[...] = (_sigmoid(ga_ref[...]) * ya + _sigmoid(gb_ref[...]) * yb).astype(o_ref.dtype)


def _merge(og, pm, wa, wb, z, *, tm, tn):
    m = og.shape[0]
    return pl.pallas_call(
        _merge_kernel,
        grid=(D_MODEL // tn, m // tm),
        in_specs=[
            pl.BlockSpec((tm, GLA_VW), lambda j, i: (i, 0)),
            pl.BlockSpec((tm, POOL_W), lambda j, i: (i, 0)),
            pl.BlockSpec((GLA_VW, tn), lambda j, i: (0, j)),
            pl.BlockSpec((POOL_W, tn), lambda j, i: (0, j)),
            pl.BlockSpec((tm, tn), lambda j, i: (i, Z_GA // tn + j)),
            pl.BlockSpec((tm, tn), lambda j, i: (i, Z_GB // tn + j)),
        ],
        out_specs=pl.BlockSpec((tm, tn), lambda j, i: (i, j)),
        out_shape=jax.ShapeDtypeStruct((m, D_MODEL), BF16),
        scratch_shapes=[pltpu.VMEM((GLA_VW, tn), BF16), pltpu.VMEM((POOL_W, tn), BF16)],
        compiler_params=_cparams(2),
        name="merge",
    )(og, pm, wa, wb, z, z)


def _up_kernel(*refs, sample, tiles_per_seq):
    if sample:
        c_ref, wg_ref, wu_ref, cw_ref, cb_ref, h0_ref, h1_ref, act_ref, tail_ref = refs
    else:
        c_ref, wg_ref, wu_ref, cw_ref, cb_ref, act_ref, tail_ref, carry_ref = refs
    c = c_ref[...]
    gate = jnp.dot(c, wg_ref[...], preferred_element_type=F32)
    up = jnp.dot(c, wu_ref[...], preferred_element_type=F32)
    tm = gate.shape[0]
    cw = cw_ref[...]
    if sample:
        g2, g1 = h0_ref[...], h1_ref[...]
        tail_ref[...] = gate
    else:
        i = pl.program_id(1)

        @pl.when(i % tiles_per_seq == 0)
        def _():
            carry_ref[...] = jnp.zeros_like(carry_ref)

        prev = carry_ref[...]
        ext = jnp.concatenate([prev, gate], axis=0)
        g1 = ext[7:7 + tm]
        g2 = ext[6:6 + tm]
        carry_ref[...] = gate[tm - 8:tm]
        tail_ref[0] = gate[tm - 2:tm]
    gconv = cb_ref[...] + cw[0:1] * g2 + cw[1:2] * g1 + cw[2:3] * gate
    act_ref[...] = (_silu(gconv) * up).astype(act_ref.dtype)


def _up_proj(c, wg, wu, conv_w, conv_b, *, tm, tn, seq_len=None, hist=None):
    m, k = c.shape
    sample = hist is not None
    nj = D_FF_PAD // tn
    in_specs = [
        pl.BlockSpec((tm, k), lambda j, i: (i, 0)),
        pl.BlockSpec((k, tn), lambda j, i: (0, j)),
        pl.BlockSpec((k, tn), lambda j, i: (0, j)),
        pl.BlockSpec((CONV_W, tn), lambda j, i: (0, j)),
        pl.BlockSpec((1, tn), lambda j, i: (0, j)),
    ]
    args = [c, wg, wu, conv_w, conv_b]
    if sample:
        in_specs += [pl.BlockSpec((tm, tn), lambda j, i: (i, j))] * 2
        args += [hist[0], hist[1]]
        tail_spec = pl.BlockSpec((tm, tn), lambda j, i: (i, j))
        tail_shape = jax.ShapeDtypeStruct((m, D_FF), F32)
        scratch = []
        tiles_per_seq = 1
    else:
        tiles_per_seq = seq_len // tm
        nseq = m // seq_len
        tail_spec = pl.BlockSpec((1, CONV_W - 1, tn), lambda j, i: (i // tiles_per_seq, 0, j))
        tail_shape = jax.ShapeDtypeStruct((nseq, CONV_W - 1, D_FF), F32)
        scratch = [pltpu.VMEM((8, tn), F32)]
    return pl.pallas_call(
        functools.partial(_up_kernel, sample=sample, tiles_per_seq=tiles_per_seq),
        grid=(nj, m // tm),
        in_specs=in_specs,
        out_specs=[pl.BlockSpec((tm, tn), lambda j, i: (i, j)), tail_spec],
        out_shape=[jax.ShapeDtypeStruct((m, D_FF), BF16), tail_shape],
        scratch_shapes=scratch,
        compiler_params=_cparams(2),
        name="up_proj_sample" if sample else "up_proj",
    )(*args)


def _ple_kernel(cn_ref, p_ref, wpg_ref, wple_ref, h_ref, o_ref, wpgbf_ref, wplebf_ref):
    @pl.when(pl.program_id(1) == 0)
    def _():
        wpgbf_ref[...] = wpg_ref[...].astype(BF16)
        wplebf_ref[...] = wple_ref[...].astype(BF16)

    pg = _sigmoid(jnp.dot(cn_ref[...], wpgbf_ref[...], preferred_element_type=F32))
    e = jnp.dot(p_ref[...].astype(BF16), wplebf_ref[...], preferred_element_type=F32)
    o_ref[...] = h_ref[...] + pg * e


def _ple(cn, p, wpg, wple, h, *, tm, tn):
    m = cn.shape[0]
    return pl.pallas_call(
        _ple_kernel,
        grid=(D_MODEL // tn, m // tm),
        in_specs=[
            pl.BlockSpec((tm, D_MODEL), lambda j, i: (i, 0)),
            pl.BlockSpec((tm, PLE_DIM), lambda j, i: (i, 0)),
            pl.BlockSpec((D_MODEL, tn), lambda j, i: (0, j)),
            pl.BlockSpec((PLE_DIM, tn), lambda j, i: (0, j)),
            pl.BlockSpec((tm, tn), lambda j, i: (i, j)),
        ],
        out_specs=pl.BlockSpec((tm, tn), lambda j, i: (i, j)),
        out_shape=jax.ShapeDtypeStruct((m, D_MODEL), F32),
        scratch_shapes=[pltpu.VMEM((D_MODEL, tn), BF16), pltpu.VMEM((PLE_DIM, tn), BF16)],
        compiler_params=_cparams(2),
        name="ple",
    )(cn, p, wpg, wple, h)


def _layer(x, p, lw, *, tm, tr, prompt_shape=None, state=None):
    sample = state is not None
    a = _rmsnorm(x, lw["norm_mix"], BF16, tr)
    z = _matmul(a, lw["w_in_z"], tm=tm, tn=512, out_dtype=F32, name="in_proj")
    glr = _matmul(a, lw["w_in_glr"], tm=tm, tn=LANE, out_dtype=F32, name="in_proj_glr")
    if sample:
        s_gla, s_pool, s_conv = state
        og, gla_new = _gla_sample(z, glr, lw["wg"], lw["bg"], lw["gn"], s_gla)
        pm = _pool_sample(z, jnp.transpose(s_pool, (1, 0, 2)), lw["w_pool"], lw["pool_scale"])
        u = z[:, Z_U:Z_U + POOL_W]
        pool_new = jnp.concatenate([s_pool[:, 1:], u[:, None, :]], axis=1)
    else:
        nb, t = prompt_shape
        og, gla_new = _gla_prompt(z, glr, lw["wg"], lw["bg"], lw["gn"], nb, t)
        pm = _pool_prompt(z, lw["w_pool"], lw["pool_scale"], nb, t, 512)
        pool_new = z.reshape(nb, t, Z_WIDTH)[:, t - POOL_BUF:, Z_U:Z_U + POOL_W]
    mg = _merge(og, pm, lw["w_branch_a"], lw["w_branch_b"], z, tm=tm, tn=512)
    h1 = _matmul(mg, lw["w_out"], tm=tm, tn=512, out_dtype=F32, res=x, name="out_proj")
    c = _rmsnorm(h1, lw["norm_ffn"], BF16, tr)
    if sample:
        act, gate = _up_proj(c, lw["w_up_g"], lw["w_up_u"], lw["conv_w"], lw["conv_b"], tm=tm, tn=512,
                             hist=(s_conv[:, 0], s_conv[:, 1]))
        conv_new = jnp.stack([s_conv[:, 1], gate], axis=1)
    else:
        act, conv_new = _up_proj(c, lw["w_up_g"], lw["w_up_u"], lw["conv_w"], lw["conv_b"], tm=tm, tn=512,
                                 seq_len=t)
    h2 = _matmul(act, lw["w_down"], tm=min(tm, 512), tn=512, out_dtype=F32, res=h1, name="down_proj")
    cn = _rmsnorm(h2, lw["norm_ple"], BF16, tr)
    h3 = _ple(cn, p, lw["w_ple_gate"], lw["w_ple"], h2, tm=tm, tn=512)
    return h3, gla_new, pool_new, conv_new


def kernel(x_prompt, x_sample, state_gla, state_pool, state_conv, p_prompt, p_sample, norm_mix, w_in, w_gate_up, b_gate, gla_norm, w_branch_a, w_pool, pool_scale, w_branch_b, w_out, norm_ffn, w_up, conv_w, conv_b, w_down, norm_ple, w_ple_gate, w_ple, norm_final):
    depth = w_in.shape[0]
    nb, t, d = x_prompt.shape
    ns = x_sample.shape[0]
    hp = x_prompt.reshape(nb * t, d)
    hs = x_sample.reshape(ns, d)
    outs = {k: [] for k in ("gp", "pp", "cp", "gs", "ps", "cs")}
    for li in range(depth):
        wi = w_in[li]
        glr_cols = wi[:, GLR_SRC:GLR_SRC + GATE_RANK]
        lw = {
            "norm_mix": norm_mix[li],
            "w_in_z": jnp.concatenate([wi[:, :GLR_SRC], wi[:, GLR_SRC + GATE_RANK:]], axis=1).astype(BF16),
            "w_in_glr": jnp.pad(glr_cols, ((0, 0), (0, LANE - GATE_RANK))).astype(BF16),
            "wg": jnp.pad(w_gate_up[li], ((0, LANE - GATE_RANK), (0, 0))),
            "bg": b_gate[li].reshape(1, GLA_KW),
            "gn": gla_norm[li].reshape(1, GLA_VW),
            "w_branch_a": w_branch_a[li],
            "w_pool": w_pool[li],
            "pool_scale": pool_scale[li].reshape(1, POOL_W),
            "w_branch_b": w_branch_b[li],
            "w_out": w_out[li],
            "norm_ffn": norm_ffn[li],
            "w_up_g": jnp.pad(w_up[li][:, :D_FF], ((0, 0), (0, D_FF_PAD - D_FF))).astype(BF16),
            "w_up_u": jnp.pad(w_up[li][:, D_FF:], ((0, 0), (0, D_FF_PAD - D_FF))).astype(BF16),
            "conv_w": conv_w[li],
            "conv_b": conv_b[li].reshape(1, D_FF),
            "w_down": w_down[li],
            "norm_ple": norm_ple[li],
            "w_ple_gate": w_ple_gate[li],
            "w_ple": w_ple[li],
        }
        hp, g1, p1, c1 = _layer(hp, p_prompt[li].reshape(nb * t, PLE_DIM), lw, tm=1024, tr=512,
                                prompt_shape=(nb, t))
        hs, g2, p2, c2 = _layer(hs, p_sample[li].reshape(ns, PLE_DIM), lw, tm=ns, tr=ns,
                                state=(state_gla[li], state_pool[li], state_conv[li]))
        for key, val in zip(("gp", "pp", "cp", "gs", "ps", "cs"), (g1, p1, c1, g2, p2, c2)):
            outs[key].append(val)
    y_prompt = _rmsnorm(hp, norm_final, F32, 512).reshape(nb, t, d)
    y_sample = _rmsnorm(hs, norm_final, F32, ns).reshape(ns, 1, d)
    st = {k: jnp.stack(v, axis=0) for k, v in outs.items()}
    return (y_prompt, y_sample, st["gp"], st["pp"], st["cp"], st["gs"], st["ps"], st["cs"])
```

```python
import functools

import jax
import jax.numpy as jnp
from jax import lax
from jax.experimental import pallas as pl
from jax.experimental.pallas import tpu as pltpu

F32 = jnp.float32
BF16 = jnp.bfloat16
HIGHEST = lax.Precision.HIGHEST

D_MODEL = 2048
GLA_HEADS = 4
GLA_DK = 256
GLA_DV = 512
GLA_KW = GLA_HEADS * GLA_DK
GLA_VW = GLA_HEADS * GLA_DV
GATE_RANK = 16
GATE_TEMP = 16.0
GLA_CHUNK = 64
POOL_WINDOWS = (2, 4, 8, 16)
POOL_GW = 256
POOL_W = 1024
POOL_BUF = 15
D_FF = 5504
CONV_W = 3
PLE_DIM = 256
EPS = 1e-6

LANE = 128
Z_Q, Z_K, Z_V, Z_R, Z_U, Z_GA, Z_GB, Z_GLR, Z_WIDTH = 0, 1024, 2048, 4096, 6144, 7168, 9216, 11264, 11392
GLR_SRC = 6144
IN_TN = 1024

VMEM_LIMIT = 56 * 1024 * 1024


def _cparams(n_axes):
    return pltpu.CompilerParams(dimension_semantics=("arbitrary",) * n_axes, vmem_limit_bytes=VMEM_LIMIT)


def _sigmoid(x):
    return 1.0 / (1.0 + jnp.exp(-x))


def _silu(x):
    return x * _sigmoid(x)


def _log_sigmoid(x):
    return jnp.minimum(x, 0.0) - jnp.log1p(jnp.exp(-jnp.abs(x)))


def _rms_kernel(x_ref, g_ref, o_ref):
    x = x_ref[...]
    ms = jnp.mean(x * x, axis=-1, keepdims=True)
    o_ref[...] = (x * lax.rsqrt(ms + EPS) * g_ref[...]).astype(o_ref.dtype)


def _rmsnorm(x, g, out_dtype, tm):
    m, d = x.shape
    return pl.pallas_call(
        _rms_kernel,
        grid=(m // tm,),
        in_specs=[pl.BlockSpec((tm, d), lambda i: (i, 0)), pl.BlockSpec((1, d), lambda i: (0, 0))],
        out_specs=pl.BlockSpec((tm, d), lambda i: (i, 0)),
        out_shape=jax.ShapeDtypeStruct((m, d), out_dtype),
        compiler_params=_cparams(1),
        name="rmsnorm",
    )(x, g.reshape(1, d))


def _mm_kernel(*refs, has_res, cast_w):
    a_ref, w_ref = refs[0], refs[1]
    pos = 2
    res_ref = None
    if has_res:
        res_ref = refs[pos]
        pos += 1
    o_ref = refs[pos]
    pos += 1
    if cast_w:
        wbf_ref = refs[pos]

        @pl.when(pl.program_id(1) == 0)
        def _():
            wbf_ref[...] = w_ref[...].astype(BF16)

        w = wbf_ref[...]
    else:
        w = w_ref[...]
    acc = jnp.dot(a_ref[...], w, preferred_element_type=F32)
    if has_res:
        acc = acc + res_ref[...]
    o_ref[...] = acc.astype(o_ref.dtype)


def _matmul(a, w, *, tm, tn, out_dtype, res=None, name):
    m, k = a.shape
    n = w.shape[1]
    cast_w = w.dtype != BF16
    in_specs = [pl.BlockSpec((tm, k), lambda j, i: (i, 0)), pl.BlockSpec((k, tn), lambda j, i: (0, j))]
    args = [a, w]
    if res is not None:
        in_specs.append(pl.BlockSpec((tm, tn), lambda j, i: (i, j)))
        args.append(res)
    return pl.pallas_call(
        functools.partial(_mm_kernel, has_res=res is not None, cast_w=cast_w),
        grid=(pl.cdiv(n, tn), m // tm),
        in_specs=in_specs,
        out_specs=pl.BlockSpec((tm, tn), lambda j, i: (i, j)),
        out_shape=jax.ShapeDtypeStruct((m, n), out_dtype),
        scratch_shapes=[pltpu.VMEM((k, tn), BF16)] if cast_w else [],
        compiler_params=_cparams(2),
        name=name,
    )(*args)


def _in_proj_kernel(ap_ref, as_ref, w_ref, wx_ref, zp_ref, zs_ref, wbf_ref, *, n_prompt_tiles):
    j = pl.program_id(0)
    i = pl.program_id(1)
    last = pl.num_programs(0) - 1
    first_shifted = GLR_SRC // IN_TN
    keep = IN_TN - GATE_RANK

    @pl.when(i == 0)
    def _():
        @pl.when(j < first_shifted)
        def _():
            wbf_ref[...] = w_ref[...].astype(BF16)

        @pl.when(jnp.logical_and(j >= first_shifted, j < last))
        def _():
            wbf_ref[:, :keep] = w_ref[:, GATE_RANK:].astype(BF16)
            wbf_ref[:, keep:] = wx_ref[:, :GATE_RANK].astype(BF16)

        @pl.when(j == last)
        def _():
            lane = lax.broadcasted_iota(jnp.int32, wx_ref.shape, 1)
            wbf_ref[:, :LANE] = jnp.where(lane < GATE_RANK, wx_ref[...], 0.0).astype(BF16)

    def emit(a_ref, z_ref):
        @pl.when(j < last)
        def _():
            z_ref[...] = jnp.dot(a_ref[...], wbf_ref[...], preferred_element_type=F32).astype(z_ref.dtype)

        @pl.when(j == last)
        def _():
            z_ref[:, :LANE] = jnp.dot(a_ref[...], wbf_ref[:, :LANE],
                                      preferred_element_type=F32).astype(z_ref.dtype)

    @pl.when(i < n_prompt_tiles)
    def _():
        emit(ap_ref, zp_ref)

    @pl.when(i == n_prompt_tiles)
    def _():
        emit(as_ref, zs_ref)


def _in_proj(a_p, a_s, w_in, *, tm):
    mp, k = a_p.shape
    ms = a_s.shape[0]
    npt = mp // tm
    nj = Z_GLR // IN_TN + 1
    lane_blocks = IN_TN // LANE

    def aux_idx(j, i):
        return (0, jnp.where(j == nj - 1, GLR_SRC // LANE, (j + 1) * lane_blocks))

    return pl.pallas_call(
        functools.partial(_in_proj_kernel, n_prompt_tiles=npt),
        grid=(nj, npt + 1),
        in_specs=[
            pl.BlockSpec((tm, k), lambda j, i: (jnp.minimum(i, npt - 1), 0)),
            pl.BlockSpec((ms, k), lambda j, i: (0, 0)),
            pl.BlockSpec((k, IN_TN), lambda j, i: (0, j)),
            pl.BlockSpec((k, LANE), aux_idx),
        ],
        out_specs=[
            pl.BlockSpec((tm, IN_TN), lambda j, i: (jnp.minimum(i, npt - 1), j)),
            pl.BlockSpec((ms, IN_TN), lambda j, i: (0, j)),
        ],
        out_shape=[jax.ShapeDtypeStruct((mp, Z_WIDTH), BF16), jax.ShapeDtypeStruct((ms, Z_WIDTH), BF16)],
        scratch_shapes=[pltpu.VMEM((k, IN_TN), BF16)],
        compiler_params=_cparams(2),
        name="in_proj",
    )(a_p, a_s, w_in, w_in)


GLA_GROUP = 8


def _gla_prompt_kernel(q_ref, k_ref, v_ref, r_ref, glr_ref, wg_ref, bg_ref, gn_ref,
                       og_ref, s_out_ref, st_ref):
    t = q_ref.shape[0]
    c = GLA_CHUNK
    rows = GLA_GROUP * c
    st_ref[...] = jnp.zeros_like(st_ref)
    row = lax.broadcasted_iota(jnp.int32, (c, c), 0)
    col = lax.broadcasted_iota(jnp.int32, (c, c), 1)
    causal = row >= col
    tri = causal.astype(BF16)
    nt = (((1,), (1,)), ((), ()))
    tn = (((0,), (0,)), ((), ()))

    def group(gi, carry):
        r0 = pl.multiple_of(gi * rows, rows)
        rs = pl.ds(r0, rows)
        zg = jnp.dot(glr_ref[rs, :], wg_ref[...], preferred_element_type=F32) + bg_ref[...]
        gl = _log_sigmoid(zg) * (1.0 / GATE_TEMP)
        gl_hi = gl.astype(BF16)
        gl_lo = (gl - gl_hi.astype(F32)).astype(BF16)
        q = q_ref[rs, :].astype(F32) * (GLA_DK ** -0.5)
        k = k_ref[rs, :].astype(F32)
        v = v_ref[rs, :]
        st = st_ref[...]
        outs = []
        for ci in range(GLA_GROUP):
            sl = slice(ci * c, (ci + 1) * c)
            b = (jnp.dot(tri, gl_hi[sl], preferred_element_type=F32)
                 + jnp.dot(tri, gl_lo[sl], preferred_element_type=F32))
            bl = b[c - 1:c, :]
            qb = (q[sl] * jnp.exp(b)).astype(BF16)
            kb = (k[sl] * jnp.exp(-b)).astype(BF16)
            kd = (k[sl] * jnp.exp(bl - b)).astype(BF16)
            att = lax.dot_general(qb, kb, nt, preferred_element_type=F32)
            att = jnp.where(causal, att, 0.0).astype(BF16)
            vc = v[sl]
            o = lax.dot_general(qb, st.astype(BF16), nt, preferred_element_type=F32)
            outs.append(o + jnp.dot(att, vc, preferred_element_type=F32))
            st = st * jnp.exp(bl) + lax.dot_general(vc, kd, tn, preferred_element_type=F32)
        st_ref[...] = st
        o = jnp.concatenate(outs, axis=0)
        ms = jnp.mean(o * o, axis=-1, keepdims=True)
        on = o * lax.rsqrt(ms + EPS) * gn_ref[...]
        og_ref[rs, :] = (on * _silu(r_ref[rs, :].astype(F32))).astype(og_ref.dtype)
        return carry

    lax.fori_loop(0, t // rows, group, 0)
    s_out_ref[0, 0] = st_ref[...].T


def _gla_prompt(z, wg, bg, gn, nb, t):
    m = z.shape[0]
    hh = GLA_HEADS
    og, s_new = pl.pallas_call(
        _gla_prompt_kernel,
        grid=(nb, hh),
        in_specs=[
            pl.BlockSpec((t, GLA_DK), lambda b, h: (b, Z_Q // GLA_DK + h)),
            pl.BlockSpec((t, GLA_DK), lambda b, h: (b, Z_K // GLA_DK + h)),
            pl.BlockSpec((t, GLA_DV), lambda b, h: (b, Z_V // GLA_DV + h)),
            pl.BlockSpec((t, GLA_DV), lambda b, h: (b, Z_R // GLA_DV + h)),
            pl.BlockSpec((t, LANE), lambda b, h: (b, Z_GLR // LANE)),
            pl.BlockSpec((LANE, GLA_DK), lambda b, h: (0, h)),
            pl.BlockSpec((1, GLA_DK), lambda b, h: (0, h)),
            pl.BlockSpec((1, GLA_DV), lambda b, h: (0, h)),
        ],
        out_specs=[
            pl.BlockSpec((t, GLA_DV), lambda b, h: (b, h)),
            pl.BlockSpec((1, 1, GLA_DK, GLA_DV), lambda b, h: (b, h, 0, 0)),
        ],
        out_shape=[
            jax.ShapeDtypeStruct((m, GLA_VW), BF16),
            jax.ShapeDtypeStruct((nb, hh, GLA_DK, GLA_DV), F32),
        ],
        scratch_shapes=[pltpu.VMEM((GLA_DV, GLA_DK), F32)],
        compiler_params=_cparams(2),
        name="gla_prompt",
    )(z, z, z, z, z, wg, bg, gn)
    return og, s_new


def _gla_sample_kernel(q_ref, k_ref, v_ref, r_ref, glr_ref, wg_ref, bg_ref, gn_ref, s_ref,
                       og_ref, s_out_ref):
    rows = 8
    tn = (((0,), (0,)), ((), ()))
    glr8 = jnp.broadcast_to(glr_ref[0].astype(F32), (rows, LANE)).astype(BF16)
    zg = jnp.dot(glr8, wg_ref[...], preferred_element_type=F32) + bg_ref[...]
    g = _log_sigmoid(zg) * (1.0 / GATE_TEMP)
    row0 = lax.broadcasted_iota(jnp.int32, (rows, GLA_DV), 0) == 0
    ones0 = jnp.where(row0, 1.0, 0.0).astype(F32)
    q_all = jnp.broadcast_to(q_ref[0].astype(F32), (rows, GLA_KW)) * (GLA_DK ** -0.5)
    k_all = jnp.broadcast_to(k_ref[0].astype(F32), (rows, GLA_KW))
    v_all = jnp.broadcast_to(v_ref[0].astype(F32), (rows, GLA_VW))
    r_all = r_ref[0].astype(F32)
    gn_all = gn_ref[...]
    for h in range(GLA_HEADS):
        ks = slice(h * GLA_DK, (h + 1) * GLA_DK)
        vs = slice(h * GLA_DV, (h + 1) * GLA_DV)
        gh = g[:, ks]
        q, k, v = q_all[:, ks], k_all[:, ks], v_all[:, vs]
        s_old = s_ref[0, h]
        decay = lax.dot_general(jnp.exp(gh), ones0, tn, precision=HIGHEST, preferred_element_type=F32)
        v0 = jnp.where(row0, v, 0.0)
        ds = lax.dot_general(k.astype(BF16), v0.astype(BF16), tn, preferred_element_type=F32)
        s_out_ref[0, h] = decay * s_old + ds
        qb = q * jnp.exp(gh)
        kb = k * jnp.exp(-gh)
        att = jnp.sum(qb * kb, axis=-1, keepdims=True)
        o = jnp.dot(qb.astype(BF16), s_old.astype(BF16), preferred_element_type=F32) + att * v
        o = o[0:1, :]
        ms = jnp.mean(o * o, axis=-1, keepdims=True)
        on = o * lax.rsqrt(ms + EPS) * gn_all[:, vs]
        og_ref[0, :, vs] = (on * _silu(r_all[:, vs])).astype(og_ref.dtype)


def _gla_sample(z, wg, bg, gn, state):
    n = z.shape[0]
    z3 = z.reshape(n, 1, Z_WIDTH)
    og, s_new = pl.pallas_call(
        _gla_sample_kernel,
        grid=(n,),
        in_specs=[
            pl.BlockSpec((1, 1, GLA_KW), lambda i: (i, 0, Z_Q // GLA_KW)),
            pl.BlockSpec((1, 1, GLA_KW), lambda i: (i, 0, Z_K // GLA_KW)),
            pl.BlockSpec((1, 1, GLA_VW), lambda i: (i, 0, Z_V // GLA_VW)),
            pl.BlockSpec((1, 1, GLA_VW), lambda i: (i, 0, Z_R // GLA_VW)),
            pl.BlockSpec((1, 1, LANE), lambda i: (i, 0, Z_GLR // LANE)),
            pl.BlockSpec((LANE, GLA_KW), lambda i: (0, 0)),
            pl.BlockSpec((1, GLA_KW), lambda i: (0, 0)),
            pl.BlockSpec((1, GLA_VW), lambda i: (0, 0)),
            pl.BlockSpec((1, GLA_HEADS, GLA_DK, GLA_DV), lambda i: (i, 0, 0, 0)),
        ],
        out_specs=[
            pl.BlockSpec((1, 1, GLA_VW), lambda i: (i, 0, 0)),
            pl.BlockSpec((1, GLA_HEADS, GLA_DK, GLA_DV), lambda i: (i, 0, 0, 0)),
        ],
        out_shape=[
            jax.ShapeDtypeStruct((n, 1, GLA_VW), BF16),
            jax.ShapeDtypeStruct((n, GLA_HEADS, GLA_DK, GLA_DV), F32),
        ],
        compiler_params=_cparams(1),
        name="gla_sample",
    )(z3, z3, z3, z3, z3, wg, bg, gn, state)
    return og.reshape(n, GLA_VW), s_new


def _pool_group_out(mix, g, wp_ref, ps_ref):
    cs = slice(g * POOL_GW, (g + 1) * POOL_GW)
    pm = jnp.dot(mix.astype(BF16), wp_ref[g].astype(BF16), preferred_element_type=F32)
    return pm * ps_ref[:, cs]


def _pool_prompt_kernel(u_ref, wp_ref, ps_ref, o_ref, ext_ref):
    i = pl.program_id(1)
    tt = u_ref.shape[0]
    hist = POOL_BUF + 1

    @pl.when(i == 0)
    def _():
        ext_ref[0:hist, :] = jnp.zeros((hist, POOL_W), F32)

    @pl.when(i > 0)
    def _():
        ext_ref[0:hist, :] = ext_ref[tt:tt + hist, :]

    ext_ref[hist:hist + tt, :] = u_ref[...].astype(F32)
    pos = i * tt + lax.broadcasted_iota(jnp.int32, (tt, 1), 0)
    for g, w in enumerate(POOL_WINDOWS):
        cs = slice(g * POOL_GW, (g + 1) * POOL_GW)
        u = ext_ref[hist:hist + tt, cs]
        s = u
        for d in range(1, w):
            s = s + ext_ref[hist - d:hist - d + tt, cs]
        cnt = jnp.minimum(pos + 1, w).astype(F32)
        mix = s / cnt - u
        o_ref[:, cs] = _pool_group_out(mix, g, wp_ref, ps_ref).astype(o_ref.dtype)


def _pool_prompt(z, w_pool, pool_scale, nb, t, tt):
    m = z.shape[0]
    nt = t // tt
    return pl.pallas_call(
        _pool_prompt_kernel,
        grid=(nb, nt),
        in_specs=[
            pl.BlockSpec((tt, POOL_W), lambda b, i: (b * nt + i, Z_U // POOL_W)),
            pl.BlockSpec((4, POOL_GW, POOL_GW), lambda b, i: (0, 0, 0)),
            pl.BlockSpec((1, POOL_W), lambda b, i: (0, 0)),
        ],
        out_specs=pl.BlockSpec((tt, POOL_W), lambda b, i: (b * nt + i, 0)),
        out_shape=jax.ShapeDtypeStruct((m, POOL_W), BF16),
        scratch_shapes=[pltpu.VMEM((tt + POOL_BUF + 1, POOL_W), F32)],
        compiler_params=_cparams(2),
        name="pool_prompt",
    )(z, w_pool, pool_scale)


def _pool_sample_kernel(u_ref, buf_ref, wp_ref, ps_ref, o_ref):
    for g, w in enumerate(POOL_WINDOWS):
        cs = slice(g * POOL_GW, (g + 1) * POOL_GW)
        u = u_ref[:, cs].astype(F32)
        s = u
        for d in range(1, w):
            s = s + buf_ref[POOL_BUF - d, :, cs]
        mix = s / float(w) - u
        o_ref[:, cs] = _pool_group_out(mix, g, wp_ref, ps_ref).astype(o_ref.dtype)


def _pool_sample(z, buf_t, w_pool, pool_scale):
    n = z.shape[0]
    return pl.pallas_call(
        _pool_sample_kernel,
        grid=(1,),
        in_specs=[
            pl.BlockSpec((n, POOL_W), lambda i: (0, Z_U // POOL_W)),
            pl.BlockSpec((POOL_BUF, n, POOL_W), lambda i: (0, 0, 0)),
            pl.BlockSpec((4, POOL_GW, POOL_GW), lambda i: (0, 0, 0)),
            pl.BlockSpec((1, POOL_W), lambda i: (0, 0)),
        ],
        out_specs=pl.BlockSpec((n, POOL_W), lambda i: (0, 0)),
        out_shape=jax.ShapeDtypeStruct((n, POOL_W), BF16),
        compiler_params=_cparams(1),
        name="pool_sample",
    )(z, buf_t, w_pool, pool_scale)


def _merge_kernel(og_ref, pm_ref, wa_ref, wb_ref, ga_ref, gb_ref, o_ref, wabf_ref, wbbf_ref):
    @pl.when(pl.program_id(1) == 0)
    def _():
        wabf_ref[...] = wa_ref[...].astype(BF16)
        wbbf_ref[...] = wb_ref[...].astype(BF16)

    ya = jnp.dot(og_ref[...], wabf_ref[...], preferred_element_type=F32)
    yb = jnp.dot(pm_ref[...], wbbf_ref[...], preferred_element_type=F32)
    ga = ga_ref[...].astype(F32)
    gb = gb_ref[...].astype(F32)
    o_ref[...] = (_sigmoid(ga) * ya + _sigmoid(gb) * yb).astype(o_ref.dtype)


def _merge(og, pm, wa, wb, z, *, tm, tn):
    m = og.shape[0]
    return pl.pallas_call(
        _merge_kernel,
        grid=(D_MODEL // tn, m // tm),
        in_specs=[
            pl.BlockSpec((tm, GLA_VW), lambda j, i: (i, 0)),
            pl.BlockSpec((tm, POOL_W), lambda j, i: (i, 0)),
            pl.BlockSpec((GLA_VW, tn), lambda j, i: (0, j)),
            pl.BlockSpec((POOL_W, tn), lambda j, i: (0, j)),
            pl.BlockSpec((tm, tn), lambda j, i: (i, Z_GA // tn + j)),
            pl.BlockSpec((tm, tn), lambda j, i: (i, Z_GB // tn + j)),
        ],
        out_specs=pl.BlockSpec((tm, tn), lambda j, i: (i, j)),
        out_shape=jax.ShapeDtypeStruct((m, D_MODEL), BF16),
        scratch_shapes=[pltpu.VMEM((GLA_VW, tn), BF16), pltpu.VMEM((POOL_W, tn), BF16)],
        compiler_params=_cparams(2),
        name="merge",
    )(og, pm, wa, wb, z, z)


UP_TN = 512
UP_SUB = 256
UP_PARTS = UP_TN // LANE
SUBLANES = 8


def _up_kernel(*refs, n_prompt_tiles, tiles_per_seq):
    cp_ref, cs_ref, wg_ref = refs[:3]
    wu_refs = refs[3:3 + UP_PARTS]
    (cw_ref, cb_ref, h0_ref, h1_ref, actp_ref, acts_ref, tailp_ref, gates_ref,
     wgbf_ref, wubf_ref, carry_ref) = refs[3 + UP_PARTS:]
    i = pl.program_id(1)

    @pl.when(i == 0)
    def _():
        wgbf_ref[...] = wg_ref[...].astype(BF16)
        for part, wu_ref in enumerate(wu_refs):
            wubf_ref[:, part * LANE:(part + 1) * LANE] = wu_ref[...].astype(BF16)

    def conv_act(gate, up, g1, g2, cols):
        cw = cw_ref[:, cols]
        gconv = cb_ref[:, cols] + cw[0:1] * g2 + cw[1:2] * g1 + cw[2:3] * gate
        return _silu(gconv) * up

    @pl.when(i < n_prompt_tiles)
    def _():
        @pl.when(i % tiles_per_seq == 0)
        def _():
            carry_ref[...] = jnp.zeros_like(carry_ref)

        c = cp_ref[...]
        tm = c.shape[0]
        sub = lax.broadcasted_iota(jnp.int32, (SUBLANES, UP_SUB), 0)
        for h in range(UP_TN // UP_SUB):
            cols = slice(h * UP_SUB, (h + 1) * UP_SUB)
            gate = jnp.dot(c, wgbf_ref[:, cols], preferred_element_type=F32)
            up = jnp.dot(c, wubf_ref[:, cols], preferred_element_type=F32)
            prev = carry_ref[:, cols]
            r1 = pltpu.roll(gate, 1, 0)
            r2 = pltpu.roll(gate, 2, 0)
            head1 = jnp.where(sub < 1, pltpu.roll(prev, 1, 0), r1[0:SUBLANES])
            head2 = jnp.where(sub < 2, pltpu.roll(prev, 2, 0), r2[0:SUBLANES])
            g1 = jnp.concatenate([head1, r1[SUBLANES:]], axis=0)
            g2 = jnp.concatenate([head2, r2[SUBLANES:]], axis=0)
            carry_ref[:, cols] = gate[tm - SUBLANES:tm]
            tailp_ref[0, :, cols] = gate[tm - (CONV_W - 1):tm]
            actp_ref[:, cols] = conv_act(gate, up, g1, g2, cols).astype(actp_ref.dtype)

    @pl.when(i == n_prompt_tiles)
    def _():
        c = cs_ref[...]
        gate = jnp.dot(c, wgbf_ref[...], preferred_element_type=F32)
        up = jnp.dot(c, wubf_ref[...], preferred_element_type=F32)
        gates_ref[...] = gate
        acts_ref[...] = conv_act(gate, up, h1_ref[...], h0_ref[...], slice(None)).astype(acts_ref.dtype)


def _up_proj(c_p, c_s, w_up, conv_w, conv_b, hist0, hist1, *, tm, seq_len):
    mp, k = c_p.shape
    ms = c_s.shape[0]
    npt = mp // tm
    tiles_per_seq = seq_len // tm
    nseq = mp // seq_len
    tn = UP_TN
    nj = pl.cdiv(D_FF, tn)
    up_block0 = D_FF // LANE
    last_block = w_up.shape[1] // LANE - 1

    def row_p(j, i):
        return jnp.minimum(i, npt - 1)

    def wu_spec(part):
        return pl.BlockSpec(
            (k, LANE), lambda j, i: (0, jnp.minimum(up_block0 + j * UP_PARTS + part, last_block)))

    in_specs = [
        pl.BlockSpec((tm, k), lambda j, i: (row_p(j, i), 0)),
        pl.BlockSpec((ms, k), lambda j, i: (0, 0)),
        pl.BlockSpec((k, tn), lambda j, i: (0, j)),
        *[wu_spec(part) for part in range(UP_PARTS)],
        pl.BlockSpec((CONV_W, tn), lambda j, i: (0, j)),
        pl.BlockSpec((1, tn), lambda j, i: (0, j)),
        pl.BlockSpec((ms, tn), lambda j, i: (0, j)),
        pl.BlockSpec((ms, tn), lambda j, i: (0, j)),
    ]
    out_specs = [
        pl.BlockSpec((tm, tn), lambda j, i: (row_p(j, i), j)),
        pl.BlockSpec((ms, tn), lambda j, i: (0, j)),
        pl.BlockSpec((1, CONV_W - 1, tn), lambda j, i: (row_p(j, i) // tiles_per_seq, 0, j)),
        pl.BlockSpec((ms, tn), lambda j, i: (0, j)),
    ]
    out_shape = [
        jax.ShapeDtypeStruct((mp, D_FF), BF16),
        jax.ShapeDtypeStruct((ms, D_FF), BF16),
        jax.ShapeDtypeStruct((nseq, CONV_W - 1, D_FF), F32),
        jax.ShapeDtypeStruct((ms, D_FF), F32),
    ]
    return pl.pallas_call(
        functools.partial(_up_kernel, n_prompt_tiles=npt, tiles_per_seq=tiles_per_seq),
        grid=(nj, npt + 1),
        in_specs=in_specs,
        out_specs=out_specs,
        out_shape=out_shape,
        scratch_shapes=[pltpu.VMEM((k, tn), BF16), pltpu.VMEM((k, tn), BF16), pltpu.VMEM((SUBLANES, tn), F32)],
        compiler_params=_cparams(2),
        name="up_proj",
    )(c_p, c_s, w_up, *([w_up] * UP_PARTS), conv_w, conv_b, hist0, hist1)


def _ple_kernel(cn_ref, p_ref, wpg_ref, wple_ref, h_ref, o_ref, wpgbf_ref, wplebf_ref):
    @pl.when(pl.program_id(1) == 0)
    def _():
        wpgbf_ref[...] = wpg_ref[...].astype(BF16)
        wplebf_ref[...] = wple_ref[...].astype(BF16)

    pg = _sigmoid(jnp.dot(cn_ref[...], wpgbf_ref[...], preferred_element_type=F32))
    e = jnp.dot(p_ref[...].astype(BF16), wplebf_ref[...], preferred_element_type=F32)
    o_ref[...] = h_ref[...] + pg * e


def _ple(cn, p, wpg, wple, h, *, tm, tn):
    m = cn.shape[0]
    return pl.pallas_call(
        _ple_kernel,
        grid=(D_MODEL // tn, m // tm),
        in_specs=[
            pl.BlockSpec((tm, D_MODEL), lambda j, i: (i, 0)),
            pl.BlockSpec((tm, PLE_DIM), lambda j, i: (i, 0)),
            pl.BlockSpec((D_MODEL, tn), lambda j, i: (0, j)),
            pl.BlockSpec((PLE_DIM, tn), lambda j, i: (0, j)),
            pl.BlockSpec((tm, tn), lambda j, i: (i, j)),
        ],
        out_specs=pl.BlockSpec((tm, tn), lambda j, i: (i, j)),
        out_shape=jax.ShapeDtypeStruct((m, D_MODEL), F32),
        scratch_shapes=[pltpu.VMEM((D_MODEL, tn), BF16), pltpu.VMEM((PLE_DIM, tn), BF16)],
        compiler_params=_cparams(2),
        name="ple",
    )(cn, p, wpg, wple, h)


def _mixers(x, z, lw, *, tm, tr, prompt_shape=None, state=None):
    sample = state is not None
    if sample:
        s_gla, s_pool = state
        og, gla_new = _gla_sample(z, lw["wg"], lw["bg"], lw["gn"], s_gla)
        pm = _pool_sample(z, jnp.transpose(s_pool, (1, 0, 2)), lw["w_pool"], lw["pool_scale"])
        u = z[:, Z_U:Z_U + POOL_W].astype(F32)
        pool_new = jnp.concatenate([s_pool[:, 1:], u[:, None, :]], axis=1)
    else:
        nb, t = prompt_shape
        og, gla_new = _gla_prompt(z, lw["wg"], lw["bg"], lw["gn"], nb, t)
        pm = _pool_prompt(z, lw["w_pool"], lw["pool_scale"], nb, t, 512)
        pool_new = z.reshape(nb, t, Z_WIDTH)[:, t - POOL_BUF:, Z_U:Z_U + POOL_W].astype(F32)
    mg = _merge(og, pm, lw["w_branch_a"], lw["w_branch_b"], z, tm=tm, tn=512)
    h1 = _matmul(mg, lw["w_out"], tm=tm, tn=512, out_dtype=F32, res=x, name="out_proj")
    c = _rmsnorm(h1, lw["norm_ffn"], BF16, tr)
    return h1, c, gla_new, pool_new


def _ffn_tail(h1, act, p, lw, *, tm, tr):
    h2 = _matmul(act, lw["w_down"], tm=min(tm, 512), tn=512, out_dtype=F32, res=h1, name="down_proj")
    cn = _rmsnorm(h2, lw["norm_ple"], BF16, tr)
    return _ple(cn, p, lw["w_ple_gate"], lw["w_ple"], h2, tm=tm, tn=512)


def kernel(x_prompt, x_sample, state_gla, state_pool, state_conv, p_prompt, p_sample, norm_mix, w_in, w_gate_up, b_gate, gla_norm, w_branch_a, w_pool, pool_scale, w_branch_b, w_out, norm_ffn, w_up, conv_w, conv_b, w_down, norm_ple, w_ple_gate, w_ple, norm_final):
    depth = w_in.shape[0]
    nb, t, d = x_prompt.shape
    ns = x_sample.shape[0]
    hp = x_prompt.reshape(nb * t, d)
    hs = x_sample.reshape(ns, d)
    outs = {k: [] for k in ("gp", "pp", "cp", "gs", "ps", "cs")}
    for li in range(depth):
        lw = {
            "wg": jnp.pad(w_gate_up[li], ((0, LANE - GATE_RANK), (0, 0))).astype(BF16),
            "bg": b_gate[li].reshape(1, GLA_KW),
            "gn": gla_norm[li].reshape(1, GLA_VW),
            "w_branch_a": w_branch_a[li],
            "w_pool": w_pool[li],
            "pool_scale": pool_scale[li].reshape(1, POOL_W),
            "w_branch_b": w_branch_b[li],
            "w_out": w_out[li],
            "norm_ffn": norm_ffn[li],
            "w_down": w_down[li],
            "norm_ple": norm_ple[li],
            "w_ple_gate": w_ple_gate[li],
            "w_ple": w_ple[li],
        }
        a_p = _rmsnorm(hp, norm_mix[li], BF16, 512)
        a_s = _rmsnorm(hs, norm_mix[li], BF16, ns)
        z_p, z_s = _in_proj(a_p, a_s, w_in[li], tm=1024)
        h1_p, c_p, g1, p1 = _mixers(hp, z_p, lw, tm=1024, tr=512, prompt_shape=(nb, t))
        h1_s, c_s, g2, p2 = _mixers(hs, z_s, lw, tm=ns, tr=ns, state=(state_gla[li], state_pool[li]))
        s_conv = state_conv[li]
        act_p, act_s, c1, gate_s = _up_proj(c_p, c_s, w_up[li], conv_w[li], conv_b[li].reshape(1, D_FF),
                                            s_conv[:, 0], s_conv[:, 1], tm=1024, seq_len=t)
        c2 = jnp.stack([s_conv[:, 1], gate_s], axis=1)
        hp = _ffn_tail(h1_p, act_p, p_prompt[li].reshape(nb * t, PLE_DIM), lw, tm=1024, tr=512)
        hs = _ffn_tail(h1_s, act_s, p_sample[li].reshape(ns, PLE_DIM), lw, tm=ns, tr=ns)
        for key, val in zip(("gp", "pp", "cp", "gs", "ps", "cs"), (g1, p1, c1, g2, p2, c2)):
            outs[key].append(val)
    y_prompt = _rmsnorm(hp, norm_final, F32, 512).reshape(nb, t, d)
    y_sample = _rmsnorm(hs, norm_final, F32, ns).reshape(ns, 1, d)
    st = {k: jnp.stack(v, axis=0) for k, v in outs.items()}
    return (y_prompt, y_sample, st["gp"], st["pp"], st["cp"], st["gs"], st["ps"], st["cs"])
```

```python
import functools

import jax
import jax.numpy as jnp
from jax import lax
from jax.experimental import pallas as pl
from jax.experimental.pallas import tpu as pltpu

F32 = jnp.float32
BF16 = jnp.bfloat16

D_MODEL = 2048
GLA_HEADS = 4
GLA_DK = 256
GLA_DV = 512
GLA_KW = GLA_HEADS * GLA_DK
GLA_VW = GLA_HEADS * GLA_DV
GATE_RANK = 16
GATE_TEMP = 16.0
GLA_CHUNK = 64
POOL_WINDOWS = (2, 4, 8, 16)
POOL_GW = 256
POOL_W = 1024
POOL_BUF = 15
D_FF = 5504
CONV_W = 3
PLE_DIM = 256
EPS = 1e-6

LANE = 128
SUBLANES = 8
Z_Q, Z_K, Z_V, Z_R, Z_U, Z_GA, Z_GB, Z_GLR, Z_WIDTH = 0, 1024, 2048, 4096, 6144, 7168, 9216, 11264, 11392
GLR_SRC = 6144
IN_TN = 1024

VMEM_LIMIT = 56 * 1024 * 1024


def _cparams(n_axes):
    return pltpu.CompilerParams(dimension_semantics=("arbitrary",) * n_axes, vmem_limit_bytes=VMEM_LIMIT)


def _sigmoid(x):
    return 1.0 / (1.0 + jnp.exp(-x))


def _silu(x):
    return x * _sigmoid(x)


def _log_sigmoid(x):
    return jnp.minimum(x, 0.0) - jnp.log1p(jnp.exp(-jnp.abs(x)))


def _rms_kernel(x_ref, g_ref, o_ref):
    x = x_ref[...]
    ms = jnp.mean(x * x, axis=-1, keepdims=True)
    o_ref[...] = (x * lax.rsqrt(ms + EPS) * g_ref[...]).astype(o_ref.dtype)


def _rmsnorm(x, g, out_dtype, tm):
    m, d = x.shape
    return pl.pallas_call(
        _rms_kernel,
        grid=(m // tm,),
        in_specs=[pl.BlockSpec((tm, d), lambda i: (i, 0)), pl.BlockSpec((1, d), lambda i: (0, 0))],
        out_specs=pl.BlockSpec((tm, d), lambda i: (i, 0)),
        out_shape=jax.ShapeDtypeStruct((m, d), out_dtype),
        compiler_params=_cparams(1),
        name="rmsnorm",
    )(x, g.reshape(1, d))


def _mm_kernel(*refs, has_res, cast_w):
    a_ref, w_ref = refs[0], refs[1]
    pos = 2
    res_ref = None
    if has_res:
        res_ref = refs[pos]
        pos += 1
    o_ref = refs[pos]
    pos += 1
    if cast_w:
        wbf_ref = refs[pos]

        @pl.when(pl.program_id(1) == 0)
        def _():
            wbf_ref[...] = w_ref[...].astype(BF16)

        w = wbf_ref[...]
    else:
        w = w_ref[...]
    acc = jnp.dot(a_ref[...], w, preferred_element_type=F32)
    if has_res:
        acc = acc + res_ref[...]
    o_ref[...] = acc.astype(o_ref.dtype)


def _matmul(a, w, *, tm, tn, out_dtype, res=None, name):
    m, k = a.shape
    n = w.shape[1]
    cast_w = w.dtype != BF16
    in_specs = [pl.BlockSpec((tm, k), lambda j, i: (i, 0)), pl.BlockSpec((k, tn), lambda j, i: (0, j))]
    args = [a, w]
    if res is not None:
        in_specs.append(pl.BlockSpec((tm, tn), lambda j, i: (i, j)))
        args.append(res)
    return pl.pallas_call(
        functools.partial(_mm_kernel, has_res=res is not None, cast_w=cast_w),
        grid=(pl.cdiv(n, tn), m // tm),
        in_specs=in_specs,
        out_specs=pl.BlockSpec((tm, tn), lambda j, i: (i, j)),
        out_shape=jax.ShapeDtypeStruct((m, n), out_dtype),
        scratch_shapes=[pltpu.VMEM((k, tn), BF16)] if cast_w else [],
        compiler_params=_cparams(2),
        name=name,
    )(*args)


_NT = (((1,), (1,)), ((), ()))
_TN = (((0,), (0,)), ((), ()))


def _in_proj_kernel(ap_ref, as_ref, w_ref, wx_ref, zp_ref, zs_ref, wbf_ref):
    j = pl.program_id(0)
    i = pl.program_id(1)
    last = pl.num_programs(0) - 1
    first_shifted = GLR_SRC // IN_TN
    keep = IN_TN - GATE_RANK

    @pl.when(i == 0)
    def _():
        @pl.when(j < first_shifted)
        def _():
            wbf_ref[...] = w_ref[...].astype(BF16)

        @pl.when(jnp.logical_and(j >= first_shifted, j < last))
        def _():
            wbf_ref[:keep, :] = w_ref[GATE_RANK:, :].astype(BF16)
            wbf_ref[keep:, :] = wx_ref[...].astype(BF16)

        @pl.when(j == last)
        def _():
            wbf_ref[:GATE_RANK, :] = wx_ref[...].astype(BF16)
            wbf_ref[GATE_RANK:LANE, :] = jnp.zeros((LANE - GATE_RANK, wbf_ref.shape[1]), BF16)

    def emit(a_ref, z_ref):
        @pl.when(j < last)
        def _():
            z_ref[...] = lax.dot_general(a_ref[...], wbf_ref[...], _NT,
                                         preferred_element_type=F32).astype(z_ref.dtype)

        @pl.when(j == last)
        def _():
            z_ref[:, :LANE] = lax.dot_general(a_ref[...], wbf_ref[:LANE, :], _NT,
                                              preferred_element_type=F32).astype(z_ref.dtype)

    @pl.when(i == 0)
    def _():
        emit(as_ref, zs_ref)

    @pl.when(i > 0)
    def _():
        emit(ap_ref, zp_ref)


def _in_proj(a_p, a_s, wt, *, tm):
    mp, k = a_p.shape
    ms = a_s.shape[0]
    npt = mp // tm
    nj = Z_GLR // IN_TN + 1
    aux_per_tile = IN_TN // GATE_RANK

    def row_p(j, i):
        return jnp.maximum(i - 1, 0)

    def aux_idx(j, i):
        return (jnp.where(j == nj - 1, GLR_SRC // GATE_RANK, (j + 1) * aux_per_tile), 0)

    return pl.pallas_call(
        _in_proj_kernel,
        grid=(nj, npt + 1),
        in_specs=[
            pl.BlockSpec((tm, k), lambda j, i: (row_p(j, i), 0)),
            pl.BlockSpec((ms, k), lambda j, i: (0, 0)),
            pl.BlockSpec((IN_TN, k), lambda j, i: (j, 0)),
            pl.BlockSpec((GATE_RANK, k), aux_idx),
        ],
        out_specs=[
            pl.BlockSpec((tm, IN_TN), lambda j, i: (row_p(j, i), j)),
            pl.BlockSpec((ms, IN_TN), lambda j, i: (0, j)),
        ],
        out_shape=[jax.ShapeDtypeStruct((mp, Z_WIDTH), BF16), jax.ShapeDtypeStruct((ms, Z_WIDTH), BF16)],
        scratch_shapes=[pltpu.VMEM((IN_TN, k), BF16)],
        compiler_params=_cparams(2),
        name="in_proj",
    )(a_p, a_s, wt, wt)


GLA_GROUP = 8


def _gla_prompt_kernel(q_ref, k_ref, v_ref, r_ref, glr_ref, wg_ref, bg_ref, gn_ref,
                       og_ref, s_out_ref, st_ref):
    t = q_ref.shape[0]
    c = GLA_CHUNK
    rows = GLA_GROUP * c
    st_ref[...] = jnp.zeros_like(st_ref)
    row = lax.broadcasted_iota(jnp.int32, (c, c), 0)
    col = lax.broadcasted_iota(jnp.int32, (c, c), 1)
    causal = row >= col
    tri = causal.astype(BF16)
    nt = (((1,), (1,)), ((), ()))
    tn = (((0,), (0,)), ((), ()))

    def group(gi, carry):
        r0 = pl.multiple_of(gi * rows, rows)
        rs = pl.ds(r0, rows)
        zg = jnp.dot(glr_ref[rs, :], wg_ref[...], preferred_element_type=F32) + bg_ref[...]
        gl = _log_sigmoid(zg) * (1.0 / GATE_TEMP)
        gl_hi = gl.astype(BF16)
        gl_lo = (gl - gl_hi.astype(F32)).astype(BF16)
        q = q_ref[rs, :].astype(F32) * (GLA_DK ** -0.5)
        k = k_ref[rs, :].astype(F32)
        v = v_ref[rs, :]
        st = st_ref[...]
        outs = []
        for ci in range(GLA_GROUP):
            sl = slice(ci * c, (ci + 1) * c)
            b = (jnp.dot(tri, gl_hi[sl], preferred_element_type=F32)
                 + jnp.dot(tri, gl_lo[sl], preferred_element_type=F32))
            bl = b[c - 1:c, :]
            qb = (q[sl] * jnp.exp(b)).astype(BF16)
            kb = (k[sl] * jnp.exp(-b)).astype(BF16)
            kd = (k[sl] * jnp.exp(bl - b)).astype(BF16)
            att = lax.dot_general(qb, kb, nt, preferred_element_type=F32)
            att = jnp.where(causal, att, 0.0).astype(BF16)
            vc = v[sl]
            o = lax.dot_general(qb, st.astype(BF16), nt, preferred_element_type=F32)
            outs.append(o + jnp.dot(att, vc, preferred_element_type=F32))
            st = st * jnp.exp(bl) + lax.dot_general(vc, kd, tn, preferred_element_type=F32)
        st_ref[...] = st
        o = jnp.concatenate(outs, axis=0)
        ms = jnp.mean(o * o, axis=-1, keepdims=True)
        on = o * lax.rsqrt(ms + EPS) * gn_ref[...]
        og_ref[rs, :] = (on * _silu(r_ref[rs, :].astype(F32))).astype(og_ref.dtype)
        return carry

    lax.fori_loop(0, t // rows, group, 0)
    s_out_ref[0, 0] = st_ref[...].T


def _gla_prompt(z, wg, bg, gn, nb, t):
    m = z.shape[0]
    hh = GLA_HEADS
    og, s_new = pl.pallas_call(
        _gla_prompt_kernel,
        grid=(nb, hh),
        in_specs=[
            pl.BlockSpec((t, GLA_DK), lambda b, h: (b, Z_Q // GLA_DK + h)),
            pl.BlockSpec((t, GLA_DK), lambda b, h: (b, Z_K // GLA_DK + h)),
            pl.BlockSpec((t, GLA_DV), lambda b, h: (b, Z_V // GLA_DV + h)),
            pl.BlockSpec((t, GLA_DV), lambda b, h: (b, Z_R // GLA_DV + h)),
            pl.BlockSpec((t, LANE), lambda b, h: (b, Z_GLR // LANE)),
            pl.BlockSpec((LANE, GLA_DK), lambda b, h: (0, h)),
            pl.BlockSpec((1, GLA_DK), lambda b, h: (0, h)),
            pl.BlockSpec((1, GLA_DV), lambda b, h: (0, h)),
        ],
        out_specs=[
            pl.BlockSpec((t, GLA_DV), lambda b, h: (b, h)),
            pl.BlockSpec((1, 1, GLA_DK, GLA_DV), lambda b, h: (b, h, 0, 0)),
        ],
        out_shape=[
            jax.ShapeDtypeStruct((m, GLA_VW), BF16),
            jax.ShapeDtypeStruct((nb, hh, GLA_DK, GLA_DV), F32),
        ],
        scratch_shapes=[pltpu.VMEM((GLA_DV, GLA_DK), F32)],
        compiler_params=_cparams(2),
        name="gla_prompt",
    )(z, z, z, z, z, wg, bg, gn)
    return og, s_new


def _gla_sample_kernel(q_ref, k_ref, v_ref, r_ref, glr_ref, wg_ref, bg_ref, gn_ref, s_ref,
                       og_ref, s_out_ref):
    n = pl.program_id(0)
    row = pl.ds(n, 1)

    def rows8(ref):
        return jnp.broadcast_to(ref[row, :], (SUBLANES, ref.shape[1]))

    zg = jnp.dot(rows8(glr_ref).astype(BF16), wg_ref[...], preferred_element_type=F32) + bg_ref[...]
    g = _log_sigmoid(zg) * (1.0 / GATE_TEMP)
    eg = jnp.exp(g)
    eng = jnp.exp(-g)
    eg1 = eg.astype(BF16)
    rem = eg - eg1.astype(F32)
    eg2 = rem.astype(BF16)
    eg3 = (rem - eg2.astype(F32)).astype(BF16)
    first_row = lax.broadcasted_iota(jnp.int32, (SUBLANES, LANE), 0) == 0
    ones_row = jnp.where(first_row, 1.0, 0.0).astype(BF16)
    row0 = lax.broadcasted_iota(jnp.int32, (SUBLANES, GLA_DV), 0) == 0
    q_all = rows8(q_ref) * (GLA_DK ** -0.5)
    k_all = rows8(k_ref)
    v_all = rows8(v_ref)
    r_all = r_ref[row, :]
    gn_all = gn_ref[...]
    for h in range(GLA_HEADS):
        ks = slice(h * GLA_DK, (h + 1) * GLA_DK)
        vs = slice(h * GLA_DV, (h + 1) * GLA_DV)
        q, k, v = q_all[:, ks], k_all[:, ks], v_all[:, vs]
        s_old = s_ref[0, h]
        decay = (lax.dot_general(eg1[:, ks], ones_row, _TN, preferred_element_type=F32)
                 + lax.dot_general(eg2[:, ks], ones_row, _TN, preferred_element_type=F32)
                 + lax.dot_general(eg3[:, ks], ones_row, _TN, preferred_element_type=F32))
        v0 = jnp.where(row0, v, 0.0)
        ds = lax.dot_general(k.astype(BF16), v0.astype(BF16), _TN, preferred_element_type=F32)
        s_out_ref[0, h] = jnp.tile(decay, (1, GLA_DV // LANE)) * s_old + ds
        qb = q * eg[:, ks]
        kb = k * eng[:, ks]
        att = jnp.sum(qb * kb, axis=-1, keepdims=True)
        o = jnp.dot(qb.astype(BF16), s_old.astype(BF16), preferred_element_type=F32) + att * v
        o = o[0:1, :]
        ms = jnp.mean(o * o, axis=-1, keepdims=True)
        on = o * lax.rsqrt(ms + EPS) * gn_all[:, vs]
        og_ref[row, vs] = on * _silu(r_all[:, vs])


def _gla_sample(z, wg, bg, gn, state):
    n = z.shape[0]
    zf = z.astype(F32)
    og, s_new = pl.pallas_call(
        _gla_sample_kernel,
        grid=(n,),
        in_specs=[
            pl.BlockSpec((n, GLA_KW), lambda i: (0, Z_Q // GLA_KW)),
            pl.BlockSpec((n, GLA_KW), lambda i: (0, Z_K // GLA_KW)),
            pl.BlockSpec((n, GLA_VW), lambda i: (0, Z_V // GLA_VW)),
            pl.BlockSpec((n, GLA_VW), lambda i: (0, Z_R // GLA_VW)),
            pl.BlockSpec((n, LANE), lambda i: (0, Z_GLR // LANE)),
            pl.BlockSpec((LANE, GLA_KW), lambda i: (0, 0)),
            pl.BlockSpec((1, GLA_KW), lambda i: (0, 0)),
            pl.BlockSpec((1, GLA_VW), lambda i: (0, 0)),
            pl.BlockSpec((1, GLA_HEADS, GLA_DK, GLA_DV), lambda i: (i, 0, 0, 0)),
        ],
        out_specs=[
            pl.BlockSpec((n, GLA_VW), lambda i: (0, 0)),
            pl.BlockSpec((1, GLA_HEADS, GLA_DK, GLA_DV), lambda i: (i, 0, 0, 0)),
        ],
        out_shape=[
            jax.ShapeDtypeStruct((n, GLA_VW), F32),
            jax.ShapeDtypeStruct((n, GLA_HEADS, GLA_DK, GLA_DV), F32),
        ],
        compiler_params=_cparams(1),
        name="gla_sample",
    )(zf, zf, zf, zf, zf, wg, bg, gn, state)
    return og.astype(BF16), s_new


def _pool_group_out(mix, g, wp_ref, ps_ref):
    cs = slice(g * POOL_GW, (g + 1) * POOL_GW)
    pm = jnp.dot(mix.astype(BF16), wp_ref[g].astype(BF16), preferred_element_type=F32)
    return pm * ps_ref[:, cs]


def _pool_prompt_kernel(u_ref, wp_ref, ps_ref, o_ref, ext_ref):
    i = pl.program_id(1)
    tt = u_ref.shape[0]
    hist = POOL_BUF + 1

    @pl.when(i == 0)
    def _():
        ext_ref[0:hist, :] = jnp.zeros((hist, POOL_W), F32)

    @pl.when(i > 0)
    def _():
        ext_ref[0:hist, :] = ext_ref[tt:tt + hist, :]

    ext_ref[hist:hist + tt, :] = u_ref[...].astype(F32)
    pos = i * tt + lax.broadcasted_iota(jnp.int32, (tt, 1), 0)
    for g, w in enumerate(POOL_WINDOWS):
        cs = slice(g * POOL_GW, (g + 1) * POOL_GW)
        u = ext_ref[hist:hist + tt, cs]
        s = u
        for d in range(1, w):
            s = s + ext_ref[hist - d:hist - d + tt, cs]
        cnt = jnp.minimum(pos + 1, w).astype(F32)
        mix = s / cnt - u
        o_ref[:, cs] = _pool_group_out(mix, g, wp_ref, ps_ref).astype(o_ref.dtype)


def _pool_prompt(z, w_pool, pool_scale, nb, t, tt):
    m = z.shape[0]
    nt = t // tt
    return pl.pallas_call(
        _pool_prompt_kernel,
        grid=(nb, nt),
        in_specs=[
            pl.BlockSpec((tt, POOL_W), lambda b, i: (b * nt + i, Z_U // POOL_W)),
            pl.BlockSpec((4, POOL_GW, POOL_GW), lambda b, i: (0, 0, 0)),
            pl.BlockSpec((1, POOL_W), lambda b, i: (0, 0)),
        ],
        out_specs=pl.BlockSpec((tt, POOL_W), lambda b, i: (b * nt + i, 0)),
        out_shape=jax.ShapeDtypeStruct((m, POOL_W), BF16),
        scratch_shapes=[pltpu.VMEM((tt + POOL_BUF + 1, POOL_W), F32)],
        compiler_params=_cparams(2),
        name="pool_prompt",
    )(z, w_pool, pool_scale)


def _pool_sample_kernel(u_ref, buf_ref, wp_ref, ps_ref, o_ref):
    for g, w in enumerate(POOL_WINDOWS):
        cs = slice(g * POOL_GW, (g + 1) * POOL_GW)
        u = u_ref[:, cs].astype(F32)
        s = u
        for d in range(1, w):
            s = s + buf_ref[POOL_BUF - d, :, cs]
        mix = s / float(w) - u
        o_ref[:, cs] = _pool_group_out(mix, g, wp_ref, ps_ref).astype(o_ref.dtype)


def _pool_sample(z, buf_t, w_pool, pool_scale):
    n = z.shape[0]
    return pl.pallas_call(
        _pool_sample_kernel,
        grid=(1,),
        in_specs=[
            pl.BlockSpec((n, POOL_W), lambda i: (0, Z_U // POOL_W)),
            pl.BlockSpec((POOL_BUF, n, POOL_W), lambda i: (0, 0, 0)),
            pl.BlockSpec((4, POOL_GW, POOL_GW), lambda i: (0, 0, 0)),
            pl.BlockSpec((1, POOL_W), lambda i: (0, 0)),
        ],
        out_specs=pl.BlockSpec((n, POOL_W), lambda i: (0, 0)),
        out_shape=jax.ShapeDtypeStruct((n, POOL_W), BF16),
        compiler_params=_cparams(1),
        name="pool_sample",
    )(z, buf_t, w_pool, pool_scale)


def _merge_kernel(og_ref, pm_ref, wa_ref, wb_ref, ga_ref, gb_ref, o_ref, wabf_ref, wbbf_ref):
    @pl.when(pl.program_id(1) == 0)
    def _():
        wabf_ref[...] = wa_ref[...].astype(BF16)
        wbbf_ref[...] = wb_ref[...].astype(BF16)

    ya = jnp.dot(og_ref[...], wabf_ref[...], preferred_element_type=F32)
    yb = jnp.dot(pm_ref[...], wbbf_ref[...], preferred_element_type=F32)
    ga = ga_ref[...].astype(F32)
    gb = gb_ref[...].astype(F32)
    o_ref[...] = (_sigmoid(ga) * ya + _sigmoid(gb) * yb).astype(o_ref.dtype)


def _merge(og, pm, wa, wb, z, *, tm, tn):
    m = og.shape[0]
    return pl.pallas_call(
        _merge_kernel,
        grid=(D_MODEL // tn, m // tm),
        in_specs=[
            pl.BlockSpec((tm, GLA_VW), lambda j, i: (i, 0)),
            pl.BlockSpec((tm, POOL_W), lambda j, i: (i, 0)),
            pl.BlockSpec((GLA_VW, tn), lambda j, i: (0, j)),
            pl.BlockSpec((POOL_W, tn), lambda j, i: (0, j)),
            pl.BlockSpec((tm, tn), lambda j, i: (i, Z_GA // tn + j)),
            pl.BlockSpec((tm, tn), lambda j, i: (i, Z_GB // tn + j)),
        ],
        out_specs=pl.BlockSpec((tm, tn), lambda j, i: (i, j)),
        out_shape=jax.ShapeDtypeStruct((m, D_MODEL), BF16),
        scratch_shapes=[pltpu.VMEM((GLA_VW, tn), BF16), pltpu.VMEM((POOL_W, tn), BF16)],
        compiler_params=_cparams(2),
        name="merge",
    )(og, pm, wa, wb, z, z)


UP_TN = 512
UP_SUB = 256
UP_PARTS = UP_TN // LANE


def _up_kernel(*refs, tiles_per_seq):
    cp_ref, cs_ref, wg_ref = refs[:3]
    wu_refs = refs[3:3 + UP_PARTS]
    (cw_ref, cb_ref, h0_ref, h1_ref, actp_ref, acts_ref, tailp_ref, gates_ref,
     wgbf_ref, wubf_ref, carry_ref) = refs[3 + UP_PARTS:]
    i = pl.program_id(1)

    @pl.when(i == 0)
    def _():
        wgbf_ref[...] = wg_ref[...].astype(BF16)
        for part, wu_ref in enumerate(wu_refs):
            wubf_ref[:, part * LANE:(part + 1) * LANE] = wu_ref[...].astype(BF16)

    def conv_act(gate, up, g1, g2, cols):
        cw = cw_ref[:, cols]
        gconv = cb_ref[:, cols] + cw[0:1] * g2 + cw[1:2] * g1 + cw[2:3] * gate
        return _silu(gconv) * up

    @pl.when(i > 0)
    def _():
        @pl.when((i - 1) % tiles_per_seq == 0)
        def _():
            carry_ref[...] = jnp.zeros_like(carry_ref)

        c = cp_ref[...]
        tm = c.shape[0]
        sub = lax.broadcasted_iota(jnp.int32, (SUBLANES, UP_SUB), 0)
        for h in range(UP_TN // UP_SUB):
            cols = slice(h * UP_SUB, (h + 1) * UP_SUB)
            gate = jnp.dot(c, wgbf_ref[:, cols], preferred_element_type=F32)
            up = jnp.dot(c, wubf_ref[:, cols], preferred_element_type=F32)
            prev = carry_ref[:, cols]
            r1 = pltpu.roll(gate, 1, 0)
            r2 = pltpu.roll(gate, 2, 0)
            head1 = jnp.where(sub < 1, pltpu.roll(prev, 1, 0), r1[0:SUBLANES])
            head2 = jnp.where(sub < 2, pltpu.roll(prev, 2, 0), r2[0:SUBLANES])
            g1 = jnp.concatenate([head1, r1[SUBLANES:]], axis=0)
            g2 = jnp.concatenate([head2, r2[SUBLANES:]], axis=0)
            carry_ref[:, cols] = gate[tm - SUBLANES:tm]
            tailp_ref[0, :, cols] = gate[tm - (CONV_W - 1):tm]
            actp_ref[:, cols] = conv_act(gate, up, g1, g2, cols).astype(actp_ref.dtype)

    @pl.when(i == 0)
    def _():
        c = cs_ref[...]
        gate = jnp.dot(c, wgbf_ref[...], preferred_element_type=F32)
        up = jnp.dot(c, wubf_ref[...], preferred_element_type=F32)
        gates_ref[...] = gate
        acts_ref[...] = conv_act(gate, up, h1_ref[...], h0_ref[...], slice(None)).astype(acts_ref.dtype)


def _up_proj(c_p, c_s, w_up, conv_w, conv_b, hist0, hist1, *, tm, seq_len):
    mp, k = c_p.shape
    ms = c_s.shape[0]
    npt = mp // tm
    tiles_per_seq = seq_len // tm
    nseq = mp // seq_len
    tn = UP_TN
    nj = pl.cdiv(D_FF, tn)
    up_block0 = D_FF // LANE
    last_block = w_up.shape[1] // LANE - 1

    def row_p(j, i):
        return jnp.maximum(i - 1, 0)

    def wu_spec(part):
        return pl.BlockSpec(
            (k, LANE), lambda j, i: (0, jnp.minimum(up_block0 + j * UP_PARTS + part, last_block)))

    in_specs = [
        pl.BlockSpec((tm, k), lambda j, i: (row_p(j, i), 0)),
        pl.BlockSpec((ms, k), lambda j, i: (0, 0)),
        pl.BlockSpec((k, tn), lambda j, i: (0, j)),
        *[wu_spec(part) for part in range(UP_PARTS)],
        pl.BlockSpec((CONV_W, tn), lambda j, i: (0, j)),
        pl.BlockSpec((1, tn), lambda j, i: (0, j)),
        pl.BlockSpec((ms, tn), lambda j, i: (0, j)),
        pl.BlockSpec((ms, tn), lambda j, i: (0, j)),
    ]
    out_specs = [
        pl.BlockSpec((tm, tn), lambda j, i: (row_p(j, i), j)),
        pl.BlockSpec((ms, tn), lambda j, i: (0, j)),
        pl.BlockSpec((1, CONV_W - 1, tn), lambda j, i: (row_p(j, i) // tiles_per_seq, 0, j)),
        pl.BlockSpec((ms, tn), lambda j, i: (0, j)),
    ]
    out_shape = [
        jax.ShapeDtypeStruct((mp, D_FF), BF16),
        jax.ShapeDtypeStruct((ms, D_FF), BF16),
        jax.ShapeDtypeStruct((nseq, CONV_W - 1, D_FF), F32),
        jax.ShapeDtypeStruct((ms, D_FF), F32),
    ]
    return pl.pallas_call(
        functools.partial(_up_kernel, tiles_per_seq=tiles_per_seq),
        grid=(nj, npt + 1),
        in_specs=in_specs,
        out_specs=out_specs,
        out_shape=out_shape,
        scratch_shapes=[pltpu.VMEM((k, tn), BF16), pltpu.VMEM((k, tn), BF16), pltpu.VMEM((SUBLANES, tn), F32)],
        compiler_params=_cparams(2),
        name="up_proj",
    )(c_p, c_s, w_up, *([w_up] * UP_PARTS), conv_w, conv_b, hist0, hist1)


def _rms(x, g):
    return x * lax.rsqrt(jnp.mean(x * x, axis=-1, keepdims=True) + EPS) * g


def _two_group_specs(tm, ms, widths):
    specs = []
    for w in widths:
        specs.append(pl.BlockSpec((tm, w), lambda i: (jnp.maximum(i - 1, 0), 0)))
        specs.append(pl.BlockSpec((ms, w), lambda i: (0, 0)))
    return specs


def _resident(shape):
    return pl.BlockSpec(shape, lambda i: (0,) * len(shape), pipeline_mode=pl.Buffered(1))


def _out_norm_kernel(mp_ref, ms_ref, xp_ref, xs_ref, w_ref, g_ref, hp_ref, hs_ref, cp_ref, cs_ref, wbf_ref):
    i = pl.program_id(0)

    @pl.when(i == 0)
    def _():
        wbf_ref[...] = w_ref[...].astype(BF16)

    def emit(m_ref, x_ref, h_ref, c_ref):
        h = x_ref[...] + jnp.dot(m_ref[...], wbf_ref[...], preferred_element_type=F32)
        h_ref[...] = h
        c_ref[...] = _rms(h, g_ref[...]).astype(c_ref.dtype)

    @pl.when(i == 0)
    def _():
        emit(ms_ref, xs_ref, hs_ref, cs_ref)

    @pl.when(i > 0)
    def _():
        emit(mp_ref, xp_ref, hp_ref, cp_ref)


def _out_norm(m_p, m_s, x_p, x_s, w_out, g, *, tm):
    mp, d = x_p.shape
    ms = x_s.shape[0]
    return pl.pallas_call(
        _out_norm_kernel,
        grid=(mp // tm + 1,),
        in_specs=[*_two_group_specs(tm, ms, (d, d)), _resident((d, d)), _resident((1, d))],
        out_specs=_two_group_specs(tm, ms, (d, d)),
        out_shape=[
            jax.ShapeDtypeStruct((mp, d), F32), jax.ShapeDtypeStruct((ms, d), F32),
            jax.ShapeDtypeStruct((mp, d), BF16), jax.ShapeDtypeStruct((ms, d), BF16),
        ],
        scratch_shapes=[pltpu.VMEM((d, d), BF16)],
        compiler_params=_cparams(1),
        name="out_norm",
    )(m_p, m_s, x_p, x_s, w_out, g.reshape(1, d))


def _ple_kernel(hp_ref, hs_ref, pp_ref, ps_ref, wpg_ref, wple_ref, gple_ref, gfin_ref, yp_ref, ys_ref,
                wpgbf_ref, wplebf_ref, *, final_norm):
    i = pl.program_id(0)

    @pl.when(i == 0)
    def _():
        wpgbf_ref[...] = wpg_ref[...].astype(BF16)
        wplebf_ref[...] = wple_ref[...].astype(BF16)

    def emit(h_ref, p_ref, y_ref):
        h2 = h_ref[...]
        cn = _rms(h2, gple_ref[...]).astype(BF16)
        pg = _sigmoid(jnp.dot(cn, wpgbf_ref[...], preferred_element_type=F32))
        e = jnp.dot(p_ref[...].astype(BF16), wplebf_ref[...], preferred_element_type=F32)
        h3 = h2 + pg * e
        y_ref[...] = _rms(h3, gfin_ref[...]) if final_norm else h3

    @pl.when(i == 0)
    def _():
        emit(hs_ref, ps_ref, ys_ref)

    @pl.when(i > 0)
    def _():
        emit(hp_ref, pp_ref, yp_ref)


def _ple(h_p, h_s, p_p, p_s, wpg, wple, g_ple, g_final, *, tm, final_norm):
    mp, d = h_p.shape
    ms = h_s.shape[0]
    pd = p_p.shape[1]
    return pl.pallas_call(
        functools.partial(_ple_kernel, final_norm=final_norm),
        grid=(mp // tm + 1,),
        in_specs=[*_two_group_specs(tm, ms, (d, pd)), _resident((d, d)), _resident((pd, d)),
                  _resident((1, d)), _resident((1, d))],
        out_specs=_two_group_specs(tm, ms, (d,)),
        out_shape=[jax.ShapeDtypeStruct((mp, d), F32), jax.ShapeDtypeStruct((ms, d), F32)],
        scratch_shapes=[pltpu.VMEM((d, d), BF16), pltpu.VMEM((pd, d), BF16)],
        compiler_params=_cparams(1),
        name="ple",
    )(h_p, h_s, p_p, p_s, wpg, wple, g_ple.reshape(1, d), g_final.reshape(1, d))


def _mixers(z, lw, *, tm, prompt_shape=None, state=None):
    sample = state is not None
    if sample:
        s_gla, s_pool = state
        og, gla_new = _gla_sample(z, lw["wg"], lw["bg"], lw["gn"], s_gla)
        pm = _pool_sample(z, jnp.transpose(s_pool, (1, 0, 2)), lw["w_pool"], lw["pool_scale"])
        u = z[:, Z_U:Z_U + POOL_W].astype(F32)
        pool_new = jnp.concatenate([s_pool[:, 1:], u[:, None, :]], axis=1)
    else:
        nb, t = prompt_shape
        og, gla_new = _gla_prompt(z, lw["wg"], lw["bg"], lw["gn"], nb, t)
        pm = _pool_prompt(z, lw["w_pool"], lw["pool_scale"], nb, t, 512)
        pool_new = z.reshape(nb, t, Z_WIDTH)[:, t - POOL_BUF:, Z_U:Z_U + POOL_W].astype(F32)
    mg = _merge(og, pm, lw["w_branch_a"], lw["w_branch_b"], z, tm=tm, tn=512)
    return mg, gla_new, pool_new


def kernel(x_prompt, x_sample, state_gla, state_pool, state_conv, p_prompt, p_sample, norm_mix, w_in, w_gate_up, b_gate, gla_norm, w_branch_a, w_pool, pool_scale, w_branch_b, w_out, norm_ffn, w_up, conv_w, conv_b, w_down, norm_ple, w_ple_gate, w_ple, norm_final):
    depth = w_in.shape[0]
    nb, t, d = x_prompt.shape
    ns = x_sample.shape[0]
    hp = x_prompt.reshape(nb * t, d)
    hs = x_sample.reshape(ns, d)
    outs = {k: [] for k in ("gp", "pp", "cp", "gs", "ps", "cs")}
    for li in range(depth):
        lw = {
            "wg": jnp.pad(w_gate_up[li], ((0, LANE - GATE_RANK), (0, 0))).astype(BF16),
            "bg": b_gate[li].reshape(1, GLA_KW),
            "gn": gla_norm[li].reshape(1, GLA_VW),
            "w_branch_a": w_branch_a[li],
            "w_pool": w_pool[li],
            "pool_scale": pool_scale[li].reshape(1, POOL_W),
            "w_branch_b": w_branch_b[li],
        }
        a_p = _rmsnorm(hp, norm_mix[li], BF16, 512)
        a_s = _rmsnorm(hs, norm_mix[li], BF16, ns)
        z_p, z_s = _in_proj(a_p, a_s, w_in[li].T, tm=1024)
        mg_p, g1, p1 = _mixers(z_p, lw, tm=1024, prompt_shape=(nb, t))
        mg_s, g2, p2 = _mixers(z_s, lw, tm=ns, state=(state_gla[li], state_pool[li]))
        h1_p, h1_s, c_p, c_s = _out_norm(mg_p, mg_s, hp, hs, w_out[li], norm_ffn[li], tm=512)
        s_conv = state_conv[li]
        act_p, act_s, c1, gate_s = _up_proj(c_p, c_s, w_up[li], conv_w[li], conv_b[li].reshape(1, D_FF),
                                            s_conv[:, 0], s_conv[:, 1], tm=1024, seq_len=t)
        c2 = jnp.stack([s_conv[:, 1], gate_s], axis=1)
        h2_p = _matmul(act_p, w_down[li], tm=512, tn=512, out_dtype=F32, res=h1_p, name="down_proj")
        h2_s = _matmul(act_s, w_down[li], tm=ns, tn=512, out_dtype=F32, res=h1_s, name="down_proj")
        hp, hs = _ple(h2_p, h2_s, p_prompt[li].reshape(nb * t, PLE_DIM), p_sample[li].reshape(ns, PLE_DIM),
                      w_ple_gate[li], w_ple[li], norm_ple[li], norm_final, tm=512,
                      final_norm=li == depth - 1)
        for key, val in zip(("gp", "pp", "cp", "gs", "ps", "cs"), (g1, p1, c1, g2, p2, c2)):
            outs[key].append(val)
    st = {k: jnp.stack(v, axis=0) for k, v in outs.items()}
    return (hp.reshape(nb, t, d), hs.reshape(ns, 1, d),
            st["gp"], st["pp"], st["cp"], st["gs"], st["ps"], st["cs"])
```

```python
import functools

import jax
import jax.numpy as jnp
from jax import lax
from jax.experimental import pallas as pl
from jax.experimental.pallas import tpu as pltpu

F32 = jnp.float32
BF16 = jnp.bfloat16

D_MODEL = 2048
GLA_HEADS = 4
GLA_DK = 256
GLA_DV = 512
GLA_KW = GLA_HEADS * GLA_DK
GLA_VW = GLA_HEADS * GLA_DV
GATE_RANK = 16
GATE_TEMP = 16.0
GLA_CHUNK = 64
POOL_WINDOWS = (2, 4, 8, 16)
POOL_GW = 256
POOL_W = 1024
POOL_BUF = 15
D_FF = 5504
CONV_W = 3
PLE_DIM = 256
EPS = 1e-6

LANE = 128
SUBLANES = 8
Z_Q, Z_K, Z_V, Z_R, Z_U, Z_GA, Z_GB, Z_GLR, Z_WIDTH = 0, 1024, 2048, 4096, 6144, 7168, 9216, 11264, 11392
GLR_SRC = 6144
IN_TN = 1024

VMEM_LIMIT = 56 * 1024 * 1024


def _cparams(n_axes):
    return pltpu.CompilerParams(dimension_semantics=("arbitrary",) * n_axes, vmem_limit_bytes=VMEM_LIMIT)


def _sigmoid(x):
    return 1.0 / (1.0 + jnp.exp(-x))


def _silu(x):
    return x * _sigmoid(x)


def _log_sigmoid(x):
    return jnp.minimum(x, 0.0) - jnp.log(1.0 + jnp.exp(-jnp.abs(x)))


def _rms_kernel(x_ref, g_ref, o_ref):
    x = x_ref[...]
    ms = jnp.mean(x * x, axis=-1, keepdims=True)
    o_ref[...] = (x * lax.rsqrt(ms + EPS) * g_ref[...]).astype(o_ref.dtype)


def _rmsnorm(x, g, out_dtype, tm):
    m, d = x.shape
    return pl.pallas_call(
        _rms_kernel,
        grid=(m // tm,),
        in_specs=[pl.BlockSpec((tm, d), lambda i: (i, 0)), pl.BlockSpec((1, d), lambda i: (0, 0))],
        out_specs=pl.BlockSpec((tm, d), lambda i: (i, 0)),
        out_shape=jax.ShapeDtypeStruct((m, d), out_dtype),
        compiler_params=_cparams(1),
        name="rmsnorm",
    )(x, g.reshape(1, d))


def _mm_kernel(*refs, has_res, cast_w):
    a_ref, w_ref = refs[0], refs[1]
    pos = 2
    res_ref = None
    if has_res:
        res_ref = refs[pos]
        pos += 1
    o_ref = refs[pos]
    pos += 1
    if cast_w:
        wbf_ref = refs[pos]

        @pl.when(pl.program_id(1) == 0)
        def _():
            wbf_ref[...] = w_ref[...].astype(BF16)

        w = wbf_ref[...]
    else:
        w = w_ref[...]
    acc = jnp.dot(a_ref[...], w, preferred_element_type=F32)
    if has_res:
        acc = acc + res_ref[...]
    o_ref[...] = acc.astype(o_ref.dtype)


def _matmul(a, w, *, tm, tn, out_dtype, res=None, name):
    m, k = a.shape
    n = w.shape[1]
    cast_w = w.dtype != BF16
    in_specs = [pl.BlockSpec((tm, k), lambda j, i: (i, 0)), pl.BlockSpec((k, tn), lambda j, i: (0, j))]
    args = [a, w]
    if res is not None:
        in_specs.append(pl.BlockSpec((tm, tn), lambda j, i: (i, j)))
        args.append(res)
    return pl.pallas_call(
        functools.partial(_mm_kernel, has_res=res is not None, cast_w=cast_w),
        grid=(pl.cdiv(n, tn), m // tm),
        in_specs=in_specs,
        out_specs=pl.BlockSpec((tm, tn), lambda j, i: (i, j)),
        out_shape=jax.ShapeDtypeStruct((m, n), out_dtype),
        scratch_shapes=[pltpu.VMEM((k, tn), BF16)] if cast_w else [],
        compiler_params=_cparams(2),
        name=name,
    )(*args)


_NT = (((1,), (1,)), ((), ()))
_TN = (((0,), (0,)), ((), ()))


def _in_proj_kernel(ap_ref, as_ref, w_ref, wx_ref, zp_ref, zs_ref, wbf_ref):
    j = pl.program_id(0)
    i = pl.program_id(1)
    last = pl.num_programs(0) - 1
    first_shifted = GLR_SRC // IN_TN
    keep = IN_TN - GATE_RANK

    @pl.when(i == 0)
    def _():
        @pl.when(j < first_shifted)
        def _():
            wbf_ref[...] = w_ref[...].astype(BF16)

        @pl.when(jnp.logical_and(j >= first_shifted, j < last))
        def _():
            wbf_ref[:keep, :] = w_ref[GATE_RANK:, :].astype(BF16)
            wbf_ref[keep:, :] = wx_ref[...].astype(BF16)

        @pl.when(j == last)
        def _():
            wbf_ref[:GATE_RANK, :] = wx_ref[...].astype(BF16)
            wbf_ref[GATE_RANK:LANE, :] = jnp.zeros((LANE - GATE_RANK, wbf_ref.shape[1]), BF16)

    def emit(a_ref, z_ref):
        @pl.when(j < last)
        def _():
            z_ref[...] = lax.dot_general(a_ref[...], wbf_ref[...], _NT,
                                         preferred_element_type=F32).astype(z_ref.dtype)

        @pl.when(j == last)
        def _():
            z_ref[:, :LANE] = lax.dot_general(a_ref[...], wbf_ref[:LANE, :], _NT,
                                              preferred_element_type=F32).astype(z_ref.dtype)

    @pl.when(i == 0)
    def _():
        emit(as_ref, zs_ref)

    @pl.when(i > 0)
    def _():
        emit(ap_ref, zp_ref)


def _in_proj(a_p, a_s, wt, *, tm):
    mp, k = a_p.shape
    ms = a_s.shape[0]
    npt = mp // tm
    nj = Z_GLR // IN_TN + 1
    aux_per_tile = IN_TN // GATE_RANK

    def row_p(j, i):
        return jnp.maximum(i - 1, 0)

    def aux_idx(j, i):
        return (jnp.where(j == nj - 1, GLR_SRC // GATE_RANK, (j + 1) * aux_per_tile), 0)

    return pl.pallas_call(
        _in_proj_kernel,
        grid=(nj, npt + 1),
        in_specs=[
            pl.BlockSpec((tm, k), lambda j, i: (row_p(j, i), 0)),
            pl.BlockSpec((ms, k), lambda j, i: (0, 0)),
            pl.BlockSpec((IN_TN, k), lambda j, i: (j, 0)),
            pl.BlockSpec((GATE_RANK, k), aux_idx),
        ],
        out_specs=[
            pl.BlockSpec((tm, IN_TN), lambda j, i: (row_p(j, i), j)),
            pl.BlockSpec((ms, IN_TN), lambda j, i: (0, j)),
        ],
        out_shape=[jax.ShapeDtypeStruct((mp, Z_WIDTH), BF16), jax.ShapeDtypeStruct((ms, Z_WIDTH), BF16)],
        scratch_shapes=[pltpu.VMEM((IN_TN, k), BF16)],
        compiler_params=_cparams(2),
        name="in_proj",
    )(a_p, a_s, wt, wt)


GLA_GROUP = 8


def _gla_prompt_kernel(q_ref, k_ref, v_ref, r_ref, glr_ref, wg_ref, bg_ref, gn_ref,
                       og_ref, s_out_ref, st_ref):
    t = q_ref.shape[0]
    c = GLA_CHUNK
    rows = GLA_GROUP * c
    st_ref[...] = jnp.zeros_like(st_ref)
    row = lax.broadcasted_iota(jnp.int32, (c, c), 0)
    col = lax.broadcasted_iota(jnp.int32, (c, c), 1)
    causal = row >= col
    tri = causal.astype(BF16)
    nt = (((1,), (1,)), ((), ()))
    tn = (((0,), (0,)), ((), ()))

    def group(gi, carry):
        r0 = pl.multiple_of(gi * rows, rows)
        rs = pl.ds(r0, rows)
        zg = jnp.dot(glr_ref[rs, :], wg_ref[...], preferred_element_type=F32) + bg_ref[...]
        gl = _log_sigmoid(zg) * (1.0 / GATE_TEMP)
        gl_hi = gl.astype(BF16)
        gl_lo = (gl - gl_hi.astype(F32)).astype(BF16)
        q = q_ref[rs, :].astype(F32) * (GLA_DK ** -0.5)
        k = k_ref[rs, :].astype(F32)
        v = v_ref[rs, :]
        qbs, decays, o_intra, dsts = [], [], [], []
        for ci in range(GLA_GROUP):
            sl = slice(ci * c, (ci + 1) * c)
            b = (jnp.dot(tri, gl_hi[sl], preferred_element_type=F32)
                 + jnp.dot(tri, gl_lo[sl], preferred_element_type=F32))
            bl = b[c - 1:c, :]
            qb = (q[sl] * jnp.exp(b)).astype(BF16)
            kb = (k[sl] * jnp.exp(-b)).astype(BF16)
            kd = (k[sl] * jnp.exp(bl - b)).astype(BF16)
            att = lax.dot_general(qb, kb, nt, preferred_element_type=F32)
            att = jnp.where(causal, att, 0.0).astype(BF16)
            vc = v[sl]
            qbs.append(qb)
            decays.append(jnp.exp(bl))
            o_intra.append(jnp.dot(att, vc, preferred_element_type=F32))
            dsts.append(lax.dot_general(vc, kd, tn, preferred_element_type=F32))
        st = st_ref[...]
        sts = []
        for ci in range(GLA_GROUP):
            sts.append(st.astype(BF16))
            st = st * decays[ci] + dsts[ci]
        st_ref[...] = st
        outs = [o_intra[ci] + lax.dot_general(qbs[ci], sts[ci], nt, preferred_element_type=F32)
                for ci in range(GLA_GROUP)]
        o = jnp.concatenate(outs, axis=0)
        ms = jnp.mean(o * o, axis=-1, keepdims=True)
        on = o * lax.rsqrt(ms + EPS) * gn_ref[...]
        og_ref[rs, :] = (on * _silu(r_ref[rs, :].astype(F32))).astype(og_ref.dtype)
        return carry

    lax.fori_loop(0, t // rows, group, 0)
    s_out_ref[0, 0] = st_ref[...].T


def _gla_prompt(z, wg, bg, gn, nb, t):
    m = z.shape[0]
    hh = GLA_HEADS
    og, s_new = pl.pallas_call(
        _gla_prompt_kernel,
        grid=(nb, hh),
        in_specs=[
            pl.BlockSpec((t, GLA_DK), lambda b, h: (b, Z_Q // GLA_DK + h)),
            pl.BlockSpec((t, GLA_DK), lambda b, h: (b, Z_K // GLA_DK + h)),
            pl.BlockSpec((t, GLA_DV), lambda b, h: (b, Z_V // GLA_DV + h)),
            pl.BlockSpec((t, GLA_DV), lambda b, h: (b, Z_R // GLA_DV + h)),
            pl.BlockSpec((t, LANE), lambda b, h: (b, Z_GLR // LANE)),
            pl.BlockSpec((LANE, GLA_DK), lambda b, h: (0, h)),
            pl.BlockSpec((1, GLA_DK), lambda b, h: (0, h)),
            pl.BlockSpec((1, GLA_DV), lambda b, h: (0, h)),
        ],
        out_specs=[
            pl.BlockSpec((t, GLA_DV), lambda b, h: (b, h)),
            pl.BlockSpec((1, 1, GLA_DK, GLA_DV), lambda b, h: (b, h, 0, 0)),
        ],
        out_shape=[
            jax.ShapeDtypeStruct((m, GLA_VW), BF16),
            jax.ShapeDtypeStruct((nb, hh, GLA_DK, GLA_DV), F32),
        ],
        scratch_shapes=[pltpu.VMEM((GLA_DV, GLA_DK), F32)],
        compiler_params=_cparams(2),
        name="gla_prompt",
    )(z, z, z, z, z, wg, bg, gn)
    return og, s_new


GLA_SAMPLE_BLOCK = 4


def _gla_sample_kernel(q_ref, k_ref, v_ref, r_ref, glr_ref, wg_ref, bg_ref, gn_ref, s_ref,
                       og_ref, s_out_ref):
    def one_sequence(si, carry):
        _gla_sample_sequence(pl.program_id(0) * GLA_SAMPLE_BLOCK + si, si, q_ref, k_ref, v_ref, r_ref, glr_ref,
                             wg_ref, bg_ref, gn_ref, s_ref, og_ref, s_out_ref)
        return carry

    lax.fori_loop(0, GLA_SAMPLE_BLOCK, one_sequence, 0)


def _gla_sample_sequence(n, si, q_ref, k_ref, v_ref, r_ref, glr_ref, wg_ref, bg_ref, gn_ref, s_ref,
                         og_ref, s_out_ref):
    row = pl.ds(n, 1)

    def rows8(ref):
        return jnp.broadcast_to(ref[row, :], (SUBLANES, ref.shape[1]))

    zg = jnp.dot(rows8(glr_ref).astype(BF16), wg_ref[...], preferred_element_type=F32) + bg_ref[...]
    g = _log_sigmoid(zg) * (1.0 / GATE_TEMP)
    eg = jnp.exp(g)
    eng = jnp.exp(-g)
    eg1 = eg.astype(BF16)
    rem = eg - eg1.astype(F32)
    eg2 = rem.astype(BF16)
    eg3 = (rem - eg2.astype(F32)).astype(BF16)
    first_row = lax.broadcasted_iota(jnp.int32, (SUBLANES, LANE), 0) == 0
    ones_row = jnp.where(first_row, 1.0, 0.0).astype(BF16)
    row0 = lax.broadcasted_iota(jnp.int32, (SUBLANES, GLA_DV), 0) == 0
    q_all = rows8(q_ref) * (GLA_DK ** -0.5)
    k_all = rows8(k_ref)
    v_all = rows8(v_ref)
    r_all = r_ref[row, :]
    gn_all = gn_ref[...]
    for h in range(GLA_HEADS):
        ks = slice(h * GLA_DK, (h + 1) * GLA_DK)
        vs = slice(h * GLA_DV, (h + 1) * GLA_DV)
        q, k, v = q_all[:, ks], k_all[:, ks], v_all[:, vs]
        s_old = s_ref[si, h]
        decay = (lax.dot_general(eg1[:, ks], ones_row, _TN, preferred_element_type=F32)
                 + lax.dot_general(eg2[:, ks], ones_row, _TN, preferred_element_type=F32)
                 + lax.dot_general(eg3[:, ks], ones_row, _TN, preferred_element_type=F32))
        v0 = jnp.where(row0, v, 0.0)
        ds = lax.dot_general(k.astype(BF16), v0.astype(BF16), _TN, preferred_element_type=F32)
        s_out_ref[si, h] = jnp.tile(decay, (1, GLA_DV // LANE)) * s_old + ds
        qb = q * eg[:, ks]
        kb = k * eng[:, ks]
        att = jnp.sum(qb * kb, axis=-1, keepdims=True)
        o = jnp.dot(qb.astype(BF16), s_old.astype(BF16), preferred_element_type=F32) + att * v
        o = o[0:1, :]
        ms = jnp.mean(o * o, axis=-1, keepdims=True)
        on = o * lax.rsqrt(ms + EPS) * gn_all[:, vs]
        og_ref[row, vs] = on * _silu(r_all[:, vs])


def _gla_sample(z, wg, bg, gn, state):
    n = z.shape[0]
    zf = z.astype(F32)
    nb = GLA_SAMPLE_BLOCK
    og, s_new = pl.pallas_call(
        _gla_sample_kernel,
        grid=(n // nb,),
        in_specs=[
            pl.BlockSpec((n, GLA_KW), lambda i: (0, Z_Q // GLA_KW)),
            pl.BlockSpec((n, GLA_KW), lambda i: (0, Z_K // GLA_KW)),
            pl.BlockSpec((n, GLA_VW), lambda i: (0, Z_V // GLA_VW)),
            pl.BlockSpec((n, GLA_VW), lambda i: (0, Z_R // GLA_VW)),
            pl.BlockSpec((n, LANE), lambda i: (0, Z_GLR // LANE)),
            pl.BlockSpec((LANE, GLA_KW), lambda i: (0, 0)),
            pl.BlockSpec((1, GLA_KW), lambda i: (0, 0)),
            pl.BlockSpec((1, GLA_VW), lambda i: (0, 0)),
            pl.BlockSpec((nb, GLA_HEADS, GLA_DK, GLA_DV), lambda i: (i, 0, 0, 0)),
        ],
        out_specs=[
            pl.BlockSpec((n, GLA_VW), lambda i: (0, 0)),
            pl.BlockSpec((nb, GLA_HEADS, GLA_DK, GLA_DV), lambda i: (i, 0, 0, 0)),
        ],
        out_shape=[
            jax.ShapeDtypeStruct((n, GLA_VW), F32),
            jax.ShapeDtypeStruct((n, GLA_HEADS, GLA_DK, GLA_DV), F32),
        ],
        compiler_params=_cparams(1),
        name="gla_sample",
    )(zf, zf, zf, zf, zf, wg, bg, gn, state)
    return og.astype(BF16), s_new


def _pool_group_out(mix, g, wp_ref, ps_ref):
    cs = slice(g * POOL_GW, (g + 1) * POOL_GW)
    pm = jnp.dot(mix.astype(BF16), wp_ref[g].astype(BF16), preferred_element_type=F32)
    return pm * ps_ref[:, cs]


def _pool_prompt_kernel(u_ref, wp_ref, ps_ref, o_ref, ext_ref):
    i = pl.program_id(1)
    tt = u_ref.shape[0]
    hist = POOL_BUF + 1

    @pl.when(i == 0)
    def _():
        ext_ref[0:hist, :] = jnp.zeros((hist, POOL_W), F32)

    @pl.when(i > 0)
    def _():
        ext_ref[0:hist, :] = ext_ref[tt:tt + hist, :]

    ext_ref[hist:hist + tt, :] = u_ref[...].astype(F32)
    pos = i * tt + lax.broadcasted_iota(jnp.int32, (tt, 1), 0)
    for g, w in enumerate(POOL_WINDOWS):
        cs = slice(g * POOL_GW, (g + 1) * POOL_GW)
        u = ext_ref[hist:hist + tt, cs]
        s = u
        for d in range(1, w):
            s = s + ext_ref[hist - d:hist - d + tt, cs]
        cnt = jnp.minimum(pos + 1, w).astype(F32)
        mix = s / cnt - u
        o_ref[:, cs] = _pool_group_out(mix, g, wp_ref, ps_ref).astype(o_ref.dtype)


def _pool_prompt(z, w_pool, pool_scale, nb, t, tt):
    m = z.shape[0]
    nt = t // tt
    return pl.pallas_call(
        _pool_prompt_kernel,
        grid=(nb, nt),
        in_specs=[
            pl.BlockSpec((tt, POOL_W), lambda b, i: (b * nt + i, Z_U // POOL_W)),
            pl.BlockSpec((4, POOL_GW, POOL_GW), lambda b, i: (0, 0, 0)),
            pl.BlockSpec((1, POOL_W), lambda b, i: (0, 0)),
        ],
        out_specs=pl.BlockSpec((tt, POOL_W), lambda b, i: (b * nt + i, 0)),
        out_shape=jax.ShapeDtypeStruct((m, POOL_W), BF16),
        scratch_shapes=[pltpu.VMEM((tt + POOL_BUF + 1, POOL_W), F32)],
        compiler_params=_cparams(2),
        name="pool_prompt",
    )(z, w_pool, pool_scale)


def _pool_sample_kernel(u_ref, buf_ref, wp_ref, ps_ref, o_ref):
    for g, w in enumerate(POOL_WINDOWS):
        cs = slice(g * POOL_GW, (g + 1) * POOL_GW)
        u = u_ref[:, cs].astype(F32)
        s = u
        for d in range(1, w):
            s = s + buf_ref[POOL_BUF - d, :, cs]
        mix = s / float(w) - u
        o_ref[:, cs] = _pool_group_out(mix, g, wp_ref, ps_ref).astype(o_ref.dtype)


def _pool_sample(z, buf_t, w_pool, pool_scale):
    n = z.shape[0]
    return pl.pallas_call(
        _pool_sample_kernel,
        grid=(1,),
        in_specs=[
            pl.BlockSpec((n, POOL_W), lambda i: (0, Z_U // POOL_W)),
            pl.BlockSpec((POOL_BUF, n, POOL_W), lambda i: (0, 0, 0)),
            pl.BlockSpec((4, POOL_GW, POOL_GW), lambda i: (0, 0, 0)),
            pl.BlockSpec((1, POOL_W), lambda i: (0, 0)),
        ],
        out_specs=pl.BlockSpec((n, POOL_W), lambda i: (0, 0)),
        out_shape=jax.ShapeDtypeStruct((n, POOL_W), BF16),
        compiler_params=_cparams(1),
        name="pool_sample",
    )(z, buf_t, w_pool, pool_scale)


def _merge_kernel(og_ref, pm_ref, wa_ref, wb_ref, ga_ref, gb_ref, o_ref, wabf_ref, wbbf_ref):
    @pl.when(pl.program_id(1) == 0)
    def _():
        wabf_ref[...] = wa_ref[...].astype(BF16)
        wbbf_ref[...] = wb_ref[...].astype(BF16)

    ya = jnp.dot(og_ref[...], wabf_ref[...], preferred_element_type=F32)
    yb = jnp.dot(pm_ref[...], wbbf_ref[...], preferred_element_type=F32)
    ga = ga_ref[...].astype(F32)
    gb = gb_ref[...].astype(F32)
    o_ref[...] = (_sigmoid(ga) * ya + _sigmoid(gb) * yb).astype(o_ref.dtype)


def _merge(og, pm, wa, wb, z, *, tm, tn):
    m = og.shape[0]
    return pl.pallas_call(
        _merge_kernel,
        grid=(D_MODEL // tn, m // tm),
        in_specs=[
            pl.BlockSpec((tm, GLA_VW), lambda j, i: (i, 0)),
            pl.BlockSpec((tm, POOL_W), lambda j, i: (i, 0)),
            pl.BlockSpec((GLA_VW, tn), lambda j, i: (0, j)),
            pl.BlockSpec((POOL_W, tn), lambda j, i: (0, j)),
            pl.BlockSpec((tm, tn), lambda j, i: (i, Z_GA // tn + j)),
            pl.BlockSpec((tm, tn), lambda j, i: (i, Z_GB // tn + j)),
        ],
        out_specs=pl.BlockSpec((tm, tn), lambda j, i: (i, j)),
        out_shape=jax.ShapeDtypeStruct((m, D_MODEL), BF16),
        scratch_shapes=[pltpu.VMEM((GLA_VW, tn), BF16), pltpu.VMEM((POOL_W, tn), BF16)],
        compiler_params=_cparams(2),
        name="merge",
    )(og, pm, wa, wb, z, z)


UP_TN = 512
UP_SUB = 256
UP_PARTS = UP_TN // LANE


def _up_kernel(*refs, tiles_per_seq):
    cp_ref, cs_ref, wg_ref = refs[:3]
    wu_refs = refs[3:3 + UP_PARTS]
    (cw_ref, cb_ref, h0_ref, h1_ref, actp_ref, acts_ref, tailp_ref, gates_ref,
     wgbf_ref, wubf_ref, carry_ref) = refs[3 + UP_PARTS:]
    i = pl.program_id(1)

    @pl.when(i == 0)
    def _():
        wgbf_ref[...] = wg_ref[...].astype(BF16)
        for part, wu_ref in enumerate(wu_refs):
            wubf_ref[:, part * LANE:(part + 1) * LANE] = wu_ref[...].astype(BF16)

    def conv_act(gate, up, g1, g2, cols):
        cw = cw_ref[:, cols]
        gconv = cb_ref[:, cols] + cw[0:1] * g2 + cw[1:2] * g1 + cw[2:3] * gate
        return _silu(gconv) * up

    @pl.when(i > 0)
    def _():
        @pl.when((i - 1) % tiles_per_seq == 0)
        def _():
            carry_ref[...] = jnp.zeros_like(carry_ref)

        c = cp_ref[...]
        tm = c.shape[0]
        sub = lax.broadcasted_iota(jnp.int32, (SUBLANES, UP_SUB), 0)
        for h in range(UP_TN // UP_SUB):
            cols = slice(h * UP_SUB, (h + 1) * UP_SUB)
            gate = jnp.dot(c, wgbf_ref[:, cols], preferred_element_type=F32)
            up = jnp.dot(c, wubf_ref[:, cols], preferred_element_type=F32)
            prev = carry_ref[:, cols]
            r1 = pltpu.roll(gate, 1, 0)
            r2 = pltpu.roll(gate, 2, 0)
            head1 = jnp.where(sub < 1, pltpu.roll(prev, 1, 0), r1[0:SUBLANES])
            head2 = jnp.where(sub < 2, pltpu.roll(prev, 2, 0), r2[0:SUBLANES])
            g1 = jnp.concatenate([head1, r1[SUBLANES:]], axis=0)
            g2 = jnp.concatenate([head2, r2[SUBLANES:]], axis=0)
            carry_ref[:, cols] = gate[tm - SUBLANES:tm]
            tailp_ref[0, :, cols] = gate[tm - (CONV_W - 1):tm]
            actp_ref[:, cols] = conv_act(gate, up, g1, g2, cols).astype(actp_ref.dtype)

    @pl.when(i == 0)
    def _():
        c = cs_ref[...]
        gate = jnp.dot(c, wgbf_ref[...], preferred_element_type=F32)
        up = jnp.dot(c, wubf_ref[...], preferred_element_type=F32)
        gates_ref[...] = gate
        acts_ref[...] = conv_act(gate, up, h1_ref[...], h0_ref[...], slice(None)).astype(acts_ref.dtype)


def _up_proj(c_p, c_s, w_up, conv_w, conv_b, hist0, hist1, *, tm, seq_len):
    mp, k = c_p.shape
    ms = c_s.shape[0]
    npt = mp // tm
    tiles_per_seq = seq_len // tm
    nseq = mp // seq_len
    tn = UP_TN
    nj = pl.cdiv(D_FF, tn)
    up_block0 = D_FF // LANE
    last_block = w_up.shape[1] // LANE - 1

    def row_p(j, i):
        return jnp.maximum(i - 1, 0)

    def wu_spec(part):
        return pl.BlockSpec(
            (k, LANE), lambda j, i: (0, jnp.minimum(up_block0 + j * UP_PARTS + part, last_block)))

    in_specs = [
        pl.BlockSpec((tm, k), lambda j, i: (row_p(j, i), 0)),
        pl.BlockSpec((ms, k), lambda j, i: (0, 0)),
        pl.BlockSpec((k, tn), lambda j, i: (0, j)),
        *[wu_spec(part) for part in range(UP_PARTS)],
        pl.BlockSpec((CONV_W, tn), lambda j, i: (0, j)),
        pl.BlockSpec((1, tn), lambda j, i: (0, j)),
        pl.BlockSpec((ms, tn), lambda j, i: (0, j)),
        pl.BlockSpec((ms, tn), lambda j, i: (0, j)),
    ]
    out_specs = [
        pl.BlockSpec((tm, tn), lambda j, i: (row_p(j, i), j)),
        pl.BlockSpec((ms, tn), lambda j, i: (0, j)),
        pl.BlockSpec((1, CONV_W - 1, tn), lambda j, i: (row_p(j, i) // tiles_per_seq, 0, j)),
        pl.BlockSpec((ms, tn), lambda j, i: (0, j)),
    ]
    out_shape = [
        jax.ShapeDtypeStruct((mp, D_FF), BF16),
        jax.ShapeDtypeStruct((ms, D_FF), BF16),
        jax.ShapeDtypeStruct((nseq, CONV_W - 1, D_FF), F32),
        jax.ShapeDtypeStruct((ms, D_FF), F32),
    ]
    return pl.pallas_call(
        functools.partial(_up_kernel, tiles_per_seq=tiles_per_seq),
        grid=(nj, npt + 1),
        in_specs=in_specs,
        out_specs=out_specs,
        out_shape=out_shape,
        scratch_shapes=[pltpu.VMEM((k, tn), BF16), pltpu.VMEM((k, tn), BF16), pltpu.VMEM((SUBLANES, tn), F32)],
        compiler_params=_cparams(2),
        name="up_proj",
    )(c_p, c_s, w_up, *([w_up] * UP_PARTS), conv_w, conv_b, hist0, hist1)


def _rms(x, g):
    return x * lax.rsqrt(jnp.mean(x * x, axis=-1, keepdims=True) + EPS) * g


def _two_group_specs(tm, ms, widths):
    specs = []
    for w in widths:
        specs.append(pl.BlockSpec((tm, w), lambda i: (jnp.maximum(i - 1, 0), 0)))
        specs.append(pl.BlockSpec((ms, w), lambda i: (0, 0)))
    return specs


def _resident(shape):
    return pl.BlockSpec(shape, lambda i: (0,) * len(shape), pipeline_mode=pl.Buffered(1))


def _out_norm_kernel(mp_ref, ms_ref, xp_ref, xs_ref, w_ref, g_ref, hp_ref, hs_ref, cp_ref, cs_ref, wbf_ref):
    i = pl.program_id(0)

    @pl.when(i == 0)
    def _():
        wbf_ref[...] = w_ref[...].astype(BF16)

    def emit(m_ref, x_ref, h_ref, c_ref):
        h = x_ref[...] + jnp.dot(m_ref[...], wbf_ref[...], preferred_element_type=F32)
        h_ref[...] = h
        c_ref[...] = _rms(h, g_ref[...]).astype(c_ref.dtype)

    @pl.when(i == 0)
    def _():
        emit(ms_ref, xs_ref, hs_ref, cs_ref)

    @pl.when(i > 0)
    def _():
        emit(mp_ref, xp_ref, hp_ref, cp_ref)


def _out_norm(m_p, m_s, x_p, x_s, w_out, g, *, tm):
    mp, d = x_p.shape
    ms = x_s.shape[0]
    return pl.pallas_call(
        _out_norm_kernel,
        grid=(mp // tm + 1,),
        in_specs=[*_two_group_specs(tm, ms, (d, d)), _resident((d, d)), _resident((1, d))],
        out_specs=_two_group_specs(tm, ms, (d, d)),
        out_shape=[
            jax.ShapeDtypeStruct((mp, d), F32), jax.ShapeDtypeStruct((ms, d), F32),
            jax.ShapeDtypeStruct((mp, d), BF16), jax.ShapeDtypeStruct((ms, d), BF16),
        ],
        scratch_shapes=[pltpu.VMEM((d, d), BF16)],
        compiler_params=_cparams(1),
        name="out_norm",
    )(m_p, m_s, x_p, x_s, w_out, g.reshape(1, d))


def _ple_kernel(hp_ref, hs_ref, pp_ref, ps_ref, wpg_ref, wple_ref, gple_ref, gfin_ref, yp_ref, ys_ref,
                wpgbf_ref, wplebf_ref, *, final_norm):
    i = pl.program_id(0)

    @pl.when(i == 0)
    def _():
        wpgbf_ref[...] = wpg_ref[...].astype(BF16)
        wplebf_ref[...] = wple_ref[...].astype(BF16)

    def emit(h_ref, p_ref, y_ref):
        h2 = h_ref[...]
        cn = _rms(h2, gple_ref[...]).astype(BF16)
        pg = _sigmoid(jnp.dot(cn, wpgbf_ref[...], preferred_element_type=F32))
        e = jnp.dot(p_ref[...].astype(BF16), wplebf_ref[...], preferred_element_type=F32)
        h3 = h2 + pg * e
        y_ref[...] = _rms(h3, gfin_ref[...]) if final_norm else h3

    @pl.when(i == 0)
    def _():
        emit(hs_ref, ps_ref, ys_ref)

    @pl.when(i > 0)
    def _():
        emit(hp_ref, pp_ref, yp_ref)


def _ple(h_p, h_s, p_p, p_s, wpg, wple, g_ple, g_final, *, tm, final_norm):
    mp, d = h_p.shape
    ms = h_s.shape[0]
    pd = p_p.shape[1]
    return pl.pallas_call(
        functools.partial(_ple_kernel, final_norm=final_norm),
        grid=(mp // tm + 1,),
        in_specs=[*_two_group_specs(tm, ms, (d, pd)), _resident((d, d)), _resident((pd, d)),
                  _resident((1, d)), _resident((1, d))],
        out_specs=_two_group_specs(tm, ms, (d,)),
        out_shape=[jax.ShapeDtypeStruct((mp, d), F32), jax.ShapeDtypeStruct((ms, d), F32)],
        scratch_shapes=[pltpu.VMEM((d, d), BF16), pltpu.VMEM((pd, d), BF16)],
        compiler_params=_cparams(1),
        name="ple",
    )(h_p, h_s, p_p, p_s, wpg, wple, g_ple.reshape(1, d), g_final.reshape(1, d))


def _mixers(z, lw, *, tm, prompt_shape=None, state=None):
    sample = state is not None
    if sample:
        s_gla, s_pool = state
        og, gla_new = _gla_sample(z, lw["wg"], lw["bg"], lw["gn"], s_gla)
        pm = _pool_sample(z, jnp.transpose(s_pool, (1, 0, 2)), lw["w_pool"], lw["pool_scale"])
        u = z[:, Z_U:Z_U + POOL_W].astype(F32)
        pool_new = jnp.concatenate([s_pool[:, 1:], u[:, None, :]], axis=1)
    else:
        nb, t = prompt_shape
        og, gla_new = _gla_prompt(z, lw["wg"], lw["bg"], lw["gn"], nb, t)
        pm = _pool_prompt(z, lw["w_pool"], lw["pool_scale"], nb, t, 512)
        pool_new = z.reshape(nb, t, Z_WIDTH)[:, t - POOL_BUF:, Z_U:Z_U + POOL_W].astype(F32)
    mg = _merge(og, pm, lw["w_branch_a"], lw["w_branch_b"], z, tm=tm, tn=512)
    return mg, gla_new, pool_new


def kernel(x_prompt, x_sample, state_gla, state_pool, state_conv, p_prompt, p_sample, norm_mix, w_in, w_gate_up, b_gate, gla_norm, w_branch_a, w_pool, pool_scale, w_branch_b, w_out, norm_ffn, w_up, conv_w, conv_b, w_down, norm_ple, w_ple_gate, w_ple, norm_final):
    depth = w_in.shape[0]
    nb, t, d = x_prompt.shape
    ns = x_sample.shape[0]
    hp = x_prompt.reshape(nb * t, d)
    hs = x_sample.reshape(ns, d)
    outs = {k: [] for k in ("gp", "pp", "cp", "gs", "ps", "cs")}
    for li in range(depth):
        lw = {
            "wg": jnp.pad(w_gate_up[li], ((0, LANE - GATE_RANK), (0, 0))).astype(BF16),
            "bg": b_gate[li].reshape(1, GLA_KW),
            "gn": gla_norm[li].reshape(1, GLA_VW),
            "w_branch_a": w_branch_a[li],
            "w_pool": w_pool[li],
            "pool_scale": pool_scale[li].reshape(1, POOL_W),
            "w_branch_b": w_branch_b[li],
        }
        a_p = _rmsnorm(hp, norm_mix[li], BF16, 512)
        a_s = _rmsnorm(hs, norm_mix[li], BF16, ns)
        z_p, z_s = _in_proj(a_p, a_s, w_in[li].T, tm=1024)
        mg_p, g1, p1 = _mixers(z_p, lw, tm=1024, prompt_shape=(nb, t))
        mg_s, g2, p2 = _mixers(z_s, lw, tm=ns, state=(state_gla[li], state_pool[li]))
        h1_p, h1_s, c_p, c_s = _out_norm(mg_p, mg_s, hp, hs, w_out[li], norm_ffn[li], tm=512)
        s_conv = state_conv[li]
        act_p, act_s, c1, gate_s = _up_proj(c_p, c_s, w_up[li], conv_w[li], conv_b[li].reshape(1, D_FF),
                                            s_conv[:, 0], s_conv[:, 1], tm=1024, seq_len=t)
        c2 = jnp.stack([s_conv[:, 1], gate_s], axis=1)
        h2_p = _matmul(act_p, w_down[li], tm=512, tn=512, out_dtype=F32, res=h1_p, name="down_proj")
        h2_s = _matmul(act_s, w_down[li], tm=ns, tn=512, out_dtype=F32, res=h1_s, name="down_proj")
        hp, hs = _ple(h2_p, h2_s, p_prompt[li].reshape(nb * t, PLE_DIM), p_sample[li].reshape(ns, PLE_DIM),
                      w_ple_gate[li], w_ple[li], norm_ple[li], norm_final, tm=512,
                      final_norm=li == depth - 1)
        for key, val in zip(("gp", "pp", "cp", "gs", "ps", "cs"), (g1, p1, c1, g2, p2, c2)):
            outs[key].append(val)
    st = {k: jnp.stack(v, axis=0) for k, v in outs.items()}
    return (hp.reshape(nb, t, d), hs.reshape(ns, 1, d),
            st["gp"], st["pp"], st["cp"], st["gs"], st["ps"], st["cs"])
```

```python
import functools

import jax
import jax.numpy as jnp
from jax import lax
from jax.experimental import pallas as pl
from jax.experimental.pallas import tpu as pltpu

F32 = jnp.float32
BF16 = jnp.bfloat16

D_MODEL = 2048
GLA_HEADS = 4
GLA_DK = 256
GLA_DV = 512
GLA_KW = GLA_HEADS * GLA_DK
GLA_VW = GLA_HEADS * GLA_DV
GATE_RANK = 16
GATE_TEMP = 16.0
GLA_CHUNK = 64
POOL_WINDOWS = (2, 4, 8, 16)
POOL_GW = 256
POOL_W = 1024
POOL_BUF = 15
D_FF = 5504
CONV_W = 3
PLE_DIM = 256
EPS = 1e-6

LANE = 128
SUBLANES = 8
Z_Q, Z_K, Z_V, Z_R, Z_U, Z_GA, Z_GB, Z_GLR, Z_WIDTH = 0, 1024, 2048, 4096, 6144, 7168, 9216, 11264, 11392
GLR_SRC = 6144
IN_TN = 1024
IN_SUB = 512

VMEM_LIMIT = 58 * 1024 * 1024


def _cparams(n_axes):
    return pltpu.CompilerParams(dimension_semantics=("arbitrary",) * n_axes, vmem_limit_bytes=VMEM_LIMIT)


def _sigmoid(x):
    return 1.0 / (1.0 + jnp.exp(-x))


def _silu(x):
    return x * _sigmoid(x)


def _log_sigmoid(x):
    return jnp.minimum(x, 0.0) - jnp.log(1.0 + jnp.exp(-jnp.abs(x)))


def _rms_kernel(x_ref, g_ref, o_ref):
    x = x_ref[...]
    ms = jnp.mean(x * x, axis=-1, keepdims=True)
    o_ref[...] = (x * lax.rsqrt(ms + EPS) * g_ref[...]).astype(o_ref.dtype)


def _rmsnorm(x, g, out_dtype, tm):
    m, d = x.shape
    return pl.pallas_call(
        _rms_kernel,
        grid=(m // tm,),
        in_specs=[pl.BlockSpec((tm, d), lambda i: (i, 0)), pl.BlockSpec((1, d), lambda i: (0, 0))],
        out_specs=pl.BlockSpec((tm, d), lambda i: (i, 0)),
        out_shape=jax.ShapeDtypeStruct((m, d), out_dtype),
        compiler_params=_cparams(1),
        name="rmsnorm",
    )(x, g.reshape(1, d))


def _col_tiled_specs(tm, ms, width, col0=0):
    return [
        pl.BlockSpec((tm, width), lambda j, i: (jnp.maximum(i - 1, 0), col0 + j)),
        pl.BlockSpec((ms, width), lambda j, i: (0, col0 + j)),
    ]


def _row_tiled_specs(tm, ms, width):
    return [
        pl.BlockSpec((tm, width), lambda j, i: (jnp.maximum(i - 1, 0), 0)),
        pl.BlockSpec((ms, width), lambda j, i: (0, 0)),
    ]


def _down_kernel(ap_ref, as_ref, w_ref, rp_ref, rs_ref, op_ref, os_ref, wbf_ref):
    i = pl.program_id(1)

    @pl.when(i == 0)
    def _():
        wbf_ref[...] = w_ref[...].astype(BF16)
        os_ref[...] = rs_ref[...] + jnp.dot(as_ref[...], wbf_ref[...], preferred_element_type=F32)

    @pl.when(i > 0)
    def _():
        op_ref[...] = rp_ref[...] + jnp.dot(ap_ref[...], wbf_ref[...], preferred_element_type=F32)


def _down_proj(a_p, a_s, w, res_p, res_s, *, tm, tn):
    mp, k = a_p.shape
    ms = a_s.shape[0]
    n = w.shape[1]
    return pl.pallas_call(
        _down_kernel,
        grid=(n // tn, mp // tm + 1),
        in_specs=[
            *_row_tiled_specs(tm, ms, k),
            pl.BlockSpec((k, tn), lambda j, i: (0, j)),
            *_col_tiled_specs(tm, ms, tn),
        ],
        out_specs=_col_tiled_specs(tm, ms, tn),
        out_shape=[jax.ShapeDtypeStruct((mp, n), F32), jax.ShapeDtypeStruct((ms, n), F32)],
        scratch_shapes=[pltpu.VMEM((k, tn), BF16)],
        compiler_params=_cparams(2),
        name="down_proj",
    )(a_p, a_s, w, res_p, res_s)


_NT = (((1,), (1,)), ((), ()))
_TN = (((0,), (0,)), ((), ()))


def _in_proj_kernel(ap_ref, as_ref, w_ref, wx_ref, zp_ref, zs_ref, wbf_ref):
    j = pl.program_id(0)
    i = pl.program_id(1)
    last = pl.num_programs(0) - 1
    first_shifted = GLR_SRC // IN_TN
    keep = IN_TN - GATE_RANK

    @pl.when(i == 0)
    def _():
        @pl.when(j < first_shifted)
        def _():
            wbf_ref[...] = w_ref[...].astype(BF16)

        @pl.when(jnp.logical_and(j >= first_shifted, j < last))
        def _():
            wbf_ref[:keep, :] = w_ref[GATE_RANK:, :].astype(BF16)
            wbf_ref[keep:, :] = wx_ref[...].astype(BF16)

        @pl.when(j == last)
        def _():
            wbf_ref[:GATE_RANK, :] = wx_ref[...].astype(BF16)
            wbf_ref[GATE_RANK:LANE, :] = jnp.zeros((LANE - GATE_RANK, wbf_ref.shape[1]), BF16)

    def emit(a_ref, z_ref):
        @pl.when(j < last)
        def _():
            a = a_ref[...]
            for part in range(IN_TN // IN_SUB):
                cols = slice(part * IN_SUB, (part + 1) * IN_SUB)
                z_ref[:, cols] = lax.dot_general(a, wbf_ref[cols, :], _NT,
                                                 preferred_element_type=F32).astype(z_ref.dtype)

        @pl.when(j == last)
        def _():
            z_ref[:, :LANE] = lax.dot_general(a_ref[...], wbf_ref[:LANE, :], _NT,
                                              preferred_element_type=F32).astype(z_ref.dtype)

    @pl.when(i == 0)
    def _():
        emit(as_ref, zs_ref)

    @pl.when(i > 0)
    def _():
        emit(ap_ref, zp_ref)


def _in_proj(a_p, a_s, wt, *, tm):
    mp, k = a_p.shape
    ms = a_s.shape[0]
    npt = mp // tm
    nj = Z_GLR // IN_TN + 1
    aux_per_tile = IN_TN // GATE_RANK

    def row_p(j, i):
        return jnp.maximum(i - 1, 0)

    def aux_idx(j, i):
        return (jnp.where(j == nj - 1, GLR_SRC // GATE_RANK, (j + 1) * aux_per_tile), 0)

    return pl.pallas_call(
        _in_proj_kernel,
        grid=(nj, npt + 1),
        in_specs=[
            pl.BlockSpec((tm, k), lambda j, i: (row_p(j, i), 0)),
            pl.BlockSpec((ms, k), lambda j, i: (0, 0)),
            pl.BlockSpec((IN_TN, k), lambda j, i: (j, 0)),
            pl.BlockSpec((GATE_RANK, k), aux_idx),
        ],
        out_specs=[
            pl.BlockSpec((tm, IN_TN), lambda j, i: (row_p(j, i), j)),
            pl.BlockSpec((ms, IN_TN), lambda j, i: (0, j)),
        ],
        out_shape=[jax.ShapeDtypeStruct((mp, Z_WIDTH), BF16), jax.ShapeDtypeStruct((ms, Z_WIDTH), BF16)],
        scratch_shapes=[pltpu.VMEM((IN_TN, k), BF16)],
        compiler_params=_cparams(2),
        name="in_proj",
    )(a_p, a_s, wt, wt)


GLA_GROUP = 8


def _gla_prompt_kernel(q_ref, k_ref, v_ref, r_ref, glr_ref, wg_ref, bg_ref, gn_ref,
                       og_ref, s_out_ref, st_ref):
    t = q_ref.shape[0]
    c = GLA_CHUNK
    rows = GLA_GROUP * c
    st_ref[...] = jnp.zeros_like(st_ref)
    row = lax.broadcasted_iota(jnp.int32, (c, c), 0)
    col = lax.broadcasted_iota(jnp.int32, (c, c), 1)
    causal = row >= col
    tri = causal.astype(BF16)
    nt = (((1,), (1,)), ((), ()))
    tn = (((0,), (0,)), ((), ()))

    def group(gi, carry):
        r0 = pl.multiple_of(gi * rows, rows)
        rs = pl.ds(r0, rows)
        zg = jnp.dot(glr_ref[rs, :], wg_ref[...], preferred_element_type=F32) + bg_ref[...]
        gl = _log_sigmoid(zg) * (1.0 / GATE_TEMP)
        gl_hi = gl.astype(BF16)
        gl_lo = (gl - gl_hi.astype(F32)).astype(BF16)
        q = q_ref[rs, :].astype(F32) * (GLA_DK ** -0.5)
        k = k_ref[rs, :].astype(F32)
        v = v_ref[rs, :]
        qbs, decays, o_intra, dsts = [], [], [], []
        for ci in range(GLA_GROUP):
            sl = slice(ci * c, (ci + 1) * c)
            b = (jnp.dot(tri, gl_hi[sl], preferred_element_type=F32)
                 + jnp.dot(tri, gl_lo[sl], preferred_element_type=F32))
            bl = b[c - 1:c, :]
            qb = (q[sl] * jnp.exp(b)).astype(BF16)
            kb = (k[sl] * jnp.exp(-b)).astype(BF16)
            kd = (k[sl] * jnp.exp(bl - b)).astype(BF16)
            att = lax.dot_general(qb, kb, nt, preferred_element_type=F32)
            att = jnp.where(causal, att, 0.0).astype(BF16)
            vc = v[sl]
            qbs.append(qb)
            decays.append(jnp.exp(bl))
            o_intra.append(jnp.dot(att, vc, preferred_element_type=F32))
            dsts.append(lax.dot_general(vc, kd, tn, preferred_element_type=F32))
        st = st_ref[...]
        sts = []
        for ci in range(GLA_GROUP):
            sts.append(st.astype(BF16))
            st = st * decays[ci] + dsts[ci]
        st_ref[...] = st
        outs = [o_intra[ci] + lax.dot_general(qbs[ci], sts[ci], nt, preferred_element_type=F32)
                for ci in range(GLA_GROUP)]
        o = jnp.concatenate(outs, axis=0)
        ms = jnp.mean(o * o, axis=-1, keepdims=True)
        on = o * lax.rsqrt(ms + EPS) * gn_ref[...]
        og_ref[rs, :] = (on * _silu(r_ref[rs, :].astype(F32))).astype(og_ref.dtype)
        return carry

    lax.fori_loop(0, t // rows, group, 0)
    s_out_ref[0, 0] = st_ref[...].T


def _gla_prompt(z, wg, bg, gn, nb, t):
    m = z.shape[0]
    hh = GLA_HEADS
    og, s_new = pl.pallas_call(
        _gla_prompt_kernel,
        grid=(nb, hh),
        in_specs=[
            pl.BlockSpec((t, GLA_DK), lambda b, h: (b, Z_Q // GLA_DK + h)),
            pl.BlockSpec((t, GLA_DK), lambda b, h: (b, Z_K // GLA_DK + h)),
            pl.BlockSpec((t, GLA_DV), lambda b, h: (b, Z_V // GLA_DV + h)),
            pl.BlockSpec((t, GLA_DV), lambda b, h: (b, Z_R // GLA_DV + h)),
            pl.BlockSpec((t, LANE), lambda b, h: (b, Z_GLR // LANE)),
            pl.BlockSpec((LANE, GLA_DK), lambda b, h: (0, h)),
            pl.BlockSpec((1, GLA_DK), lambda b, h: (0, h)),
            pl.BlockSpec((1, GLA_DV), lambda b, h: (0, h)),
        ],
        out_specs=[
            pl.BlockSpec((t, GLA_DV), lambda b, h: (b, h)),
            pl.BlockSpec((1, 1, GLA_DK, GLA_DV), lambda b, h: (b, h, 0, 0)),
        ],
        out_shape=[
            jax.ShapeDtypeStruct((m, GLA_VW), BF16),
            jax.ShapeDtypeStruct((nb, hh, GLA_DK, GLA_DV), F32),
        ],
        scratch_shapes=[pltpu.VMEM((GLA_DV, GLA_DK), F32)],
        compiler_params=_cparams(2),
        name="gla_prompt",
    )(z, z, z, z, z, wg, bg, gn)
    return og, s_new


GLA_SAMPLE_BLOCK = 4


def _gla_sample_kernel(q_ref, k_ref, v_ref, r_ref, glr_ref, wg_ref, bg_ref, gn_ref, s_ref,
                       og_ref, s_out_ref):
    def one_sequence(si, carry):
        _gla_sample_sequence(pl.program_id(0) * GLA_SAMPLE_BLOCK + si, si, q_ref, k_ref, v_ref, r_ref, glr_ref,
                             wg_ref, bg_ref, gn_ref, s_ref, og_ref, s_out_ref)
        return carry

    lax.fori_loop(0, GLA_SAMPLE_BLOCK, one_sequence, 0)


def _gla_sample_sequence(n, si, q_ref, k_ref, v_ref, r_ref, glr_ref, wg_ref, bg_ref, gn_ref, s_ref,
                         og_ref, s_out_ref):
    row = pl.ds(n, 1)

    def rows8(ref):
        return jnp.broadcast_to(ref[row, :], (SUBLANES, ref.shape[1]))

    zg = jnp.dot(rows8(glr_ref).astype(BF16), wg_ref[...], preferred_element_type=F32) + bg_ref[...]
    g = _log_sigmoid(zg) * (1.0 / GATE_TEMP)
    eg = jnp.exp(g)
    eng = jnp.exp(-g)
    eg1 = eg.astype(BF16)
    rem = eg - eg1.astype(F32)
    eg2 = rem.astype(BF16)
    eg3 = (rem - eg2.astype(F32)).astype(BF16)
    first_row = lax.broadcasted_iota(jnp.int32, (SUBLANES, LANE), 0) == 0
    ones_row = jnp.where(first_row, 1.0, 0.0).astype(BF16)
    row0 = lax.broadcasted_iota(jnp.int32, (SUBLANES, GLA_DV), 0) == 0
    q_all = rows8(q_ref) * (GLA_DK ** -0.5)
    k_all = rows8(k_ref)
    v_all = rows8(v_ref)
    r_all = r_ref[row, :]
    gn_all = gn_ref[...]
    for h in range(GLA_HEADS):
        ks = slice(h * GLA_DK, (h + 1) * GLA_DK)
        vs = slice(h * GLA_DV, (h + 1) * GLA_DV)
        q, k, v = q_all[:, ks], k_all[:, ks], v_all[:, vs]
        s_old = s_ref[si, h]
        decay = (lax.dot_general(eg1[:, ks], ones_row, _TN, preferred_element_type=F32)
                 + lax.dot_general(eg2[:, ks], ones_row, _TN, preferred_element_type=F32)
                 + lax.dot_general(eg3[:, ks], ones_row, _TN, preferred_element_type=F32))
        v0 = jnp.where(row0, v, 0.0)
        ds = lax.dot_general(k.astype(BF16), v0.astype(BF16), _TN, preferred_element_type=F32)
        s_out_ref[si, h] = jnp.tile(decay, (1, GLA_DV // LANE)) * s_old + ds
        qb = q * eg[:, ks]
        kb = k * eng[:, ks]
        att = jnp.sum(qb * kb, axis=-1, keepdims=True)
        o = jnp.dot(qb.astype(BF16), s_old.astype(BF16), preferred_element_type=F32) + att * v
        o = o[0:1, :]
        ms = jnp.mean(o * o, axis=-1, keepdims=True)
        on = o * lax.rsqrt(ms + EPS) * gn_all[:, vs]
        og_ref[row, vs] = on * _silu(r_all[:, vs])


def _gla_sample(z, wg, bg, gn, state):
    n = z.shape[0]
    zf = z.astype(F32)
    nb = GLA_SAMPLE_BLOCK
    og, s_new = pl.pallas_call(
        _gla_sample_kernel,
        grid=(n // nb,),
        in_specs=[
            pl.BlockSpec((n, GLA_KW), lambda i: (0, Z_Q // GLA_KW)),
            pl.BlockSpec((n, GLA_KW), lambda i: (0, Z_K // GLA_KW)),
            pl.BlockSpec((n, GLA_VW), lambda i: (0, Z_V // GLA_VW)),
            pl.BlockSpec((n, GLA_VW), lambda i: (0, Z_R // GLA_VW)),
            pl.BlockSpec((n, LANE), lambda i: (0, Z_GLR // LANE)),
            pl.BlockSpec((LANE, GLA_KW), lambda i: (0, 0)),
            pl.BlockSpec((1, GLA_KW), lambda i: (0, 0)),
            pl.BlockSpec((1, GLA_VW), lambda i: (0, 0)),
            pl.BlockSpec((nb, GLA_HEADS, GLA_DK, GLA_DV), lambda i: (i, 0, 0, 0)),
        ],
        out_specs=[
            pl.BlockSpec((n, GLA_VW), lambda i: (0, 0)),
            pl.BlockSpec((nb, GLA_HEADS, GLA_DK, GLA_DV), lambda i: (i, 0, 0, 0)),
        ],
        out_shape=[
            jax.ShapeDtypeStruct((n, GLA_VW), F32),
            jax.ShapeDtypeStruct((n, GLA_HEADS, GLA_DK, GLA_DV), F32),
        ],
        compiler_params=_cparams(1),
        name="gla_sample",
    )(zf, zf, zf, zf, zf, wg, bg, gn, state)
    return og.astype(BF16), s_new


def _pool_group_out(mix, g, wp_ref, ps_ref):
    cs = slice(g * POOL_GW, (g + 1) * POOL_GW)
    pm = jnp.dot(mix.astype(BF16), wp_ref[g].astype(BF16), preferred_element_type=F32)
    return pm * ps_ref[:, cs]


def _pool_prompt_kernel(u_ref, wp_ref, ps_ref, o_ref, ext_ref):
    i = pl.program_id(1)
    tt = u_ref.shape[0]
    hist = POOL_BUF + 1

    @pl.when(i == 0)
    def _():
        ext_ref[0:hist, :] = jnp.zeros((hist, POOL_W), F32)

    @pl.when(i > 0)
    def _():
        ext_ref[0:hist, :] = ext_ref[tt:tt + hist, :]

    ext_ref[hist:hist + tt, :] = u_ref[...].astype(F32)
    pos = i * tt + lax.broadcasted_iota(jnp.int32, (tt, 1), 0)
    for g, w in enumerate(POOL_WINDOWS):
        cs = slice(g * POOL_GW, (g + 1) * POOL_GW)
        u = ext_ref[hist:hist + tt, cs]
        s = u
        for d in range(1, w):
            s = s + ext_ref[hist - d:hist - d + tt, cs]
        cnt = jnp.minimum(pos + 1, w).astype(F32)
        mix = s / cnt - u
        o_ref[:, cs] = _pool_group_out(mix, g, wp_ref, ps_ref).astype(o_ref.dtype)


def _pool_prompt(z, w_pool, pool_scale, nb, t, tt):
    m = z.shape[0]
    nt = t // tt
    return pl.pallas_call(
        _pool_prompt_kernel,
        grid=(nb, nt),
        in_specs=[
            pl.BlockSpec((tt, POOL_W), lambda b, i: (b * nt + i, Z_U // POOL_W)),
            pl.BlockSpec((4, POOL_GW, POOL_GW), lambda b, i: (0, 0, 0)),
            pl.BlockSpec((1, POOL_W), lambda b, i: (0, 0)),
        ],
        out_specs=pl.BlockSpec((tt, POOL_W), lambda b, i: (b * nt + i, 0)),
        out_shape=jax.ShapeDtypeStruct((m, POOL_W), BF16),
        scratch_shapes=[pltpu.VMEM((tt + POOL_BUF + 1, POOL_W), F32)],
        compiler_params=_cparams(2),
        name="pool_prompt",
    )(z, w_pool, pool_scale)


def _pool_sample_kernel(u_ref, buf_ref, wp_ref, ps_ref, o_ref):
    for g, w in enumerate(POOL_WINDOWS):
        cs = slice(g * POOL_GW, (g + 1) * POOL_GW)
        u = u_ref[:, cs].astype(F32)
        s = u
        for d in range(1, w):
            s = s + buf_ref[POOL_BUF - d, :, cs]
        mix = s / float(w) - u
        o_ref[:, cs] = _pool_group_out(mix, g, wp_ref, ps_ref).astype(o_ref.dtype)


def _pool_sample(z, buf_t, w_pool, pool_scale):
    n = z.shape[0]
    return pl.pallas_call(
        _pool_sample_kernel,
        grid=(1,),
        in_specs=[
            pl.BlockSpec((n, POOL_W), lambda i: (0, Z_U // POOL_W)),
            pl.BlockSpec((POOL_BUF, n, POOL_W), lambda i: (0, 0, 0)),
            pl.BlockSpec((4, POOL_GW, POOL_GW), lambda i: (0, 0, 0)),
            pl.BlockSpec((1, POOL_W), lambda i: (0, 0)),
        ],
        out_specs=pl.BlockSpec((n, POOL_W), lambda i: (0, 0)),
        out_shape=jax.ShapeDtypeStruct((n, POOL_W), BF16),
        compiler_params=_cparams(1),
        name="pool_sample",
    )(z, buf_t, w_pool, pool_scale)


def _merge_kernel(ogp_ref, ogs_ref, pmp_ref, pms_ref, wa_ref, wb_ref, gap_ref, gas_ref, gbp_ref, gbs_ref,
                  op_ref, os_ref, wabf_ref, wbbf_ref):
    i = pl.program_id(1)

    @pl.when(i == 0)
    def _():
        wabf_ref[...] = wa_ref[...].astype(BF16)
        wbbf_ref[...] = wb_ref[...].astype(BF16)

    def emit(og_ref, pm_ref, ga_ref, gb_ref, o_ref):
        ya = jnp.dot(og_ref[...], wabf_ref[...], preferred_element_type=F32)
        yb = jnp.dot(pm_ref[...], wbbf_ref[...], preferred_element_type=F32)
        ga = ga_ref[...].astype(F32)
        gb = gb_ref[...].astype(F32)
        o_ref[...] = (_sigmoid(ga) * ya + _sigmoid(gb) * yb).astype(o_ref.dtype)

    @pl.when(i == 0)
    def _():
        emit(ogs_ref, pms_ref, gas_ref, gbs_ref, os_ref)

    @pl.when(i > 0)
    def _():
        emit(ogp_ref, pmp_ref, gap_ref, gbp_ref, op_ref)


def _merge(og_p, og_s, pm_p, pm_s, wa, wb, z_p, z_s, *, tm, tn):
    mp = og_p.shape[0]
    ms = og_s.shape[0]
    return pl.pallas_call(
        _merge_kernel,
        grid=(D_MODEL // tn, mp // tm + 1),
        in_specs=[
            *_row_tiled_specs(tm, ms, GLA_VW),
            *_row_tiled_specs(tm, ms, POOL_W),
            pl.BlockSpec((GLA_VW, tn), lambda j, i: (0, j)),
            pl.BlockSpec((POOL_W, tn), lambda j, i: (0, j)),
            *_col_tiled_specs(tm, ms, tn, Z_GA // tn),
            *_col_tiled_specs(tm, ms, tn, Z_GB // tn),
        ],
        out_specs=_col_tiled_specs(tm, ms, tn),
        out_shape=[jax.ShapeDtypeStruct((mp, D_MODEL), BF16), jax.ShapeDtypeStruct((ms, D_MODEL), BF16)],
        scratch_shapes=[pltpu.VMEM((GLA_VW, tn), BF16), pltpu.VMEM((POOL_W, tn), BF16)],
        compiler_params=_cparams(2),
        name="merge",
    )(og_p, og_s, pm_p, pm_s, wa, wb, z_p, z_s, z_p, z_s)


UP_TN = 512
UP_SUB = 256
UP_PARTS = UP_TN // LANE


def _up_kernel(*refs, tiles_per_seq):
    cp_ref, cs_ref, wg_ref = refs[:3]
    wu_refs = refs[3:3 + UP_PARTS]
    (cw_ref, cb_ref, h0_ref, h1_ref, actp_ref, acts_ref, tailp_ref, gates_ref,
     wgbf_ref, wubf_ref, carry_ref) = refs[3 + UP_PARTS:]
    i = pl.program_id(1)

    @pl.when(i == 0)
    def _():
        wgbf_ref[...] = wg_ref[...].astype(BF16)
        for part, wu_ref in enumerate(wu_refs):
            wubf_ref[:, part * LANE:(part + 1) * LANE] = wu_ref[...].astype(BF16)

    def conv_act(gate, up, g1, g2, cols):
        cw = cw_ref[:, cols]
        gconv = cb_ref[:, cols] + cw[0:1] * g2 + cw[1:2] * g1 + cw[2:3] * gate
        return _silu(gconv) * up

    @pl.when(i > 0)
    def _():
        @pl.when((i - 1) % tiles_per_seq == 0)
        def _():
            carry_ref[...] = jnp.zeros_like(carry_ref)

        c = cp_ref[...]
        tm = c.shape[0]
        sub = lax.broadcasted_iota(jnp.int32, (SUBLANES, UP_SUB), 0)
        for h in range(UP_TN // UP_SUB):
            cols = slice(h * UP_SUB, (h + 1) * UP_SUB)
            gate = jnp.dot(c, wgbf_ref[:, cols], preferred_element_type=F32)
            up = jnp.dot(c, wubf_ref[:, cols], preferred_element_type=F32)
            prev = carry_ref[:, cols]
            r1 = pltpu.roll(gate, 1, 0)
            r2 = pltpu.roll(gate, 2, 0)
            head1 = jnp.where(sub < 1, pltpu.roll(prev, 1, 0), r1[0:SUBLANES])
            head2 = jnp.where(sub < 2, pltpu.roll(prev, 2, 0), r2[0:SUBLANES])
            g1 = jnp.concatenate([head1, r1[SUBLANES:]], axis=0)
            g2 = jnp.concatenate([head2, r2[SUBLANES:]], axis=0)
            carry_ref[:, cols] = gate[tm - SUBLANES:tm]
            tailp_ref[0, :, cols] = gate[tm - (CONV_W - 1):tm]
            actp_ref[:, cols] = conv_act(gate, up, g1, g2, cols).astype(actp_ref.dtype)

    @pl.when(i == 0)
    def _():
        c = cs_ref[...]
        gate = jnp.dot(c, wgbf_ref[...], preferred_element_type=F32)
        up = jnp.dot(c, wubf_ref[...], preferred_element_type=F32)
        gates_ref[...] = gate
        acts_ref[...] = conv_act(gate, up, h1_ref[...], h0_ref[...], slice(None)).astype(acts_ref.dtype)


def _up_proj(c_p, c_s, w_up, conv_w, conv_b, hist0, hist1, *, tm, seq_len):
    mp, k = c_p.shape
    ms = c_s.shape[0]
    npt = mp // tm
    tiles_per_seq = seq_len // tm
    nseq = mp // seq_len
    tn = UP_TN
    nj = pl.cdiv(D_FF, tn)
    up_block0 = D_FF // LANE
    last_block = w_up.shape[1] // LANE - 1

    def row_p(j, i):
        return jnp.maximum(i - 1, 0)

    def wu_spec(part):
        return pl.BlockSpec(
            (k, LANE), lambda j, i: (0, jnp.minimum(up_block0 + j * UP_PARTS + part, last_block)))

    in_specs = [
        pl.BlockSpec((tm, k), lambda j, i: (row_p(j, i), 0)),
        pl.BlockSpec((ms, k), lambda j, i: (0, 0)),
        pl.BlockSpec((k, tn), lambda j, i: (0, j)),
        *[wu_spec(part) for part in range(UP_PARTS)],
        pl.BlockSpec((CONV_W, tn), lambda j, i: (0, j)),
        pl.BlockSpec((1, tn), lambda j, i: (0, j)),
        pl.BlockSpec((ms, tn), lambda j, i: (0, j)),
        pl.BlockSpec((ms, tn), lambda j, i: (0, j)),
    ]
    out_specs = [
        pl.BlockSpec((tm, tn), lambda j, i: (row_p(j, i), j)),
        pl.BlockSpec((ms, tn), lambda j, i: (0, j)),
        pl.BlockSpec((1, CONV_W - 1, tn), lambda j, i: (row_p(j, i) // tiles_per_seq, 0, j)),
        pl.BlockSpec((ms, tn), lambda j, i: (0, j)),
    ]
    out_shape = [
        jax.ShapeDtypeStruct((mp, D_FF), BF16),
        jax.ShapeDtypeStruct((ms, D_FF), BF16),
        jax.ShapeDtypeStruct((nseq, CONV_W - 1, D_FF), F32),
        jax.ShapeDtypeStruct((ms, D_FF), F32),
    ]
    return pl.pallas_call(
        functools.partial(_up_kernel, tiles_per_seq=tiles_per_seq),
        grid=(nj, npt + 1),
        in_specs=in_specs,
        out_specs=out_specs,
        out_shape=out_shape,
        scratch_shapes=[pltpu.VMEM((k, tn), BF16), pltpu.VMEM((k, tn), BF16), pltpu.VMEM((SUBLANES, tn), F32)],
        compiler_params=_cparams(2),
        name="up_proj",
    )(c_p, c_s, w_up, *([w_up] * UP_PARTS), conv_w, conv_b, hist0, hist1)


def _rms(x, g):
    return x * lax.rsqrt(jnp.mean(x * x, axis=-1, keepdims=True) + EPS) * g


def _two_group_specs(tm, ms, widths):
    specs = []
    for w in widths:
        specs.append(pl.BlockSpec((tm, w), lambda i: (jnp.maximum(i - 1, 0), 0)))
        specs.append(pl.BlockSpec((ms, w), lambda i: (0, 0)))
    return specs


def _resident(shape):
    return pl.BlockSpec(shape, lambda i: (0,) * len(shape), pipeline_mode=pl.Buffered(1))


def _out_norm_kernel(mp_ref, ms_ref, xp_ref, xs_ref, w_ref, g_ref, hp_ref, hs_ref, cp_ref, cs_ref, wbf_ref):
    i = pl.program_id(0)

    @pl.when(i == 0)
    def _():
        wbf_ref[...] = w_ref[...].astype(BF16)

    def emit(m_ref, x_ref, h_ref, c_ref):
        h = x_ref[...] + jnp.dot(m_ref[...], wbf_ref[...], preferred_element_type=F32)
        h_ref[...] = h
        c_ref[...] = _rms(h, g_ref[...]).astype(c_ref.dtype)

    @pl.when(i == 0)
    def _():
        emit(ms_ref, xs_ref, hs_ref, cs_ref)

    @pl.when(i > 0)
    def _():
        emit(mp_ref, xp_ref, hp_ref, cp_ref)


def _out_norm(m_p, m_s, x_p, x_s, w_out, g, *, tm):
    mp, d = x_p.shape
    ms = x_s.shape[0]
    return pl.pallas_call(
        _out_norm_kernel,
        grid=(mp // tm + 1,),
        in_specs=[*_two_group_specs(tm, ms, (d, d)), _resident((d, d)), _resident((1, d))],
        out_specs=_two_group_specs(tm, ms, (d, d)),
        out_shape=[
            jax.ShapeDtypeStruct((mp, d), F32), jax.ShapeDtypeStruct((ms, d), F32),
            jax.ShapeDtypeStruct((mp, d), BF16), jax.ShapeDtypeStruct((ms, d), BF16),
        ],
        scratch_shapes=[pltpu.VMEM((d, d), BF16)],
        compiler_params=_cparams(1),
        name="out_norm",
    )(m_p, m_s, x_p, x_s, w_out, g.reshape(1, d))


def _ple_kernel(hp_ref, hs_ref, pp_ref, ps_ref, wpg_ref, wple_ref, gple_ref, gfin_ref, yp_ref, ys_ref,
                wpgbf_ref, wplebf_ref, *, final_norm):
    i = pl.program_id(0)

    @pl.when(i == 0)
    def _():
        wpgbf_ref[...] = wpg_ref[...].astype(BF16)
        wplebf_ref[...] = wple_ref[...].astype(BF16)

    def emit(h_ref, p_ref, y_ref):
        h2 = h_ref[...]
        cn = _rms(h2, gple_ref[...]).astype(BF16)
        pg = _sigmoid(jnp.dot(cn, wpgbf_ref[...], preferred_element_type=F32))
        e = jnp.dot(p_ref[...].astype(BF16), wplebf_ref[...], preferred_element_type=F32)
        h3 = h2 + pg * e
        y_ref[...] = _rms(h3, gfin_ref[...]) if final_norm else h3

    @pl.when(i == 0)
    def _():
        emit(hs_ref, ps_ref, ys_ref)

    @pl.when(i > 0)
    def _():
        emit(hp_ref, pp_ref, yp_ref)


def _ple(h_p, h_s, p_p, p_s, wpg, wple, g_ple, g_final, *, tm, final_norm):
    mp, d = h_p.shape
    ms = h_s.shape[0]
    pd = p_p.shape[1]
    return pl.pallas_call(
        functools.partial(_ple_kernel, final_norm=final_norm),
        grid=(mp // tm + 1,),
        in_specs=[*_two_group_specs(tm, ms, (d, pd)), _resident((d, d)), _resident((pd, d)),
                  _resident((1, d)), _resident((1, d))],
        out_specs=_two_group_specs(tm, ms, (d,)),
        out_shape=[jax.ShapeDtypeStruct((mp, d), F32), jax.ShapeDtypeStruct((ms, d), F32)],
        scratch_shapes=[pltpu.VMEM((d, d), BF16), pltpu.VMEM((pd, d), BF16)],
        compiler_params=_cparams(1),
        name="ple",
    )(h_p, h_s, p_p, p_s, wpg, wple, g_ple.reshape(1, d), g_final.reshape(1, d))


def _mixers(z, lw, *, prompt_shape=None, state=None):
    sample = state is not None
    if sample:
        s_gla, s_pool = state
        og, gla_new = _gla_sample(z, lw["wg"], lw["bg"], lw["gn"], s_gla)
        pm = _pool_sample(z, jnp.transpose(s_pool, (1, 0, 2)), lw["w_pool"], lw["pool_scale"])
        u = z[:, Z_U:Z_U + POOL_W].astype(F32)
        pool_new = jnp.concatenate([s_pool[:, 1:], u[:, None, :]], axis=1)
    else:
        nb, t = prompt_shape
        og, gla_new = _gla_prompt(z, lw["wg"], lw["bg"], lw["gn"], nb, t)
        pm = _pool_prompt(z, lw["w_pool"], lw["pool_scale"], nb, t, 1024)
        pool_new = z.reshape(nb, t, Z_WIDTH)[:, t - POOL_BUF:, Z_U:Z_U + POOL_W].astype(F32)
    return og, pm, gla_new, pool_new


def kernel(x_prompt, x_sample, state_gla, state_pool, state_conv, p_prompt, p_sample, norm_mix, w_in, w_gate_up, b_gate, gla_norm, w_branch_a, w_pool, pool_scale, w_branch_b, w_out, norm_ffn, w_up, conv_w, conv_b, w_down, norm_ple, w_ple_gate, w_ple, norm_final):
    depth = w_in.shape[0]
    nb, t, d = x_prompt.shape
    ns = x_sample.shape[0]
    hp = x_prompt.reshape(nb * t, d)
    hs = x_sample.reshape(ns, d)
    outs = {k: [] for k in ("gp", "pp", "cp", "gs", "ps", "cs")}
    for li in range(depth):
        lw = {
            "wg": jnp.pad(w_gate_up[li], ((0, LANE - GATE_RANK), (0, 0))).astype(BF16),
            "bg": b_gate[li].reshape(1, GLA_KW),
            "gn": gla_norm[li].reshape(1, GLA_VW),
            "w_pool": w_pool[li],
            "pool_scale": pool_scale[li].reshape(1, POOL_W),
        }
        a_p = _rmsnorm(hp, norm_mix[li], BF16, 1024)
        a_s = _rmsnorm(hs, norm_mix[li], BF16, ns)
        z_p, z_s = _in_proj(a_p, a_s, w_in[li].T, tm=2048)
        og_p, pm_p, g1, p1 = _mixers(z_p, lw, prompt_shape=(nb, t))
        og_s, pm_s, g2, p2 = _mixers(z_s, lw, state=(state_gla[li], state_pool[li]))
        mg_p, mg_s = _merge(og_p, og_s, pm_p, pm_s, w_branch_a[li], w_branch_b[li], z_p, z_s, tm=1024, tn=512)
        h1_p, h1_s, c_p, c_s = _out_norm(mg_p, mg_s, hp, hs, w_out[li], norm_ffn[li], tm=512)
        s_conv = state_conv[li]
        act_p, act_s, c1, gate_s = _up_proj(c_p, c_s, w_up[li], conv_w[li], conv_b[li].reshape(1, D_FF),
                                            s_conv[:, 0], s_conv[:, 1], tm=2048, seq_len=t)
        c2 = jnp.stack([s_conv[:, 1], gate_s], axis=1)
        h2_p, h2_s = _down_proj(act_p, act_s, w_down[li], h1_p, h1_s, tm=512, tn=512)
        hp, hs = _ple(h2_p, h2_s, p_prompt[li].reshape(nb * t, PLE_DIM), p_sample[li].reshape(ns, PLE_DIM),
                      w_ple_gate[li], w_ple[li], norm_ple[li], norm_final, tm=512,
                      final_norm=li == depth - 1)
        for key, val in zip(("gp", "pp", "cp", "gs", "ps", "cs"), (g1, p1, c1, g2, p2, c2)):
            outs[key].append(val)
    st = {k: jnp.stack(v, axis=0) for k, v in outs.items()}
    return (hp.reshape(nb, t, d), hs.reshape(ns, 1, d),
            st["gp"], st["pp"], st["cp"], st["gs"], st["ps"], st["cs"])
```

```python
import functools

import jax
import jax.numpy as jnp
from jax import lax
from jax.experimental import pallas as pl
from jax.experimental.pallas import tpu as pltpu

F32 = jnp.float32
BF16 = jnp.bfloat16

D_MODEL = 2048
GLA_HEADS = 4
GLA_DK = 256
GLA_DV = 512
GLA_KW = GLA_HEADS * GLA_DK
GLA_VW = GLA_HEADS * GLA_DV
GATE_RANK = 16
GATE_TEMP = 16.0
GLA_CHUNK = 64
POOL_WINDOWS = (2, 4, 8, 16)
POOL_GW = 256
POOL_W = 1024
POOL_BUF = 15
D_FF = 5504
CONV_W = 3
PLE_DIM = 256
EPS = 1e-6

LANE = 128
SUBLANES = 8
Z_Q, Z_K, Z_V, Z_R, Z_U, Z_GA, Z_GB, Z_GLR, Z_WIDTH = 0, 1024, 2048, 4096, 6144, 7168, 9216, 11264, 11392
GLR_SRC = 6144
IN_TN = 1024
IN_SUB = 512

VMEM_LIMIT = 58 * 1024 * 1024


def _cparams(n_axes):
    return pltpu.CompilerParams(dimension_semantics=("arbitrary",) * n_axes, vmem_limit_bytes=VMEM_LIMIT)


def _sigmoid(x):
    return 1.0 / (1.0 + jnp.exp(-x))


def _silu(x):
    return x * _sigmoid(x)


def _log_sigmoid(x):
    return jnp.minimum(x, 0.0) - jnp.log(1.0 + jnp.exp(-jnp.abs(x)))


def _rms_kernel(x_ref, g_ref, o_ref):
    x = x_ref[...]
    ms = jnp.mean(x * x, axis=-1, keepdims=True)
    o_ref[...] = (x * lax.rsqrt(ms + EPS) * g_ref[...]).astype(o_ref.dtype)


def _rmsnorm(x, g, out_dtype, tm):
    m, d = x.shape
    return pl.pallas_call(
        _rms_kernel,
        grid=(m // tm,),
        in_specs=[pl.BlockSpec((tm, d), lambda i: (i, 0)), pl.BlockSpec((1, d), lambda i: (0, 0))],
        out_specs=pl.BlockSpec((tm, d), lambda i: (i, 0)),
        out_shape=jax.ShapeDtypeStruct((m, d), out_dtype),
        compiler_params=_cparams(1),
        name="rmsnorm",
    )(x, g.reshape(1, d))


def _col_tiled_specs(tm, ms, width, col0=0):
    return [
        pl.BlockSpec((tm, width), lambda j, i: (jnp.maximum(i - 1, 0), col0 + j)),
        pl.BlockSpec((ms, width), lambda j, i: (0, col0 + j)),
    ]


def _row_tiled_specs(tm, ms, width):
    return [
        pl.BlockSpec((tm, width), lambda j, i: (jnp.maximum(i - 1, 0), 0)),
        pl.BlockSpec((ms, width), lambda j, i: (0, 0)),
    ]


def _down_kernel(ap_ref, as_ref, w_ref, rp_ref, rs_ref, op_ref, os_ref, wbf_ref):
    i = pl.program_id(1)

    @pl.when(i == 0)
    def _():
        wbf_ref[...] = w_ref[...].astype(BF16)
        os_ref[...] = rs_ref[...] + jnp.dot(as_ref[...], wbf_ref[...], preferred_element_type=F32)

    @pl.when(i > 0)
    def _():
        op_ref[...] = rp_ref[...] + jnp.dot(ap_ref[...], wbf_ref[...], preferred_element_type=F32)


def _down_proj(a_p, a_s, w, res_p, res_s, *, tm, tn):
    mp, k = a_p.shape
    ms = a_s.shape[0]
    n = w.shape[1]
    return pl.pallas_call(
        _down_kernel,
        grid=(n // tn, mp // tm + 1),
        in_specs=[
            *_row_tiled_specs(tm, ms, k),
            pl.BlockSpec((k, tn), lambda j, i: (0, j)),
            *_col_tiled_specs(tm, ms, tn),
        ],
        out_specs=_col_tiled_specs(tm, ms, tn),
        out_shape=[jax.ShapeDtypeStruct((mp, n), F32), jax.ShapeDtypeStruct((ms, n), F32)],
        scratch_shapes=[pltpu.VMEM((k, tn), BF16)],
        compiler_params=_cparams(2),
        name="down_proj",
    )(a_p, a_s, w, res_p, res_s)


_NT = (((1,), (1,)), ((), ()))
_TN = (((0,), (0,)), ((), ()))


def _in_proj_kernel(ap_ref, as_ref, w_ref, wx_ref, zp_ref, zs_ref, wbf_ref):
    j = pl.program_id(0)
    i = pl.program_id(1)
    last = pl.num_programs(0) - 1
    first_shifted = GLR_SRC // IN_TN
    keep = IN_TN - GATE_RANK

    @pl.when(i == 0)
    def _():
        @pl.when(j < first_shifted)
        def _():
            wbf_ref[...] = w_ref[...].astype(BF16)

        @pl.when(jnp.logical_and(j >= first_shifted, j < last))
        def _():
            wbf_ref[:keep, :] = w_ref[GATE_RANK:, :].astype(BF16)
            wbf_ref[keep:, :] = wx_ref[...].astype(BF16)

        @pl.when(j == last)
        def _():
            wbf_ref[:GATE_RANK, :] = wx_ref[...].astype(BF16)
            wbf_ref[GATE_RANK:LANE, :] = jnp.zeros((LANE - GATE_RANK, wbf_ref.shape[1]), BF16)

    def emit(a_ref, z_ref):
        @pl.when(j < last)
        def _():
            a = a_ref[...]
            for part in range(IN_TN // IN_SUB):
                cols = slice(part * IN_SUB, (part + 1) * IN_SUB)
                z_ref[:, cols] = lax.dot_general(a, wbf_ref[cols, :], _NT,
                                                 preferred_element_type=F32).astype(z_ref.dtype)

        @pl.when(j == last)
        def _():
            z_ref[:, :LANE] = lax.dot_general(a_ref[...], wbf_ref[:LANE, :], _NT,
                                              preferred_element_type=F32).astype(z_ref.dtype)

    @pl.when(i == 0)
    def _():
        emit(as_ref, zs_ref)

    @pl.when(i > 0)
    def _():
        emit(ap_ref, zp_ref)


def _in_proj(a_p, a_s, wt, *, tm):
    mp, k = a_p.shape
    ms = a_s.shape[0]
    npt = mp // tm
    nj = Z_GLR // IN_TN + 1
    aux_per_tile = IN_TN // GATE_RANK

    def row_p(j, i):
        return jnp.maximum(i - 1, 0)

    def aux_idx(j, i):
        return (jnp.where(j == nj - 1, GLR_SRC // GATE_RANK, (j + 1) * aux_per_tile), 0)

    return pl.pallas_call(
        _in_proj_kernel,
        grid=(nj, npt + 1),
        in_specs=[
            pl.BlockSpec((tm, k), lambda j, i: (row_p(j, i), 0)),
            pl.BlockSpec((ms, k), lambda j, i: (0, 0)),
            pl.BlockSpec((IN_TN, k), lambda j, i: (j, 0)),
            pl.BlockSpec((GATE_RANK, k), aux_idx),
        ],
        out_specs=[
            pl.BlockSpec((tm, IN_TN), lambda j, i: (row_p(j, i), j)),
            pl.BlockSpec((ms, IN_TN), lambda j, i: (0, j)),
        ],
        out_shape=[jax.ShapeDtypeStruct((mp, Z_WIDTH), BF16), jax.ShapeDtypeStruct((ms, Z_WIDTH), BF16)],
        scratch_shapes=[pltpu.VMEM((IN_TN, k), BF16)],
        compiler_params=_cparams(2),
        name="in_proj",
    )(a_p, a_s, wt, wt)


GLA_GROUP = 16


GLA_HEADS_PER_STEP = 1


def _gla_prompt_kernel(q_ref, k_ref, v_ref, r_ref, glr_ref, wg_ref, bg_ref, gn_ref,
                       og_ref, s_out_ref, st_ref):
    t = q_ref.shape[0]
    c = GLA_CHUNK
    rows = GLA_GROUP * c
    nh = GLA_HEADS_PER_STEP
    st_ref[...] = jnp.zeros_like(st_ref)
    row = lax.broadcasted_iota(jnp.int32, (c, c), 0)
    col = lax.broadcasted_iota(jnp.int32, (c, c), 1)
    causal = row >= col
    tri = causal.astype(BF16)

    def group(gi, carry):
        r0 = pl.multiple_of(gi * rows, rows)
        rs = pl.ds(r0, rows)
        glr = glr_ref[rs, :]

        for h in range(nh):
            ks = slice(h * GLA_DK, (h + 1) * GLA_DK)
            vs = slice(h * GLA_DV, (h + 1) * GLA_DV)
            zg = jnp.dot(glr, wg_ref[:, ks], preferred_element_type=F32) + bg_ref[:, ks]
            v = v_ref[rs, vs]
            vals = [dict() for _ in range(GLA_GROUP)]
            state = {"st": st_ref[h]}

            def s_gate(ci, d):
                gl = _log_sigmoid(zg[ci * c:(ci + 1) * c]) * (1.0 / GATE_TEMP)
                d["gl_hi"] = gl.astype(BF16)
                d["gl_lo"] = (gl - d["gl_hi"].astype(F32)).astype(BF16)

            def s_cumsum(ci, d):
                d["b"] = (jnp.dot(tri, d.pop("gl_hi"), preferred_element_type=F32)
                          + jnp.dot(tri, d.pop("gl_lo"), preferred_element_type=F32))

            def s_scale(ci, d):
                cr = pl.ds(r0 + ci * c, c)
                b = d.pop("b")
                bl = b[c - 1:c, :]
                q = q_ref[cr, ks].astype(F32) * (GLA_DK ** -0.5)
                k = k_ref[cr, ks].astype(F32)
                d["qb"] = (q * jnp.exp(b)).astype(BF16)
                d["kb"] = (k * jnp.exp(-b)).astype(BF16)
                d["kd"] = (k * jnp.exp(bl - b)).astype(BF16)
                d["decay"] = jnp.exp(bl)

            def s_att(ci, d):
                sl = slice(ci * c, (ci + 1) * c)
                d["att"] = lax.dot_general(d["qb"], d.pop("kb"), _NT, preferred_element_type=F32)
                d["dst"] = lax.dot_general(v[sl], d.pop("kd"), _TN, preferred_element_type=F32)

            def s_mask(ci, d):
                d["att"] = jnp.where(causal, d["att"], 0.0).astype(BF16)

            def s_intra(ci, d):
                sl = slice(ci * c, (ci + 1) * c)
                d["o"] = jnp.dot(d.pop("att"), v[sl], preferred_element_type=F32)

            def s_chain(ci, d):
                st = state["st"]
                d["st_in"] = st.astype(BF16)
                state["st"] = st * d.pop("decay") + d.pop("dst")

            def s_inter(ci, d):
                d["o"] = d["o"] + lax.dot_general(d.pop("qb"), d.pop("st_in"), _NT, preferred_element_type=F32)

            def s_finish(ci, d):
                cr = pl.ds(r0 + ci * c, c)
                on = _rms(d.pop("o"), gn_ref[:, vs])
                og_ref[cr, vs] = (on * _silu(r_ref[cr, vs].astype(F32))).astype(og_ref.dtype)

            stages = (s_gate, s_cumsum, s_scale, s_att, s_mask, s_intra, s_chain, s_inter, s_finish)
            for step in range(GLA_GROUP + len(stages) - 1):
                for si, stage in enumerate(stages):
                    ci = step - si
                    if 0 <= ci < GLA_GROUP:
                        stage(ci, vals[ci])
            st_ref[h] = state["st"]
        return carry

    lax.fori_loop(0, t // rows, group, 0)
    for h in range(nh):
        s_out_ref[0, h] = st_ref[h].T


def _gla_prompt(z, wg, bg, gn, nb, t):
    m = z.shape[0]
    nh = GLA_HEADS_PER_STEP
    kw, vw = nh * GLA_DK, nh * GLA_DV
    og, s_new = pl.pallas_call(
        _gla_prompt_kernel,
        grid=(nb, GLA_HEADS // nh),
        in_specs=[
            pl.BlockSpec((t, kw), lambda b, h: (b, Z_Q // kw + h)),
            pl.BlockSpec((t, kw), lambda b, h: (b, Z_K // kw + h)),
            pl.BlockSpec((t, vw), lambda b, h: (b, Z_V // vw + h)),
            pl.BlockSpec((t, vw), lambda b, h: (b, Z_R // vw + h)),
            pl.BlockSpec((t, LANE), lambda b, h: (b, Z_GLR // LANE)),
            pl.BlockSpec((LANE, kw), lambda b, h: (0, h)),
            pl.BlockSpec((1, kw), lambda b, h: (0, h)),
            pl.BlockSpec((1, vw), lambda b, h: (0, h)),
        ],
        out_specs=[
            pl.BlockSpec((t, vw), lambda b, h: (b, h)),
            pl.BlockSpec((1, nh, GLA_DK, GLA_DV), lambda b, h: (b, h, 0, 0)),
        ],
        out_shape=[
            jax.ShapeDtypeStruct((m, GLA_VW), BF16),
            jax.ShapeDtypeStruct((nb, GLA_HEADS, GLA_DK, GLA_DV), F32),
        ],
        scratch_shapes=[pltpu.VMEM((nh, GLA_DV, GLA_DK), F32)],
        compiler_params=_cparams(2),
        name="gla_prompt",
    )(z, z, z, z, z, wg, bg, gn)
    return og, s_new


GLA_SAMPLE_BLOCK = 4


def _gla_sample_kernel(q_ref, k_ref, v_ref, r_ref, glr_ref, wg_ref, bg_ref, gn_ref, s_ref,
                       og_ref, s_out_ref):
    def one_sequence(si, carry):
        _gla_sample_sequence(pl.program_id(0) * GLA_SAMPLE_BLOCK + si, si, q_ref, k_ref, v_ref, r_ref, glr_ref,
                             wg_ref, bg_ref, gn_ref, s_ref, og_ref, s_out_ref)
        return carry

    lax.fori_loop(0, GLA_SAMPLE_BLOCK, one_sequence, 0)


def _gla_sample_sequence(n, si, q_ref, k_ref, v_ref, r_ref, glr_ref, wg_ref, bg_ref, gn_ref, s_ref,
                         og_ref, s_out_ref):
    row = pl.ds(n, 1)

    def rows8(ref):
        return jnp.broadcast_to(ref[row, :], (SUBLANES, ref.shape[1]))

    zg = jnp.dot(rows8(glr_ref).astype(BF16), wg_ref[...], preferred_element_type=F32) + bg_ref[...]
    g = _log_sigmoid(zg) * (1.0 / GATE_TEMP)
    eg = jnp.exp(g)
    eng = jnp.exp(-g)
    eg1 = eg.astype(BF16)
    rem = eg - eg1.astype(F32)
    eg2 = rem.astype(BF16)
    eg3 = (rem - eg2.astype(F32)).astype(BF16)
    first_row = lax.broadcasted_iota(jnp.int32, (SUBLANES, LANE), 0) == 0
    ones_row = jnp.where(first_row, 1.0, 0.0).astype(BF16)
    row0 = lax.broadcasted_iota(jnp.int32, (SUBLANES, GLA_DV), 0) == 0
    q_all = rows8(q_ref) * (GLA_DK ** -0.5)
    k_all = rows8(k_ref)
    v_all = rows8(v_ref)
    r_all = r_ref[row, :]
    gn_all = gn_ref[...]
    for h in range(GLA_HEADS):
        ks = slice(h * GLA_DK, (h + 1) * GLA_DK)
        vs = slice(h * GLA_DV, (h + 1) * GLA_DV)
        q, k, v = q_all[:, ks], k_all[:, ks], v_all[:, vs]
        s_old = s_ref[si, h]
        decay = (lax.dot_general(eg1[:, ks], ones_row, _TN, preferred_element_type=F32)
                 + lax.dot_general(eg2[:, ks], ones_row, _TN, preferred_element_type=F32)
                 + lax.dot_general(eg3[:, ks], ones_row, _TN, preferred_element_type=F32))
        v0 = jnp.where(row0, v, 0.0)
        ds = lax.dot_general(k.astype(BF16), v0.astype(BF16), _TN, preferred_element_type=F32)
        s_out_ref[si, h] = jnp.tile(decay, (1, GLA_DV // LANE)) * s_old + ds
        qb = q * eg[:, ks]
        kb = k * eng[:, ks]
        att = jnp.sum(qb * kb, axis=-1, keepdims=True)
        o = jnp.dot(qb.astype(BF16), s_old.astype(BF16), preferred_element_type=F32) + att * v
        o = o[0:1, :]
        ms = jnp.mean(o * o, axis=-1, keepdims=True)
        on = o * lax.rsqrt(ms + EPS) * gn_all[:, vs]
        og_ref[row, vs] = on * _silu(r_all[:, vs])


def _gla_sample(z, wg, bg, gn, state):
    n = z.shape[0]
    zf = z.astype(F32)
    nb = GLA_SAMPLE_BLOCK
    og, s_new = pl.pallas_call(
        _gla_sample_kernel,
        grid=(n // nb,),
        in_specs=[
            pl.BlockSpec((n, GLA_KW), lambda i: (0, Z_Q // GLA_KW)),
            pl.BlockSpec((n, GLA_KW), lambda i: (0, Z_K // GLA_KW)),
            pl.BlockSpec((n, GLA_VW), lambda i: (0, Z_V // GLA_VW)),
            pl.BlockSpec((n, GLA_VW), lambda i: (0, Z_R // GLA_VW)),
            pl.BlockSpec((n, LANE), lambda i: (0, Z_GLR // LANE)),
            pl.BlockSpec((LANE, GLA_KW), lambda i: (0, 0)),
            pl.BlockSpec((1, GLA_KW), lambda i: (0, 0)),
            pl.BlockSpec((1, GLA_VW), lambda i: (0, 0)),
            pl.BlockSpec((nb, GLA_HEADS, GLA_DK, GLA_DV), lambda i: (i, 0, 0, 0)),
        ],
        out_specs=[
            pl.BlockSpec((n, GLA_VW), lambda i: (0, 0)),
            pl.BlockSpec((nb, GLA_HEADS, GLA_DK, GLA_DV), lambda i: (i, 0, 0, 0)),
        ],
        out_shape=[
            jax.ShapeDtypeStruct((n, GLA_VW), F32),
            jax.ShapeDtypeStruct((n, GLA_HEADS, GLA_DK, GLA_DV), F32),
        ],
        compiler_params=_cparams(1),
        name="gla_sample",
    )(zf, zf, zf, zf, zf, wg, bg, gn, state)
    return og.astype(BF16), s_new


def _pool_group_out(mix, g, wp_ref, ps_ref):
    cs = slice(g * POOL_GW, (g + 1) * POOL_GW)
    pm = jnp.dot(mix.astype(BF16), wp_ref[g].astype(BF16), preferred_element_type=F32)
    return pm * ps_ref[:, cs]


def _pool_prompt_kernel(u_ref, wp_ref, ps_ref, o_ref, ext_ref):
    i = pl.program_id(1)
    tt = u_ref.shape[0]
    hist = POOL_BUF + 1

    @pl.when(i == 0)
    def _():
        ext_ref[0:hist, :] = jnp.zeros((hist, POOL_W), F32)

    @pl.when(i > 0)
    def _():
        ext_ref[0:hist, :] = ext_ref[tt:tt + hist, :]

    ext_ref[hist:hist + tt, :] = u_ref[...].astype(F32)
    pos = i * tt + lax.broadcasted_iota(jnp.int32, (tt, 1), 0)
    for g, w in enumerate(POOL_WINDOWS):
        cs = slice(g * POOL_GW, (g + 1) * POOL_GW)
        u = ext_ref[hist:hist + tt, cs]
        s = u
        for d in range(1, w):
            s = s + ext_ref[hist - d:hist - d + tt, cs]
        cnt = jnp.minimum(pos + 1, w).astype(F32)
        mix = s / cnt - u
        o_ref[:, cs] = _pool_group_out(mix, g, wp_ref, ps_ref).astype(o_ref.dtype)


def _pool_prompt(z, w_pool, pool_scale, nb, t, tt):
    m = z.shape[0]
    nt = t // tt
    return pl.pallas_call(
        _pool_prompt_kernel,
        grid=(nb, nt),
        in_specs=[
            pl.BlockSpec((tt, POOL_W), lambda b, i: (b * nt + i, Z_U // POOL_W)),
            pl.BlockSpec((4, POOL_GW, POOL_GW), lambda b, i: (0, 0, 0)),
            pl.BlockSpec((1, POOL_W), lambda b, i: (0, 0)),
        ],
        out_specs=pl.BlockSpec((tt, POOL_W), lambda b, i: (b * nt + i, 0)),
        out_shape=jax.ShapeDtypeStruct((m, POOL_W), BF16),
        scratch_shapes=[pltpu.VMEM((tt + POOL_BUF + 1, POOL_W), F32)],
        compiler_params=_cparams(2),
        name="pool_prompt",
    )(z, w_pool, pool_scale)


def _pool_sample_kernel(u_ref, buf_ref, wp_ref, ps_ref, o_ref):
    for g, w in enumerate(POOL_WINDOWS):
        cs = slice(g * POOL_GW, (g + 1) * POOL_GW)
        u = u_ref[:, cs].astype(F32)
        s = u
        for d in range(1, w):
            s = s + buf_ref[POOL_BUF - d, :, cs]
        mix = s / float(w) - u
        o_ref[:, cs] = _pool_group_out(mix, g, wp_ref, ps_ref).astype(o_ref.dtype)


def _pool_sample(z, buf_t, w_pool, pool_scale):
    n = z.shape[0]
    return pl.pallas_call(
        _pool_sample_kernel,
        grid=(1,),
        in_specs=[
            pl.BlockSpec((n, POOL_W), lambda i: (0, Z_U // POOL_W)),
            pl.BlockSpec((POOL_BUF, n, POOL_W), lambda i: (0, 0, 0)),
            pl.BlockSpec((4, POOL_GW, POOL_GW), lambda i: (0, 0, 0)),
            pl.BlockSpec((1, POOL_W), lambda i: (0, 0)),
        ],
        out_specs=pl.BlockSpec((n, POOL_W), lambda i: (0, 0)),
        out_shape=jax.ShapeDtypeStruct((n, POOL_W), BF16),
        compiler_params=_cparams(1),
        name="pool_sample",
    )(z, buf_t, w_pool, pool_scale)


def _merge_kernel(ogp_ref, ogs_ref, pmp_ref, pms_ref, wa_ref, wb_ref, gap_ref, gas_ref, gbp_ref, gbs_ref,
                  op_ref, os_ref, wabf_ref, wbbf_ref):
    i = pl.program_id(1)

    @pl.when(i == 0)
    def _():
        wabf_ref[...] = wa_ref[...].astype(BF16)
        wbbf_ref[...] = wb_ref[...].astype(BF16)

    def emit(og_ref, pm_ref, ga_ref, gb_ref, o_ref):
        ya = jnp.dot(og_ref[...], wabf_ref[...], preferred_element_type=F32)
        yb = jnp.dot(pm_ref[...], wbbf_ref[...], preferred_element_type=F32)
        ga = ga_ref[...].astype(F32)
        gb = gb_ref[...].astype(F32)
        o_ref[...] = (_sigmoid(ga) * ya + _sigmoid(gb) * yb).astype(o_ref.dtype)

    @pl.when(i == 0)
    def _():
        emit(ogs_ref, pms_ref, gas_ref, gbs_ref, os_ref)

    @pl.when(i > 0)
    def _():
        emit(ogp_ref, pmp_ref, gap_ref, gbp_ref, op_ref)


def _merge(og_p, og_s, pm_p, pm_s, wa, wb, z_p, z_s, *, tm, tn):
    mp = og_p.shape[0]
    ms = og_s.shape[0]
    return pl.pallas_call(
        _merge_kernel,
        grid=(D_MODEL // tn, mp // tm + 1),
        in_specs=[
            *_row_tiled_specs(tm, ms, GLA_VW),
            *_row_tiled_specs(tm, ms, POOL_W),
            pl.BlockSpec((GLA_VW, tn), lambda j, i: (0, j)),
            pl.BlockSpec((POOL_W, tn), lambda j, i: (0, j)),
            *_col_tiled_specs(tm, ms, tn, Z_GA // tn),
            *_col_tiled_specs(tm, ms, tn, Z_GB // tn),
        ],
        out_specs=_col_tiled_specs(tm, ms, tn),
        out_shape=[jax.ShapeDtypeStruct((mp, D_MODEL), BF16), jax.ShapeDtypeStruct((ms, D_MODEL), BF16)],
        scratch_shapes=[pltpu.VMEM((GLA_VW, tn), BF16), pltpu.VMEM((POOL_W, tn), BF16)],
        compiler_params=_cparams(2),
        name="merge",
    )(og_p, og_s, pm_p, pm_s, wa, wb, z_p, z_s, z_p, z_s)


UP_TN = 512
UP_SUB = 256
UP_ROWS = 1024
UP_PARTS = UP_TN // LANE


def _up_kernel(*refs, tiles_per_seq):
    cp_ref, cs_ref, wg_ref = refs[:3]
    wu_refs = refs[3:3 + UP_PARTS]
    (cw_ref, cb_ref, h0_ref, h1_ref, actp_ref, acts_ref, tailp_ref, gates_ref,
     wgbf_ref, wubf_ref, carry_ref) = refs[3 + UP_PARTS:]
    i = pl.program_id(1)

    @pl.when(i == 0)
    def _():
        wgbf_ref[...] = wg_ref[...].astype(BF16)
        for part, wu_ref in enumerate(wu_refs):
            wubf_ref[:, part * LANE:(part + 1) * LANE] = wu_ref[...].astype(BF16)

    def conv_act(gate, up, g1, g2, cols):
        cw = cw_ref[:, cols]
        gconv = cb_ref[:, cols] + cw[0:1] * g2 + cw[1:2] * g1 + cw[2:3] * gate
        return _silu(gconv) * up

    @pl.when(i > 0)
    def _():
        @pl.when((i - 1) % tiles_per_seq == 0)
        def _():
            carry_ref[...] = jnp.zeros_like(carry_ref)

        tm = cp_ref.shape[0]
        rows = min(tm, UP_ROWS)
        sub = lax.broadcasted_iota(jnp.int32, (SUBLANES, UP_SUB), 0)
        n_cols = UP_TN // UP_SUB
        prev = [carry_ref[:, h * UP_SUB:(h + 1) * UP_SUB] for h in range(n_cols)]
        for rt in range(tm // rows):
            rsl = slice(rt * rows, (rt + 1) * rows)
            c = cp_ref[rsl, :]
            for h in range(n_cols):
                cols = slice(h * UP_SUB, (h + 1) * UP_SUB)
                gate = jnp.dot(c, wgbf_ref[:, cols], preferred_element_type=F32)
                up = jnp.dot(c, wubf_ref[:, cols], preferred_element_type=F32)
                r1 = pltpu.roll(gate, 1, 0)
                r2 = pltpu.roll(gate, 2, 0)
                head1 = jnp.where(sub < 1, pltpu.roll(prev[h], 1, 0), r1[0:SUBLANES])
                head2 = jnp.where(sub < 2, pltpu.roll(prev[h], 2, 0), r2[0:SUBLANES])
                g1 = jnp.concatenate([head1, r1[SUBLANES:]], axis=0)
                g2 = jnp.concatenate([head2, r2[SUBLANES:]], axis=0)
                prev[h] = gate[rows - SUBLANES:rows]
                actp_ref[rsl, cols] = conv_act(gate, up, g1, g2, cols).astype(actp_ref.dtype)
        for h in range(n_cols):
            cols = slice(h * UP_SUB, (h + 1) * UP_SUB)
            carry_ref[:, cols] = prev[h]
            tailp_ref[0, :, cols] = prev[h][SUBLANES - (CONV_W - 1):SUBLANES]

    @pl.when(i == 0)
    def _():
        c = cs_ref[...]
        gate = jnp.dot(c, wgbf_ref[...], preferred_element_type=F32)
        up = jnp.dot(c, wubf_ref[...], preferred_element_type=F32)
        gates_ref[...] = gate
        acts_ref[...] = conv_act(gate, up, h1_ref[...], h0_ref[...], slice(None)).astype(acts_ref.dtype)


def _up_proj(c_p, c_s, w_up, conv_w, conv_b, hist0, hist1, *, tm, seq_len):
    mp, k = c_p.shape
    ms = c_s.shape[0]
    npt = mp // tm
    tiles_per_seq = seq_len // tm
    nseq = mp // seq_len
    tn = UP_TN
    nj = pl.cdiv(D_FF, tn)
    up_block0 = D_FF // LANE
    last_block = w_up.shape[1] // LANE - 1

    def row_p(j, i):
        return jnp.maximum(i - 1, 0)

    def wu_spec(part):
        return pl.BlockSpec(
            (k, LANE), lambda j, i: (0, jnp.minimum(up_block0 + j * UP_PARTS + part, last_block)))

    in_specs = [
        pl.BlockSpec((tm, k), lambda j, i: (row_p(j, i), 0)),
        pl.BlockSpec((ms, k), lambda j, i: (0, 0)),
        pl.BlockSpec((k, tn), lambda j, i: (0, j)),
        *[wu_spec(part) for part in range(UP_PARTS)],
        pl.BlockSpec((CONV_W, tn), lambda j, i: (0, j)),
        pl.BlockSpec((1, tn), lambda j, i: (0, j)),
        pl.BlockSpec((ms, tn), lambda j, i: (0, j)),
        pl.BlockSpec((ms, tn), lambda j, i: (0, j)),
    ]
    out_specs = [
        pl.BlockSpec((tm, tn), lambda j, i: (row_p(j, i), j)),
        pl.BlockSpec((ms, tn), lambda j, i: (0, j)),
        pl.BlockSpec((1, CONV_W - 1, tn), lambda j, i: (row_p(j, i) // tiles_per_seq, 0, j)),
        pl.BlockSpec((ms, tn), lambda j, i: (0, j)),
    ]
    out_shape = [
        jax.ShapeDtypeStruct((mp, D_FF), BF16),
        jax.ShapeDtypeStruct((ms, D_FF), BF16),
        jax.ShapeDtypeStruct((nseq, CONV_W - 1, D_FF), F32),
        jax.ShapeDtypeStruct((ms, D_FF), F32),
    ]
    return pl.pallas_call(
        functools.partial(_up_kernel, tiles_per_seq=tiles_per_seq),
        grid=(nj, npt + 1),
        in_specs=in_specs,
        out_specs=out_specs,
        out_shape=out_shape,
        scratch_shapes=[pltpu.VMEM((k, tn), BF16), pltpu.VMEM((k, tn), BF16), pltpu.VMEM((SUBLANES, tn), F32)],
        compiler_params=_cparams(2),
        name="up_proj",
    )(c_p, c_s, w_up, *([w_up] * UP_PARTS), conv_w, conv_b, hist0, hist1)


def _rms(x, g):
    return x * lax.rsqrt(jnp.mean(x * x, axis=-1, keepdims=True) + EPS) * g


def _two_group_specs(tm, ms, widths):
    specs = []
    for w in widths:
        specs.append(pl.BlockSpec((tm, w), lambda i: (jnp.maximum(i - 1, 0), 0)))
        specs.append(pl.BlockSpec((ms, w), lambda i: (0, 0)))
    return specs


def _resident(shape):
    return pl.BlockSpec(shape, lambda i: (0,) * len(shape), pipeline_mode=pl.Buffered(1))


def _out_norm_kernel(mp_ref, ms_ref, xp_ref, xs_ref, w_ref, g_ref, hp_ref, hs_ref, cp_ref, cs_ref, wbf_ref):
    i = pl.program_id(0)

    @pl.when(i == 0)
    def _():
        wbf_ref[...] = w_ref[...].astype(BF16)

    def emit(m_ref, x_ref, h_ref, c_ref):
        h = x_ref[...] + jnp.dot(m_ref[...], wbf_ref[...], preferred_element_type=F32)
        h_ref[...] = h
        c_ref[...] = _rms(h, g_ref[...]).astype(c_ref.dtype)

    @pl.when(i == 0)
    def _():
        emit(ms_ref, xs_ref, hs_ref, cs_ref)

    @pl.when(i > 0)
    def _():
        emit(mp_ref, xp_ref, hp_ref, cp_ref)


def _out_norm(m_p, m_s, x_p, x_s, w_out, g, *, tm):
    mp, d = x_p.shape
    ms = x_s.shape[0]
    return pl.pallas_call(
        _out_norm_kernel,
        grid=(mp // tm + 1,),
        in_specs=[*_two_group_specs(tm, ms, (d, d)), _resident((d, d)), _resident((1, d))],
        out_specs=_two_group_specs(tm, ms, (d, d)),
        out_shape=[
            jax.ShapeDtypeStruct((mp, d), F32), jax.ShapeDtypeStruct((ms, d), F32),
            jax.ShapeDtypeStruct((mp, d), BF16), jax.ShapeDtypeStruct((ms, d), BF16),
        ],
        scratch_shapes=[pltpu.VMEM((d, d), BF16)],
        compiler_params=_cparams(1),
        name="out_norm",
    )(m_p, m_s, x_p, x_s, w_out, g.reshape(1, d))


def _ple_kernel(hp_ref, hs_ref, pp_ref, ps_ref, wpg_ref, wple_ref, gple_ref, gfin_ref, yp_ref, ys_ref,
                wpgbf_ref, wplebf_ref, *, final_norm):
    i = pl.program_id(0)

    @pl.when(i == 0)
    def _():
        wpgbf_ref[...] = wpg_ref[...].astype(BF16)
        wplebf_ref[...] = wple_ref[...].astype(BF16)

    def emit(h_ref, p_ref, y_ref):
        h2 = h_ref[...]
        cn = _rms(h2, gple_ref[...]).astype(BF16)
        pg = _sigmoid(jnp.dot(cn, wpgbf_ref[...], preferred_element_type=F32))
        e = jnp.dot(p_ref[...].astype(BF16), wplebf_ref[...], preferred_element_type=F32)
        h3 = h2 + pg * e
        y_ref[...] = _rms(h3, gfin_ref[...]) if final_norm else h3

    @pl.when(i == 0)
    def _():
        emit(hs_ref, ps_ref, ys_ref)

    @pl.when(i > 0)
    def _():
        emit(hp_ref, pp_ref, yp_ref)


def _ple(h_p, h_s, p_p, p_s, wpg, wple, g_ple, g_final, *, tm, final_norm):
    mp, d = h_p.shape
    ms = h_s.shape[0]
    pd = p_p.shape[1]
    return pl.pallas_call(
        functools.partial(_ple_kernel, final_norm=final_norm),
        grid=(mp // tm + 1,),
        in_specs=[*_two_group_specs(tm, ms, (d, pd)), _resident((d, d)), _resident((pd, d)),
                  _resident((1, d)), _resident((1, d))],
        out_specs=_two_group_specs(tm, ms, (d,)),
        out_shape=[jax.ShapeDtypeStruct((mp, d), F32), jax.ShapeDtypeStruct((ms, d), F32)],
        scratch_shapes=[pltpu.VMEM((d, d), BF16), pltpu.VMEM((pd, d), BF16)],
        compiler_params=_cparams(1),
        name="ple",
    )(h_p, h_s, p_p, p_s, wpg, wple, g_ple.reshape(1, d), g_final.reshape(1, d))


def _mixers(z, lw, *, prompt_shape=None, state=None):
    sample = state is not None
    if sample:
        s_gla, s_pool = state
        og, gla_new = _gla_sample(z, lw["wg"], lw["bg"], lw["gn"], s_gla)
        pm = _pool_sample(z, jnp.transpose(s_pool, (1, 0, 2)), lw["w_pool"], lw["pool_scale"])
        u = z[:, Z_U:Z_U + POOL_W].astype(F32)
        pool_new = jnp.concatenate([s_pool[:, 1:], u[:, None, :]], axis=1)
    else:
        nb, t = prompt_shape
        og, gla_new = _gla_prompt(z, lw["wg"], lw["bg"], lw["gn"], nb, t)
        pm = _pool_prompt(z, lw["w_pool"], lw["pool_scale"], nb, t, 1024)
        pool_new = z.reshape(nb, t, Z_WIDTH)[:, t - POOL_BUF:, Z_U:Z_U + POOL_W].astype(F32)
    return og, pm, gla_new, pool_new


def kernel(x_prompt, x_sample, state_gla, state_pool, state_conv, p_prompt, p_sample, norm_mix, w_in, w_gate_up, b_gate, gla_norm, w_branch_a, w_pool, pool_scale, w_branch_b, w_out, norm_ffn, w_up, conv_w, conv_b, w_down, norm_ple, w_ple_gate, w_ple, norm_final):
    depth = w_in.shape[0]
    nb, t, d = x_prompt.shape
    ns = x_sample.shape[0]
    hp = x_prompt.reshape(nb * t, d)
    hs = x_sample.reshape(ns, d)
    outs = {k: [] for k in ("gp", "pp", "cp", "gs", "ps", "cs")}
    for li in range(depth):
        lw = {
            "wg": jnp.pad(w_gate_up[li], ((0, LANE - GATE_RANK), (0, 0))).astype(BF16),
            "bg": b_gate[li].reshape(1, GLA_KW),
            "gn": gla_norm[li].reshape(1, GLA_VW),
            "w_pool": w_pool[li],
            "pool_scale": pool_scale[li].reshape(1, POOL_W),
        }
        a_p = _rmsnorm(hp, norm_mix[li], BF16, 1024)
        a_s = _rmsnorm(hs, norm_mix[li], BF16, ns)
        z_p, z_s = _in_proj(a_p, a_s, w_in[li].T, tm=2048)
        og_p, pm_p, g1, p1 = _mixers(z_p, lw, prompt_shape=(nb, t))
        og_s, pm_s, g2, p2 = _mixers(z_s, lw, state=(state_gla[li], state_pool[li]))
        mg_p, mg_s = _merge(og_p, og_s, pm_p, pm_s, w_branch_a[li], w_branch_b[li], z_p, z_s, tm=1024, tn=512)
        h1_p, h1_s, c_p, c_s = _out_norm(mg_p, mg_s, hp, hs, w_out[li], norm_ffn[li], tm=512)
        s_conv = state_conv[li]
        act_p, act_s, c1, gate_s = _up_proj(c_p, c_s, w_up[li], conv_w[li], conv_b[li].reshape(1, D_FF),
                                            s_conv[:, 0], s_conv[:, 1], tm=2048, seq_len=t)
        c2 = jnp.stack([s_conv[:, 1], gate_s], axis=1)
        h2_p, h2_s = _down_proj(act_p, act_s, w_down[li], h1_p, h1_s, tm=512, tn=512)
        hp, hs = _ple(h2_p, h2_s, p_prompt[li].reshape(nb * t, PLE_DIM), p_sample[li].reshape(ns, PLE_DIM),
                      w_ple_gate[li], w_ple[li], norm_ple[li], norm_final, tm=512,
                      final_norm=li == depth - 1)
        for key, val in zip(("gp", "pp", "cp", "gs", "ps", "cs"), (g1, p1, c1, g2, p2, c2)):
            outs[key].append(val)
    st = {k: jnp.stack(v, axis=0) for k, v in outs.items()}
    return (hp.reshape(nb, t, d), hs.reshape(ns, 1, d),
            st["gp"], st["pp"], st["cp"], st["gs"], st["ps"], st["cs"])
```

```python
import functools

import jax
import jax.numpy as jnp
from jax import lax
from jax.experimental import pallas as pl
from jax.experimental.pallas import tpu as pltpu

F32 = jnp.float32
BF16 = jnp.bfloat16

D_MODEL = 2048
GLA_HEADS = 4
GLA_DK = 256
GLA_DV = 512
GLA_KW = GLA_HEADS * GLA_DK
GLA_VW = GLA_HEADS * GLA_DV
GATE_RANK = 16
GATE_TEMP = 16.0
GLA_CHUNK = 64
POOL_WINDOWS = (2, 4, 8, 16)
POOL_GW = 256
POOL_W = 1024
POOL_BUF = 15
D_FF = 5504
CONV_W = 3
PLE_DIM = 256
EPS = 1e-6

LANE = 128
SUBLANES = 8
Z_Q, Z_K, Z_V, Z_R, Z_U, Z_GA, Z_GB, Z_GLR, Z_WIDTH = 0, 1024, 2048, 4096, 6144, 7168, 9216, 11264, 11392
GLR_SRC = 6144
IN_TN = 1024
IN_SUB = 512

VMEM_LIMIT = 58 * 1024 * 1024


def _cparams(n_axes):
    return pltpu.CompilerParams(dimension_semantics=("arbitrary",) * n_axes, vmem_limit_bytes=VMEM_LIMIT)


def _sigmoid(x):
    return 1.0 / (1.0 + jnp.exp(-x))


def _silu(x):
    return x * _sigmoid(x)


def _log_sigmoid(x):
    return jnp.minimum(x, 0.0) - jnp.log(1.0 + jnp.exp(-jnp.abs(x)))


def _rms_kernel(x_ref, g_ref, o_ref):
    x = x_ref[...]
    ms = jnp.mean(x * x, axis=-1, keepdims=True)
    o_ref[...] = (x * lax.rsqrt(ms + EPS) * g_ref[...]).astype(o_ref.dtype)


def _rmsnorm(x, g, out_dtype, tm):
    m, d = x.shape
    return pl.pallas_call(
        _rms_kernel,
        grid=(m // tm,),
        in_specs=[pl.BlockSpec((tm, d), lambda i: (i, 0)), pl.BlockSpec((1, d), lambda i: (0, 0))],
        out_specs=pl.BlockSpec((tm, d), lambda i: (i, 0)),
        out_shape=jax.ShapeDtypeStruct((m, d), out_dtype),
        compiler_params=_cparams(1),
        name="rmsnorm",
    )(x, g.reshape(1, d))


def _col_tiled_specs(tm, ms, width, col0=0):
    return [
        pl.BlockSpec((tm, width), lambda j, i: (jnp.maximum(i - 1, 0), col0 + j)),
        pl.BlockSpec((ms, width), lambda j, i: (0, col0 + j)),
    ]


def _row_tiled_specs(tm, ms, width):
    return [
        pl.BlockSpec((tm, width), lambda j, i: (jnp.maximum(i - 1, 0), 0)),
        pl.BlockSpec((ms, width), lambda j, i: (0, 0)),
    ]


def _down_kernel(ap_ref, as_ref, w_ref, rp_ref, rs_ref, op_ref, os_ref, wbf_ref):
    i = pl.program_id(1)

    @pl.when(i == 0)
    def _():
        wbf_ref[...] = w_ref[...].astype(BF16)
        os_ref[...] = rs_ref[...] + jnp.dot(as_ref[...], wbf_ref[...], preferred_element_type=F32)

    @pl.when(i > 0)
    def _():
        op_ref[...] = rp_ref[...] + jnp.dot(ap_ref[...], wbf_ref[...], preferred_element_type=F32)


def _down_proj(a_p, a_s, w, res_p, res_s, *, tm, tn):
    mp, k = a_p.shape
    ms = a_s.shape[0]
    n = w.shape[1]
    return pl.pallas_call(
        _down_kernel,
        grid=(n // tn, mp // tm + 1),
        in_specs=[
            *_row_tiled_specs(tm, ms, k),
            pl.BlockSpec((k, tn), lambda j, i: (0, j)),
            *_col_tiled_specs(tm, ms, tn),
        ],
        out_specs=_col_tiled_specs(tm, ms, tn),
        out_shape=[jax.ShapeDtypeStruct((mp, n), F32), jax.ShapeDtypeStruct((ms, n), F32)],
        scratch_shapes=[pltpu.VMEM((k, tn), BF16)],
        compiler_params=_cparams(2),
        name="down_proj",
    )(a_p, a_s, w, res_p, res_s)


_NT = (((1,), (1,)), ((), ()))
_TN = (((0,), (0,)), ((), ()))


def _in_proj_kernel(ap_ref, as_ref, w_ref, wx_ref, zp_ref, zs_ref, wbf_ref):
    j = pl.program_id(0)
    i = pl.program_id(1)
    last = pl.num_programs(0) - 1
    first_shifted = GLR_SRC // IN_TN
    keep = IN_TN - GATE_RANK

    @pl.when(i == 0)
    def _():
        @pl.when(j < first_shifted)
        def _():
            wbf_ref[...] = w_ref[...].astype(BF16)

        @pl.when(jnp.logical_and(j >= first_shifted, j < last))
        def _():
            wbf_ref[:keep, :] = w_ref[GATE_RANK:, :].astype(BF16)
            wbf_ref[keep:, :] = wx_ref[...].astype(BF16)

        @pl.when(j == last)
        def _():
            wbf_ref[:GATE_RANK, :] = wx_ref[...].astype(BF16)
            wbf_ref[GATE_RANK:LANE, :] = jnp.zeros((LANE - GATE_RANK, wbf_ref.shape[1]), BF16)

    def emit(a_ref, z_ref):
        @pl.when(j < last)
        def _():
            a = a_ref[...]
            for part in range(IN_TN // IN_SUB):
                cols = slice(part * IN_SUB, (part + 1) * IN_SUB)
                z_ref[:, cols] = lax.dot_general(a, wbf_ref[cols, :], _NT,
                                                 preferred_element_type=F32).astype(z_ref.dtype)

        @pl.when(j == last)
        def _():
            z_ref[:, :LANE] = lax.dot_general(a_ref[...], wbf_ref[:LANE, :], _NT,
                                              preferred_element_type=F32).astype(z_ref.dtype)

    @pl.when(i == 0)
    def _():
        emit(as_ref, zs_ref)

    @pl.when(i > 0)
    def _():
        emit(ap_ref, zp_ref)


def _in_proj(a_p, a_s, wt, *, tm):
    mp, k = a_p.shape
    ms = a_s.shape[0]
    npt = mp // tm
    nj = Z_GLR // IN_TN + 1
    aux_per_tile = IN_TN // GATE_RANK

    def row_p(j, i):
        return jnp.maximum(i - 1, 0)

    def aux_idx(j, i):
        return (jnp.where(j == nj - 1, GLR_SRC // GATE_RANK, (j + 1) * aux_per_tile), 0)

    return pl.pallas_call(
        _in_proj_kernel,
        grid=(nj, npt + 1),
        in_specs=[
            pl.BlockSpec((tm, k), lambda j, i: (row_p(j, i), 0)),
            pl.BlockSpec((ms, k), lambda j, i: (0, 0)),
            pl.BlockSpec((IN_TN, k), lambda j, i: (j, 0)),
            pl.BlockSpec((GATE_RANK, k), aux_idx),
        ],
        out_specs=[
            pl.BlockSpec((tm, IN_TN), lambda j, i: (row_p(j, i), j)),
            pl.BlockSpec((ms, IN_TN), lambda j, i: (0, j)),
        ],
        out_shape=[jax.ShapeDtypeStruct((mp, Z_WIDTH), BF16), jax.ShapeDtypeStruct((ms, Z_WIDTH), BF16)],
        scratch_shapes=[pltpu.VMEM((IN_TN, k), BF16)],
        compiler_params=_cparams(2),
        name="in_proj",
    )(a_p, a_s, wt, wt)


GLA_GROUP = 16


GLA_HEADS_PER_STEP = 1


def _gla_prompt_kernel(q_ref, k_ref, v_ref, r_ref, glr_ref, wg_ref, bg_ref, gn_ref,
                       og_ref, s_out_ref, st_ref):
    t = q_ref.shape[0]
    c = GLA_CHUNK
    rows = GLA_GROUP * c
    nh = GLA_HEADS_PER_STEP
    st_ref[...] = jnp.zeros_like(st_ref)
    row = lax.broadcasted_iota(jnp.int32, (c, c), 0)
    col = lax.broadcasted_iota(jnp.int32, (c, c), 1)
    causal = row >= col
    tri = causal.astype(BF16)

    def group(gi, carry):
        r0 = pl.multiple_of(gi * rows, rows)
        rs = pl.ds(r0, rows)
        glr = glr_ref[rs, :]

        for h in range(nh):
            ks = slice(h * GLA_DK, (h + 1) * GLA_DK)
            vs = slice(h * GLA_DV, (h + 1) * GLA_DV)
            zg = jnp.dot(glr, wg_ref[:, ks], preferred_element_type=F32) + bg_ref[:, ks]
            v = v_ref[rs, vs]
            vals = [dict() for _ in range(GLA_GROUP)]
            state = {"st": st_ref[h]}

            def s_gate(ci, d):
                gl = _log_sigmoid(zg[ci * c:(ci + 1) * c]) * (1.0 / GATE_TEMP)
                d["gl_hi"] = gl.astype(BF16)
                d["gl_lo"] = (gl - d["gl_hi"].astype(F32)).astype(BF16)

            def s_cumsum(ci, d):
                d["b"] = (jnp.dot(tri, d.pop("gl_hi"), preferred_element_type=F32)
                          + jnp.dot(tri, d.pop("gl_lo"), preferred_element_type=F32))

            def s_scale(ci, d):
                cr = pl.ds(r0 + ci * c, c)
                b = d.pop("b")
                bl = b[c - 1:c, :]
                q = q_ref[cr, ks].astype(F32) * (GLA_DK ** -0.5)
                k = k_ref[cr, ks].astype(F32)
                d["qb"] = (q * jnp.exp(b)).astype(BF16)
                d["kb"] = (k * jnp.exp(-b)).astype(BF16)
                d["kd"] = (k * jnp.exp(bl - b)).astype(BF16)
                d["decay"] = jnp.exp(bl)

            def s_att(ci, d):
                sl = slice(ci * c, (ci + 1) * c)
                d["att"] = lax.dot_general(d["qb"], d.pop("kb"), _NT, preferred_element_type=F32)
                d["dst"] = lax.dot_general(v[sl], d.pop("kd"), _TN, preferred_element_type=F32)

            def s_mask(ci, d):
                d["att"] = jnp.where(causal, d["att"], 0.0).astype(BF16)

            def s_intra(ci, d):
                sl = slice(ci * c, (ci + 1) * c)
                d["o"] = jnp.dot(d.pop("att"), v[sl], preferred_element_type=F32)

            def s_chain(ci, d):
                st = state["st"]
                d["st_in"] = st.astype(BF16)
                state["st"] = st * d.pop("decay") + d.pop("dst")

            def s_inter(ci, d):
                d["o"] = d["o"] + lax.dot_general(d.pop("qb"), d.pop("st_in"), _NT, preferred_element_type=F32)

            def s_finish(ci, d):
                cr = pl.ds(r0 + ci * c, c)
                on = _rms(d.pop("o"), gn_ref[:, vs])
                og_ref[cr, vs] = (on * _silu(r_ref[cr, vs].astype(F32))).astype(og_ref.dtype)

            stages = (s_gate, s_cumsum, s_scale, s_att, s_mask, s_intra, s_chain, s_inter, s_finish)
            for step in range(GLA_GROUP + len(stages) - 1):
                for si, stage in enumerate(stages):
                    ci = step - si
                    if 0 <= ci < GLA_GROUP:
                        stage(ci, vals[ci])
            st_ref[h] = state["st"]
        return carry

    lax.fori_loop(0, t // rows, group, 0)
    for h in range(nh):
        s_out_ref[0, h] = st_ref[h].T


def _gla_prompt(z, wg, bg, gn, nb, t):
    m = z.shape[0]
    nh = GLA_HEADS_PER_STEP
    kw, vw = nh * GLA_DK, nh * GLA_DV
    og, s_new = pl.pallas_call(
        _gla_prompt_kernel,
        grid=(nb, GLA_HEADS // nh),
        in_specs=[
            pl.BlockSpec((t, kw), lambda b, h: (b, Z_Q // kw + h)),
            pl.BlockSpec((t, kw), lambda b, h: (b, Z_K // kw + h)),
            pl.BlockSpec((t, vw), lambda b, h: (b, Z_V // vw + h)),
            pl.BlockSpec((t, vw), lambda b, h: (b, Z_R // vw + h)),
            pl.BlockSpec((t, LANE), lambda b, h: (b, Z_GLR // LANE)),
            pl.BlockSpec((LANE, kw), lambda b, h: (0, h)),
            pl.BlockSpec((1, kw), lambda b, h: (0, h)),
            pl.BlockSpec((1, vw), lambda b, h: (0, h)),
        ],
        out_specs=[
            pl.BlockSpec((t, vw), lambda b, h: (b, h)),
            pl.BlockSpec((1, nh, GLA_DK, GLA_DV), lambda b, h: (b, h, 0, 0)),
        ],
        out_shape=[
            jax.ShapeDtypeStruct((m, GLA_VW), BF16),
            jax.ShapeDtypeStruct((nb, GLA_HEADS, GLA_DK, GLA_DV), F32),
        ],
        scratch_shapes=[pltpu.VMEM((nh, GLA_DV, GLA_DK), F32)],
        compiler_params=_cparams(2),
        name="gla_prompt",
    )(z, z, z, z, z, wg, bg, gn)
    return og, s_new


GLA_SAMPLE_BLOCK = 4


def _gla_sample_kernel(q_ref, k_ref, v_ref, r_ref, glr_ref, wg_ref, bg_ref, gn_ref, s_ref,
                       og_ref, s_out_ref):
    def one_sequence(si, carry):
        _gla_sample_sequence(pl.program_id(0) * GLA_SAMPLE_BLOCK + si, si, q_ref, k_ref, v_ref, r_ref, glr_ref,
                             wg_ref, bg_ref, gn_ref, s_ref, og_ref, s_out_ref)
        return carry

    lax.fori_loop(0, GLA_SAMPLE_BLOCK, one_sequence, 0)


def _gla_sample_sequence(n, si, q_ref, k_ref, v_ref, r_ref, glr_ref, wg_ref, bg_ref, gn_ref, s_ref,
                         og_ref, s_out_ref):
    row = pl.ds(n, 1)

    def rows8(ref):
        return jnp.broadcast_to(ref[row, :], (SUBLANES, ref.shape[1]))

    zg = jnp.dot(rows8(glr_ref).astype(BF16), wg_ref[...], preferred_element_type=F32) + bg_ref[...]
    g = _log_sigmoid(zg) * (1.0 / GATE_TEMP)
    eg = jnp.exp(g)
    eng = jnp.exp(-g)
    eg1 = eg.astype(BF16)
    rem = eg - eg1.astype(F32)
    eg2 = rem.astype(BF16)
    eg3 = (rem - eg2.astype(F32)).astype(BF16)
    first_row = lax.broadcasted_iota(jnp.int32, (SUBLANES, LANE), 0) == 0
    ones_row = jnp.where(first_row, 1.0, 0.0).astype(BF16)
    row0 = lax.broadcasted_iota(jnp.int32, (SUBLANES, GLA_DV), 0) == 0
    q_all = rows8(q_ref) * (GLA_DK ** -0.5)
    k_all = rows8(k_ref)
    v_all = rows8(v_ref)
    r_all = r_ref[row, :]
    gn_all = gn_ref[...]
    for h in range(GLA_HEADS):
        ks = slice(h * GLA_DK, (h + 1) * GLA_DK)
        vs = slice(h * GLA_DV, (h + 1) * GLA_DV)
        q, k, v = q_all[:, ks], k_all[:, ks], v_all[:, vs]
        s_old = s_ref[si, h]
        decay = (lax.dot_general(eg1[:, ks], ones_row, _TN, preferred_element_type=F32)
                 + lax.dot_general(eg2[:, ks], ones_row, _TN, preferred_element_type=F32)
                 + lax.dot_general(eg3[:, ks], ones_row, _TN, preferred_element_type=F32))
        v0 = jnp.where(row0, v, 0.0)
        ds = lax.dot_general(k.astype(BF16), v0.astype(BF16), _TN, preferred_element_type=F32)
        s_out_ref[si, h] = jnp.tile(decay, (1, GLA_DV // LANE)) * s_old + ds
        qb = q * eg[:, ks]
        kb = k * eng[:, ks]
        att = jnp.sum(qb * kb, axis=-1, keepdims=True)
        o = jnp.dot(qb.astype(BF16), s_old.astype(BF16), preferred_element_type=F32) + att * v
        o = o[0:1, :]
        ms = jnp.mean(o * o, axis=-1, keepdims=True)
        on = o * lax.rsqrt(ms + EPS) * gn_all[:, vs]
        og_ref[row, vs] = on * _silu(r_all[:, vs])


def _gla_sample(z, wg, bg, gn, state):
    n = z.shape[0]
    zf = z.astype(F32)
    nb = GLA_SAMPLE_BLOCK
    og, s_new = pl.pallas_call(
        _gla_sample_kernel,
        grid=(n // nb,),
        in_specs=[
            pl.BlockSpec((n, GLA_KW), lambda i: (0, Z_Q // GLA_KW)),
            pl.BlockSpec((n, GLA_KW), lambda i: (0, Z_K // GLA_KW)),
            pl.BlockSpec((n, GLA_VW), lambda i: (0, Z_V // GLA_VW)),
            pl.BlockSpec((n, GLA_VW), lambda i: (0, Z_R // GLA_VW)),
            pl.BlockSpec((n, LANE), lambda i: (0, Z_GLR // LANE)),
            pl.BlockSpec((LANE, GLA_KW), lambda i: (0, 0)),
            pl.BlockSpec((1, GLA_KW), lambda i: (0, 0)),
            pl.BlockSpec((1, GLA_VW), lambda i: (0, 0)),
            pl.BlockSpec((nb, GLA_HEADS, GLA_DK, GLA_DV), lambda i: (i, 0, 0, 0)),
        ],
        out_specs=[
            pl.BlockSpec((n, GLA_VW), lambda i: (0, 0)),
            pl.BlockSpec((nb, GLA_HEADS, GLA_DK, GLA_DV), lambda i: (i, 0, 0, 0)),
        ],
        out_shape=[
            jax.ShapeDtypeStruct((n, GLA_VW), F32),
            jax.ShapeDtypeStruct((n, GLA_HEADS, GLA_DK, GLA_DV), F32),
        ],
        compiler_params=_cparams(1),
        name="gla_sample",
    )(zf, zf, zf, zf, zf, wg, bg, gn, state)
    return og.astype(BF16), s_new


def _pool_group_out(mix, g, wp_ref, ps_ref):
    cs = slice(g * POOL_GW, (g + 1) * POOL_GW)
    pm = jnp.dot(mix.astype(BF16), wp_ref[g].astype(BF16), preferred_element_type=F32)
    return pm * ps_ref[:, cs]


POOL_LEVELS = tuple(w.bit_length() - 1 for w in POOL_WINDOWS)
assert all(1 << lv == w for lv, w in zip(POOL_LEVELS, POOL_WINDOWS)), "windows must be powers of two"
POOL_HIST = SUBLANES * max(POOL_LEVELS)
assert POOL_HIST > POOL_BUF + 1


def _pool_prompt_kernel(u_ref, wp_ref, ps_ref, o_ref, ext_ref, lvl_ref):
    i = pl.program_id(1)
    tt = u_ref.shape[0]
    hist = POOL_HIST
    n = hist + tt

    @pl.when(i == 0)
    def _():
        ext_ref[0:hist, :] = jnp.zeros((hist, POOL_W), F32)

    @pl.when(i > 0)
    def _():
        ext_ref[0:hist, :] = ext_ref[tt:n, :]

    ext_ref[hist:n, :] = u_ref[...].astype(F32)
    pos = i * tt + lax.broadcasted_iota(jnp.int32, (tt, 1), 0)
    for g, (w, levels) in enumerate(zip(POOL_WINDOWS, POOL_LEVELS)):
        cs = slice(g * POOL_GW, (g + 1) * POOL_GW)
        cur = None
        for lev in range(levels):
            sh = 1 << lev
            start = SUBLANES * (lev + 1)
            if lev == 0:
                cur = ext_ref[start:n, cs] + ext_ref[start - sh:n - sh, cs]
            else:
                prev = lvl_ref.at[(lev - 1) % 2]
                cur = prev[start:n, :] + prev[start - sh:n - sh, :]
            if lev + 1 < levels:
                lvl_ref[lev % 2, start:n, :] = cur
        s = cur[hist - SUBLANES * levels:]
        u = ext_ref[hist:n, cs]
        cnt = jnp.minimum(pos + 1, w).astype(F32)
        mix = s / cnt - u
        o_ref[:, cs] = _pool_group_out(mix, g, wp_ref, ps_ref).astype(o_ref.dtype)


def _pool_prompt(z, w_pool, pool_scale, nb, t, tt):
    m = z.shape[0]
    nt = t // tt
    return pl.pallas_call(
        _pool_prompt_kernel,
        grid=(nb, nt),
        in_specs=[
            pl.BlockSpec((tt, POOL_W), lambda b, i: (b * nt + i, Z_U // POOL_W)),
            pl.BlockSpec((4, POOL_GW, POOL_GW), lambda b, i: (0, 0, 0)),
            pl.BlockSpec((1, POOL_W), lambda b, i: (0, 0)),
        ],
        out_specs=pl.BlockSpec((tt, POOL_W), lambda b, i: (b * nt + i, 0)),
        out_shape=jax.ShapeDtypeStruct((m, POOL_W), BF16),
        scratch_shapes=[pltpu.VMEM((tt + POOL_HIST, POOL_W), F32), pltpu.VMEM((2, tt + POOL_HIST, POOL_GW), F32)],
        compiler_params=_cparams(2),
        name="pool_prompt",
    )(z, w_pool, pool_scale)


def _pool_sample_kernel(u_ref, buf_ref, wp_ref, ps_ref, o_ref):
    for g, w in enumerate(POOL_WINDOWS):
        cs = slice(g * POOL_GW, (g + 1) * POOL_GW)
        u = u_ref[:, cs].astype(F32)
        s = u
        for d in range(1, w):
            s = s + buf_ref[POOL_BUF - d, :, cs]
        mix = s / float(w) - u
        o_ref[:, cs] = _pool_group_out(mix, g, wp_ref, ps_ref).astype(o_ref.dtype)


def _pool_sample(z, buf_t, w_pool, pool_scale):
    n = z.shape[0]
    return pl.pallas_call(
        _pool_sample_kernel,
        grid=(1,),
        in_specs=[
            pl.BlockSpec((n, POOL_W), lambda i: (0, Z_U // POOL_W)),
            pl.BlockSpec((POOL_BUF, n, POOL_W), lambda i: (0, 0, 0)),
            pl.BlockSpec((4, POOL_GW, POOL_GW), lambda i: (0, 0, 0)),
            pl.BlockSpec((1, POOL_W), lambda i: (0, 0)),
        ],
        out_specs=pl.BlockSpec((n, POOL_W), lambda i: (0, 0)),
        out_shape=jax.ShapeDtypeStruct((n, POOL_W), BF16),
        compiler_params=_cparams(1),
        name="pool_sample",
    )(z, buf_t, w_pool, pool_scale)


MERGE_SUB = 256


def _merge_kernel(ogp_ref, ogs_ref, pmp_ref, pms_ref, wa_ref, wb_ref, gap_ref, gas_ref, gbp_ref, gbs_ref,
                  op_ref, os_ref, wabf_ref, wbbf_ref):
    i = pl.program_id(1)

    @pl.when(i == 0)
    def _():
        wabf_ref[...] = wa_ref[...].astype(BF16)
        wbbf_ref[...] = wb_ref[...].astype(BF16)

    def emit(og_ref, pm_ref, ga_ref, gb_ref, o_ref):
        og = og_ref[...]
        pm = pm_ref[...]
        for part in range(o_ref.shape[1] // MERGE_SUB):
            cols = slice(part * MERGE_SUB, (part + 1) * MERGE_SUB)
            ya = jnp.dot(og, wabf_ref[:, cols], preferred_element_type=F32)
            yb = jnp.dot(pm, wbbf_ref[:, cols], preferred_element_type=F32)
            ga = ga_ref[:, cols].astype(F32)
            gb = gb_ref[:, cols].astype(F32)
            o_ref[:, cols] = (_sigmoid(ga) * ya + _sigmoid(gb) * yb).astype(o_ref.dtype)

    @pl.when(i == 0)
    def _():
        emit(ogs_ref, pms_ref, gas_ref, gbs_ref, os_ref)

    @pl.when(i > 0)
    def _():
        emit(ogp_ref, pmp_ref, gap_ref, gbp_ref, op_ref)


def _merge(og_p, og_s, pm_p, pm_s, wa, wb, z_p, z_s, *, tm, tn):
    mp = og_p.shape[0]
    ms = og_s.shape[0]
    return pl.pallas_call(
        _merge_kernel,
        grid=(D_MODEL // tn, mp // tm + 1),
        in_specs=[
            *_row_tiled_specs(tm, ms, GLA_VW),
            *_row_tiled_specs(tm, ms, POOL_W),
            pl.BlockSpec((GLA_VW, tn), lambda j, i: (0, j)),
            pl.BlockSpec((POOL_W, tn), lambda j, i: (0, j)),
            *_col_tiled_specs(tm, ms, tn, Z_GA // tn),
            *_col_tiled_specs(tm, ms, tn, Z_GB // tn),
        ],
        out_specs=_col_tiled_specs(tm, ms, tn),
        out_shape=[jax.ShapeDtypeStruct((mp, D_MODEL), BF16), jax.ShapeDtypeStruct((ms, D_MODEL), BF16)],
        scratch_shapes=[pltpu.VMEM((GLA_VW, tn), BF16), pltpu.VMEM((POOL_W, tn), BF16)],
        compiler_params=_cparams(2),
        name="merge",
    )(og_p, og_s, pm_p, pm_s, wa, wb, z_p, z_s, z_p, z_s)


UP_TN = 512
UP_SUB = 256
UP_ROWS = 1024
UP_PARTS = UP_TN // LANE


def _up_kernel(*refs, tiles_per_seq):
    cp_ref, cs_ref, wg_ref = refs[:3]
    wu_refs = refs[3:3 + UP_PARTS]
    (cw_ref, cb_ref, h0_ref, h1_ref, actp_ref, acts_ref, tailp_ref, gates_ref,
     wgbf_ref, wubf_ref, carry_ref) = refs[3 + UP_PARTS:]
    i = pl.program_id(1)

    @pl.when(i == 0)
    def _():
        wgbf_ref[...] = wg_ref[...].astype(BF16)
        for part, wu_ref in enumerate(wu_refs):
            wubf_ref[:, part * LANE:(part + 1) * LANE] = wu_ref[...].astype(BF16)

    def conv_act(gate, up, g1, g2, cols):
        cw = cw_ref[:, cols]
        gconv = cb_ref[:, cols] + cw[0:1] * g2 + cw[1:2] * g1 + cw[2:3] * gate
        return _silu(gconv) * up

    @pl.when(i > 0)
    def _():
        @pl.when((i - 1) % tiles_per_seq == 0)
        def _():
            carry_ref[...] = jnp.zeros_like(carry_ref)

        tm = cp_ref.shape[0]
        rows = min(tm, UP_ROWS)
        sub = lax.broadcasted_iota(jnp.int32, (SUBLANES, UP_SUB), 0)
        n_cols = UP_TN // UP_SUB
        prev = [carry_ref[:, h * UP_SUB:(h + 1) * UP_SUB] for h in range(n_cols)]
        for rt in range(tm // rows):
            rsl = slice(rt * rows, (rt + 1) * rows)
            c = cp_ref[rsl, :]
            for h in range(n_cols):
                cols = slice(h * UP_SUB, (h + 1) * UP_SUB)
                gate = jnp.dot(c, wgbf_ref[:, cols], preferred_element_type=F32)
                up = jnp.dot(c, wubf_ref[:, cols], preferred_element_type=F32)
                r1 = pltpu.roll(gate, 1, 0)
                r2 = pltpu.roll(gate, 2, 0)
                head1 = jnp.where(sub < 1, pltpu.roll(prev[h], 1, 0), r1[0:SUBLANES])
                head2 = jnp.where(sub < 2, pltpu.roll(prev[h], 2, 0), r2[0:SUBLANES])
                g1 = jnp.concatenate([head1, r1[SUBLANES:]], axis=0)
                g2 = jnp.concatenate([head2, r2[SUBLANES:]], axis=0)
                prev[h] = gate[rows - SUBLANES:rows]
                actp_ref[rsl, cols] = conv_act(gate, up, g1, g2, cols).astype(actp_ref.dtype)
        for h in range(n_cols):
            cols = slice(h * UP_SUB, (h + 1) * UP_SUB)
            carry_ref[:, cols] = prev[h]
            tailp_ref[0, :, cols] = prev[h][SUBLANES - (CONV_W - 1):SUBLANES]

    @pl.when(i == 0)
    def _():
        c = cs_ref[...]
        gate = jnp.dot(c, wgbf_ref[...], preferred_element_type=F32)
        up = jnp.dot(c, wubf_ref[...], preferred_element_type=F32)
        gates_ref[...] = gate
        acts_ref[...] = conv_act(gate, up, h1_ref[...], h0_ref[...], slice(None)).astype(acts_ref.dtype)


def _up_proj(c_p, c_s, w_up, conv_w, conv_b, hist0, hist1, *, tm, seq_len):
    mp, k = c_p.shape
    ms = c_s.shape[0]
    npt = mp // tm
    tiles_per_seq = seq_len // tm
    nseq = mp // seq_len
    tn = UP_TN
    nj = pl.cdiv(D_FF, tn)
    up_block0 = D_FF // LANE
    last_block = w_up.shape[1] // LANE - 1

    def row_p(j, i):
        return jnp.maximum(i - 1, 0)

    def wu_spec(part):
        return pl.BlockSpec(
            (k, LANE), lambda j, i: (0, jnp.minimum(up_block0 + j * UP_PARTS + part, last_block)))

    in_specs = [
        pl.BlockSpec((tm, k), lambda j, i: (row_p(j, i), 0)),
        pl.BlockSpec((ms, k), lambda j, i: (0, 0)),
        pl.BlockSpec((k, tn), lambda j, i: (0, j)),
        *[wu_spec(part) for part in range(UP_PARTS)],
        pl.BlockSpec((CONV_W, tn), lambda j, i: (0, j)),
        pl.BlockSpec((1, tn), lambda j, i: (0, j)),
        pl.BlockSpec((ms, tn), lambda j, i: (0, j)),
        pl.BlockSpec((ms, tn), lambda j, i: (0, j)),
    ]
    out_specs = [
        pl.BlockSpec((tm, tn), lambda j, i: (row_p(j, i), j)),
        pl.BlockSpec((ms, tn), lambda j, i: (0, j)),
        pl.BlockSpec((1, CONV_W - 1, tn), lambda j, i: (row_p(j, i) // tiles_per_seq, 0, j)),
        pl.BlockSpec((ms, tn), lambda j, i: (0, j)),
    ]
    out_shape = [
        jax.ShapeDtypeStruct((mp, D_FF), BF16),
        jax.ShapeDtypeStruct((ms, D_FF), BF16),
        jax.ShapeDtypeStruct((nseq, CONV_W - 1, D_FF), F32),
        jax.ShapeDtypeStruct((ms, D_FF), F32),
    ]
    return pl.pallas_call(
        functools.partial(_up_kernel, tiles_per_seq=tiles_per_seq),
        grid=(nj, npt + 1),
        in_specs=in_specs,
        out_specs=out_specs,
        out_shape=out_shape,
        scratch_shapes=[pltpu.VMEM((k, tn), BF16), pltpu.VMEM((k, tn), BF16), pltpu.VMEM((SUBLANES, tn), F32)],
        compiler_params=_cparams(2),
        name="up_proj",
    )(c_p, c_s, w_up, *([w_up] * UP_PARTS), conv_w, conv_b, hist0, hist1)


ROW_PASS = 256


def _rms(x, g):
    return x * lax.rsqrt(jnp.mean(x * x, axis=-1, keepdims=True) + EPS) * g


def _two_group_specs(tm, ms, widths):
    specs = []
    for w in widths:
        specs.append(pl.BlockSpec((tm, w), lambda i: (jnp.maximum(i - 1, 0), 0)))
        specs.append(pl.BlockSpec((ms, w), lambda i: (0, 0)))
    return specs


def _resident(shape):
    return pl.BlockSpec(shape, lambda i: (0,) * len(shape), pipeline_mode=pl.Buffered(1))


def _out_norm_kernel(mp_ref, ms_ref, xp_ref, xs_ref, w_ref, g_ref, hp_ref, hs_ref, cp_ref, cs_ref, wbf_ref):
    i = pl.program_id(0)

    @pl.when(i == 0)
    def _():
        wbf_ref[...] = w_ref[...].astype(BF16)

    def emit(m_ref, x_ref, h_ref, c_ref):
        rows = min(m_ref.shape[0], ROW_PASS)
        for part in range(m_ref.shape[0] // rows):
            rsl = slice(part * rows, (part + 1) * rows)
            h = x_ref[rsl, :] + jnp.dot(m_ref[rsl, :], wbf_ref[...], preferred_element_type=F32)
            h_ref[rsl, :] = h
            c_ref[rsl, :] = _rms(h, g_ref[...]).astype(c_ref.dtype)

    @pl.when(i == 0)
    def _():
        emit(ms_ref, xs_ref, hs_ref, cs_ref)

    @pl.when(i > 0)
    def _():
        emit(mp_ref, xp_ref, hp_ref, cp_ref)


def _out_norm(m_p, m_s, x_p, x_s, w_out, g, *, tm):
    mp, d = x_p.shape
    ms = x_s.shape[0]
    return pl.pallas_call(
        _out_norm_kernel,
        grid=(mp // tm + 1,),
        in_specs=[*_two_group_specs(tm, ms, (d, d)), _resident((d, d)), _resident((1, d))],
        out_specs=_two_group_specs(tm, ms, (d, d)),
        out_shape=[
            jax.ShapeDtypeStruct((mp, d), F32), jax.ShapeDtypeStruct((ms, d), F32),
            jax.ShapeDtypeStruct((mp, d), BF16), jax.ShapeDtypeStruct((ms, d), BF16),
        ],
        scratch_shapes=[pltpu.VMEM((d, d), BF16)],
        compiler_params=_cparams(1),
        name="out_norm",
    )(m_p, m_s, x_p, x_s, w_out, g.reshape(1, d))


def _ple_kernel(hp_ref, hs_ref, pp_ref, ps_ref, wpg_ref, wple_ref, gple_ref, gfin_ref, yp_ref, ys_ref,
                wpgbf_ref, wplebf_ref, *, final_norm):
    i = pl.program_id(0)

    @pl.when(i == 0)
    def _():
        wpgbf_ref[...] = wpg_ref[...].astype(BF16)
        wplebf_ref[...] = wple_ref[...].astype(BF16)

    def emit(h_ref, p_ref, y_ref):
        rows = min(h_ref.shape[0], ROW_PASS)
        n_pass = h_ref.shape[0] // rows
        cns = [_rms(h_ref[part * rows:(part + 1) * rows, :], gple_ref[...]).astype(BF16) for part in range(n_pass)]
        for part in range(n_pass):
            rsl = slice(part * rows, (part + 1) * rows)
            pg = _sigmoid(jnp.dot(cns[part], wpgbf_ref[...], preferred_element_type=F32))
            e = jnp.dot(p_ref[rsl, :].astype(BF16), wplebf_ref[...], preferred_element_type=F32)
            h3 = h_ref[rsl, :] + pg * e
            y_ref[rsl, :] = _rms(h3, gfin_ref[...]) if final_norm else h3

    @pl.when(i == 0)
    def _():
        emit(hs_ref, ps_ref, ys_ref)

    @pl.when(i > 0)
    def _():
        emit(hp_ref, pp_ref, yp_ref)


def _ple(h_p, h_s, p_p, p_s, wpg, wple, g_ple, g_final, *, tm, final_norm):
    mp, d = h_p.shape
    ms = h_s.shape[0]
    pd = p_p.shape[1]
    return pl.pallas_call(
        functools.partial(_ple_kernel, final_norm=final_norm),
        grid=(mp // tm + 1,),
        in_specs=[*_two_group_specs(tm, ms, (d, pd)), _resident((d, d)), _resident((pd, d)),
                  _resident((1, d)), _resident((1, d))],
        out_specs=_two_group_specs(tm, ms, (d,)),
        out_shape=[jax.ShapeDtypeStruct((mp, d), F32), jax.ShapeDtypeStruct((ms, d), F32)],
        scratch_shapes=[pltpu.VMEM((d, d), BF16), pltpu.VMEM((pd, d), BF16)],
        compiler_params=_cparams(1),
        name="ple",
    )(h_p, h_s, p_p, p_s, wpg, wple, g_ple.reshape(1, d), g_final.reshape(1, d))


def _mixers(z, lw, *, prompt_shape=None, state=None):
    sample = state is not None
    if sample:
        s_gla, s_pool = state
        og, gla_new = _gla_sample(z, lw["wg"], lw["bg"], lw["gn"], s_gla)
        pm = _pool_sample(z, jnp.transpose(s_pool, (1, 0, 2)), lw["w_pool"], lw["pool_scale"])
        u = z[:, Z_U:Z_U + POOL_W].astype(F32)
        pool_new = jnp.concatenate([s_pool[:, 1:], u[:, None, :]], axis=1)
    else:
        nb, t = prompt_shape
        og, gla_new = _gla_prompt(z, lw["wg"], lw["bg"], lw["gn"], nb, t)
        pm = _pool_prompt(z, lw["w_pool"], lw["pool_scale"], nb, t, 1024)
        pool_new = z.reshape(nb, t, Z_WIDTH)[:, t - POOL_BUF:, Z_U:Z_U + POOL_W].astype(F32)
    return og, pm, gla_new, pool_new


def kernel(x_prompt, x_sample, state_gla, state_pool, state_conv, p_prompt, p_sample, norm_mix, w_in, w_gate_up, b_gate, gla_norm, w_branch_a, w_pool, pool_scale, w_branch_b, w_out, norm_ffn, w_up, conv_w, conv_b, w_down, norm_ple, w_ple_gate, w_ple, norm_final):
    depth = w_in.shape[0]
    nb, t, d = x_prompt.shape
    ns = x_sample.shape[0]
    hp = x_prompt.reshape(nb * t, d)
    hs = x_sample.reshape(ns, d)
    outs = {k: [] for k in ("gp", "pp", "cp", "gs", "ps", "cs")}
    for li in range(depth):
        lw = {
            "wg": jnp.pad(w_gate_up[li], ((0, LANE - GATE_RANK), (0, 0))).astype(BF16),
            "bg": b_gate[li].reshape(1, GLA_KW),
            "gn": gla_norm[li].reshape(1, GLA_VW),
            "w_pool": w_pool[li],
            "pool_scale": pool_scale[li].reshape(1, POOL_W),
        }
        a_p = _rmsnorm(hp, norm_mix[li], BF16, 1024)
        a_s = _rmsnorm(hs, norm_mix[li], BF16, ns)
        z_p, z_s = _in_proj(a_p, a_s, w_in[li].T, tm=2048)
        og_p, pm_p, g1, p1 = _mixers(z_p, lw, prompt_shape=(nb, t))
        og_s, pm_s, g2, p2 = _mixers(z_s, lw, state=(state_gla[li], state_pool[li]))
        mg_p, mg_s = _merge(og_p, og_s, pm_p, pm_s, w_branch_a[li], w_branch_b[li], z_p, z_s, tm=1024, tn=512)
        h1_p, h1_s, c_p, c_s = _out_norm(mg_p, mg_s, hp, hs, w_out[li], norm_ffn[li], tm=512)
        s_conv = state_conv[li]
        act_p, act_s, c1, gate_s = _up_proj(c_p, c_s, w_up[li], conv_w[li], conv_b[li].reshape(1, D_FF),
                                            s_conv[:, 0], s_conv[:, 1], tm=2048, seq_len=t)
        c2 = jnp.stack([s_conv[:, 1], gate_s], axis=1)
        h2_p, h2_s = _down_proj(act_p, act_s, w_down[li], h1_p, h1_s, tm=512, tn=512)
        hp, hs = _ple(h2_p, h2_s, p_prompt[li].reshape(nb * t, PLE_DIM), p_sample[li].reshape(ns, PLE_DIM),
                      w_ple_gate[li], w_ple[li], norm_ple[li], norm_final, tm=512,
                      final_norm=li == depth - 1)
        for key, val in zip(("gp", "pp", "cp", "gs", "ps", "cs"), (g1, p1, c1, g2, p2, c2)):
            outs[key].append(val)
    st = {k: jnp.stack(v, axis=0) for k, v in outs.items()}
    return (hp.reshape(nb, t, d), hs.reshape(ns, 1, d),
            st["gp"], st["pp"], st["cp"], st["gs"], st["ps"], st["cs"])
```

```python
import functools

import jax
import jax.numpy as jnp
from jax import lax
from jax.experimental import pallas as pl
from jax.experimental.pallas import tpu as pltpu

F32 = jnp.float32
BF16 = jnp.bfloat16

D_MODEL = 2048
GLA_HEADS = 4
GLA_DK = 256
GLA_DV = 512
GLA_KW = GLA_HEADS * GLA_DK
GLA_VW = GLA_HEADS * GLA_DV
GATE_RANK = 16
GATE_TEMP = 16.0
GLA_CHUNK = 64
POOL_WINDOWS = (2, 4, 8, 16)
POOL_GW = 256
POOL_W = 1024
POOL_BUF = 15
D_FF = 5504
CONV_W = 3
PLE_DIM = 256
EPS = 1e-6

LANE = 128
SUBLANES = 8
Z_Q, Z_K, Z_V, Z_R, Z_U, Z_GA, Z_GB, Z_GLR, Z_WIDTH = 0, 1024, 2048, 4096, 6144, 7168, 9216, 11264, 11392
GLR_SRC = 6144
IN_TN = 1024
IN_SUB = 512

VMEM_LIMIT = 58 * 1024 * 1024


def _cparams(n_axes):
    return pltpu.CompilerParams(dimension_semantics=("arbitrary",) * n_axes, vmem_limit_bytes=VMEM_LIMIT)


def _sigmoid(x):
    return 1.0 / (1.0 + jnp.exp(-x))


def _silu(x):
    return x * _sigmoid(x)


def _log_sigmoid(x):
    return jnp.minimum(x, 0.0) - jnp.log(1.0 + jnp.exp(-jnp.abs(x)))


def _rms_kernel(x_ref, g_ref, o_ref):
    x = x_ref[...]
    ms = jnp.mean(x * x, axis=-1, keepdims=True)
    o_ref[...] = (x * lax.rsqrt(ms + EPS) * g_ref[...]).astype(o_ref.dtype)


def _rmsnorm(x, g, out_dtype, tm):
    m, d = x.shape
    return pl.pallas_call(
        _rms_kernel,
        grid=(m // tm,),
        in_specs=[pl.BlockSpec((tm, d), lambda i: (i, 0)), pl.BlockSpec((1, d), lambda i: (0, 0))],
        out_specs=pl.BlockSpec((tm, d), lambda i: (i, 0)),
        out_shape=jax.ShapeDtypeStruct((m, d), out_dtype),
        compiler_params=_cparams(1),
        name="rmsnorm",
    )(x, g.reshape(1, d))


def _col_tiled_specs(tm, ms, width, col0=0):
    return [
        pl.BlockSpec((tm, width), lambda j, i: (jnp.maximum(i - 1, 0), col0 + j)),
        pl.BlockSpec((ms, width), lambda j, i: (0, col0 + j)),
    ]


def _row_tiled_specs(tm, ms, width):
    return [
        pl.BlockSpec((tm, width), lambda j, i: (jnp.maximum(i - 1, 0), 0)),
        pl.BlockSpec((ms, width), lambda j, i: (0, 0)),
    ]


def _down_kernel(ap_ref, as_ref, w_ref, rp_ref, rs_ref, op_ref, os_ref, wbf_ref):
    i = pl.program_id(1)

    @pl.when(i == 0)
    def _():
        wbf_ref[...] = w_ref[...].astype(BF16)
        os_ref[...] = rs_ref[...] + jnp.dot(as_ref[...], wbf_ref[...], preferred_element_type=F32)

    @pl.when(i > 0)
    def _():
        op_ref[...] = rp_ref[...] + jnp.dot(ap_ref[...], wbf_ref[...], preferred_element_type=F32)


def _down_proj(a_p, a_s, w, res_p, res_s, *, tm, tn):
    mp, k = a_p.shape
    ms = a_s.shape[0]
    n = w.shape[1]
    return pl.pallas_call(
        _down_kernel,
        grid=(n // tn, mp // tm + 1),
        in_specs=[
            *_row_tiled_specs(tm, ms, k),
            pl.BlockSpec((k, tn), lambda j, i: (0, j), pipeline_mode=pl.Buffered(1)),
            *_col_tiled_specs(tm, ms, tn),
        ],
        out_specs=_col_tiled_specs(tm, ms, tn),
        out_shape=[jax.ShapeDtypeStruct((mp, n), F32), jax.ShapeDtypeStruct((ms, n), F32)],
        scratch_shapes=[pltpu.VMEM((k, tn), BF16)],
        compiler_params=_cparams(2),
        name="down_proj",
    )(a_p, a_s, w, res_p, res_s)


_NT = (((1,), (1,)), ((), ()))
_TN = (((0,), (0,)), ((), ()))


def _in_proj_kernel(ap_ref, as_ref, w_ref, wx_ref, zp_ref, zs_ref, wbf_ref):
    j = pl.program_id(0)
    i = pl.program_id(1)
    last = pl.num_programs(0) - 1
    first_shifted = GLR_SRC // IN_TN
    keep = IN_TN - GATE_RANK

    @pl.when(i == 0)
    def _():
        @pl.when(j < first_shifted)
        def _():
            wbf_ref[...] = w_ref[...].astype(BF16)

        @pl.when(jnp.logical_and(j >= first_shifted, j < last))
        def _():
            wbf_ref[:keep, :] = w_ref[GATE_RANK:, :].astype(BF16)
            wbf_ref[keep:, :] = wx_ref[...].astype(BF16)

        @pl.when(j == last)
        def _():
            wbf_ref[:GATE_RANK, :] = wx_ref[...].astype(BF16)
            wbf_ref[GATE_RANK:LANE, :] = jnp.zeros((LANE - GATE_RANK, wbf_ref.shape[1]), BF16)

    def emit(a_ref, z_ref):
        @pl.when(j < last)
        def _():
            a = a_ref[...]
            for part in range(IN_TN // IN_SUB):
                cols = slice(part * IN_SUB, (part + 1) * IN_SUB)
                z_ref[:, cols] = lax.dot_general(a, wbf_ref[cols, :], _NT,
                                                 preferred_element_type=F32).astype(z_ref.dtype)

        @pl.when(j == last)
        def _():
            z_ref[:, :LANE] = lax.dot_general(a_ref[...], wbf_ref[:LANE, :], _NT,
                                              preferred_element_type=F32).astype(z_ref.dtype)

    @pl.when(i == 0)
    def _():
        emit(as_ref, zs_ref)

    @pl.when(i > 0)
    def _():
        emit(ap_ref, zp_ref)


def _in_proj(a_p, a_s, wt, *, tm):
    mp, k = a_p.shape
    ms = a_s.shape[0]
    npt = mp // tm
    nj = Z_GLR // IN_TN + 1
    aux_per_tile = IN_TN // GATE_RANK

    def row_p(j, i):
        return jnp.maximum(i - 1, 0)

    def aux_idx(j, i):
        return (jnp.where(j == nj - 1, GLR_SRC // GATE_RANK, (j + 1) * aux_per_tile), 0)

    return pl.pallas_call(
        _in_proj_kernel,
        grid=(nj, npt + 1),
        in_specs=[
            pl.BlockSpec((tm, k), lambda j, i: (row_p(j, i), 0)),
            pl.BlockSpec((ms, k), lambda j, i: (0, 0)),
            pl.BlockSpec((IN_TN, k), lambda j, i: (j, 0)),
            pl.BlockSpec((GATE_RANK, k), aux_idx),
        ],
        out_specs=[
            pl.BlockSpec((tm, IN_TN), lambda j, i: (row_p(j, i), j)),
            pl.BlockSpec((ms, IN_TN), lambda j, i: (0, j)),
        ],
        out_shape=[jax.ShapeDtypeStruct((mp, Z_WIDTH), BF16), jax.ShapeDtypeStruct((ms, Z_WIDTH), BF16)],
        scratch_shapes=[pltpu.VMEM((IN_TN, k), BF16)],
        compiler_params=_cparams(2),
        name="in_proj",
    )(a_p, a_s, wt, wt)


GLA_GROUP = 16


GLA_HEADS_PER_STEP = 1


def _gla_prompt_kernel(q_ref, k_ref, v_ref, r_ref, glr_ref, wg_ref, bg_ref, gn_ref,
                       og_ref, s_out_ref, st_ref):
    t = q_ref.shape[0]
    c = GLA_CHUNK
    rows = GLA_GROUP * c
    nh = GLA_HEADS_PER_STEP
    st_ref[...] = jnp.zeros_like(st_ref)
    row = lax.broadcasted_iota(jnp.int32, (c, c), 0)
    col = lax.broadcasted_iota(jnp.int32, (c, c), 1)
    causal = row >= col
    tri = causal.astype(BF16)

    def group(gi, carry):
        r0 = pl.multiple_of(gi * rows, rows)
        rs = pl.ds(r0, rows)
        glr = glr_ref[rs, :]

        for h in range(nh):
            ks = slice(h * GLA_DK, (h + 1) * GLA_DK)
            vs = slice(h * GLA_DV, (h + 1) * GLA_DV)
            zg = jnp.dot(glr, wg_ref[:, ks], preferred_element_type=F32) + bg_ref[:, ks]
            v = v_ref[rs, vs]
            vals = [dict() for _ in range(GLA_GROUP)]
            state = {"st": st_ref[h]}

            def s_gate(ci, d):
                gl = _log_sigmoid(zg[ci * c:(ci + 1) * c]) * (1.0 / GATE_TEMP)
                d["gl_hi"] = gl.astype(BF16)
                d["gl_lo"] = (gl - d["gl_hi"].astype(F32)).astype(BF16)

            def s_cumsum(ci, d):
                d["b"] = (jnp.dot(tri, d.pop("gl_hi"), preferred_element_type=F32)
                          + jnp.dot(tri, d.pop("gl_lo"), preferred_element_type=F32))

            def s_scale(ci, d):
                cr = pl.ds(r0 + ci * c, c)
                b = d.pop("b")
                bl = b[c - 1:c, :]
                q = q_ref[cr, ks].astype(F32) * (GLA_DK ** -0.5)
                k = k_ref[cr, ks].astype(F32)
                d["qb"] = (q * jnp.exp(b)).astype(BF16)
                d["kb"] = (k * jnp.exp(-b)).astype(BF16)
                d["kd"] = (k * jnp.exp(bl - b)).astype(BF16)
                d["decay"] = jnp.exp(bl)

            def s_att(ci, d):
                sl = slice(ci * c, (ci + 1) * c)
                d["att"] = lax.dot_general(d["qb"], d.pop("kb"), _NT, preferred_element_type=F32)
                d["dst"] = lax.dot_general(v[sl], d.pop("kd"), _TN, preferred_element_type=F32)

            def s_mask(ci, d):
                d["att"] = jnp.where(causal, d["att"], 0.0).astype(BF16)

            def s_intra(ci, d):
                sl = slice(ci * c, (ci + 1) * c)
                d["o"] = jnp.dot(d.pop("att"), v[sl], preferred_element_type=F32)

            def s_chain(ci, d):
                st = state["st"]
                d["st_in"] = st.astype(BF16)
                state["st"] = st * d.pop("decay") + d.pop("dst")

            def s_inter(ci, d):
                d["o"] = d["o"] + lax.dot_general(d.pop("qb"), d.pop("st_in"), _NT, preferred_element_type=F32)

            def s_finish(ci, d):
                cr = pl.ds(r0 + ci * c, c)
                on = _rms(d.pop("o"), gn_ref[:, vs])
                og_ref[cr, vs] = (on * _silu(r_ref[cr, vs].astype(F32))).astype(og_ref.dtype)

            stages = (s_gate, s_cumsum, s_scale, s_att, s_mask, s_intra, s_chain, s_inter, s_finish)
            for step in range(GLA_GROUP + len(stages) - 1):
                for si, stage in enumerate(stages):
                    ci = step - si
                    if 0 <= ci < GLA_GROUP:
                        stage(ci, vals[ci])
            st_ref[h] = state["st"]
        return carry

    lax.fori_loop(0, t // rows, group, 0)
    for h in range(nh):
        s_out_ref[0, h] = st_ref[h].T


def _gla_prompt(z, wg, bg, gn, nb, t):
    m = z.shape[0]
    nh = GLA_HEADS_PER_STEP
    kw, vw = nh * GLA_DK, nh * GLA_DV
    og, s_new = pl.pallas_call(
        _gla_prompt_kernel,
        grid=(nb, GLA_HEADS // nh),
        in_specs=[
            pl.BlockSpec((t, kw), lambda b, h: (b, Z_Q // kw + h)),
            pl.BlockSpec((t, kw), lambda b, h: (b, Z_K // kw + h)),
            pl.BlockSpec((t, vw), lambda b, h: (b, Z_V // vw + h)),
            pl.BlockSpec((t, vw), lambda b, h: (b, Z_R // vw + h)),
            pl.BlockSpec((t, LANE), lambda b, h: (b, Z_GLR // LANE)),
            pl.BlockSpec((LANE, kw), lambda b, h: (0, h)),
            pl.BlockSpec((1, kw), lambda b, h: (0, h)),
            pl.BlockSpec((1, vw), lambda b, h: (0, h)),
        ],
        out_specs=[
            pl.BlockSpec((t, vw), lambda b, h: (b, h)),
            pl.BlockSpec((1, nh, GLA_DK, GLA_DV), lambda b, h: (b, h, 0, 0)),
        ],
        out_shape=[
            jax.ShapeDtypeStruct((m, GLA_VW), BF16),
            jax.ShapeDtypeStruct((nb, GLA_HEADS, GLA_DK, GLA_DV), F32),
        ],
        scratch_shapes=[pltpu.VMEM((nh, GLA_DV, GLA_DK), F32)],
        compiler_params=_cparams(2),
        name="gla_prompt",
    )(z, z, z, z, z, wg, bg, gn)
    return og, s_new


GLA_SAMPLE_BLOCK = 4


def _gla_sample_kernel(q_ref, k_ref, v_ref, r_ref, glr_ref, wg_ref, bg_ref, gn_ref, s_ref,
                       og_ref, s_out_ref):
    def one_sequence(si, carry):
        _gla_sample_sequence(pl.program_id(0) * GLA_SAMPLE_BLOCK + si, si, q_ref, k_ref, v_ref, r_ref, glr_ref,
                             wg_ref, bg_ref, gn_ref, s_ref, og_ref, s_out_ref)
        return carry

    lax.fori_loop(0, GLA_SAMPLE_BLOCK, one_sequence, 0)


def _gla_sample_sequence(n, si, q_ref, k_ref, v_ref, r_ref, glr_ref, wg_ref, bg_ref, gn_ref, s_ref,
                         og_ref, s_out_ref):
    row = pl.ds(n, 1)

    def rows8(ref):
        return jnp.broadcast_to(ref[row, :], (SUBLANES, ref.shape[1]))

    zg = jnp.dot(rows8(glr_ref).astype(BF16), wg_ref[...], preferred_element_type=F32) + bg_ref[...]
    g = _log_sigmoid(zg) * (1.0 / GATE_TEMP)
    eg = jnp.exp(g)
    eng = jnp.exp(-g)
    eg1 = eg.astype(BF16)
    rem = eg - eg1.astype(F32)
    eg2 = rem.astype(BF16)
    eg3 = (rem - eg2.astype(F32)).astype(BF16)
    first_row = lax.broadcasted_iota(jnp.int32, (SUBLANES, LANE), 0) == 0
    ones_row = jnp.where(first_row, 1.0, 0.0).astype(BF16)
    row0 = lax.broadcasted_iota(jnp.int32, (SUBLANES, GLA_DV), 0) == 0
    q_all = rows8(q_ref) * (GLA_DK ** -0.5)
    k_all = rows8(k_ref)
    v_all = rows8(v_ref)
    r_all = r_ref[row, :]
    gn_all = gn_ref[...]
    for h in range(GLA_HEADS):
        ks = slice(h * GLA_DK, (h + 1) * GLA_DK)
        vs = slice(h * GLA_DV, (h + 1) * GLA_DV)
        q, k, v = q_all[:, ks], k_all[:, ks], v_all[:, vs]
        s_old = s_ref[si, h]
        decay = (lax.dot_general(eg1[:, ks], ones_row, _TN, preferred_element_type=F32)
                 + lax.dot_general(eg2[:, ks], ones_row, _TN, preferred_element_type=F32)
                 + lax.dot_general(eg3[:, ks], ones_row, _TN, preferred_element_type=F32))
        v0 = jnp.where(row0, v, 0.0)
        ds = lax.dot_general(k.astype(BF16), v0.astype(BF16), _TN, preferred_element_type=F32)
        s_out_ref[si, h] = jnp.tile(decay, (1, GLA_DV // LANE)) * s_old + ds
        qb = q * eg[:, ks]
        kb = k * eng[:, ks]
        att = jnp.sum(qb * kb, axis=-1, keepdims=True)
        o = jnp.dot(qb.astype(BF16), s_old.astype(BF16), preferred_element_type=F32) + att * v
        o = o[0:1, :]
        ms = jnp.mean(o * o, axis=-1, keepdims=True)
        on = o * lax.rsqrt(ms + EPS) * gn_all[:, vs]
        og_ref[row, vs] = on * _silu(r_all[:, vs])


def _gla_sample(z, wg, bg, gn, state):
    n = z.shape[0]
    zf = z.astype(F32)
    nb = GLA_SAMPLE_BLOCK
    og, s_new = pl.pallas_call(
        _gla_sample_kernel,
        grid=(n // nb,),
        in_specs=[
            pl.BlockSpec((n, GLA_KW), lambda i: (0, Z_Q // GLA_KW)),
            pl.BlockSpec((n, GLA_KW), lambda i: (0, Z_K // GLA_KW)),
            pl.BlockSpec((n, GLA_VW), lambda i: (0, Z_V // GLA_VW)),
            pl.BlockSpec((n, GLA_VW), lambda i: (0, Z_R // GLA_VW)),
            pl.BlockSpec((n, LANE), lambda i: (0, Z_GLR // LANE)),
            pl.BlockSpec((LANE, GLA_KW), lambda i: (0, 0)),
            pl.BlockSpec((1, GLA_KW), lambda i: (0, 0)),
            pl.BlockSpec((1, GLA_VW), lambda i: (0, 0)),
            pl.BlockSpec((nb, GLA_HEADS, GLA_DK, GLA_DV), lambda i: (i, 0, 0, 0)),
        ],
        out_specs=[
            pl.BlockSpec((n, GLA_VW), lambda i: (0, 0)),
            pl.BlockSpec((nb, GLA_HEADS, GLA_DK, GLA_DV), lambda i: (i, 0, 0, 0)),
        ],
        out_shape=[
            jax.ShapeDtypeStruct((n, GLA_VW), F32),
            jax.ShapeDtypeStruct((n, GLA_HEADS, GLA_DK, GLA_DV), F32),
        ],
        compiler_params=_cparams(1),
        name="gla_sample",
    )(zf, zf, zf, zf, zf, wg, bg, gn, state)
    return og.astype(BF16), s_new


def _pool_group_out(mix, g, wp_ref, ps_ref):
    cs = slice(g * POOL_GW, (g + 1) * POOL_GW)
    pm = jnp.dot(mix.astype(BF16), wp_ref[g].astype(BF16), preferred_element_type=F32)
    return pm * ps_ref[:, cs]


POOL_LEVELS = tuple(w.bit_length() - 1 for w in POOL_WINDOWS)
assert all(1 << lv == w for lv, w in zip(POOL_LEVELS, POOL_WINDOWS)), "windows must be powers of two"
POOL_HIST = SUBLANES * max(POOL_LEVELS)
assert POOL_HIST > POOL_BUF + 1


def _pool_prompt_kernel(u_ref, wp_ref, ps_ref, o_ref, ext_ref, lvl_ref):
    i = pl.program_id(1)
    tt = u_ref.shape[0]
    hist = POOL_HIST
    n = hist + tt

    @pl.when(i == 0)
    def _():
        ext_ref[0:hist, :] = jnp.zeros((hist, POOL_W), F32)

    @pl.when(i > 0)
    def _():
        ext_ref[0:hist, :] = ext_ref[tt:n, :]

    ext_ref[hist:n, :] = u_ref[...].astype(F32)
    pos = i * tt + lax.broadcasted_iota(jnp.int32, (tt, 1), 0)
    for g, (w, levels) in enumerate(zip(POOL_WINDOWS, POOL_LEVELS)):
        cs = slice(g * POOL_GW, (g + 1) * POOL_GW)
        cur = None
        for lev in range(levels):
            sh = 1 << lev
            start = SUBLANES * (lev + 1)
            if lev == 0:
                cur = ext_ref[start:n, cs] + ext_ref[start - sh:n - sh, cs]
            else:
                prev = lvl_ref.at[(lev - 1) % 2]
                cur = prev[start:n, :] + prev[start - sh:n - sh, :]
            if lev + 1 < levels:
                lvl_ref[lev % 2, start:n, :] = cur
        s = cur[hist - SUBLANES * levels:]
        u = ext_ref[hist:n, cs]
        cnt = jnp.minimum(pos + 1, w).astype(F32)
        mix = s / cnt - u
        o_ref[:, cs] = _pool_group_out(mix, g, wp_ref, ps_ref).astype(o_ref.dtype)


def _pool_prompt(z, w_pool, pool_scale, nb, t, tt):
    m = z.shape[0]
    nt = t // tt
    return pl.pallas_call(
        _pool_prompt_kernel,
        grid=(nb, nt),
        in_specs=[
            pl.BlockSpec((tt, POOL_W), lambda b, i: (b * nt + i, Z_U // POOL_W)),
            pl.BlockSpec((4, POOL_GW, POOL_GW), lambda b, i: (0, 0, 0)),
            pl.BlockSpec((1, POOL_W), lambda b, i: (0, 0)),
        ],
        out_specs=pl.BlockSpec((tt, POOL_W), lambda b, i: (b * nt + i, 0)),
        out_shape=jax.ShapeDtypeStruct((m, POOL_W), BF16),
        scratch_shapes=[pltpu.VMEM((tt + POOL_HIST, POOL_W), F32), pltpu.VMEM((2, tt + POOL_HIST, POOL_GW), F32)],
        compiler_params=_cparams(2),
        name="pool_prompt",
    )(z, w_pool, pool_scale)


def _pool_sample_kernel(u_ref, buf_ref, wp_ref, ps_ref, o_ref):
    for g, w in enumerate(POOL_WINDOWS):
        cs = slice(g * POOL_GW, (g + 1) * POOL_GW)
        u = u_ref[:, cs].astype(F32)
        s = u
        for d in range(1, w):
            s = s + buf_ref[POOL_BUF - d, :, cs]
        mix = s / float(w) - u
        o_ref[:, cs] = _pool_group_out(mix, g, wp_ref, ps_ref).astype(o_ref.dtype)


def _pool_sample(z, buf_t, w_pool, pool_scale):
    n = z.shape[0]
    return pl.pallas_call(
        _pool_sample_kernel,
        grid=(1,),
        in_specs=[
            pl.BlockSpec((n, POOL_W), lambda i: (0, Z_U // POOL_W)),
            pl.BlockSpec((POOL_BUF, n, POOL_W), lambda i: (0, 0, 0)),
            pl.BlockSpec((4, POOL_GW, POOL_GW), lambda i: (0, 0, 0)),
            pl.BlockSpec((1, POOL_W), lambda i: (0, 0)),
        ],
        out_specs=pl.BlockSpec((n, POOL_W), lambda i: (0, 0)),
        out_shape=jax.ShapeDtypeStruct((n, POOL_W), BF16),
        compiler_params=_cparams(1),
        name="pool_sample",
    )(z, buf_t, w_pool, pool_scale)


MERGE_SUB = 256


def _merge_kernel(ogp_ref, ogs_ref, pmp_ref, pms_ref, wa_ref, wb_ref, gap_ref, gas_ref, gbp_ref, gbs_ref,
                  op_ref, os_ref, wabf_ref, wbbf_ref):
    i = pl.program_id(1)

    @pl.when(i == 0)
    def _():
        wabf_ref[...] = wa_ref[...].astype(BF16)
        wbbf_ref[...] = wb_ref[...].astype(BF16)

    def emit(og_ref, pm_ref, ga_ref, gb_ref, o_ref):
        og = og_ref[...]
        pm = pm_ref[...]
        for part in range(o_ref.shape[1] // MERGE_SUB):
            cols = slice(part * MERGE_SUB, (part + 1) * MERGE_SUB)
            ya = jnp.dot(og, wabf_ref[:, cols], preferred_element_type=F32)
            yb = jnp.dot(pm, wbbf_ref[:, cols], preferred_element_type=F32)
            ga = ga_ref[:, cols].astype(F32)
            gb = gb_ref[:, cols].astype(F32)
            o_ref[:, cols] = (_sigmoid(ga) * ya + _sigmoid(gb) * yb).astype(o_ref.dtype)

    @pl.when(i == 0)
    def _():
        emit(ogs_ref, pms_ref, gas_ref, gbs_ref, os_ref)

    @pl.when(i > 0)
    def _():
        emit(ogp_ref, pmp_ref, gap_ref, gbp_ref, op_ref)


def _merge(og_p, og_s, pm_p, pm_s, wa, wb, z_p, z_s, *, tm, tn):
    mp = og_p.shape[0]
    ms = og_s.shape[0]
    return pl.pallas_call(
        _merge_kernel,
        grid=(D_MODEL // tn, mp // tm + 1),
        in_specs=[
            *_row_tiled_specs(tm, ms, GLA_VW),
            *_row_tiled_specs(tm, ms, POOL_W),
            pl.BlockSpec((GLA_VW, tn), lambda j, i: (0, j)),
            pl.BlockSpec((POOL_W, tn), lambda j, i: (0, j)),
            *_col_tiled_specs(tm, ms, tn, Z_GA // tn),
            *_col_tiled_specs(tm, ms, tn, Z_GB // tn),
        ],
        out_specs=_col_tiled_specs(tm, ms, tn),
        out_shape=[jax.ShapeDtypeStruct((mp, D_MODEL), BF16), jax.ShapeDtypeStruct((ms, D_MODEL), BF16)],
        scratch_shapes=[pltpu.VMEM((GLA_VW, tn), BF16), pltpu.VMEM((POOL_W, tn), BF16)],
        compiler_params=_cparams(2),
        name="merge",
    )(og_p, og_s, pm_p, pm_s, wa, wb, z_p, z_s, z_p, z_s)


UP_TN = 512
UP_SUB = 256
UP_ROWS = 1024
UP_PARTS = UP_TN // LANE


def _up_kernel(*refs, tiles_per_seq):
    cp_ref, cs_ref, wg_ref = refs[:3]
    wu_refs = refs[3:3 + UP_PARTS]
    (cw_ref, cb_ref, h0_ref, h1_ref, actp_ref, acts_ref, tailp_ref, gates_ref,
     wgbf_ref, wubf_ref, carry_ref) = refs[3 + UP_PARTS:]
    i = pl.program_id(1)

    @pl.when(i == 0)
    def _():
        wgbf_ref[...] = wg_ref[...].astype(BF16)
        for part, wu_ref in enumerate(wu_refs):
            wubf_ref[:, part * LANE:(part + 1) * LANE] = wu_ref[...].astype(BF16)

    def conv_act(gate, up, g1, g2, cols):
        cw = cw_ref[:, cols]
        gconv = cb_ref[:, cols] + cw[0:1] * g2 + cw[1:2] * g1 + cw[2:3] * gate
        return _silu(gconv) * up

    @pl.when(i > 0)
    def _():
        @pl.when((i - 1) % tiles_per_seq == 0)
        def _():
            carry_ref[...] = jnp.zeros_like(carry_ref)

        tm = cp_ref.shape[0]
        rows = min(tm, UP_ROWS)
        sub = lax.broadcasted_iota(jnp.int32, (SUBLANES, UP_SUB), 0)
        n_cols = UP_TN // UP_SUB
        prev = [carry_ref[:, h * UP_SUB:(h + 1) * UP_SUB] for h in range(n_cols)]
        for rt in range(tm // rows):
            rsl = slice(rt * rows, (rt + 1) * rows)
            c = cp_ref[rsl, :]
            for h in range(n_cols):
                cols = slice(h * UP_SUB, (h + 1) * UP_SUB)
                gate = jnp.dot(c, wgbf_ref[:, cols], preferred_element_type=F32)
                up = jnp.dot(c, wubf_ref[:, cols], preferred_element_type=F32)
                r1 = pltpu.roll(gate, 1, 0)
                r2 = pltpu.roll(gate, 2, 0)
                head1 = jnp.where(sub < 1, pltpu.roll(prev[h], 1, 0), r1[0:SUBLANES])
                head2 = jnp.where(sub < 2, pltpu.roll(prev[h], 2, 0), r2[0:SUBLANES])
                g1 = jnp.concatenate([head1, r1[SUBLANES:]], axis=0)
                g2 = jnp.concatenate([head2, r2[SUBLANES:]], axis=0)
                prev[h] = gate[rows - SUBLANES:rows]
                actp_ref[rsl, cols] = conv_act(gate, up, g1, g2, cols).astype(actp_ref.dtype)
        for h in range(n_cols):
            cols = slice(h * UP_SUB, (h + 1) * UP_SUB)
            carry_ref[:, cols] = prev[h]
            tailp_ref[0, :, cols] = prev[h][SUBLANES - (CONV_W - 1):SUBLANES]

    @pl.when(i == 0)
    def _():
        c = cs_ref[...]
        gate = jnp.dot(c, wgbf_ref[...], preferred_element_type=F32)
        up = jnp.dot(c, wubf_ref[...], preferred_element_type=F32)
        gates_ref[...] = gate
        acts_ref[...] = conv_act(gate, up, h1_ref[...], h0_ref[...], slice(None)).astype(acts_ref.dtype)


def _up_proj(c_p, c_s, w_up, conv_w, conv_b, hist0, hist1, *, tm, seq_len):
    mp, k = c_p.shape
    ms = c_s.shape[0]
    npt = mp // tm
    tiles_per_seq = seq_len // tm
    nseq = mp // seq_len
    tn = UP_TN
    nj = pl.cdiv(D_FF, tn)
    up_block0 = D_FF // LANE
    last_block = w_up.shape[1] // LANE - 1

    def row_p(j, i):
        return jnp.maximum(i - 1, 0)

    def wu_spec(part):
        return pl.BlockSpec(
            (k, LANE), lambda j, i: (0, jnp.minimum(up_block0 + j * UP_PARTS + part, last_block)))

    in_specs = [
        pl.BlockSpec((tm, k), lambda j, i: (row_p(j, i), 0)),
        pl.BlockSpec((ms, k), lambda j, i: (0, 0)),
        pl.BlockSpec((k, tn), lambda j, i: (0, j)),
        *[wu_spec(part) for part in range(UP_PARTS)],
        pl.BlockSpec((CONV_W, tn), lambda j, i: (0, j)),
        pl.BlockSpec((1, tn), lambda j, i: (0, j)),
        pl.BlockSpec((ms, tn), lambda j, i: (0, j)),
        pl.BlockSpec((ms, tn), lambda j, i: (0, j)),
    ]
    out_specs = [
        pl.BlockSpec((tm, tn), lambda j, i: (row_p(j, i), j)),
        pl.BlockSpec((ms, tn), lambda j, i: (0, j)),
        pl.BlockSpec((1, CONV_W - 1, tn), lambda j, i: (row_p(j, i) // tiles_per_seq, 0, j)),
        pl.BlockSpec((ms, tn), lambda j, i: (0, j)),
    ]
    out_shape = [
        jax.ShapeDtypeStruct((mp, D_FF), BF16),
        jax.ShapeDtypeStruct((ms, D_FF), BF16),
        jax.ShapeDtypeStruct((nseq, CONV_W - 1, D_FF), F32),
        jax.ShapeDtypeStruct((ms, D_FF), F32),
    ]
    return pl.pallas_call(
        functools.partial(_up_kernel, tiles_per_seq=tiles_per_seq),
        grid=(nj, npt + 1),
        in_specs=in_specs,
        out_specs=out_specs,
        out_shape=out_shape,
        scratch_shapes=[pltpu.VMEM((k, tn), BF16), pltpu.VMEM((k, tn), BF16), pltpu.VMEM((SUBLANES, tn), F32)],
        compiler_params=_cparams(2),
        name="up_proj",
    )(c_p, c_s, w_up, *([w_up] * UP_PARTS), conv_w, conv_b, hist0, hist1)


def _rms(x, g):
    return x * lax.rsqrt(jnp.mean(x * x, axis=-1, keepdims=True) + EPS) * g


def _two_group_specs(tm, ms, widths):
    specs = []
    for w in widths:
        specs.append(pl.BlockSpec((tm, w), lambda i: (jnp.maximum(i - 1, 0), 0)))
        specs.append(pl.BlockSpec((ms, w), lambda i: (0, 0)))
    return specs


def _resident(shape):
    return pl.BlockSpec(shape, lambda i: (0,) * len(shape), pipeline_mode=pl.Buffered(1))


def _out_norm_kernel(mp_ref, ms_ref, xp_ref, xs_ref, w_ref, g_ref, hp_ref, hs_ref, cp_ref, cs_ref, wbf_ref):
    i = pl.program_id(0)

    @pl.when(i == 0)
    def _():
        wbf_ref[...] = w_ref[...].astype(BF16)

    def emit(m_ref, x_ref, h_ref, c_ref):
        h = x_ref[...] + jnp.dot(m_ref[...], wbf_ref[...], preferred_element_type=F32)
        h_ref[...] = h
        c_ref[...] = _rms(h, g_ref[...]).astype(c_ref.dtype)

    @pl.when(i == 0)
    def _():
        emit(ms_ref, xs_ref, hs_ref, cs_ref)

    @pl.when(i > 0)
    def _():
        emit(mp_ref, xp_ref, hp_ref, cp_ref)


def _out_norm(m_p, m_s, x_p, x_s, w_out, g, *, tm):
    mp, d = x_p.shape
    ms = x_s.shape[0]
    return pl.pallas_call(
        _out_norm_kernel,
        grid=(mp // tm + 1,),
        in_specs=[*_two_group_specs(tm, ms, (d, d)), _resident((d, d)), _resident((1, d))],
        out_specs=_two_group_specs(tm, ms, (d, d)),
        out_shape=[
            jax.ShapeDtypeStruct((mp, d), F32), jax.ShapeDtypeStruct((ms, d), F32),
            jax.ShapeDtypeStruct((mp, d), BF16), jax.ShapeDtypeStruct((ms, d), BF16),
        ],
        scratch_shapes=[pltpu.VMEM((d, d), BF16)],
        compiler_params=_cparams(1),
        name="out_norm",
    )(m_p, m_s, x_p, x_s, w_out, g.reshape(1, d))


def _ple_kernel(hp_ref, hs_ref, pp_ref, ps_ref, wpg_ref, wple_ref, gple_ref, gfin_ref, yp_ref, ys_ref,
                wpgbf_ref, wplebf_ref, *, final_norm):
    i = pl.program_id(0)

    @pl.when(i == 0)
    def _():
        wpgbf_ref[...] = wpg_ref[...].astype(BF16)
        wplebf_ref[...] = wple_ref[...].astype(BF16)

    def emit(h_ref, p_ref, y_ref):
        h2 = h_ref[...]
        cn = _rms(h2, gple_ref[...]).astype(BF16)
        pg = _sigmoid(jnp.dot(cn, wpgbf_ref[...], preferred_element_type=F32))
        e = jnp.dot(p_ref[...].astype(BF16), wplebf_ref[...], preferred_element_type=F32)
        h3 = h2 + pg * e
        y_ref[...] = _rms(h3, gfin_ref[...]) if final_norm else h3

    @pl.when(i == 0)
    def _():
        emit(hs_ref, ps_ref, ys_ref)

    @pl.when(i > 0)
    def _():
        emit(hp_ref, pp_ref, yp_ref)


def _ple(h_p, h_s, p_p, p_s, wpg, wple, g_ple, g_final, *, tm, final_norm):
    mp, d = h_p.shape
    ms = h_s.shape[0]
    pd = p_p.shape[1]
    return pl.pallas_call(
        functools.partial(_ple_kernel, final_norm=final_norm),
        grid=(mp // tm + 1,),
        in_specs=[*_two_group_specs(tm, ms, (d, pd)), _resident((d, d)), _resident((pd, d)),
                  _resident((1, d)), _resident((1, d))],
        out_specs=_two_group_specs(tm, ms, (d,)),
        out_shape=[jax.ShapeDtypeStruct((mp, d), F32), jax.ShapeDtypeStruct((ms, d), F32)],
        scratch_shapes=[pltpu.VMEM((d, d), BF16), pltpu.VMEM((pd, d), BF16)],
        compiler_params=_cparams(1),
        name="ple",
    )(h_p, h_s, p_p, p_s, wpg, wple, g_ple.reshape(1, d), g_final.reshape(1, d))


def _mixers(z, lw, *, prompt_shape=None, state=None):
    sample = state is not None
    if sample:
        s_gla, s_pool = state
        og, gla_new = _gla_sample(z, lw["wg"], lw["bg"], lw["gn"], s_gla)
        pm = _pool_sample(z, jnp.transpose(s_pool, (1, 0, 2)), lw["w_pool"], lw["pool_scale"])
        u = z[:, Z_U:Z_U + POOL_W].astype(F32)
        pool_new = jnp.concatenate([s_pool[:, 1:], u[:, None, :]], axis=1)
    else:
        nb, t = prompt_shape
        og, gla_new = _gla_prompt(z, lw["wg"], lw["bg"], lw["gn"], nb, t)
        pm = _pool_prompt(z, lw["w_pool"], lw["pool_scale"], nb, t, 1024)
        pool_new = z.reshape(nb, t, Z_WIDTH)[:, t - POOL_BUF:, Z_U:Z_U + POOL_W].astype(F32)
    return og, pm, gla_new, pool_new


def kernel(x_prompt, x_sample, state_gla, state_pool, state_conv, p_prompt, p_sample, norm_mix, w_in, w_gate_up, b_gate, gla_norm, w_branch_a, w_pool, pool_scale, w_branch_b, w_out, norm_ffn, w_up, conv_w, conv_b, w_down, norm_ple, w_ple_gate, w_ple, norm_final):
    depth = w_in.shape[0]
    nb, t, d = x_prompt.shape
    ns = x_sample.shape[0]
    hp = x_prompt.reshape(nb * t, d)
    hs = x_sample.reshape(ns, d)
    outs = {k: [] for k in ("gp", "pp", "cp", "gs", "ps", "cs")}
    for li in range(depth):
        lw = {
            "wg": jnp.pad(w_gate_up[li], ((0, LANE - GATE_RANK), (0, 0))).astype(BF16),
            "bg": b_gate[li].reshape(1, GLA_KW),
            "gn": gla_norm[li].reshape(1, GLA_VW),
            "w_pool": w_pool[li],
            "pool_scale": pool_scale[li].reshape(1, POOL_W),
        }
        a_p = _rmsnorm(hp, norm_mix[li], BF16, 1024)
        a_s = _rmsnorm(hs, norm_mix[li], BF16, ns)
        z_p, z_s = _in_proj(a_p, a_s, w_in[li].T, tm=2048)
        og_p, pm_p, g1, p1 = _mixers(z_p, lw, prompt_shape=(nb, t))
        og_s, pm_s, g2, p2 = _mixers(z_s, lw, state=(state_gla[li], state_pool[li]))
        mg_p, mg_s = _merge(og_p, og_s, pm_p, pm_s, w_branch_a[li], w_branch_b[li], z_p, z_s, tm=1024, tn=512)
        h1_p, h1_s, c_p, c_s = _out_norm(mg_p, mg_s, hp, hs, w_out[li], norm_ffn[li], tm=512)
        s_conv = state_conv[li]
        act_p, act_s, c1, gate_s = _up_proj(c_p, c_s, w_up[li], conv_w[li], conv_b[li].reshape(1, D_FF),
                                            s_conv[:, 0], s_conv[:, 1], tm=2048, seq_len=t)
        c2 = jnp.stack([s_conv[:, 1], gate_s], axis=1)
        h2_p, h2_s = _down_proj(act_p, act_s, w_down[li], h1_p, h1_s, tm=512, tn=1024)
        hp, hs = _ple(h2_p, h2_s, p_prompt[li].reshape(nb * t, PLE_DIM), p_sample[li].reshape(ns, PLE_DIM),
                      w_ple_gate[li], w_ple[li], norm_ple[li], norm_final, tm=512,
                      final_norm=li == depth - 1)
        for key, val in zip(("gp", "pp", "cp", "gs", "ps", "cs"), (g1, p1, c1, g2, p2, c2)):
            outs[key].append(val)
    st = {k: jnp.stack(v, axis=0) for k, v in outs.items()}
    return (hp.reshape(nb, t, d), hs.reshape(ns, 1, d),
            st["gp"], st["pp"], st["cp"], st["gs"], st["ps"], st["cs"])
```

```python
import functools

import jax
import jax.numpy as jnp
from jax import lax
from jax.experimental import pallas as pl
from jax.experimental.pallas import tpu as pltpu

F32 = jnp.float32
BF16 = jnp.bfloat16

D_MODEL = 2048
GLA_HEADS = 4
GLA_DK = 256
GLA_DV = 512
GLA_KW = GLA_HEADS * GLA_DK
GLA_VW = GLA_HEADS * GLA_DV
GATE_RANK = 16
GATE_TEMP = 16.0
GLA_CHUNK = 64
POOL_WINDOWS = (2, 4, 8, 16)
POOL_GW = 256
POOL_W = 1024
POOL_BUF = 15
D_FF = 5504
CONV_W = 3
PLE_DIM = 256
EPS = 1e-6

LANE = 128
SUBLANES = 8
Z_Q, Z_K, Z_V, Z_R, Z_U, Z_GA, Z_GB, Z_GLR, Z_WIDTH = 0, 1024, 2048, 4096, 6144, 7168, 9216, 11264, 11392
GLR_SRC = 6144
IN_TN = 1024
IN_SUB = 512

VMEM_LIMIT = 58 * 1024 * 1024


def _cparams(n_axes):
    return pltpu.CompilerParams(dimension_semantics=("arbitrary",) * n_axes, vmem_limit_bytes=VMEM_LIMIT)


def _sigmoid(x):
    return 1.0 / (1.0 + jnp.exp(-x))


def _silu(x):
    return x * _sigmoid(x)


def _log_sigmoid(x):
    return jnp.minimum(x, 0.0) - jnp.log(1.0 + jnp.exp(-jnp.abs(x)))


def _rms(x, g):
    return x * lax.rsqrt(jnp.mean(x * x, axis=-1, keepdims=True) + EPS) * g


def _rms_kernel(x_ref, g_ref, o_ref):
    o_ref[...] = _rms(x_ref[...], g_ref[...]).astype(o_ref.dtype)


def _rmsnorm(x, g, out_dtype, tm):
    m, d = x.shape
    return pl.pallas_call(
        _rms_kernel,
        grid=(m // tm,),
        in_specs=[pl.BlockSpec((tm, d), lambda i: (i, 0)), pl.BlockSpec((1, d), lambda i: (0, 0))],
        out_specs=pl.BlockSpec((tm, d), lambda i: (i, 0)),
        out_shape=jax.ShapeDtypeStruct((m, d), out_dtype),
        compiler_params=_cparams(1),
        name="rmsnorm",
    )(x, g.reshape(1, d))


def _col_tiled_specs(tm, ms, width, col0=0):
    return [
        pl.BlockSpec((tm, width), lambda j, i: (jnp.maximum(i - 1, 0), col0 + j)),
        pl.BlockSpec((ms, width), lambda j, i: (0, col0 + j)),
    ]


def _row_tiled_specs(tm, ms, width):
    return [
        pl.BlockSpec((tm, width), lambda j, i: (jnp.maximum(i - 1, 0), 0)),
        pl.BlockSpec((ms, width), lambda j, i: (0, 0)),
    ]


def _down_kernel(ap_ref, as_ref, w_ref, rp_ref, rs_ref, op_ref, os_ref, wbf_ref):
    i = pl.program_id(1)

    @pl.when(i == 0)
    def _():
        wbf_ref[...] = w_ref[...].astype(BF16)
        os_ref[...] = rs_ref[...] + jnp.dot(as_ref[...], wbf_ref[...], preferred_element_type=F32)

    @pl.when(i > 0)
    def _():
        op_ref[...] = rp_ref[...] + jnp.dot(ap_ref[...], wbf_ref[...], preferred_element_type=F32)


def _down_proj(a_p, a_s, w, res_p, res_s, *, tm, tn):
    mp, k = a_p.shape
    ms = a_s.shape[0]
    n = w.shape[1]
    return pl.pallas_call(
        _down_kernel,
        grid=(n // tn, mp // tm + 1),
        in_specs=[
            *_row_tiled_specs(tm, ms, k),
            pl.BlockSpec((k, tn), lambda j, i: (0, j), pipeline_mode=pl.Buffered(1)),
            *_col_tiled_specs(tm, ms, tn),
        ],
        out_specs=_col_tiled_specs(tm, ms, tn),
        out_shape=[jax.ShapeDtypeStruct((mp, n), F32), jax.ShapeDtypeStruct((ms, n), F32)],
        scratch_shapes=[pltpu.VMEM((k, tn), BF16)],
        compiler_params=_cparams(2),
        name="down_proj",
    )(a_p, a_s, w, res_p, res_s)


_NT = (((1,), (1,)), ((), ()))
_TN = (((0,), (0,)), ((), ()))


def _in_proj_kernel(ap_ref, as_ref, w_ref, wx_ref, zp_ref, zs_ref, wbf_ref):
    j = pl.program_id(0)
    i = pl.program_id(1)
    last = pl.num_programs(0) - 1
    first_shifted = GLR_SRC // IN_TN
    keep = IN_TN - GATE_RANK

    @pl.when(i == 0)
    def _():
        @pl.when(j < first_shifted)
        def _():
            wbf_ref[...] = w_ref[...].astype(BF16)

        @pl.when(jnp.logical_and(j >= first_shifted, j < last))
        def _():
            wbf_ref[:keep, :] = w_ref[GATE_RANK:, :].astype(BF16)
            wbf_ref[keep:, :] = wx_ref[...].astype(BF16)

        @pl.when(j == last)
        def _():
            wbf_ref[:GATE_RANK, :] = wx_ref[...].astype(BF16)
            wbf_ref[GATE_RANK:LANE, :] = jnp.zeros((LANE - GATE_RANK, wbf_ref.shape[1]), BF16)

    def emit(a_ref, z_ref):
        @pl.when(j < last)
        def _():
            a = a_ref[...]
            for part in range(IN_TN // IN_SUB):
                cols = slice(part * IN_SUB, (part + 1) * IN_SUB)
                z_ref[:, cols] = lax.dot_general(a, wbf_ref[cols, :], _NT,
                                                 preferred_element_type=F32).astype(z_ref.dtype)

        @pl.when(j == last)
        def _():
            z_ref[:, :LANE] = lax.dot_general(a_ref[...], wbf_ref[:LANE, :], _NT,
                                              preferred_element_type=F32).astype(z_ref.dtype)

    @pl.when(i == 0)
    def _():
        emit(as_ref, zs_ref)

    @pl.when(i > 0)
    def _():
        emit(ap_ref, zp_ref)


def _in_proj(a_p, a_s, wt, *, tm):
    mp, k = a_p.shape
    ms = a_s.shape[0]
    npt = mp // tm
    nj = Z_GLR // IN_TN + 1
    aux_per_tile = IN_TN // GATE_RANK

    def row_p(j, i):
        return jnp.maximum(i - 1, 0)

    def aux_idx(j, i):
        return (jnp.where(j == nj - 1, GLR_SRC // GATE_RANK, (j + 1) * aux_per_tile), 0)

    return pl.pallas_call(
        _in_proj_kernel,
        grid=(nj, npt + 1),
        in_specs=[
            pl.BlockSpec((tm, k), lambda j, i: (row_p(j, i), 0)),
            pl.BlockSpec((ms, k), lambda j, i: (0, 0)),
            pl.BlockSpec((IN_TN, k), lambda j, i: (j, 0)),
            pl.BlockSpec((GATE_RANK, k), aux_idx),
        ],
        out_specs=[
            pl.BlockSpec((tm, IN_TN), lambda j, i: (row_p(j, i), j)),
            pl.BlockSpec((ms, IN_TN), lambda j, i: (0, j)),
        ],
        out_shape=[jax.ShapeDtypeStruct((mp, Z_WIDTH), BF16), jax.ShapeDtypeStruct((ms, Z_WIDTH), BF16)],
        scratch_shapes=[pltpu.VMEM((IN_TN, k), BF16)],
        compiler_params=_cparams(2),
        name="in_proj",
    )(a_p, a_s, wt, wt)


GLA_GROUP = 16


def _gla_prompt_kernel(q_ref, k_ref, v_ref, r_ref, glr_ref, wg_ref, bg_ref, gn_ref,
                       og_ref, s_out_ref, st_ref):
    t = q_ref.shape[0]
    c = GLA_CHUNK
    rows = GLA_GROUP * c
    st_ref[...] = jnp.zeros_like(st_ref)
    row = lax.broadcasted_iota(jnp.int32, (c, c), 0)
    col = lax.broadcasted_iota(jnp.int32, (c, c), 1)
    causal = row >= col
    tri = causal.astype(BF16)

    def group(gi, carry):
        r0 = pl.multiple_of(gi * rows, rows)
        rs = pl.ds(r0, rows)
        zg = jnp.dot(glr_ref[rs, :], wg_ref[...], preferred_element_type=F32) + bg_ref[...]
        v = v_ref[rs, :]
        vals = [dict() for _ in range(GLA_GROUP)]
        state = {"st": st_ref[...]}

        def s_gate(ci, d):
            gl = _log_sigmoid(zg[ci * c:(ci + 1) * c]) * (1.0 / GATE_TEMP)
            d["gl_hi"] = gl.astype(BF16)
            d["gl_lo"] = (gl - d["gl_hi"].astype(F32)).astype(BF16)

        def s_cumsum(ci, d):
            d["b"] = (jnp.dot(tri, d.pop("gl_hi"), preferred_element_type=F32)
                      + jnp.dot(tri, d.pop("gl_lo"), preferred_element_type=F32))

        def s_scale(ci, d):
            cr = pl.ds(r0 + ci * c, c)
            b = d.pop("b")
            bl = b[c - 1:c, :]
            q = q_ref[cr, :].astype(F32) * (GLA_DK ** -0.5)
            k = k_ref[cr, :].astype(F32)
            d["qb"] = (q * jnp.exp(b)).astype(BF16)
            d["kb"] = (k * jnp.exp(-b)).astype(BF16)
            d["kd"] = (k * jnp.exp(bl - b)).astype(BF16)
            d["decay"] = jnp.exp(bl)

        def s_att(ci, d):
            sl = slice(ci * c, (ci + 1) * c)
            d["att"] = lax.dot_general(d["qb"], d.pop("kb"), _NT, preferred_element_type=F32)
            d["dst"] = lax.dot_general(v[sl], d.pop("kd"), _TN, preferred_element_type=F32)

        def s_mask(ci, d):
            d["att"] = jnp.where(causal, d["att"], 0.0).astype(BF16)

        def s_intra(ci, d):
            sl = slice(ci * c, (ci + 1) * c)
            d["o"] = jnp.dot(d.pop("att"), v[sl], preferred_element_type=F32)

        def s_chain(ci, d):
            st = state["st"]
            d["st_in"] = st.astype(BF16)
            state["st"] = st * d.pop("decay") + d.pop("dst")

        def s_inter(ci, d):
            d["o"] = d["o"] + lax.dot_general(d.pop("qb"), d.pop("st_in"), _NT, preferred_element_type=F32)

        def s_finish(ci, d):
            cr = pl.ds(r0 + ci * c, c)
            on = _rms(d.pop("o"), gn_ref[...])
            og_ref[cr, :] = (on * _silu(r_ref[cr, :].astype(F32))).astype(og_ref.dtype)

        stages = (s_gate, s_cumsum, s_scale, s_att, s_mask, s_intra, s_chain, s_inter, s_finish)
        for step in range(GLA_GROUP + len(stages) - 1):
            for si, stage in enumerate(stages):
                ci = step - si
                if 0 <= ci < GLA_GROUP:
                    stage(ci, vals[ci])
        st_ref[...] = state["st"]
        return carry

    lax.fori_loop(0, t // rows, group, 0)
    s_out_ref[0, 0] = st_ref[...].T


def _gla_prompt(z, wg, bg, gn, nb, t):
    m = z.shape[0]
    og, s_new = pl.pallas_call(
        _gla_prompt_kernel,
        grid=(nb, GLA_HEADS),
        in_specs=[
            pl.BlockSpec((t, GLA_DK), lambda b, h: (b, Z_Q // GLA_DK + h)),
            pl.BlockSpec((t, GLA_DK), lambda b, h: (b, Z_K // GLA_DK + h)),
            pl.BlockSpec((t, GLA_DV), lambda b, h: (b, Z_V // GLA_DV + h)),
            pl.BlockSpec((t, GLA_DV), lambda b, h: (b, Z_R // GLA_DV + h)),
            pl.BlockSpec((t, LANE), lambda b, h: (b, Z_GLR // LANE)),
            pl.BlockSpec((LANE, GLA_DK), lambda b, h: (0, h)),
            pl.BlockSpec((1, GLA_DK), lambda b, h: (0, h)),
            pl.BlockSpec((1, GLA_DV), lambda b, h: (0, h)),
        ],
        out_specs=[
            pl.BlockSpec((t, GLA_DV), lambda b, h: (b, h)),
            pl.BlockSpec((1, 1, GLA_DK, GLA_DV), lambda b, h: (b, h, 0, 0)),
        ],
        out_shape=[
            jax.ShapeDtypeStruct((m, GLA_VW), BF16),
            jax.ShapeDtypeStruct((nb, GLA_HEADS, GLA_DK, GLA_DV), F32),
        ],
        scratch_shapes=[pltpu.VMEM((GLA_DV, GLA_DK), F32)],
        compiler_params=_cparams(2),
        name="gla_prompt",
    )(z, z, z, z, z, wg, bg, gn)
    return og, s_new


GLA_SAMPLE_BLOCK = 4


def _gla_sample_kernel(q_ref, k_ref, v_ref, r_ref, glr_ref, wg_ref, bg_ref, gn_ref, s_ref,
                       og_ref, s_out_ref):
    def one_sequence(si, carry):
        _gla_sample_sequence(pl.program_id(0) * GLA_SAMPLE_BLOCK + si, si, q_ref, k_ref, v_ref, r_ref, glr_ref,
                             wg_ref, bg_ref, gn_ref, s_ref, og_ref, s_out_ref)
        return carry

    lax.fori_loop(0, GLA_SAMPLE_BLOCK, one_sequence, 0)


def _gla_sample_sequence(n, si, q_ref, k_ref, v_ref, r_ref, glr_ref, wg_ref, bg_ref, gn_ref, s_ref,
                         og_ref, s_out_ref):
    row = pl.ds(n, 1)

    def rows8(ref):
        return jnp.broadcast_to(ref[row, :], (SUBLANES, ref.shape[1]))

    zg = jnp.dot(rows8(glr_ref).astype(BF16), wg_ref[...], preferred_element_type=F32) + bg_ref[...]
    g = _log_sigmoid(zg) * (1.0 / GATE_TEMP)
    eg = jnp.exp(g)
    eng = jnp.exp(-g)
    eg1 = eg.astype(BF16)
    rem = eg - eg1.astype(F32)
    eg2 = rem.astype(BF16)
    eg3 = (rem - eg2.astype(F32)).astype(BF16)
    first_row = lax.broadcasted_iota(jnp.int32, (SUBLANES, LANE), 0) == 0
    ones_row = jnp.where(first_row, 1.0, 0.0).astype(BF16)
    row0 = lax.broadcasted_iota(jnp.int32, (SUBLANES, GLA_DV), 0) == 0
    q_all = rows8(q_ref) * (GLA_DK ** -0.5)
    k_all = rows8(k_ref)
    v_all = rows8(v_ref)
    r_all = r_ref[row, :]
    gn_all = gn_ref[...]
    for h in range(GLA_HEADS):
        ks = slice(h * GLA_DK, (h + 1) * GLA_DK)
        vs = slice(h * GLA_DV, (h + 1) * GLA_DV)
        q, k, v = q_all[:, ks], k_all[:, ks], v_all[:, vs]
        s_old = s_ref[si, h]
        decay = (lax.dot_general(eg1[:, ks], ones_row, _TN, preferred_element_type=F32)
                 + lax.dot_general(eg2[:, ks], ones_row, _TN, preferred_element_type=F32)
                 + lax.dot_general(eg3[:, ks], ones_row, _TN, preferred_element_type=F32))
        v0 = jnp.where(row0, v, 0.0)
        ds = lax.dot_general(k.astype(BF16), v0.astype(BF16), _TN, preferred_element_type=F32)
        s_out_ref[si, h] = jnp.tile(decay, (1, GLA_DV // LANE)) * s_old + ds
        qb = q * eg[:, ks]
        kb = k * eng[:, ks]
        att = jnp.sum(qb * kb, axis=-1, keepdims=True)
        o = jnp.dot(qb.astype(BF16), s_old.astype(BF16), preferred_element_type=F32) + att * v
        o = o[0:1, :]
        ms = jnp.mean(o * o, axis=-1, keepdims=True)
        on = o * lax.rsqrt(ms + EPS) * gn_all[:, vs]
        og_ref[row, vs] = on * _silu(r_all[:, vs])


def _gla_sample(z, wg, bg, gn, state):
    n = z.shape[0]
    zf = z.astype(F32)
    nb = GLA_SAMPLE_BLOCK
    og, s_new = pl.pallas_call(
        _gla_sample_kernel,
        grid=(n // nb,),
        in_specs=[
            pl.BlockSpec((n, GLA_KW), lambda i: (0, Z_Q // GLA_KW)),
            pl.BlockSpec((n, GLA_KW), lambda i: (0, Z_K // GLA_KW)),
            pl.BlockSpec((n, GLA_VW), lambda i: (0, Z_V // GLA_VW)),
            pl.BlockSpec((n, GLA_VW), lambda i: (0, Z_R // GLA_VW)),
            pl.BlockSpec((n, LANE), lambda i: (0, Z_GLR // LANE)),
            pl.BlockSpec((LANE, GLA_KW), lambda i: (0, 0)),
            pl.BlockSpec((1, GLA_KW), lambda i: (0, 0)),
            pl.BlockSpec((1, GLA_VW), lambda i: (0, 0)),
            pl.BlockSpec((nb, GLA_HEADS, GLA_DK, GLA_DV), lambda i: (i, 0, 0, 0)),
        ],
        out_specs=[
            pl.BlockSpec((n, GLA_VW), lambda i: (0, 0)),
            pl.BlockSpec((nb, GLA_HEADS, GLA_DK, GLA_DV), lambda i: (i, 0, 0, 0)),
        ],
        out_shape=[
            jax.ShapeDtypeStruct((n, GLA_VW), F32),
            jax.ShapeDtypeStruct((n, GLA_HEADS, GLA_DK, GLA_DV), F32),
        ],
        compiler_params=_cparams(1),
        name="gla_sample",
    )(zf, zf, zf, zf, zf, wg, bg, gn, state)
    return og.astype(BF16), s_new


def _pool_group_out(mix, g, wp_ref, ps_ref):
    cs = slice(g * POOL_GW, (g + 1) * POOL_GW)
    pm = jnp.dot(mix.astype(BF16), wp_ref[g].astype(BF16), preferred_element_type=F32)
    return pm * ps_ref[:, cs]


POOL_LEVELS = tuple(w.bit_length() - 1 for w in POOL_WINDOWS)
assert all(1 << lv == w for lv, w in zip(POOL_LEVELS, POOL_WINDOWS)), "windows must be powers of two"
POOL_HIST = SUBLANES * max(POOL_LEVELS)
assert POOL_HIST > POOL_BUF + 1


def _pool_prompt_kernel(u_ref, wp_ref, ps_ref, o_ref, ext_ref, lvl_ref):
    i = pl.program_id(1)
    tt = u_ref.shape[0]
    hist = POOL_HIST
    n = hist + tt

    @pl.when(i == 0)
    def _():
        ext_ref[0:hist, :] = jnp.zeros((hist, POOL_W), F32)

    @pl.when(i > 0)
    def _():
        ext_ref[0:hist, :] = ext_ref[tt:n, :]

    ext_ref[hist:n, :] = u_ref[...].astype(F32)
    pos = i * tt + lax.broadcasted_iota(jnp.int32, (tt, 1), 0)
    for g, (w, levels) in enumerate(zip(POOL_WINDOWS, POOL_LEVELS)):
        cs = slice(g * POOL_GW, (g + 1) * POOL_GW)
        cur = None
        for lev in range(levels):
            sh = 1 << lev
            start = SUBLANES * (lev + 1)
            if lev == 0:
                cur = ext_ref[start:n, cs] + ext_ref[start - sh:n - sh, cs]
            else:
                prev = lvl_ref.at[(lev - 1) % 2]
                cur = prev[start:n, :] + prev[start - sh:n - sh, :]
            if lev + 1 < levels:
                lvl_ref[lev % 2, start:n, :] = cur
        s = cur[hist - SUBLANES * levels:]
        u = ext_ref[hist:n, cs]
        cnt = jnp.minimum(pos + 1, w).astype(F32)
        mix = s / cnt - u
        o_ref[:, cs] = _pool_group_out(mix, g, wp_ref, ps_ref).astype(o_ref.dtype)


def _pool_prompt(z, w_pool, pool_scale, nb, t, tt):
    m = z.shape[0]
    nt = t // tt
    return pl.pallas_call(
        _pool_prompt_kernel,
        grid=(nb, nt),
        in_specs=[
            pl.BlockSpec((tt, POOL_W), lambda b, i: (b * nt + i, Z_U // POOL_W)),
            pl.BlockSpec((4, POOL_GW, POOL_GW), lambda b, i: (0, 0, 0)),
            pl.BlockSpec((1, POOL_W), lambda b, i: (0, 0)),
        ],
        out_specs=pl.BlockSpec((tt, POOL_W), lambda b, i: (b * nt + i, 0)),
        out_shape=jax.ShapeDtypeStruct((m, POOL_W), BF16),
        scratch_shapes=[pltpu.VMEM((tt + POOL_HIST, POOL_W), F32), pltpu.VMEM((2, tt + POOL_HIST, POOL_GW), F32)],
        compiler_params=_cparams(2),
        name="pool_prompt",
    )(z, w_pool, pool_scale)


def _pool_sample_kernel(u_ref, buf_ref, wp_ref, ps_ref, o_ref):
    for g, w in enumerate(POOL_WINDOWS):
        cs = slice(g * POOL_GW, (g + 1) * POOL_GW)
        u = u_ref[:, cs].astype(F32)
        s = u
        for d in range(1, w):
            s = s + buf_ref[POOL_BUF - d, :, cs]
        mix = s / float(w) - u
        o_ref[:, cs] = _pool_group_out(mix, g, wp_ref, ps_ref).astype(o_ref.dtype)


def _pool_sample(z, buf_t, w_pool, pool_scale):
    n = z.shape[0]
    return pl.pallas_call(
        _pool_sample_kernel,
        grid=(1,),
        in_specs=[
            pl.BlockSpec((n, POOL_W), lambda i: (0, Z_U // POOL_W)),
            pl.BlockSpec((POOL_BUF, n, POOL_W), lambda i: (0, 0, 0)),
            pl.BlockSpec((4, POOL_GW, POOL_GW), lambda i: (0, 0, 0)),
            pl.BlockSpec((1, POOL_W), lambda i: (0, 0)),
        ],
        out_specs=pl.BlockSpec((n, POOL_W), lambda i: (0, 0)),
        out_shape=jax.ShapeDtypeStruct((n, POOL_W), BF16),
        compiler_params=_cparams(1),
        name="pool_sample",
    )(z, buf_t, w_pool, pool_scale)


MERGE_SUB = 256


def _merge_kernel(ogp_ref, ogs_ref, pmp_ref, pms_ref, wa_ref, wb_ref, gap_ref, gas_ref, gbp_ref, gbs_ref,
                  op_ref, os_ref, wabf_ref, wbbf_ref):
    i = pl.program_id(1)

    @pl.when(i == 0)
    def _():
        wabf_ref[...] = wa_ref[...].astype(BF16)
        wbbf_ref[...] = wb_ref[...].astype(BF16)

    def emit(og_ref, pm_ref, ga_ref, gb_ref, o_ref):
        og = og_ref[...]
        pm = pm_ref[...]
        for part in range(o_ref.shape[1] // MERGE_SUB):
            cols = slice(part * MERGE_SUB, (part + 1) * MERGE_SUB)
            ya = jnp.dot(og, wabf_ref[:, cols], preferred_element_type=F32)
            yb = jnp.dot(pm, wbbf_ref[:, cols], preferred_element_type=F32)
            ga = ga_ref[:, cols].astype(F32)
            gb = gb_ref[:, cols].astype(F32)
            o_ref[:, cols] = (_sigmoid(ga) * ya + _sigmoid(gb) * yb).astype(o_ref.dtype)

    @pl.when(i == 0)
    def _():
        emit(ogs_ref, pms_ref, gas_ref, gbs_ref, os_ref)

    @pl.when(i > 0)
    def _():
        emit(ogp_ref, pmp_ref, gap_ref, gbp_ref, op_ref)


def _merge(og_p, og_s, pm_p, pm_s, wa, wb, z_p, z_s, *, tm, tn):
    mp = og_p.shape[0]
    ms = og_s.shape[0]
    return pl.pallas_call(
        _merge_kernel,
        grid=(D_MODEL // tn, mp // tm + 1),
        in_specs=[
            *_row_tiled_specs(tm, ms, GLA_VW),
            *_row_tiled_specs(tm, ms, POOL_W),
            pl.BlockSpec((GLA_VW, tn), lambda j, i: (0, j), pipeline_mode=pl.Buffered(1)),
            pl.BlockSpec((POOL_W, tn), lambda j, i: (0, j), pipeline_mode=pl.Buffered(1)),
            *_col_tiled_specs(tm, ms, tn, Z_GA // tn),
            *_col_tiled_specs(tm, ms, tn, Z_GB // tn),
        ],
        out_specs=_col_tiled_specs(tm, ms, tn),
        out_shape=[jax.ShapeDtypeStruct((mp, D_MODEL), BF16), jax.ShapeDtypeStruct((ms, D_MODEL), BF16)],
        scratch_shapes=[pltpu.VMEM((GLA_VW, tn), BF16), pltpu.VMEM((POOL_W, tn), BF16)],
        compiler_params=_cparams(2),
        name="merge",
    )(og_p, og_s, pm_p, pm_s, wa, wb, z_p, z_s, z_p, z_s)


UP_TN = 512
UP_SUB = 256
UP_ROWS = 1024
UP_PARTS = UP_TN // LANE


def _up_kernel(*refs, tiles_per_seq):
    cp_ref, cs_ref, wg_ref = refs[:3]
    wu_refs = refs[3:3 + UP_PARTS]
    (cw_ref, cb_ref, h0_ref, h1_ref, actp_ref, acts_ref, tailp_ref, gates_ref,
     wgbf_ref, wubf_ref, carry_ref) = refs[3 + UP_PARTS:]
    i = pl.program_id(1)

    @pl.when(i == 0)
    def _():
        wgbf_ref[...] = wg_ref[...].astype(BF16)
        for part, wu_ref in enumerate(wu_refs):
            wubf_ref[:, part * LANE:(part + 1) * LANE] = wu_ref[...].astype(BF16)

    def conv_act(gate, up, g1, g2, cols):
        cw = cw_ref[:, cols]
        gconv = cb_ref[:, cols] + cw[0:1] * g2 + cw[1:2] * g1 + cw[2:3] * gate
        return _silu(gconv) * up

    @pl.when(i > 0)
    def _():
        @pl.when((i - 1) % tiles_per_seq == 0)
        def _():
            carry_ref[...] = jnp.zeros_like(carry_ref)

        tm = cp_ref.shape[0]
        rows = min(tm, UP_ROWS)
        sub = lax.broadcasted_iota(jnp.int32, (SUBLANES, UP_SUB), 0)
        n_cols = UP_TN // UP_SUB
        prev = [carry_ref[:, h * UP_SUB:(h + 1) * UP_SUB] for h in range(n_cols)]
        for rt in range(tm // rows):
            rsl = slice(rt * rows, (rt + 1) * rows)
            c = cp_ref[rsl, :]
            for h in range(n_cols):
                cols = slice(h * UP_SUB, (h + 1) * UP_SUB)
                gate = jnp.dot(c, wgbf_ref[:, cols], preferred_element_type=F32)
                up = jnp.dot(c, wubf_ref[:, cols], preferred_element_type=F32)
                r1 = pltpu.roll(gate, 1, 0)
                r2 = pltpu.roll(gate, 2, 0)
                head1 = jnp.where(sub < 1, pltpu.roll(prev[h], 1, 0), r1[0:SUBLANES])
                head2 = jnp.where(sub < 2, pltpu.roll(prev[h], 2, 0), r2[0:SUBLANES])
                g1 = jnp.concatenate([head1, r1[SUBLANES:]], axis=0)
                g2 = jnp.concatenate([head2, r2[SUBLANES:]], axis=0)
                prev[h] = gate[rows - SUBLANES:rows]
                actp_ref[rsl, cols] = conv_act(gate, up, g1, g2, cols).astype(actp_ref.dtype)
        for h in range(n_cols):
            cols = slice(h * UP_SUB, (h + 1) * UP_SUB)
            carry_ref[:, cols] = prev[h]
            tailp_ref[0, :, cols] = prev[h][SUBLANES - (CONV_W - 1):SUBLANES]

    @pl.when(i == 0)
    def _():
        c = cs_ref[...]
        gate = jnp.dot(c, wgbf_ref[...], preferred_element_type=F32)
        up = jnp.dot(c, wubf_ref[...], preferred_element_type=F32)
        gates_ref[...] = gate
        acts_ref[...] = conv_act(gate, up, h1_ref[...], h0_ref[...], slice(None)).astype(acts_ref.dtype)


def _up_proj(c_p, c_s, w_up, conv_w, conv_b, hist0, hist1, *, tm, seq_len):
    mp, k = c_p.shape
    ms = c_s.shape[0]
    npt = mp // tm
    tiles_per_seq = seq_len // tm
    nseq = mp // seq_len
    tn = UP_TN
    nj = pl.cdiv(D_FF, tn)
    up_block0 = D_FF // LANE
    last_block = w_up.shape[1] // LANE - 1

    def row_p(j, i):
        return jnp.maximum(i - 1, 0)

    def wu_spec(part):
        return pl.BlockSpec(
            (k, LANE), lambda j, i: (0, jnp.minimum(up_block0 + j * UP_PARTS + part, last_block)))

    in_specs = [
        pl.BlockSpec((tm, k), lambda j, i: (row_p(j, i), 0)),
        pl.BlockSpec((ms, k), lambda j, i: (0, 0)),
        pl.BlockSpec((k, tn), lambda j, i: (0, j)),
        *[wu_spec(part) for part in range(UP_PARTS)],
        pl.BlockSpec((CONV_W, tn), lambda j, i: (0, j)),
        pl.BlockSpec((1, tn), lambda j, i: (0, j)),
        pl.BlockSpec((ms, tn), lambda j, i: (0, j)),
        pl.BlockSpec((ms, tn), lambda j, i: (0, j)),
    ]
    out_specs = [
        pl.BlockSpec((tm, tn), lambda j, i: (row_p(j, i), j)),
        pl.BlockSpec((ms, tn), lambda j, i: (0, j)),
        pl.BlockSpec((1, CONV_W - 1, tn), lambda j, i: (row_p(j, i) // tiles_per_seq, 0, j)),
        pl.BlockSpec((ms, tn), lambda j, i: (0, j)),
    ]
    out_shape = [
        jax.ShapeDtypeStruct((mp, D_FF), BF16),
        jax.ShapeDtypeStruct((ms, D_FF), BF16),
        jax.ShapeDtypeStruct((nseq, CONV_W - 1, D_FF), F32),
        jax.ShapeDtypeStruct((ms, D_FF), F32),
    ]
    return pl.pallas_call(
        functools.partial(_up_kernel, tiles_per_seq=tiles_per_seq),
        grid=(nj, npt + 1),
        in_specs=in_specs,
        out_specs=out_specs,
        out_shape=out_shape,
        scratch_shapes=[pltpu.VMEM((k, tn), BF16), pltpu.VMEM((k, tn), BF16), pltpu.VMEM((SUBLANES, tn), F32)],
        compiler_params=_cparams(2),
        name="up_proj",
    )(c_p, c_s, w_up, *([w_up] * UP_PARTS), conv_w, conv_b, hist0, hist1)


def _two_group_specs(tm, ms, widths):
    specs = []
    for w in widths:
        specs.append(pl.BlockSpec((tm, w), lambda i: (jnp.maximum(i - 1, 0), 0)))
        specs.append(pl.BlockSpec((ms, w), lambda i: (0, 0)))
    return specs


def _resident(shape):
    return pl.BlockSpec(shape, lambda i: (0,) * len(shape), pipeline_mode=pl.Buffered(1))


def _out_norm_kernel(mp_ref, ms_ref, xp_ref, xs_ref, w_ref, g_ref, hp_ref, hs_ref, cp_ref, cs_ref, wbf_ref):
    i = pl.program_id(0)

    @pl.when(i == 0)
    def _():
        wbf_ref[...] = w_ref[...].astype(BF16)

    def emit(m_ref, x_ref, h_ref, c_ref):
        h = x_ref[...] + jnp.dot(m_ref[...], wbf_ref[...], preferred_element_type=F32)
        h_ref[...] = h
        c_ref[...] = _rms(h, g_ref[...]).astype(c_ref.dtype)

    @pl.when(i == 0)
    def _():
        emit(ms_ref, xs_ref, hs_ref, cs_ref)

    @pl.when(i > 0)
    def _():
        emit(mp_ref, xp_ref, hp_ref, cp_ref)


def _out_norm(m_p, m_s, x_p, x_s, w_out, g, *, tm):
    mp, d = x_p.shape
    ms = x_s.shape[0]
    return pl.pallas_call(
        _out_norm_kernel,
        grid=(mp // tm + 1,),
        in_specs=[*_two_group_specs(tm, ms, (d, d)), _resident((d, d)), _resident((1, d))],
        out_specs=_two_group_specs(tm, ms, (d, d)),
        out_shape=[
            jax.ShapeDtypeStruct((mp, d), F32), jax.ShapeDtypeStruct((ms, d), F32),
            jax.ShapeDtypeStruct((mp, d), BF16), jax.ShapeDtypeStruct((ms, d), BF16),
        ],
        scratch_shapes=[pltpu.VMEM((d, d), BF16)],
        compiler_params=_cparams(1),
        name="out_norm",
    )(m_p, m_s, x_p, x_s, w_out, g.reshape(1, d))


def _ple_kernel(hp_ref, hs_ref, pp_ref, ps_ref, wpg_ref, wple_ref, gple_ref, gfin_ref, yp_ref, ys_ref,
                wpgbf_ref, wplebf_ref, *, final_norm):
    i = pl.program_id(0)

    @pl.when(i == 0)
    def _():
        wpgbf_ref[...] = wpg_ref[...].astype(BF16)
        wplebf_ref[...] = wple_ref[...].astype(BF16)

    def emit(h_ref, p_ref, y_ref):
        h2 = h_ref[...]
        cn = _rms(h2, gple_ref[...]).astype(BF16)
        pg = _sigmoid(jnp.dot(cn, wpgbf_ref[...], preferred_element_type=F32))
        e = jnp.dot(p_ref[...].astype(BF16), wplebf_ref[...], preferred_element_type=F32)
        h3 = h2 + pg * e
        y_ref[...] = _rms(h3, gfin_ref[...]) if final_norm else h3

    @pl.when(i == 0)
    def _():
        emit(hs_ref, ps_ref, ys_ref)

    @pl.when(i > 0)
    def _():
        emit(hp_ref, pp_ref, yp_ref)


def _ple(h_p, h_s, p_p, p_s, wpg, wple, g_ple, g_final, *, tm, final_norm):
    mp, d = h_p.shape
    ms = h_s.shape[0]
    pd = p_p.shape[1]
    return pl.pallas_call(
        functools.partial(_ple_kernel, final_norm=final_norm),
        grid=(mp // tm + 1,),
        in_specs=[*_two_group_specs(tm, ms, (d, pd)), _resident((d, d)), _resident((pd, d)),
                  _resident((1, d)), _resident((1, d))],
        out_specs=_two_group_specs(tm, ms, (d,)),
        out_shape=[jax.ShapeDtypeStruct((mp, d), F32), jax.ShapeDtypeStruct((ms, d), F32)],
        scratch_shapes=[pltpu.VMEM((d, d), BF16), pltpu.VMEM((pd, d), BF16)],
        compiler_params=_cparams(1),
        name="ple",
    )(h_p, h_s, p_p, p_s, wpg, wple, g_ple.reshape(1, d), g_final.reshape(1, d))


def _mixers(z, lw, *, prompt_shape=None, state=None):
    sample = state is not None
    if sample:
        s_gla, s_pool = state
        og, gla_new = _gla_sample(z, lw["wg"], lw["bg"], lw["gn"], s_gla)
        pm = _pool_sample(z, jnp.transpose(s_pool, (1, 0, 2)), lw["w_pool"], lw["pool_scale"])
        u = z[:, Z_U:Z_U + POOL_W].astype(F32)
        pool_new = jnp.concatenate([s_pool[:, 1:], u[:, None, :]], axis=1)
    else:
        nb, t = prompt_shape
        og, gla_new = _gla_prompt(z, lw["wg"], lw["bg"], lw["gn"], nb, t)
        pm = _pool_prompt(z, lw["w_pool"], lw["pool_scale"], nb, t, 1024)
        pool_new = z.reshape(nb, t, Z_WIDTH)[:, t - POOL_BUF:, Z_U:Z_U + POOL_W].astype(F32)
    return og, pm, gla_new, pool_new


def kernel(x_prompt, x_sample, state_gla, state_pool, state_conv, p_prompt, p_sample, norm_mix, w_in, w_gate_up, b_gate, gla_norm, w_branch_a, w_pool, pool_scale, w_branch_b, w_out, norm_ffn, w_up, conv_w, conv_b, w_down, norm_ple, w_ple_gate, w_ple, norm_final):
    depth = w_in.shape[0]
    nb, t, d = x_prompt.shape
    ns = x_sample.shape[0]
    hp = x_prompt.reshape(nb * t, d)
    hs = x_sample.reshape(ns, d)
    outs = {k: [] for k in ("gp", "pp", "cp", "gs", "ps", "cs")}
    for li in range(depth):
        lw = {
            "wg": jnp.pad(w_gate_up[li], ((0, LANE - GATE_RANK), (0, 0))).astype(BF16),
            "bg": b_gate[li].reshape(1, GLA_KW),
            "gn": gla_norm[li].reshape(1, GLA_VW),
            "w_pool": w_pool[li],
            "pool_scale": pool_scale[li].reshape(1, POOL_W),
        }
        a_p = _rmsnorm(hp, norm_mix[li], BF16, 1024)
        a_s = _rmsnorm(hs, norm_mix[li], BF16, ns)
        z_p, z_s = _in_proj(a_p, a_s, w_in[li].T, tm=2048)
        og_p, pm_p, g1, p1 = _mixers(z_p, lw, prompt_shape=(nb, t))
        og_s, pm_s, g2, p2 = _mixers(z_s, lw, state=(state_gla[li], state_pool[li]))
        mg_p, mg_s = _merge(og_p, og_s, pm_p, pm_s, w_branch_a[li], w_branch_b[li], z_p, z_s, tm=1024, tn=1024)
        h1_p, h1_s, c_p, c_s = _out_norm(mg_p, mg_s, hp, hs, w_out[li], norm_ffn[li], tm=512)
        s_conv = state_conv[li]
        act_p, act_s, c1, gate_s = _up_proj(c_p, c_s, w_up[li], conv_w[li], conv_b[li].reshape(1, D_FF),
                                            s_conv[:, 0], s_conv[:, 1], tm=2048, seq_len=t)
        c2 = jnp.stack([s_conv[:, 1], gate_s], axis=1)
        h2_p, h2_s = _down_proj(act_p, act_s, w_down[li], h1_p, h1_s, tm=512, tn=1024)
        hp, hs = _ple(h2_p, h2_s, p_prompt[li].reshape(nb * t, PLE_DIM), p_sample[li].reshape(ns, PLE_DIM),
                      w_ple_gate[li], w_ple[li], norm_ple[li], norm_final, tm=512,
                      final_norm=li == depth - 1)
        for key, val in zip(("gp", "pp", "cp", "gs", "ps", "cs"), (g1, p1, c1, g2, p2, c2)):
            outs[key].append(val)
    st = {k: jnp.stack(v, axis=0) for k, v in outs.items()}
    return (hp.reshape(nb, t, d), hs.reshape(ns, 1, d),
            st["gp"], st["pp"], st["cp"], st["gs"], st["ps"], st["cs"])
```

```python
import functools

import jax
import jax.numpy as jnp
from jax import lax
from jax.experimental import pallas as pl
from jax.experimental.pallas import tpu as pltpu

F32 = jnp.float32
BF16 = jnp.bfloat16

D_MODEL = 2048
GLA_HEADS = 4
GLA_DK = 256
GLA_DV = 512
GLA_KW = GLA_HEADS * GLA_DK
GLA_VW = GLA_HEADS * GLA_DV
GATE_RANK = 16
GATE_TEMP = 16.0
GLA_CHUNK = 64
POOL_WINDOWS = (2, 4, 8, 16)
POOL_GW = 256
POOL_W = 1024
POOL_BUF = 15
D_FF = 5504
CONV_W = 3
PLE_DIM = 256
EPS = 1e-6

LANE = 128
SUBLANES = 8
Z_Q, Z_K, Z_V, Z_R, Z_U, Z_GA, Z_GB, Z_GLR, Z_WIDTH = 0, 1024, 2048, 4096, 6144, 7168, 9216, 11264, 11392
GLR_SRC = 6144
IN_TN = 1024
IN_SUB = 512

VMEM_LIMIT = 58 * 1024 * 1024


def _cparams(n_axes):
    return pltpu.CompilerParams(dimension_semantics=("arbitrary",) * n_axes, vmem_limit_bytes=VMEM_LIMIT)


def _sigmoid(x):
    return 1.0 / (1.0 + jnp.exp(-x))


def _silu(x):
    return x * _sigmoid(x)


def _log_sigmoid(x):
    return jnp.minimum(x, 0.0) - jnp.log(1.0 + jnp.exp(-jnp.abs(x)))


def _rms(x, g):
    return x * lax.rsqrt(jnp.mean(x * x, axis=-1, keepdims=True) + EPS) * g


def _rms_kernel(x_ref, g_ref, o_ref):
    o_ref[...] = _rms(x_ref[...], g_ref[...]).astype(o_ref.dtype)


def _rmsnorm(x, g, out_dtype, tm):
    m, d = x.shape
    return pl.pallas_call(
        _rms_kernel,
        grid=(m // tm,),
        in_specs=[pl.BlockSpec((tm, d), lambda i: (i, 0)), pl.BlockSpec((1, d), lambda i: (0, 0))],
        out_specs=pl.BlockSpec((tm, d), lambda i: (i, 0)),
        out_shape=jax.ShapeDtypeStruct((m, d), out_dtype),
        compiler_params=_cparams(1),
        name="rmsnorm",
    )(x, g.reshape(1, d))


def _col_tiled_specs(tm, ms, width, col0=0):
    return [
        pl.BlockSpec((tm, width), lambda j, i: (jnp.maximum(i - 1, 0), col0 + j)),
        pl.BlockSpec((ms, width), lambda j, i: (0, col0 + j)),
    ]


def _row_tiled_specs(tm, ms, width):
    return [
        pl.BlockSpec((tm, width), lambda j, i: (jnp.maximum(i - 1, 0), 0)),
        pl.BlockSpec((ms, width), lambda j, i: (0, 0)),
    ]


def _down_kernel(ap_ref, as_ref, w_ref, rp_ref, rs_ref, op_ref, os_ref, wbf_ref):
    i = pl.program_id(1)

    @pl.when(i == 0)
    def _():
        wbf_ref[...] = w_ref[...].astype(BF16)
        os_ref[...] = rs_ref[...] + jnp.dot(as_ref[...], wbf_ref[...], preferred_element_type=F32)

    @pl.when(i > 0)
    def _():
        op_ref[...] = rp_ref[...] + jnp.dot(ap_ref[...], wbf_ref[...], preferred_element_type=F32)


def _down_proj(a_p, a_s, w, res_p, res_s, *, tm, tn):
    mp, k = a_p.shape
    ms = a_s.shape[0]
    n = w.shape[1]
    return pl.pallas_call(
        _down_kernel,
        grid=(n // tn, mp // tm + 1),
        in_specs=[
            *_row_tiled_specs(tm, ms, k),
            pl.BlockSpec((k, tn), lambda j, i: (0, j), pipeline_mode=pl.Buffered(1)),
            *_col_tiled_specs(tm, ms, tn),
        ],
        out_specs=_col_tiled_specs(tm, ms, tn),
        out_shape=[jax.ShapeDtypeStruct((mp, n), F32), jax.ShapeDtypeStruct((ms, n), F32)],
        scratch_shapes=[pltpu.VMEM((k, tn), BF16)],
        compiler_params=_cparams(2),
        name="down_proj",
    )(a_p, a_s, w, res_p, res_s)


_NT = (((1,), (1,)), ((), ()))
_TN = (((0,), (0,)), ((), ()))


def _in_proj_kernel(ap_ref, as_ref, w_ref, wx_ref, zp_ref, zs_ref, wbf_ref):
    j = pl.program_id(0)
    i = pl.program_id(1)
    last = pl.num_programs(0) - 1
    first_shifted = GLR_SRC // IN_TN
    keep = IN_TN - GATE_RANK

    @pl.when(i == 0)
    def _():
        @pl.when(j < first_shifted)
        def _():
            wbf_ref[...] = w_ref[...].astype(BF16)

        @pl.when(jnp.logical_and(j >= first_shifted, j < last))
        def _():
            wbf_ref[:keep, :] = w_ref[GATE_RANK:, :].astype(BF16)
            wbf_ref[keep:, :] = wx_ref[...].astype(BF16)

        @pl.when(j == last)
        def _():
            wbf_ref[:GATE_RANK, :] = wx_ref[...].astype(BF16)
            wbf_ref[GATE_RANK:LANE, :] = jnp.zeros((LANE - GATE_RANK, wbf_ref.shape[1]), BF16)

    def emit(a_ref, z_ref):
        @pl.when(j < last)
        def _():
            a = a_ref[...]
            for part in range(IN_TN // IN_SUB):
                cols = slice(part * IN_SUB, (part + 1) * IN_SUB)
                z_ref[:, cols] = lax.dot_general(a, wbf_ref[cols, :], _NT,
                                                 preferred_element_type=F32).astype(z_ref.dtype)

        @pl.when(j == last)
        def _():
            z_ref[:, :LANE] = lax.dot_general(a_ref[...], wbf_ref[:LANE, :], _NT,
                                              preferred_element_type=F32).astype(z_ref.dtype)

    @pl.when(i == 0)
    def _():
        emit(as_ref, zs_ref)

    @pl.when(i > 0)
    def _():
        emit(ap_ref, zp_ref)


def _in_proj(a_p, a_s, wt, *, tm):
    mp, k = a_p.shape
    ms = a_s.shape[0]
    npt = mp // tm
    nj = Z_GLR // IN_TN + 1
    aux_per_tile = IN_TN // GATE_RANK

    def row_p(j, i):
        return jnp.maximum(i - 1, 0)

    def aux_idx(j, i):
        return (jnp.where(j == nj - 1, GLR_SRC // GATE_RANK, (j + 1) * aux_per_tile), 0)

    return pl.pallas_call(
        _in_proj_kernel,
        grid=(nj, npt + 1),
        in_specs=[
            pl.BlockSpec((tm, k), lambda j, i: (row_p(j, i), 0)),
            pl.BlockSpec((ms, k), lambda j, i: (0, 0)),
            pl.BlockSpec((IN_TN, k), lambda j, i: (j, 0)),
            pl.BlockSpec((GATE_RANK, k), aux_idx),
        ],
        out_specs=[
            pl.BlockSpec((tm, IN_TN), lambda j, i: (row_p(j, i), j)),
            pl.BlockSpec((ms, IN_TN), lambda j, i: (0, j)),
        ],
        out_shape=[jax.ShapeDtypeStruct((mp, Z_WIDTH), BF16), jax.ShapeDtypeStruct((ms, Z_WIDTH), F32)],
        scratch_shapes=[pltpu.VMEM((IN_TN, k), BF16)],
        compiler_params=_cparams(2),
        name="in_proj",
    )(a_p, a_s, wt, wt)


GLA_GROUP = 16


def _gla_prompt_kernel(q_ref, k_ref, v_ref, r_ref, glr_ref, wg_ref, bg_ref, gn_ref,
                       og_ref, s_out_ref, st_ref):
    t = q_ref.shape[0]
    c = GLA_CHUNK
    rows = GLA_GROUP * c
    st_ref[...] = jnp.zeros_like(st_ref)
    row = lax.broadcasted_iota(jnp.int32, (c, c), 0)
    col = lax.broadcasted_iota(jnp.int32, (c, c), 1)
    causal = row >= col
    tri = causal.astype(BF16)

    def group(gi, carry):
        r0 = pl.multiple_of(gi * rows, rows)
        rs = pl.ds(r0, rows)
        zg = jnp.dot(glr_ref[rs, :], wg_ref[...], preferred_element_type=F32) + bg_ref[...]
        v = v_ref[rs, :]
        vals = [dict() for _ in range(GLA_GROUP)]
        state = {"st": st_ref[...]}

        def s_gate(ci, d):
            gl = _log_sigmoid(zg[ci * c:(ci + 1) * c]) * (1.0 / GATE_TEMP)
            d["gl_hi"] = gl.astype(BF16)
            d["gl_lo"] = (gl - d["gl_hi"].astype(F32)).astype(BF16)

        def s_cumsum(ci, d):
            d["b"] = (jnp.dot(tri, d.pop("gl_hi"), preferred_element_type=F32)
                      + jnp.dot(tri, d.pop("gl_lo"), preferred_element_type=F32))

        def s_scale(ci, d):
            cr = pl.ds(r0 + ci * c, c)
            b = d.pop("b")
            bl = b[c - 1:c, :]
            q = q_ref[cr, :].astype(F32) * (GLA_DK ** -0.5)
            k = k_ref[cr, :].astype(F32)
            d["qb"] = (q * jnp.exp(b)).astype(BF16)
            d["kb"] = (k * jnp.exp(-b)).astype(BF16)
            d["kd"] = (k * jnp.exp(bl - b)).astype(BF16)
            d["decay"] = jnp.exp(bl)

        def s_att(ci, d):
            sl = slice(ci * c, (ci + 1) * c)
            d["att"] = lax.dot_general(d["qb"], d.pop("kb"), _NT, preferred_element_type=F32)
            d["dst"] = lax.dot_general(v[sl], d.pop("kd"), _TN, preferred_element_type=F32)

        def s_mask(ci, d):
            d["att"] = jnp.where(causal, d["att"], 0.0).astype(BF16)

        def s_intra(ci, d):
            sl = slice(ci * c, (ci + 1) * c)
            d["o"] = jnp.dot(d.pop("att"), v[sl], preferred_element_type=F32)

        def s_chain(ci, d):
            st = state["st"]
            d["st_in"] = st.astype(BF16)
            state["st"] = st * d.pop("decay") + d.pop("dst")

        def s_inter(ci, d):
            d["o"] = d["o"] + lax.dot_general(d.pop("qb"), d.pop("st_in"), _NT, preferred_element_type=F32)

        def s_finish(ci, d):
            cr = pl.ds(r0 + ci * c, c)
            on = _rms(d.pop("o"), gn_ref[...])
            og_ref[cr, :] = (on * _silu(r_ref[cr, :].astype(F32))).astype(og_ref.dtype)

        stages = (s_gate, s_cumsum, s_scale, s_att, s_mask, s_intra, s_chain, s_inter, s_finish)
        for step in range(GLA_GROUP + len(stages) - 1):
            for si, stage in enumerate(stages):
                ci = step - si
                if 0 <= ci < GLA_GROUP:
                    stage(ci, vals[ci])
        st_ref[...] = state["st"]
        return carry

    lax.fori_loop(0, t // rows, group, 0)
    s_out_ref[0, 0] = st_ref[...].T


def _gla_prompt(z, wg, bg, gn, nb, t):
    m = z.shape[0]
    og, s_new = pl.pallas_call(
        _gla_prompt_kernel,
        grid=(nb, GLA_HEADS),
        in_specs=[
            pl.BlockSpec((t, GLA_DK), lambda b, h: (b, Z_Q // GLA_DK + h)),
            pl.BlockSpec((t, GLA_DK), lambda b, h: (b, Z_K // GLA_DK + h)),
            pl.BlockSpec((t, GLA_DV), lambda b, h: (b, Z_V // GLA_DV + h)),
            pl.BlockSpec((t, GLA_DV), lambda b, h: (b, Z_R // GLA_DV + h)),
            pl.BlockSpec((t, LANE), lambda b, h: (b, Z_GLR // LANE)),
            pl.BlockSpec((LANE, GLA_DK), lambda b, h: (0, h)),
            pl.BlockSpec((1, GLA_DK), lambda b, h: (0, h)),
            pl.BlockSpec((1, GLA_DV), lambda b, h: (0, h)),
        ],
        out_specs=[
            pl.BlockSpec((t, GLA_DV), lambda b, h: (b, h)),
            pl.BlockSpec((1, 1, GLA_DK, GLA_DV), lambda b, h: (b, h, 0, 0)),
        ],
        out_shape=[
            jax.ShapeDtypeStruct((m, GLA_VW), BF16),
            jax.ShapeDtypeStruct((nb, GLA_HEADS, GLA_DK, GLA_DV), F32),
        ],
        scratch_shapes=[pltpu.VMEM((GLA_DV, GLA_DK), F32)],
        compiler_params=_cparams(2),
        name="gla_prompt",
    )(z, z, z, z, z, wg, bg, gn)
    return og, s_new


GLA_SAMPLE_BLOCK = 4


def _gla_sample_kernel(q_ref, k_ref, v_ref, r_ref, glr_ref, wg_ref, bg_ref, gn_ref, s_ref,
                       og_ref, s_out_ref):
    def one_sequence(si, carry):
        _gla_sample_sequence(pl.program_id(0) * GLA_SAMPLE_BLOCK + si, si, q_ref, k_ref, v_ref, r_ref, glr_ref,
                             wg_ref, bg_ref, gn_ref, s_ref, og_ref, s_out_ref)
        return carry

    lax.fori_loop(0, GLA_SAMPLE_BLOCK, one_sequence, 0)


def _gla_sample_sequence(n, si, q_ref, k_ref, v_ref, r_ref, glr_ref, wg_ref, bg_ref, gn_ref, s_ref,
                         og_ref, s_out_ref):
    row = pl.ds(n, 1)

    def rows8(ref):
        return jnp.broadcast_to(ref[row, :], (SUBLANES, ref.shape[1]))

    zg = jnp.dot(rows8(glr_ref).astype(BF16), wg_ref[...], preferred_element_type=F32) + bg_ref[...]
    g = _log_sigmoid(zg) * (1.0 / GATE_TEMP)
    eg = jnp.exp(g)
    eng = jnp.exp(-g)
    eg1 = eg.astype(BF16)
    rem = eg - eg1.astype(F32)
    eg2 = rem.astype(BF16)
    eg3 = (rem - eg2.astype(F32)).astype(BF16)
    first_row = lax.broadcasted_iota(jnp.int32, (SUBLANES, LANE), 0) == 0
    ones_row = jnp.where(first_row, 1.0, 0.0).astype(BF16)
    row0 = lax.broadcasted_iota(jnp.int32, (SUBLANES, GLA_DV), 0) == 0
    q_all = rows8(q_ref) * (GLA_DK ** -0.5)
    k_all = rows8(k_ref)
    v_all = rows8(v_ref)
    r_all = r_ref[row, :]
    gn_all = gn_ref[...]
    for h in range(GLA_HEADS):
        ks = slice(h * GLA_DK, (h + 1) * GLA_DK)
        vs = slice(h * GLA_DV, (h + 1) * GLA_DV)
        q, k, v = q_all[:, ks], k_all[:, ks], v_all[:, vs]
        s_old = s_ref[si, h]
        decay = (lax.dot_general(eg1[:, ks], ones_row, _TN, preferred_element_type=F32)
                 + lax.dot_general(eg2[:, ks], ones_row, _TN, preferred_element_type=F32)
                 + lax.dot_general(eg3[:, ks], ones_row, _TN, preferred_element_type=F32))
        v0 = jnp.where(row0, v, 0.0)
        ds = lax.dot_general(k.astype(BF16), v0.astype(BF16), _TN, preferred_element_type=F32)
        s_out_ref[si, h] = jnp.tile(decay, (1, GLA_DV // LANE)) * s_old + ds
        qb = q * eg[:, ks]
        kb = k * eng[:, ks]
        att = jnp.sum(qb * kb, axis=-1, keepdims=True)
        o = jnp.dot(qb.astype(BF16), s_old.astype(BF16), preferred_element_type=F32) + att * v
        o = o[0:1, :]
        ms = jnp.mean(o * o, axis=-1, keepdims=True)
        on = o * lax.rsqrt(ms + EPS) * gn_all[:, vs]
        og_ref[row, vs] = on * _silu(r_all[:, vs])


def _gla_sample(z, wg, bg, gn, state):
    n = z.shape[0]
    nb = GLA_SAMPLE_BLOCK
    og, s_new = pl.pallas_call(
        _gla_sample_kernel,
        grid=(n // nb,),
        in_specs=[
            pl.BlockSpec((n, GLA_KW), lambda i: (0, Z_Q // GLA_KW)),
            pl.BlockSpec((n, GLA_KW), lambda i: (0, Z_K // GLA_KW)),
            pl.BlockSpec((n, GLA_VW), lambda i: (0, Z_V // GLA_VW)),
            pl.BlockSpec((n, GLA_VW), lambda i: (0, Z_R // GLA_VW)),
            pl.BlockSpec((n, LANE), lambda i: (0, Z_GLR // LANE)),
            pl.BlockSpec((LANE, GLA_KW), lambda i: (0, 0)),
            pl.BlockSpec((1, GLA_KW), lambda i: (0, 0)),
            pl.BlockSpec((1, GLA_VW), lambda i: (0, 0)),
            pl.BlockSpec((nb, GLA_HEADS, GLA_DK, GLA_DV), lambda i: (i, 0, 0, 0)),
        ],
        out_specs=[
            pl.BlockSpec((n, GLA_VW), lambda i: (0, 0)),
            pl.BlockSpec((nb, GLA_HEADS, GLA_DK, GLA_DV), lambda i: (i, 0, 0, 0)),
        ],
        out_shape=[
            jax.ShapeDtypeStruct((n, GLA_VW), F32),
            jax.ShapeDtypeStruct((n, GLA_HEADS, GLA_DK, GLA_DV), F32),
        ],
        compiler_params=_cparams(1),
        name="gla_sample",
    )(z, z, z, z, z, wg, bg, gn, state)
    return og, s_new


def _pool_group_out(mix, g, wp_ref, ps_ref):
    cs = slice(g * POOL_GW, (g + 1) * POOL_GW)
    pm = jnp.dot(mix.astype(BF16), wp_ref[g].astype(BF16), preferred_element_type=F32)
    return pm * ps_ref[:, cs]


POOL_LEVELS = tuple(w.bit_length() - 1 for w in POOL_WINDOWS)
assert all(1 << lv == w for lv, w in zip(POOL_LEVELS, POOL_WINDOWS)), "windows must be powers of two"
POOL_HIST = SUBLANES * max(POOL_LEVELS)
assert POOL_HIST > POOL_BUF + 1


def _pool_prompt_kernel(u_ref, wp_ref, ps_ref, o_ref, ext_ref, lvl_ref):
    i = pl.program_id(1)
    tt = u_ref.shape[0]
    hist = POOL_HIST
    n = hist + tt

    @pl.when(i == 0)
    def _():
        ext_ref[0:hist, :] = jnp.zeros((hist, POOL_W), F32)

    @pl.when(i > 0)
    def _():
        ext_ref[0:hist, :] = ext_ref[tt:n, :]

    ext_ref[hist:n, :] = u_ref[...].astype(F32)
    pos = i * tt + lax.broadcasted_iota(jnp.int32, (tt, 1), 0)
    for g, (w, levels) in enumerate(zip(POOL_WINDOWS, POOL_LEVELS)):
        cs = slice(g * POOL_GW, (g + 1) * POOL_GW)
        cur = None
        for lev in range(levels):
            sh = 1 << lev
            start = SUBLANES * (lev + 1)
            if lev == 0:
                cur = ext_ref[start:n, cs] + ext_ref[start - sh:n - sh, cs]
            else:
                prev = lvl_ref.at[(lev - 1) % 2]
                cur = prev[start:n, :] + prev[start - sh:n - sh, :]
            if lev + 1 < levels:
                lvl_ref[lev % 2, start:n, :] = cur
        s = cur[hist - SUBLANES * levels:]
        u = ext_ref[hist:n, cs]
        cnt = jnp.minimum(pos + 1, w).astype(F32)
        mix = s / cnt - u
        o_ref[:, cs] = _pool_group_out(mix, g, wp_ref, ps_ref).astype(o_ref.dtype)


def _pool_prompt(z, w_pool, pool_scale, nb, t, tt):
    m = z.shape[0]
    nt = t // tt
    return pl.pallas_call(
        _pool_prompt_kernel,
        grid=(nb, nt),
        in_specs=[
            pl.BlockSpec((tt, POOL_W), lambda b, i: (b * nt + i, Z_U // POOL_W)),
            pl.BlockSpec((4, POOL_GW, POOL_GW), lambda b, i: (0, 0, 0)),
            pl.BlockSpec((1, POOL_W), lambda b, i: (0, 0)),
        ],
        out_specs=pl.BlockSpec((tt, POOL_W), lambda b, i: (b * nt + i, 0)),
        out_shape=jax.ShapeDtypeStruct((m, POOL_W), BF16),
        scratch_shapes=[pltpu.VMEM((tt + POOL_HIST, POOL_W), F32), pltpu.VMEM((2, tt + POOL_HIST, POOL_GW), F32)],
        compiler_params=_cparams(2),
        name="pool_prompt",
    )(z, w_pool, pool_scale)


def _pool_sample_kernel(u_ref, buf_ref, wp_ref, ps_ref, o_ref, buf_out_ref):
    buf_out_ref[0:POOL_BUF - 1] = buf_ref[1:POOL_BUF]
    buf_out_ref[POOL_BUF - 1] = u_ref[...].astype(F32)
    for g, w in enumerate(POOL_WINDOWS):
        cs = slice(g * POOL_GW, (g + 1) * POOL_GW)
        u = u_ref[:, cs].astype(F32)
        s = u
        for d in range(1, w):
            s = s + buf_ref[POOL_BUF - d, :, cs]
        mix = s / float(w) - u
        o_ref[:, cs] = _pool_group_out(mix, g, wp_ref, ps_ref).astype(o_ref.dtype)


def _pool_sample(z, buf_t, w_pool, pool_scale):
    n = z.shape[0]
    return pl.pallas_call(
        _pool_sample_kernel,
        grid=(1,),
        in_specs=[
            pl.BlockSpec((n, POOL_W), lambda i: (0, Z_U // POOL_W)),
            pl.BlockSpec((POOL_BUF, n, POOL_W), lambda i: (0, 0, 0)),
            pl.BlockSpec((4, POOL_GW, POOL_GW), lambda i: (0, 0, 0)),
            pl.BlockSpec((1, POOL_W), lambda i: (0, 0)),
        ],
        out_specs=[pl.BlockSpec((n, POOL_W), lambda i: (0, 0)),
                   pl.BlockSpec((POOL_BUF, n, POOL_W), lambda i: (0, 0, 0))],
        out_shape=[jax.ShapeDtypeStruct((n, POOL_W), BF16), jax.ShapeDtypeStruct((POOL_BUF, n, POOL_W), F32)],
        compiler_params=_cparams(1),
        name="pool_sample",
    )(z, buf_t, w_pool, pool_scale)


MERGE_SUB = 256


def _merge_kernel(ogp_ref, ogs_ref, pmp_ref, pms_ref, wa_ref, wb_ref, gap_ref, gas_ref, gbp_ref, gbs_ref,
                  op_ref, os_ref, wabf_ref, wbbf_ref):
    i = pl.program_id(1)

    @pl.when(i == 0)
    def _():
        wabf_ref[...] = wa_ref[...].astype(BF16)
        wbbf_ref[...] = wb_ref[...].astype(BF16)

    def emit(og_ref, pm_ref, ga_ref, gb_ref, o_ref):
        og = og_ref[...].astype(BF16)
        pm = pm_ref[...]
        for part in range(o_ref.shape[1] // MERGE_SUB):
            cols = slice(part * MERGE_SUB, (part + 1) * MERGE_SUB)
            ya = jnp.dot(og, wabf_ref[:, cols], preferred_element_type=F32)
            yb = jnp.dot(pm, wbbf_ref[:, cols], preferred_element_type=F32)
            ga = ga_ref[:, cols].astype(F32)
            gb = gb_ref[:, cols].astype(F32)
            o_ref[:, cols] = (_sigmoid(ga) * ya + _sigmoid(gb) * yb).astype(o_ref.dtype)

    @pl.when(i == 0)
    def _():
        emit(ogs_ref, pms_ref, gas_ref, gbs_ref, os_ref)

    @pl.when(i > 0)
    def _():
        emit(ogp_ref, pmp_ref, gap_ref, gbp_ref, op_ref)


def _merge(og_p, og_s, pm_p, pm_s, wa, wb, z_p, z_s, *, tm, tn):
    mp = og_p.shape[0]
    ms = og_s.shape[0]
    return pl.pallas_call(
        _merge_kernel,
        grid=(D_MODEL // tn, mp // tm + 1),
        in_specs=[
            *_row_tiled_specs(tm, ms, GLA_VW),
            *_row_tiled_specs(tm, ms, POOL_W),
            pl.BlockSpec((GLA_VW, tn), lambda j, i: (0, j), pipeline_mode=pl.Buffered(1)),
            pl.BlockSpec((POOL_W, tn), lambda j, i: (0, j), pipeline_mode=pl.Buffered(1)),
            *_col_tiled_specs(tm, ms, tn, Z_GA // tn),
            *_col_tiled_specs(tm, ms, tn, Z_GB // tn),
        ],
        out_specs=_col_tiled_specs(tm, ms, tn),
        out_shape=[jax.ShapeDtypeStruct((mp, D_MODEL), BF16), jax.ShapeDtypeStruct((ms, D_MODEL), BF16)],
        scratch_shapes=[pltpu.VMEM((GLA_VW, tn), BF16), pltpu.VMEM((POOL_W, tn), BF16)],
        compiler_params=_cparams(2),
        name="merge",
    )(og_p, og_s, pm_p, pm_s, wa, wb, z_p, z_s, z_p, z_s)


UP_TN = 512
UP_SUB = 256
UP_ROWS = 1024
UP_PARTS = UP_TN // LANE


def _up_kernel(*refs, tiles_per_seq):
    cp_ref, cs_ref, wg_ref = refs[:3]
    wu_refs = refs[3:3 + UP_PARTS]
    (cw_ref, cb_ref, h0_ref, h1_ref, actp_ref, acts_ref, tailp_ref, gates_ref,
     wgbf_ref, wubf_ref, carry_ref) = refs[3 + UP_PARTS:]
    i = pl.program_id(1)

    @pl.when(i == 0)
    def _():
        wgbf_ref[...] = wg_ref[...].astype(BF16)
        for part, wu_ref in enumerate(wu_refs):
            wubf_ref[:, part * LANE:(part + 1) * LANE] = wu_ref[...].astype(BF16)

    def conv_act(gate, up, g1, g2, cols):
        cw = cw_ref[:, cols]
        gconv = cb_ref[:, cols] + cw[0:1] * g2 + cw[1:2] * g1 + cw[2:3] * gate
        return _silu(gconv) * up

    @pl.when(i > 0)
    def _():
        @pl.when((i - 1) % tiles_per_seq == 0)
        def _():
            carry_ref[...] = jnp.zeros_like(carry_ref)

        tm = cp_ref.shape[0]
        rows = min(tm, UP_ROWS)
        sub = lax.broadcasted_iota(jnp.int32, (SUBLANES, UP_SUB), 0)
        n_cols = UP_TN // UP_SUB
        prev = [carry_ref[:, h * UP_SUB:(h + 1) * UP_SUB] for h in range(n_cols)]
        for rt in range(tm // rows):
            rsl = slice(rt * rows, (rt + 1) * rows)
            c = cp_ref[rsl, :]
            for h in range(n_cols):
                cols = slice(h * UP_SUB, (h + 1) * UP_SUB)
                gate = jnp.dot(c, wgbf_ref[:, cols], preferred_element_type=F32)
                up = jnp.dot(c, wubf_ref[:, cols], preferred_element_type=F32)
                r1 = pltpu.roll(gate, 1, 0)
                r2 = pltpu.roll(gate, 2, 0)
                head1 = jnp.where(sub < 1, pltpu.roll(prev[h], 1, 0), r1[0:SUBLANES])
                head2 = jnp.where(sub < 2, pltpu.roll(prev[h], 2, 0), r2[0:SUBLANES])
                g1 = jnp.concatenate([head1, r1[SUBLANES:]], axis=0)
                g2 = jnp.concatenate([head2, r2[SUBLANES:]], axis=0)
                prev[h] = gate[rows - SUBLANES:rows]
                actp_ref[rsl, cols] = conv_act(gate, up, g1, g2, cols).astype(actp_ref.dtype)
        for h in range(n_cols):
            cols = slice(h * UP_SUB, (h + 1) * UP_SUB)
            carry_ref[:, cols] = prev[h]
            tailp_ref[0, :, cols] = prev[h][SUBLANES - (CONV_W - 1):SUBLANES]

    @pl.when(i == 0)
    def _():
        c = cs_ref[...]
        gate = jnp.dot(c, wgbf_ref[...], preferred_element_type=F32)
        up = jnp.dot(c, wubf_ref[...], preferred_element_type=F32)
        gates_ref[...] = gate
        acts_ref[...] = conv_act(gate, up, h1_ref[...], h0_ref[...], slice(None)).astype(acts_ref.dtype)


def _up_proj(c_p, c_s, w_up, conv_w, conv_b, hist0, hist1, *, tm, seq_len):
    mp, k = c_p.shape
    ms = c_s.shape[0]
    npt = mp // tm
    tiles_per_seq = seq_len // tm
    nseq = mp // seq_len
    tn = UP_TN
    nj = pl.cdiv(D_FF, tn)
    up_block0 = D_FF // LANE
    last_block = w_up.shape[1] // LANE - 1

    def row_p(j, i):
        return jnp.maximum(i - 1, 0)

    def wu_spec(part):
        return pl.BlockSpec(
            (k, LANE), lambda j, i: (0, jnp.minimum(up_block0 + j * UP_PARTS + part, last_block)))

    in_specs = [
        pl.BlockSpec((tm, k), lambda j, i: (row_p(j, i), 0)),
        pl.BlockSpec((ms, k), lambda j, i: (0, 0)),
        pl.BlockSpec((k, tn), lambda j, i: (0, j)),
        *[wu_spec(part) for part in range(UP_PARTS)],
        pl.BlockSpec((CONV_W, tn), lambda j, i: (0, j)),
        pl.BlockSpec((1, tn), lambda j, i: (0, j)),
        pl.BlockSpec((ms, tn), lambda j, i: (0, j)),
        pl.BlockSpec((ms, tn), lambda j, i: (0, j)),
    ]
    out_specs = [
        pl.BlockSpec((tm, tn), lambda j, i: (row_p(j, i), j)),
        pl.BlockSpec((ms, tn), lambda j, i: (0, j)),
        pl.BlockSpec((1, CONV_W - 1, tn), lambda j, i: (row_p(j, i) // tiles_per_seq, 0, j)),
        pl.BlockSpec((ms, tn), lambda j, i: (0, j)),
    ]
    out_shape = [
        jax.ShapeDtypeStruct((mp, D_FF), BF16),
        jax.ShapeDtypeStruct((ms, D_FF), BF16),
        jax.ShapeDtypeStruct((nseq, CONV_W - 1, D_FF), F32),
        jax.ShapeDtypeStruct((ms, D_FF), F32),
    ]
    return pl.pallas_call(
        functools.partial(_up_kernel, tiles_per_seq=tiles_per_seq),
        grid=(nj, npt + 1),
        in_specs=in_specs,
        out_specs=out_specs,
        out_shape=out_shape,
        scratch_shapes=[pltpu.VMEM((k, tn), BF16), pltpu.VMEM((k, tn), BF16), pltpu.VMEM((SUBLANES, tn), F32)],
        compiler_params=_cparams(2),
        name="up_proj",
    )(c_p, c_s, w_up, *([w_up] * UP_PARTS), conv_w, conv_b, hist0, hist1)


def _two_group_specs(tm, ms, widths):
    specs = []
    for w in widths:
        specs.append(pl.BlockSpec((tm, w), lambda i: (jnp.maximum(i - 1, 0), 0)))
        specs.append(pl.BlockSpec((ms, w), lambda i: (0, 0)))
    return specs


def _resident(shape):
    return pl.BlockSpec(shape, lambda i: (0,) * len(shape), pipeline_mode=pl.Buffered(1))


def _out_norm_kernel(mp_ref, ms_ref, xp_ref, xs_ref, w_ref, g_ref, hp_ref, hs_ref, cp_ref, cs_ref, wbf_ref):
    i = pl.program_id(0)

    @pl.when(i == 0)
    def _():
        wbf_ref[...] = w_ref[...].astype(BF16)

    def emit(m_ref, x_ref, h_ref, c_ref):
        h = x_ref[...] + jnp.dot(m_ref[...], wbf_ref[...], preferred_element_type=F32)
        h_ref[...] = h
        c_ref[...] = _rms(h, g_ref[...]).astype(c_ref.dtype)

    @pl.when(i == 0)
    def _():
        emit(ms_ref, xs_ref, hs_ref, cs_ref)

    @pl.when(i > 0)
    def _():
        emit(mp_ref, xp_ref, hp_ref, cp_ref)


def _out_norm(m_p, m_s, x_p, x_s, w_out, g, *, tm):
    mp, d = x_p.shape
    ms = x_s.shape[0]
    return pl.pallas_call(
        _out_norm_kernel,
        grid=(mp // tm + 1,),
        in_specs=[*_two_group_specs(tm, ms, (d, d)), _resident((d, d)), _resident((1, d))],
        out_specs=_two_group_specs(tm, ms, (d, d)),
        out_shape=[
            jax.ShapeDtypeStruct((mp, d), F32), jax.ShapeDtypeStruct((ms, d), F32),
            jax.ShapeDtypeStruct((mp, d), BF16), jax.ShapeDtypeStruct((ms, d), BF16),
        ],
        scratch_shapes=[pltpu.VMEM((d, d), BF16)],
        compiler_params=_cparams(1),
        name="out_norm",
    )(m_p, m_s, x_p, x_s, w_out, g.reshape(1, d))


def _ple_kernel(hp_ref, hs_ref, pp_ref, ps_ref, wpg_ref, wple_ref, gple_ref, gfin_ref, yp_ref, ys_ref,
                wpgbf_ref, wplebf_ref, *, final_norm):
    i = pl.program_id(0)

    @pl.when(i == 0)
    def _():
        wpgbf_ref[...] = wpg_ref[...].astype(BF16)
        wplebf_ref[...] = wple_ref[...].astype(BF16)

    def emit(h_ref, p_ref, y_ref):
        h2 = h_ref[...]
        cn = _rms(h2, gple_ref[...]).astype(BF16)
        pg = _sigmoid(jnp.dot(cn, wpgbf_ref[...], preferred_element_type=F32))
        e = jnp.dot(p_ref[...].astype(BF16), wplebf_ref[...], preferred_element_type=F32)
        h3 = h2 + pg * e
        y_ref[...] = _rms(h3, gfin_ref[...]) if final_norm else h3

    @pl.when(i == 0)
    def _():
        emit(hs_ref, ps_ref, ys_ref)

    @pl.when(i > 0)
    def _():
        emit(hp_ref, pp_ref, yp_ref)


def _ple(h_p, h_s, p_p, p_s, wpg, wple, g_ple, g_final, *, tm, final_norm):
    mp, d = h_p.shape
    ms = h_s.shape[0]
    pd = p_p.shape[1]
    return pl.pallas_call(
        functools.partial(_ple_kernel, final_norm=final_norm),
        grid=(mp // tm + 1,),
        in_specs=[*_two_group_specs(tm, ms, (d, pd)), _resident((d, d)), _resident((pd, d)),
                  _resident((1, d)), _resident((1, d))],
        out_specs=_two_group_specs(tm, ms, (d,)),
        out_shape=[jax.ShapeDtypeStruct((mp, d), F32), jax.ShapeDtypeStruct((ms, d), F32)],
        scratch_shapes=[pltpu.VMEM((d, d), BF16), pltpu.VMEM((pd, d), BF16)],
        compiler_params=_cparams(1),
        name="ple",
    )(h_p, h_s, p_p, p_s, wpg, wple, g_ple.reshape(1, d), g_final.reshape(1, d))


def _mixers(z, lw, *, prompt_shape=None, state=None):
    sample = state is not None
    if sample:
        s_gla, s_pool = state
        og, gla_new = _gla_sample(z, lw["wg"], lw["bg"], lw["gn"], s_gla)
        pm, pool_new_t = _pool_sample(z, jnp.transpose(s_pool, (1, 0, 2)), lw["w_pool"], lw["pool_scale"])
        pool_new = jnp.transpose(pool_new_t, (1, 0, 2))
    else:
        nb, t = prompt_shape
        og, gla_new = _gla_prompt(z, lw["wg"], lw["bg"], lw["gn"], nb, t)
        pm = _pool_prompt(z, lw["w_pool"], lw["pool_scale"], nb, t, 1024)
        pool_new = z.reshape(nb, t, Z_WIDTH)[:, t - POOL_BUF:, Z_U:Z_U + POOL_W].astype(F32)
    return og, pm, gla_new, pool_new


def kernel(x_prompt, x_sample, state_gla, state_pool, state_conv, p_prompt, p_sample, norm_mix, w_in, w_gate_up, b_gate, gla_norm, w_branch_a, w_pool, pool_scale, w_branch_b, w_out, norm_ffn, w_up, conv_w, conv_b, w_down, norm_ple, w_ple_gate, w_ple, norm_final):
    depth = w_in.shape[0]
    nb, t, d = x_prompt.shape
    ns = x_sample.shape[0]
    hp = x_prompt.reshape(nb * t, d)
    hs = x_sample.reshape(ns, d)
    outs = {k: [] for k in ("gp", "pp", "cp", "gs", "ps", "cs")}
    for li in range(depth):
        lw = {
            "wg": jnp.pad(w_gate_up[li], ((0, LANE - GATE_RANK), (0, 0))).astype(BF16),
            "bg": b_gate[li].reshape(1, GLA_KW),
            "gn": gla_norm[li].reshape(1, GLA_VW),
            "w_pool": w_pool[li],
            "pool_scale": pool_scale[li].reshape(1, POOL_W),
        }
        a_p = _rmsnorm(hp, norm_mix[li], BF16, 1024)
        a_s = _rmsnorm(hs, norm_mix[li], BF16, ns)
        z_p, z_s = _in_proj(a_p, a_s, w_in[li].T, tm=2048)
        og_p, pm_p, g1, p1 = _mixers(z_p, lw, prompt_shape=(nb, t))
        og_s, pm_s, g2, p2 = _mixers(z_s, lw, state=(state_gla[li], state_pool[li]))
        mg_p, mg_s = _merge(og_p, og_s, pm_p, pm_s, w_branch_a[li], w_branch_b[li], z_p, z_s, tm=1024, tn=1024)
        h1_p, h1_s, c_p, c_s = _out_norm(mg_p, mg_s, hp, hs, w_out[li], norm_ffn[li], tm=512)
        s_conv = state_conv[li]
        act_p, act_s, c1, gate_s = _up_proj(c_p, c_s, w_up[li], conv_w[li], conv_b[li].reshape(1, D_FF),
                                            s_conv[:, 0], s_conv[:, 1], tm=2048, seq_len=t)
        c2 = jnp.stack([s_conv[:, 1], gate_s], axis=1)
        h2_p, h2_s = _down_proj(act_p, act_s, w_down[li], h1_p, h1_s, tm=512, tn=1024)
        hp, hs = _ple(h2_p, h2_s, p_prompt[li].reshape(nb * t, PLE_DIM), p_sample[li].reshape(ns, PLE_DIM),
                      w_ple_gate[li], w_ple[li], norm_ple[li], norm_final, tm=512,
                      final_norm=li == depth - 1)
        for key, val in zip(("gp", "pp", "cp", "gs", "ps", "cs"), (g1, p1, c1, g2, p2, c2)):
            outs[key].append(val)
    st = {k: jnp.stack(v, axis=0) for k, v in outs.items()}
    return (hp.reshape(nb, t, d), hs.reshape(ns, 1, d),
            st["gp"], st["pp"], st["cp"], st["gs"], st["ps"], st["cs"])
```

```python
import functools

import jax
import jax.numpy as jnp
from jax import lax
from jax.experimental import pallas as pl
from jax.experimental.pallas import tpu as pltpu

F32 = jnp.float32
BF16 = jnp.bfloat16

D_MODEL = 2048
GLA_HEADS = 4
GLA_DK = 256
GLA_DV = 512
GLA_KW = GLA_HEADS * GLA_DK
GLA_VW = GLA_HEADS * GLA_DV
GATE_RANK = 16
GATE_TEMP = 16.0
GLA_CHUNK = 64
POOL_WINDOWS = (2, 4, 8, 16)
POOL_GW = 256
POOL_W = 1024
POOL_BUF = 15
D_FF = 5504
CONV_W = 3
PLE_DIM = 256
EPS = 1e-6

LANE = 128
SUBLANES = 8
Z_Q, Z_K, Z_V, Z_R, Z_U, Z_GA, Z_GB, Z_GLR, Z_WIDTH = 0, 1024, 2048, 4096, 6144, 7168, 9216, 11264, 11392
GLR_SRC = 6144
IN_TN = 1024
IN_SUB = 512

VMEM_LIMIT = 58 * 1024 * 1024


def _cparams(n_axes):
    return pltpu.CompilerParams(dimension_semantics=("arbitrary",) * n_axes, vmem_limit_bytes=VMEM_LIMIT)


def _sigmoid(x):
    return 1.0 / (1.0 + jnp.exp(-x))


def _silu(x):
    return x * _sigmoid(x)


def _log_sigmoid(x):
    return jnp.minimum(x, 0.0) - jnp.log(1.0 + jnp.exp(-jnp.abs(x)))


def _rms(x, g):
    return x * lax.rsqrt(jnp.mean(x * x, axis=-1, keepdims=True) + EPS) * g


def _rms_kernel(x_ref, g_ref, o_ref):
    o_ref[...] = _rms(x_ref[...], g_ref[...]).astype(o_ref.dtype)


def _rmsnorm(x, g, out_dtype, tm):
    m, d = x.shape
    return pl.pallas_call(
        _rms_kernel,
        grid=(m // tm,),
        in_specs=[pl.BlockSpec((tm, d), lambda i: (i, 0)), pl.BlockSpec((1, d), lambda i: (0, 0))],
        out_specs=pl.BlockSpec((tm, d), lambda i: (i, 0)),
        out_shape=jax.ShapeDtypeStruct((m, d), out_dtype),
        compiler_params=_cparams(1),
        name="rmsnorm",
    )(x, g.reshape(1, d))


def _col_tiled_specs(tm, ms, width, col0=0):
    return [
        pl.BlockSpec((tm, width), lambda j, i: (jnp.maximum(i - 1, 0), col0 + j)),
        pl.BlockSpec((ms, width), lambda j, i: (0, col0 + j)),
    ]


def _row_tiled_specs(tm, ms, width):
    return [
        pl.BlockSpec((tm, width), lambda j, i: (jnp.maximum(i - 1, 0), 0)),
        pl.BlockSpec((ms, width), lambda j, i: (0, 0)),
    ]


def _down_kernel(ap_ref, as_ref, w_ref, rp_ref, rs_ref, op_ref, os_ref, wbf_ref):
    i = pl.program_id(1)

    @pl.when(i == 0)
    def _():
        wbf_ref[...] = w_ref[...].astype(BF16)
        os_ref[...] = rs_ref[...] + jnp.dot(as_ref[...], wbf_ref[...], preferred_element_type=F32)

    @pl.when(i > 0)
    def _():
        op_ref[...] = rp_ref[...] + jnp.dot(ap_ref[...], wbf_ref[...], preferred_element_type=F32)


def _down_proj(a_p, a_s, w, res_p, res_s, *, tm, tn):
    mp, k = a_p.shape
    ms = a_s.shape[0]
    n = w.shape[1]
    return pl.pallas_call(
        _down_kernel,
        grid=(n // tn, mp // tm + 1),
        in_specs=[
            *_row_tiled_specs(tm, ms, k),
            pl.BlockSpec((k, tn), lambda j, i: (0, j), pipeline_mode=pl.Buffered(1)),
            *_col_tiled_specs(tm, ms, tn),
        ],
        out_specs=_col_tiled_specs(tm, ms, tn),
        out_shape=[jax.ShapeDtypeStruct((mp, n), F32), jax.ShapeDtypeStruct((ms, n), F32)],
        scratch_shapes=[pltpu.VMEM((k, tn), BF16)],
        compiler_params=_cparams(2),
        name="down_proj",
    )(a_p, a_s, w, res_p, res_s)


_NT = (((1,), (1,)), ((), ()))
_TN = (((0,), (0,)), ((), ()))


def _in_proj_kernel(ap_ref, as_ref, w_ref, wx_ref, zp_ref, zs_ref, wbf_ref):
    j = pl.program_id(0)
    i = pl.program_id(1)
    last = pl.num_programs(0) - 1
    first_shifted = GLR_SRC // IN_TN
    keep = IN_TN - GATE_RANK

    @pl.when(i == 0)
    def _():
        @pl.when(j < first_shifted)
        def _():
            wbf_ref[...] = w_ref[...].astype(BF16)

        @pl.when(jnp.logical_and(j >= first_shifted, j < last))
        def _():
            wbf_ref[:keep, :] = w_ref[GATE_RANK:, :].astype(BF16)
            wbf_ref[keep:, :] = wx_ref[...].astype(BF16)

        @pl.when(j == last)
        def _():
            wbf_ref[:GATE_RANK, :] = wx_ref[...].astype(BF16)
            wbf_ref[GATE_RANK:LANE, :] = jnp.zeros((LANE - GATE_RANK, wbf_ref.shape[1]), BF16)

    def emit(a_ref, z_ref):
        @pl.when(j < last)
        def _():
            a = a_ref[...]
            for part in range(IN_TN // IN_SUB):
                cols = slice(part * IN_SUB, (part + 1) * IN_SUB)
                z_ref[:, cols] = lax.dot_general(a, wbf_ref[cols, :], _NT,
                                                 preferred_element_type=F32).astype(z_ref.dtype)

        @pl.when(j == last)
        def _():
            z_ref[:, :LANE] = lax.dot_general(a_ref[...], wbf_ref[:LANE, :], _NT,
                                              preferred_element_type=F32).astype(z_ref.dtype)

    @pl.when(i == 0)
    def _():
        emit(as_ref, zs_ref)

    @pl.when(i > 0)
    def _():
        emit(ap_ref, zp_ref)


def _in_proj(a_p, a_s, wt, *, tm):
    mp, k = a_p.shape
    ms = a_s.shape[0]
    npt = mp // tm
    nj = Z_GLR // IN_TN + 1
    aux_per_tile = IN_TN // GATE_RANK

    def row_p(j, i):
        return jnp.maximum(i - 1, 0)

    def aux_idx(j, i):
        return (jnp.where(j == nj - 1, GLR_SRC // GATE_RANK, (j + 1) * aux_per_tile), 0)

    return pl.pallas_call(
        _in_proj_kernel,
        grid=(nj, npt + 1),
        in_specs=[
            pl.BlockSpec((tm, k), lambda j, i: (row_p(j, i), 0)),
            pl.BlockSpec((ms, k), lambda j, i: (0, 0)),
            pl.BlockSpec((IN_TN, k), lambda j, i: (j, 0)),
            pl.BlockSpec((GATE_RANK, k), aux_idx),
        ],
        out_specs=[
            pl.BlockSpec((tm, IN_TN), lambda j, i: (row_p(j, i), j)),
            pl.BlockSpec((ms, IN_TN), lambda j, i: (0, j)),
        ],
        out_shape=[jax.ShapeDtypeStruct((mp, Z_WIDTH), BF16), jax.ShapeDtypeStruct((ms, Z_WIDTH), F32)],
        scratch_shapes=[pltpu.VMEM((IN_TN, k), BF16)],
        compiler_params=_cparams(2),
        name="in_proj",
    )(a_p, a_s, wt, wt)


GLA_GROUP = 8


def _gla_kernel(q_ref, k_ref, v_ref, r_ref, glr_ref, wg_ref, bg_ref, gn_ref,
                sq_ref, sk_ref, sv_ref, sr_ref, sglr_ref, swg_ref, sbg_ref, sgn_ref, s_ref,
                og_ref, s_out_ref, sog_ref, ss_out_ref, st_ref):
    c = GLA_CHUNK
    rows = GLA_GROUP * c
    gi = pl.program_id(2)
    step = (pl.program_id(0) * pl.num_programs(1) + pl.program_id(1)) * pl.num_programs(2) + gi
    n_sample = s_ref.shape[0]

    def one_sequence(si, carry):
        _gla_sample_sequence(step * n_sample + si, si, sq_ref, sk_ref, sv_ref, sr_ref, sglr_ref,
                             swg_ref, sbg_ref, sgn_ref, s_ref, sog_ref, ss_out_ref)
        return carry

    lax.fori_loop(0, n_sample, one_sequence, 0)

    @pl.when(gi == 0)
    def _():
        st_ref[...] = jnp.zeros_like(st_ref)

    row = lax.broadcasted_iota(jnp.int32, (c, c), 0)
    col = lax.broadcasted_iota(jnp.int32, (c, c), 1)
    causal = row >= col
    tri = causal.astype(BF16)

    def group():
        r0 = pl.multiple_of(gi * rows, rows)
        rs = pl.ds(r0, rows)
        zg = jnp.dot(glr_ref[rs, :], wg_ref[...], preferred_element_type=F32) + bg_ref[...]
        v = v_ref[rs, :]
        vals = [dict() for _ in range(GLA_GROUP)]
        state = {"st": st_ref[...]}

        def s_gate(ci, d):
            gl = _log_sigmoid(zg[ci * c:(ci + 1) * c]) * (1.0 / GATE_TEMP)
            d["gl_hi"] = gl.astype(BF16)
            d["gl_lo"] = (gl - d["gl_hi"].astype(F32)).astype(BF16)

        def s_cumsum(ci, d):
            d["b"] = (jnp.dot(tri, d.pop("gl_hi"), preferred_element_type=F32)
                      + jnp.dot(tri, d.pop("gl_lo"), preferred_element_type=F32))

        def s_scale(ci, d):
            cr = pl.ds(r0 + ci * c, c)
            b = d.pop("b")
            bl = b[c - 1:c, :]
            q = q_ref[cr, :].astype(F32) * (GLA_DK ** -0.5)
            k = k_ref[cr, :].astype(F32)
            d["qb"] = (q * jnp.exp(b)).astype(BF16)
            d["kb"] = (k * jnp.exp(-b)).astype(BF16)
            d["kd"] = (k * jnp.exp(bl - b)).astype(BF16)
            d["decay"] = jnp.exp(bl)

        def s_att(ci, d):
            sl = slice(ci * c, (ci + 1) * c)
            d["att"] = lax.dot_general(d["qb"], d.pop("kb"), _NT, preferred_element_type=F32)
            d["dst"] = lax.dot_general(v[sl], d.pop("kd"), _TN, preferred_element_type=F32)

        def s_mask(ci, d):
            d["att"] = jnp.where(causal, d["att"], 0.0).astype(BF16)

        def s_intra(ci, d):
            sl = slice(ci * c, (ci + 1) * c)
            d["o"] = jnp.dot(d.pop("att"), v[sl], preferred_element_type=F32)

        def s_chain(ci, d):
            st = state["st"]
            d["st_in"] = st.astype(BF16)
            state["st"] = st * d.pop("decay") + d.pop("dst")

        def s_inter(ci, d):
            d["o"] = d["o"] + lax.dot_general(d.pop("qb"), d.pop("st_in"), _NT, preferred_element_type=F32)

        def s_finish(ci, d):
            cr = pl.ds(r0 + ci * c, c)
            on = _rms(d.pop("o"), gn_ref[...])
            og_ref[cr, :] = (on * _silu(r_ref[cr, :].astype(F32))).astype(og_ref.dtype)

        stages = (s_gate, s_cumsum, s_scale, s_att, s_mask, s_intra, s_chain, s_inter, s_finish)
        for tick in range(GLA_GROUP + len(stages) - 1):
            for si, stage in enumerate(stages):
                ci = tick - si
                if 0 <= ci < GLA_GROUP:
                    stage(ci, vals[ci])
        st_ref[...] = state["st"]

    group()

    @pl.when(gi == pl.num_programs(2) - 1)
    def _():
        s_out_ref[0, 0] = st_ref[...].T


def _gla(z_p, z_s, wg, bg, gn, state, nb, t):
    m = z_p.shape[0]
    n = z_s.shape[0]
    n_groups = t // (GLA_GROUP * GLA_CHUNK)
    n_steps = nb * GLA_HEADS * n_groups
    n_sample = n // n_steps
    assert n_sample * n_steps == n

    def s_idx(b, h, g):
        return ((b * GLA_HEADS + h) * n_groups + g, 0, 0, 0)

    const2 = lambda b, h, g: (0, 0)
    og_p, s_new_p, og_s, s_new_s = pl.pallas_call(
        _gla_kernel,
        grid=(nb, GLA_HEADS, n_groups),
        in_specs=[
            pl.BlockSpec((t, GLA_DK), lambda b, h, g: (b, Z_Q // GLA_DK + h)),
            pl.BlockSpec((t, GLA_DK), lambda b, h, g: (b, Z_K // GLA_DK + h)),
            pl.BlockSpec((t, GLA_DV), lambda b, h, g: (b, Z_V // GLA_DV + h)),
            pl.BlockSpec((t, GLA_DV), lambda b, h, g: (b, Z_R // GLA_DV + h)),
            pl.BlockSpec((t, LANE), lambda b, h, g: (b, Z_GLR // LANE)),
            pl.BlockSpec((LANE, GLA_DK), lambda b, h, g: (0, h)),
            pl.BlockSpec((1, GLA_DK), lambda b, h, g: (0, h)),
            pl.BlockSpec((1, GLA_DV), lambda b, h, g: (0, h)),
            pl.BlockSpec((n, GLA_KW), lambda b, h, g: (0, Z_Q // GLA_KW)),
            pl.BlockSpec((n, GLA_KW), lambda b, h, g: (0, Z_K // GLA_KW)),
            pl.BlockSpec((n, GLA_VW), lambda b, h, g: (0, Z_V // GLA_VW)),
            pl.BlockSpec((n, GLA_VW), lambda b, h, g: (0, Z_R // GLA_VW)),
            pl.BlockSpec((n, LANE), lambda b, h, g: (0, Z_GLR // LANE)),
            pl.BlockSpec((LANE, GLA_KW), const2),
            pl.BlockSpec((1, GLA_KW), const2),
            pl.BlockSpec((1, GLA_VW), const2),
            pl.BlockSpec((n_sample, GLA_HEADS, GLA_DK, GLA_DV), s_idx),
        ],
        out_specs=[
            pl.BlockSpec((t, GLA_DV), lambda b, h, g: (b, h)),
            pl.BlockSpec((1, 1, GLA_DK, GLA_DV), lambda b, h, g: (b, h, 0, 0)),
            pl.BlockSpec((n, GLA_VW), const2),
            pl.BlockSpec((n_sample, GLA_HEADS, GLA_DK, GLA_DV), s_idx),
        ],
        out_shape=[
            jax.ShapeDtypeStruct((m, GLA_VW), BF16),
            jax.ShapeDtypeStruct((nb, GLA_HEADS, GLA_DK, GLA_DV), F32),
            jax.ShapeDtypeStruct((n, GLA_VW), F32),
            jax.ShapeDtypeStruct((n, GLA_HEADS, GLA_DK, GLA_DV), F32),
        ],
        scratch_shapes=[pltpu.VMEM((GLA_DV, GLA_DK), F32)],
        compiler_params=_cparams(3),
        name="gla",
    )(z_p, z_p, z_p, z_p, z_p, wg, bg, gn, z_s, z_s, z_s, z_s, z_s, wg, bg, gn, state)
    return og_p, s_new_p, og_s, s_new_s


def _gla_sample_sequence(n, si, q_ref, k_ref, v_ref, r_ref, glr_ref, wg_ref, bg_ref, gn_ref, s_ref,
                         og_ref, s_out_ref):
    row = pl.ds(n, 1)

    def rows8(ref):
        return jnp.broadcast_to(ref[row, :], (SUBLANES, ref.shape[1]))

    zg = jnp.dot(rows8(glr_ref).astype(BF16), wg_ref[...], preferred_element_type=F32) + bg_ref[...]
    g = _log_sigmoid(zg) * (1.0 / GATE_TEMP)
    eg = jnp.exp(g)
    eng = jnp.exp(-g)
    eg1 = eg.astype(BF16)
    rem = eg - eg1.astype(F32)
    eg2 = rem.astype(BF16)
    eg3 = (rem - eg2.astype(F32)).astype(BF16)
    first_row = lax.broadcasted_iota(jnp.int32, (SUBLANES, LANE), 0) == 0
    ones_row = jnp.where(first_row, 1.0, 0.0).astype(BF16)
    row0 = lax.broadcasted_iota(jnp.int32, (SUBLANES, GLA_DV), 0) == 0
    q_all = rows8(q_ref) * (GLA_DK ** -0.5)
    k_all = rows8(k_ref)
    v_all = rows8(v_ref)
    r_all = r_ref[row, :]
    gn_all = gn_ref[...]
    for h in range(GLA_HEADS):
        ks = slice(h * GLA_DK, (h + 1) * GLA_DK)
        vs = slice(h * GLA_DV, (h + 1) * GLA_DV)
        q, k, v = q_all[:, ks], k_all[:, ks], v_all[:, vs]
        s_old = s_ref[si, h]
        decay = (lax.dot_general(eg1[:, ks], ones_row, _TN, preferred_element_type=F32)
                 + lax.dot_general(eg2[:, ks], ones_row, _TN, preferred_element_type=F32)
                 + lax.dot_general(eg3[:, ks], ones_row, _TN, preferred_element_type=F32))
        v0 = jnp.where(row0, v, 0.0)
        ds = lax.dot_general(k.astype(BF16), v0.astype(BF16), _TN, preferred_element_type=F32)
        s_out_ref[si, h] = jnp.tile(decay, (1, GLA_DV // LANE)) * s_old + ds
        qb = q * eg[:, ks]
        kb = k * eng[:, ks]
        att = jnp.sum(qb * kb, axis=-1, keepdims=True)
        o = jnp.dot(qb.astype(BF16), s_old.astype(BF16), preferred_element_type=F32) + att * v
        o = o[0:1, :]
        ms = jnp.mean(o * o, axis=-1, keepdims=True)
        on = o * lax.rsqrt(ms + EPS) * gn_all[:, vs]
        og_ref[row, vs] = on * _silu(r_all[:, vs])


def _pool_group_out(mix, g, wp_ref, ps_ref):
    cs = slice(g * POOL_GW, (g + 1) * POOL_GW)
    pm = jnp.dot(mix.astype(BF16), wp_ref[g].astype(BF16), preferred_element_type=F32)
    return pm * ps_ref[:, cs]


POOL_LEVELS = tuple(w.bit_length() - 1 for w in POOL_WINDOWS)
assert all(1 << lv == w for lv, w in zip(POOL_LEVELS, POOL_WINDOWS)), "windows must be powers of two"
POOL_HIST = SUBLANES * max(POOL_LEVELS)
assert POOL_HIST > POOL_BUF + 1


def _pool_prompt_kernel(u_ref, wp_ref, ps_ref, o_ref, ext_ref, lvl_ref):
    i = pl.program_id(1)
    tt = u_ref.shape[0]
    hist = POOL_HIST
    n = hist + tt

    @pl.when(i == 0)
    def _():
        ext_ref[0:hist, :] = jnp.zeros((hist, POOL_W), F32)

    @pl.when(i > 0)
    def _():
        ext_ref[0:hist, :] = ext_ref[tt:n, :]

    ext_ref[hist:n, :] = u_ref[...].astype(F32)
    pos = i * tt + lax.broadcasted_iota(jnp.int32, (tt, 1), 0)
    for g, (w, levels) in enumerate(zip(POOL_WINDOWS, POOL_LEVELS)):
        cs = slice(g * POOL_GW, (g + 1) * POOL_GW)
        cur = None
        for lev in range(levels):
            sh = 1 << lev
            start = SUBLANES * (lev + 1)
            if lev == 0:
                cur = ext_ref[start:n, cs] + ext_ref[start - sh:n - sh, cs]
            else:
                prev = lvl_ref.at[(lev - 1) % 2]
                cur = prev[start:n, :] + prev[start - sh:n - sh, :]
            if lev + 1 < levels:
                lvl_ref[lev % 2, start:n, :] = cur
        s = cur[hist - SUBLANES * levels:]
        u = ext_ref[hist:n, cs]
        cnt = jnp.minimum(pos + 1, w).astype(F32)
        mix = s / cnt - u
        o_ref[:, cs] = _pool_group_out(mix, g, wp_ref, ps_ref).astype(o_ref.dtype)


def _pool_prompt(z, w_pool, pool_scale, nb, t, tt):
    m = z.shape[0]
    nt = t // tt
    return pl.pallas_call(
        _pool_prompt_kernel,
        grid=(nb, nt),
        in_specs=[
            pl.BlockSpec((tt, POOL_W), lambda b, i: (b * nt + i, Z_U // POOL_W)),
            pl.BlockSpec((4, POOL_GW, POOL_GW), lambda b, i: (0, 0, 0)),
            pl.BlockSpec((1, POOL_W), lambda b, i: (0, 0)),
        ],
        out_specs=pl.BlockSpec((tt, POOL_W), lambda b, i: (b * nt + i, 0)),
        out_shape=jax.ShapeDtypeStruct((m, POOL_W), BF16),
        scratch_shapes=[pltpu.VMEM((tt + POOL_HIST, POOL_W), F32), pltpu.VMEM((2, tt + POOL_HIST, POOL_GW), F32)],
        compiler_params=_cparams(2),
        name="pool_prompt",
    )(z, w_pool, pool_scale)


def _pool_sample_kernel(u_ref, buf_ref, wp_ref, ps_ref, o_ref, buf_out_ref):
    buf_out_ref[0:POOL_BUF - 1] = buf_ref[1:POOL_BUF]
    buf_out_ref[POOL_BUF - 1] = u_ref[...].astype(F32)
    for g, w in enumerate(POOL_WINDOWS):
        cs = slice(g * POOL_GW, (g + 1) * POOL_GW)
        u = u_ref[:, cs].astype(F32)
        s = u
        for d in range(1, w):
            s = s + buf_ref[POOL_BUF - d, :, cs]
        mix = s / float(w) - u
        o_ref[:, cs] = _pool_group_out(mix, g, wp_ref, ps_ref).astype(o_ref.dtype)


def _pool_sample(z, buf_t, w_pool, pool_scale):
    n = z.shape[0]
    return pl.pallas_call(
        _pool_sample_kernel,
        grid=(1,),
        in_specs=[
            pl.BlockSpec((n, POOL_W), lambda i: (0, Z_U // POOL_W)),
            pl.BlockSpec((POOL_BUF, n, POOL_W), lambda i: (0, 0, 0)),
            pl.BlockSpec((4, POOL_GW, POOL_GW), lambda i: (0, 0, 0)),
            pl.BlockSpec((1, POOL_W), lambda i: (0, 0)),
        ],
        out_specs=[pl.BlockSpec((n, POOL_W), lambda i: (0, 0)),
                   pl.BlockSpec((POOL_BUF, n, POOL_W), lambda i: (0, 0, 0))],
        out_shape=[jax.ShapeDtypeStruct((n, POOL_W), BF16), jax.ShapeDtypeStruct((POOL_BUF, n, POOL_W), F32)],
        compiler_params=_cparams(1),
        name="pool_sample",
    )(z, buf_t, w_pool, pool_scale)


MERGE_SUB = 256


def _merge_kernel(ogp_ref, ogs_ref, pmp_ref, pms_ref, wa_ref, wb_ref, gap_ref, gas_ref, gbp_ref, gbs_ref,
                  op_ref, os_ref, wabf_ref, wbbf_ref):
    i = pl.program_id(1)

    @pl.when(i == 0)
    def _():
        wabf_ref[...] = wa_ref[...].astype(BF16)
        wbbf_ref[...] = wb_ref[...].astype(BF16)

    def emit(og_ref, pm_ref, ga_ref, gb_ref, o_ref):
        og = og_ref[...].astype(BF16)
        pm = pm_ref[...]
        for part in range(o_ref.shape[1] // MERGE_SUB):
            cols = slice(part * MERGE_SUB, (part + 1) * MERGE_SUB)
            ya = jnp.dot(og, wabf_ref[:, cols], preferred_element_type=F32)
            yb = jnp.dot(pm, wbbf_ref[:, cols], preferred_element_type=F32)
            ga = ga_ref[:, cols].astype(F32)
            gb = gb_ref[:, cols].astype(F32)
            o_ref[:, cols] = (_sigmoid(ga) * ya + _sigmoid(gb) * yb).astype(o_ref.dtype)

    @pl.when(i == 0)
    def _():
        emit(ogs_ref, pms_ref, gas_ref, gbs_ref, os_ref)

    @pl.when(i > 0)
    def _():
        emit(ogp_ref, pmp_ref, gap_ref, gbp_ref, op_ref)


def _merge(og_p, og_s, pm_p, pm_s, wa, wb, z_p, z_s, *, tm, tn):
    mp = og_p.shape[0]
    ms = og_s.shape[0]
    return pl.pallas_call(
        _merge_kernel,
        grid=(D_MODEL // tn, mp // tm + 1),
        in_specs=[
            *_row_tiled_specs(tm, ms, GLA_VW),
            *_row_tiled_specs(tm, ms, POOL_W),
            pl.BlockSpec((GLA_VW, tn), lambda j, i: (0, j), pipeline_mode=pl.Buffered(1)),
            pl.BlockSpec((POOL_W, tn), lambda j, i: (0, j), pipeline_mode=pl.Buffered(1)),
            *_col_tiled_specs(tm, ms, tn, Z_GA // tn),
            *_col_tiled_specs(tm, ms, tn, Z_GB // tn),
        ],
        out_specs=_col_tiled_specs(tm, ms, tn),
        out_shape=[jax.ShapeDtypeStruct((mp, D_MODEL), BF16), jax.ShapeDtypeStruct((ms, D_MODEL), BF16)],
        scratch_shapes=[pltpu.VMEM((GLA_VW, tn), BF16), pltpu.VMEM((POOL_W, tn), BF16)],
        compiler_params=_cparams(2),
        name="merge",
    )(og_p, og_s, pm_p, pm_s, wa, wb, z_p, z_s, z_p, z_s)


UP_TN = 512
UP_SUB = 256
UP_ROWS = 1024
UP_PARTS = UP_TN // LANE


def _up_kernel(*refs, tiles_per_seq):
    cp_ref, cs_ref, wg_ref = refs[:3]
    wu_refs = refs[3:3 + UP_PARTS]
    (cw_ref, cb_ref, h0_ref, h1_ref, actp_ref, acts_ref, tailp_ref, gates_ref,
     wgbf_ref, wubf_ref, carry_ref) = refs[3 + UP_PARTS:]
    i = pl.program_id(1)

    @pl.when(i == 0)
    def _():
        wgbf_ref[...] = wg_ref[...].astype(BF16)
        for part, wu_ref in enumerate(wu_refs):
            wubf_ref[:, part * LANE:(part + 1) * LANE] = wu_ref[...].astype(BF16)

    def conv_act(gate, up, g1, g2, cols):
        cw = cw_ref[:, cols]
        gconv = cb_ref[:, cols] + cw[0:1] * g2 + cw[1:2] * g1 + cw[2:3] * gate
        return _silu(gconv) * up

    @pl.when(i > 0)
    def _():
        @pl.when((i - 1) % tiles_per_seq == 0)
        def _():
            carry_ref[...] = jnp.zeros_like(carry_ref)

        tm = cp_ref.shape[0]
        rows = min(tm, UP_ROWS)
        sub = lax.broadcasted_iota(jnp.int32, (SUBLANES, UP_SUB), 0)
        n_cols = UP_TN // UP_SUB
        prev = [carry_ref[:, h * UP_SUB:(h + 1) * UP_SUB] for h in range(n_cols)]
        for rt in range(tm // rows):
            rsl = slice(rt * rows, (rt + 1) * rows)
            c = cp_ref[rsl, :]
            for h in range(n_cols):
                cols = slice(h * UP_SUB, (h + 1) * UP_SUB)
                gate = jnp.dot(c, wgbf_ref[:, cols], preferred_element_type=F32)
                up = jnp.dot(c, wubf_ref[:, cols], preferred_element_type=F32)
                r1 = pltpu.roll(gate, 1, 0)
                r2 = pltpu.roll(gate, 2, 0)
                head1 = jnp.where(sub < 1, pltpu.roll(prev[h], 1, 0), r1[0:SUBLANES])
                head2 = jnp.where(sub < 2, pltpu.roll(prev[h], 2, 0), r2[0:SUBLANES])
                g1 = jnp.concatenate([head1, r1[SUBLANES:]], axis=0)
                g2 = jnp.concatenate([head2, r2[SUBLANES:]], axis=0)
                prev[h] = gate[rows - SUBLANES:rows]
                actp_ref[rsl, cols] = conv_act(gate, up, g1, g2, cols).astype(actp_ref.dtype)
        for h in range(n_cols):
            cols = slice(h * UP_SUB, (h + 1) * UP_SUB)
            carry_ref[:, cols] = prev[h]
            tailp_ref[0, :, cols] = prev[h][SUBLANES - (CONV_W - 1):SUBLANES]

    @pl.when(i == 0)
    def _():
        c = cs_ref[...]
        gate = jnp.dot(c, wgbf_ref[...], preferred_element_type=F32)
        up = jnp.dot(c, wubf_ref[...], preferred_element_type=F32)
        gates_ref[...] = gate
        acts_ref[...] = conv_act(gate, up, h1_ref[...], h0_ref[...], slice(None)).astype(acts_ref.dtype)


def _up_proj(c_p, c_s, w_up, conv_w, conv_b, hist0, hist1, *, tm, seq_len):
    mp, k = c_p.shape
    ms = c_s.shape[0]
    npt = mp // tm
    tiles_per_seq = seq_len // tm
    nseq = mp // seq_len
    tn = UP_TN
    nj = pl.cdiv(D_FF, tn)
    up_block0 = D_FF // LANE
    last_block = w_up.shape[1] // LANE - 1

    def row_p(j, i):
        return jnp.maximum(i - 1, 0)

    def wu_spec(part):
        return pl.BlockSpec(
            (k, LANE), lambda j, i: (0, jnp.minimum(up_block0 + j * UP_PARTS + part, last_block)))

    in_specs = [
        pl.BlockSpec((tm, k), lambda j, i: (row_p(j, i), 0)),
        pl.BlockSpec((ms, k), lambda j, i: (0, 0)),
        pl.BlockSpec((k, tn), lambda j, i: (0, j)),
        *[wu_spec(part) for part in range(UP_PARTS)],
        pl.BlockSpec((CONV_W, tn), lambda j, i: (0, j)),
        pl.BlockSpec((1, tn), lambda j, i: (0, j)),
        pl.BlockSpec((ms, tn), lambda j, i: (0, j)),
        pl.BlockSpec((ms, tn), lambda j, i: (0, j)),
    ]
    out_specs = [
        pl.BlockSpec((tm, tn), lambda j, i: (row_p(j, i), j)),
        pl.BlockSpec((ms, tn), lambda j, i: (0, j)),
        pl.BlockSpec((1, CONV_W - 1, tn), lambda j, i: (row_p(j, i) // tiles_per_seq, 0, j)),
        pl.BlockSpec((ms, tn), lambda j, i: (0, j)),
    ]
    out_shape = [
        jax.ShapeDtypeStruct((mp, D_FF), BF16),
        jax.ShapeDtypeStruct((ms, D_FF), BF16),
        jax.ShapeDtypeStruct((nseq, CONV_W - 1, D_FF), F32),
        jax.ShapeDtypeStruct((ms, D_FF), F32),
    ]
    return pl.pallas_call(
        functools.partial(_up_kernel, tiles_per_seq=tiles_per_seq),
        grid=(nj, npt + 1),
        in_specs=in_specs,
        out_specs=out_specs,
        out_shape=out_shape,
        scratch_shapes=[pltpu.VMEM((k, tn), BF16), pltpu.VMEM((k, tn), BF16), pltpu.VMEM((SUBLANES, tn), F32)],
        compiler_params=_cparams(2),
        name="up_proj",
    )(c_p, c_s, w_up, *([w_up] * UP_PARTS), conv_w, conv_b, hist0, hist1)


def _two_group_specs(tm, ms, widths):
    specs = []
    for w in widths:
        specs.append(pl.BlockSpec((tm, w), lambda i: (jnp.maximum(i - 1, 0), 0)))
        specs.append(pl.BlockSpec((ms, w), lambda i: (0, 0)))
    return specs


def _resident(shape):
    return pl.BlockSpec(shape, lambda i: (0,) * len(shape), pipeline_mode=pl.Buffered(1))


def _out_norm_kernel(mp_ref, ms_ref, xp_ref, xs_ref, w_ref, g_ref, hp_ref, hs_ref, cp_ref, cs_ref, wbf_ref):
    i = pl.program_id(0)

    @pl.when(i == 0)
    def _():
        wbf_ref[...] = w_ref[...].astype(BF16)

    def emit(m_ref, x_ref, h_ref, c_ref):
        h = x_ref[...] + jnp.dot(m_ref[...], wbf_ref[...], preferred_element_type=F32)
        h_ref[...] = h
        c_ref[...] = _rms(h, g_ref[...]).astype(c_ref.dtype)

    @pl.when(i == 0)
    def _():
        emit(ms_ref, xs_ref, hs_ref, cs_ref)

    @pl.when(i > 0)
    def _():
        emit(mp_ref, xp_ref, hp_ref, cp_ref)


def _out_norm(m_p, m_s, x_p, x_s, w_out, g, *, tm):
    mp, d = x_p.shape
    ms = x_s.shape[0]
    return pl.pallas_call(
        _out_norm_kernel,
        grid=(mp // tm + 1,),
        in_specs=[*_two_group_specs(tm, ms, (d, d)), _resident((d, d)), _resident((1, d))],
        out_specs=_two_group_specs(tm, ms, (d, d)),
        out_shape=[
            jax.ShapeDtypeStruct((mp, d), F32), jax.ShapeDtypeStruct((ms, d), F32),
            jax.ShapeDtypeStruct((mp, d), BF16), jax.ShapeDtypeStruct((ms, d), BF16),
        ],
        scratch_shapes=[pltpu.VMEM((d, d), BF16)],
        compiler_params=_cparams(1),
        name="out_norm",
    )(m_p, m_s, x_p, x_s, w_out, g.reshape(1, d))


def _ple_kernel(hp_ref, hs_ref, pp_ref, ps_ref, wpg_ref, wple_ref, gple_ref, gfin_ref, yp_ref, ys_ref,
                wpgbf_ref, wplebf_ref, *, final_norm):
    i = pl.program_id(0)

    @pl.when(i == 0)
    def _():
        wpgbf_ref[...] = wpg_ref[...].astype(BF16)
        wplebf_ref[...] = wple_ref[...].astype(BF16)

    def emit(h_ref, p_ref, y_ref):
        h2 = h_ref[...]
        cn = _rms(h2, gple_ref[...]).astype(BF16)
        pg = _sigmoid(jnp.dot(cn, wpgbf_ref[...], preferred_element_type=F32))
        e = jnp.dot(p_ref[...].astype(BF16), wplebf_ref[...], preferred_element_type=F32)
        h3 = h2 + pg * e
        y_ref[...] = _rms(h3, gfin_ref[...]) if final_norm else h3

    @pl.when(i == 0)
    def _():
        emit(hs_ref, ps_ref, ys_ref)

    @pl.when(i > 0)
    def _():
        emit(hp_ref, pp_ref, yp_ref)


def _ple(h_p, h_s, p_p, p_s, wpg, wple, g_ple, g_final, *, tm, final_norm):
    mp, d = h_p.shape
    ms = h_s.shape[0]
    pd = p_p.shape[1]
    return pl.pallas_call(
        functools.partial(_ple_kernel, final_norm=final_norm),
        grid=(mp // tm + 1,),
        in_specs=[*_two_group_specs(tm, ms, (d, pd)), _resident((d, d)), _resident((pd, d)),
                  _resident((1, d)), _resident((1, d))],
        out_specs=_two_group_specs(tm, ms, (d,)),
        out_shape=[jax.ShapeDtypeStruct((mp, d), F32), jax.ShapeDtypeStruct((ms, d), F32)],
        scratch_shapes=[pltpu.VMEM((d, d), BF16), pltpu.VMEM((pd, d), BF16)],
        compiler_params=_cparams(1),
        name="ple",
    )(h_p, h_s, p_p, p_s, wpg, wple, g_ple.reshape(1, d), g_final.reshape(1, d))


def _pool_mixers(z_p, z_s, lw, s_pool, nb, t):
    pm_s, pool_new_t = _pool_sample(z_s, jnp.transpose(s_pool, (1, 0, 2)), lw["w_pool"], lw["pool_scale"])
    pool_new_s = jnp.transpose(pool_new_t, (1, 0, 2))
    pm_p = _pool_prompt(z_p, lw["w_pool"], lw["pool_scale"], nb, t, 1024)
    pool_new_p = z_p.reshape(nb, t, Z_WIDTH)[:, t - POOL_BUF:, Z_U:Z_U + POOL_W].astype(F32)
    return pm_p, pm_s, pool_new_p, pool_new_s


def kernel(x_prompt, x_sample, state_gla, state_pool, state_conv, p_prompt, p_sample, norm_mix, w_in, w_gate_up, b_gate, gla_norm, w_branch_a, w_pool, pool_scale, w_branch_b, w_out, norm_ffn, w_up, conv_w, conv_b, w_down, norm_ple, w_ple_gate, w_ple, norm_final):
    depth = w_in.shape[0]
    nb, t, d = x_prompt.shape
    ns = x_sample.shape[0]
    hp = x_prompt.reshape(nb * t, d)
    hs = x_sample.reshape(ns, d)
    outs = {k: [] for k in ("gp", "pp", "cp", "gs", "ps", "cs")}
    for li in range(depth):
        lw = {
            "wg": jnp.pad(w_gate_up[li], ((0, LANE - GATE_RANK), (0, 0))).astype(BF16),
            "bg": b_gate[li].reshape(1, GLA_KW),
            "gn": gla_norm[li].reshape(1, GLA_VW),
            "w_pool": w_pool[li],
            "pool_scale": pool_scale[li].reshape(1, POOL_W),
        }
        a_p = _rmsnorm(hp, norm_mix[li], BF16, 1024)
        a_s = _rmsnorm(hs, norm_mix[li], BF16, ns)
        z_p, z_s = _in_proj(a_p, a_s, w_in[li].T, tm=2048)
        og_p, g1, og_s, g2 = _gla(z_p, z_s, lw["wg"], lw["bg"], lw["gn"], state_gla[li], nb, t)
        pm_p, pm_s, p1, p2 = _pool_mixers(z_p, z_s, lw, state_pool[li], nb, t)
        mg_p, mg_s = _merge(og_p, og_s, pm_p, pm_s, w_branch_a[li], w_branch_b[li], z_p, z_s, tm=1024, tn=1024)
        h1_p, h1_s, c_p, c_s = _out_norm(mg_p, mg_s, hp, hs, w_out[li], norm_ffn[li], tm=512)
        s_conv = state_conv[li]
        act_p, act_s, c1, gate_s = _up_proj(c_p, c_s, w_up[li], conv_w[li], conv_b[li].reshape(1, D_FF),
                                            s_conv[:, 0], s_conv[:, 1], tm=2048, seq_len=t)
        c2 = jnp.stack([s_conv[:, 1], gate_s], axis=1)
        h2_p, h2_s = _down_proj(act_p, act_s, w_down[li], h1_p, h1_s, tm=512, tn=1024)
        hp, hs = _ple(h2_p, h2_s, p_prompt[li].reshape(nb * t, PLE_DIM), p_sample[li].reshape(ns, PLE_DIM),
                      w_ple_gate[li], w_ple[li], norm_ple[li], norm_final, tm=512,
                      final_norm=li == depth - 1)
        for key, val in zip(("gp", "pp", "cp", "gs", "ps", "cs"), (g1, p1, c1, g2, p2, c2)):
            outs[key].append(val)
    st = {k: jnp.stack(v, axis=0) for k, v in outs.items()}
    return (hp.reshape(nb, t, d), hs.reshape(ns, 1, d),
            st["gp"], st["pp"], st["cp"], st["gs"], st["ps"], st["cs"])
```

```python
import functools

import jax
import jax.numpy as jnp
from jax import lax
from jax.experimental import pallas as pl
from jax.experimental.pallas import tpu as pltpu

F32 = jnp.float32
BF16 = jnp.bfloat16

D_MODEL = 2048
GLA_HEADS = 4
GLA_DK = 256
GLA_DV = 512
GLA_KW = GLA_HEADS * GLA_DK
GLA_VW = GLA_HEADS * GLA_DV
GATE_RANK = 16
GATE_TEMP = 16.0
GLA_CHUNK = 64
POOL_WINDOWS = (2, 4, 8, 16)
POOL_GW = 256
POOL_W = 1024
POOL_BUF = 15
D_FF = 5504
CONV_W = 3
PLE_DIM = 256
EPS = 1e-6

LANE = 128
SUBLANES = 8
Z_Q, Z_K, Z_V, Z_R, Z_U, Z_GA, Z_GB, Z_GLR, Z_WIDTH = 0, 1024, 2048, 4096, 6144, 7168, 9216, 11264, 11392
GLR_SRC = 6144
IN_TN = 1024
IN_SUB = 512

VMEM_LIMIT = 58 * 1024 * 1024


def _cparams(n_axes):
    return pltpu.CompilerParams(dimension_semantics=("arbitrary",) * n_axes, vmem_limit_bytes=VMEM_LIMIT)


def _sigmoid(x):
    return 1.0 / (1.0 + jnp.exp(-x))


def _silu(x):
    return x * _sigmoid(x)


def _log_sigmoid(x):
    return jnp.minimum(x, 0.0) - jnp.log(1.0 + jnp.exp(-jnp.abs(x)))


def _rms(x, g):
    return x * lax.rsqrt(jnp.mean(x * x, axis=-1, keepdims=True) + EPS) * g


def _rms_kernel(x_ref, g_ref, o_ref):
    o_ref[...] = _rms(x_ref[...], g_ref[...]).astype(o_ref.dtype)


def _rmsnorm(x, g, out_dtype, tm):
    m, d = x.shape
    return pl.pallas_call(
        _rms_kernel,
        grid=(m // tm,),
        in_specs=[pl.BlockSpec((tm, d), lambda i: (i, 0)), pl.BlockSpec((1, d), lambda i: (0, 0))],
        out_specs=pl.BlockSpec((tm, d), lambda i: (i, 0)),
        out_shape=jax.ShapeDtypeStruct((m, d), out_dtype),
        compiler_params=_cparams(1),
        name="rmsnorm",
    )(x, g.reshape(1, d))


def _col_tiled_specs(tm, ms, width, col0=0):
    return [
        pl.BlockSpec((tm, width), lambda j, i: (jnp.maximum(i - 1, 0), col0 + j)),
        pl.BlockSpec((ms, width), lambda j, i: (0, col0 + j)),
    ]


def _row_tiled_specs(tm, ms, width):
    return [
        pl.BlockSpec((tm, width), lambda j, i: (jnp.maximum(i - 1, 0), 0)),
        pl.BlockSpec((ms, width), lambda j, i: (0, 0)),
    ]


def _down_kernel(ap_ref, as_ref, w_ref, rp_ref, rs_ref, op_ref, os_ref, wbf_ref):
    i = pl.program_id(1)

    @pl.when(i == 0)
    def _():
        wbf_ref[...] = w_ref[...].astype(BF16)
        os_ref[...] = rs_ref[...] + jnp.dot(as_ref[...], wbf_ref[...], preferred_element_type=F32)

    @pl.when(i > 0)
    def _():
        op_ref[...] = rp_ref[...] + jnp.dot(ap_ref[...], wbf_ref[...], preferred_element_type=F32)


def _down_proj(a_p, a_s, w, res_p, res_s, *, tm, tn):
    mp, k = a_p.shape
    ms = a_s.shape[0]
    n = w.shape[1]
    return pl.pallas_call(
        _down_kernel,
        grid=(n // tn, mp // tm + 1),
        in_specs=[
            *_row_tiled_specs(tm, ms, k),
            pl.BlockSpec((k, tn), lambda j, i: (0, j), pipeline_mode=pl.Buffered(1)),
            *_col_tiled_specs(tm, ms, tn),
        ],
        out_specs=_col_tiled_specs(tm, ms, tn),
        out_shape=[jax.ShapeDtypeStruct((mp, n), F32), jax.ShapeDtypeStruct((ms, n), F32)],
        scratch_shapes=[pltpu.VMEM((k, tn), BF16)],
        compiler_params=_cparams(2),
        name="down_proj",
    )(a_p, a_s, w, res_p, res_s)


_NT = (((1,), (1,)), ((), ()))
_TN = (((0,), (0,)), ((), ()))


def _in_proj_kernel(ap_ref, as_ref, w_ref, wx_ref, zp_ref, zs_ref, wbf_ref):
    j = pl.program_id(0)
    i = pl.program_id(1)
    last = pl.num_programs(0) - 1
    first_shifted = GLR_SRC // IN_TN
    keep = IN_TN - GATE_RANK

    @pl.when(i == 0)
    def _():
        @pl.when(j < first_shifted)
        def _():
            wbf_ref[...] = w_ref[...].astype(BF16)

        @pl.when(jnp.logical_and(j >= first_shifted, j < last))
        def _():
            wbf_ref[:keep, :] = w_ref[GATE_RANK:, :].astype(BF16)
            wbf_ref[keep:, :] = wx_ref[...].astype(BF16)

        @pl.when(j == last)
        def _():
            wbf_ref[:GATE_RANK, :] = wx_ref[...].astype(BF16)
            wbf_ref[GATE_RANK:LANE, :] = jnp.zeros((LANE - GATE_RANK, wbf_ref.shape[1]), BF16)

    def emit(a_ref, z_ref):
        @pl.when(j < last)
        def _():
            a = a_ref[...]
            for part in range(IN_TN // IN_SUB):
                cols = slice(part * IN_SUB, (part + 1) * IN_SUB)
                z_ref[:, cols] = lax.dot_general(a, wbf_ref[cols, :], _NT,
                                                 preferred_element_type=F32).astype(z_ref.dtype)

        @pl.when(j == last)
        def _():
            z_ref[:, :LANE] = lax.dot_general(a_ref[...], wbf_ref[:LANE, :], _NT,
                                              preferred_element_type=F32).astype(z_ref.dtype)

    @pl.when(i == 0)
    def _():
        emit(as_ref, zs_ref)

    @pl.when(i > 0)
    def _():
        emit(ap_ref, zp_ref)


def _in_proj(a_p, a_s, wt, *, tm):
    mp, k = a_p.shape
    ms = a_s.shape[0]
    npt = mp // tm
    nj = Z_GLR // IN_TN + 1
    aux_per_tile = IN_TN // GATE_RANK

    def row_p(j, i):
        return jnp.maximum(i - 1, 0)

    def aux_idx(j, i):
        return (jnp.where(j == nj - 1, GLR_SRC // GATE_RANK, (j + 1) * aux_per_tile), 0)

    return pl.pallas_call(
        _in_proj_kernel,
        grid=(nj, npt + 1),
        in_specs=[
            pl.BlockSpec((tm, k), lambda j, i: (row_p(j, i), 0)),
            pl.BlockSpec((ms, k), lambda j, i: (0, 0)),
            pl.BlockSpec((IN_TN, k), lambda j, i: (j, 0)),
            pl.BlockSpec((GATE_RANK, k), aux_idx),
        ],
        out_specs=[
            pl.BlockSpec((tm, IN_TN), lambda j, i: (row_p(j, i), j)),
            pl.BlockSpec((ms, IN_TN), lambda j, i: (0, j)),
        ],
        out_shape=[jax.ShapeDtypeStruct((mp, Z_WIDTH), BF16), jax.ShapeDtypeStruct((ms, Z_WIDTH), F32)],
        scratch_shapes=[pltpu.VMEM((IN_TN, k), BF16)],
        compiler_params=_cparams(2),
        name="in_proj",
    )(a_p, a_s, wt, wt)


GLA_GROUP = 8


def _gla_kernel(q_ref, k_ref, v_ref, r_ref, glr_ref, wg_ref, bg_ref, gn_ref,
                sq_ref, sk_ref, sv_ref, sr_ref, sglr_ref, swg_ref, sbg_ref, sgn_ref, s_ref,
                og_ref, s_out_ref, sog_ref, ss_out_ref, st_ref):
    c = GLA_CHUNK
    rows = GLA_GROUP * c
    gi = pl.program_id(2)
    step = (pl.program_id(0) * pl.num_programs(1) + pl.program_id(1)) * pl.num_programs(2) + gi
    n_sample = s_ref.shape[0]

    sample_units = []
    for si in range(n_sample):
        sample_units += _gla_sample_units(step * n_sample + si, si, sq_ref, sk_ref, sv_ref, sr_ref, sglr_ref,
                                          swg_ref, sbg_ref, sgn_ref, s_ref, sog_ref, ss_out_ref)

    @pl.when(gi == 0)
    def _():
        st_ref[...] = jnp.zeros_like(st_ref)

    row = lax.broadcasted_iota(jnp.int32, (c, c), 0)
    col = lax.broadcasted_iota(jnp.int32, (c, c), 1)
    causal = row >= col
    tri = causal.astype(BF16)

    def group():
        r0 = pl.multiple_of(gi * rows, rows)
        rs = pl.ds(r0, rows)
        zg = jnp.dot(glr_ref[rs, :], wg_ref[...], preferred_element_type=F32) + bg_ref[...]
        v = v_ref[rs, :]
        vals = [dict() for _ in range(GLA_GROUP)]
        state = {"st": st_ref[...]}

        def s_gate(ci, d):
            gl = _log_sigmoid(zg[ci * c:(ci + 1) * c]) * (1.0 / GATE_TEMP)
            d["gl_hi"] = gl.astype(BF16)
            d["gl_lo"] = (gl - d["gl_hi"].astype(F32)).astype(BF16)

        def s_cumsum(ci, d):
            d["b"] = (jnp.dot(tri, d.pop("gl_hi"), preferred_element_type=F32)
                      + jnp.dot(tri, d.pop("gl_lo"), preferred_element_type=F32))

        def s_scale(ci, d):
            cr = pl.ds(r0 + ci * c, c)
            b = d.pop("b")
            bl = b[c - 1:c, :]
            q = q_ref[cr, :].astype(F32) * (GLA_DK ** -0.5)
            k = k_ref[cr, :].astype(F32)
            d["qb"] = (q * jnp.exp(b)).astype(BF16)
            d["kb"] = (k * jnp.exp(-b)).astype(BF16)
            d["kd"] = (k * jnp.exp(bl - b)).astype(BF16)
            d["decay"] = jnp.exp(bl)

        def s_att(ci, d):
            sl = slice(ci * c, (ci + 1) * c)
            d["att"] = lax.dot_general(d["qb"], d.pop("kb"), _NT, preferred_element_type=F32)
            d["dst"] = lax.dot_general(v[sl], d.pop("kd"), _TN, preferred_element_type=F32)

        def s_mask(ci, d):
            d["att"] = jnp.where(causal, d["att"], 0.0).astype(BF16)

        def s_intra(ci, d):
            sl = slice(ci * c, (ci + 1) * c)
            d["o"] = jnp.dot(d.pop("att"), v[sl], preferred_element_type=F32)

        def s_chain(ci, d):
            st = state["st"]
            d["st_in"] = st.astype(BF16)
            state["st"] = st * d.pop("decay") + d.pop("dst")

        def s_inter(ci, d):
            d["o"] = d["o"] + lax.dot_general(d.pop("qb"), d.pop("st_in"), _NT, preferred_element_type=F32)

        def s_finish(ci, d):
            cr = pl.ds(r0 + ci * c, c)
            on = _rms(d.pop("o"), gn_ref[...])
            og_ref[cr, :] = (on * _silu(r_ref[cr, :].astype(F32))).astype(og_ref.dtype)

        stages = (s_gate, s_cumsum, s_scale, s_att, s_mask, s_intra, s_chain, s_inter, s_finish)
        n_ticks = GLA_GROUP + len(stages) - 1
        pending = list(sample_units)
        for tick in range(n_ticks):
            for si, stage in enumerate(stages):
                ci = tick - si
                if 0 <= ci < GLA_GROUP:
                    stage(ci, vals[ci])
            if pending and (tick + 1) * len(sample_units) >= (len(sample_units) - len(pending) + 1) * n_ticks:
                pending.pop(0)()
        for unit in pending:
            unit()
        st_ref[...] = state["st"]

    group()

    @pl.when(gi == pl.num_programs(2) - 1)
    def _():
        s_out_ref[0, 0] = st_ref[...].T


def _gla(z_p, z_s, wg, bg, gn, state, nb, t):
    m = z_p.shape[0]
    n = z_s.shape[0]
    n_groups = t // (GLA_GROUP * GLA_CHUNK)
    n_steps = nb * GLA_HEADS * n_groups
    n_sample = n // n_steps
    assert n_sample * n_steps == n

    def s_idx(b, h, g):
        return ((b * GLA_HEADS + h) * n_groups + g, 0, 0, 0)

    const2 = lambda b, h, g: (0, 0)
    og_p, s_new_p, og_s, s_new_s = pl.pallas_call(
        _gla_kernel,
        grid=(nb, GLA_HEADS, n_groups),
        in_specs=[
            pl.BlockSpec((t, GLA_DK), lambda b, h, g: (b, Z_Q // GLA_DK + h)),
            pl.BlockSpec((t, GLA_DK), lambda b, h, g: (b, Z_K // GLA_DK + h)),
            pl.BlockSpec((t, GLA_DV), lambda b, h, g: (b, Z_V // GLA_DV + h)),
            pl.BlockSpec((t, GLA_DV), lambda b, h, g: (b, Z_R // GLA_DV + h)),
            pl.BlockSpec((t, LANE), lambda b, h, g: (b, Z_GLR // LANE)),
            pl.BlockSpec((LANE, GLA_DK), lambda b, h, g: (0, h)),
            pl.BlockSpec((1, GLA_DK), lambda b, h, g: (0, h)),
            pl.BlockSpec((1, GLA_DV), lambda b, h, g: (0, h)),
            pl.BlockSpec((n, GLA_KW), lambda b, h, g: (0, Z_Q // GLA_KW)),
            pl.BlockSpec((n, GLA_KW), lambda b, h, g: (0, Z_K // GLA_KW)),
            pl.BlockSpec((n, GLA_VW), lambda b, h, g: (0, Z_V // GLA_VW)),
            pl.BlockSpec((n, GLA_VW), lambda b, h, g: (0, Z_R // GLA_VW)),
            pl.BlockSpec((n, LANE), lambda b, h, g: (0, Z_GLR // LANE)),
            pl.BlockSpec((LANE, GLA_KW), const2),
            pl.BlockSpec((1, GLA_KW), const2),
            pl.BlockSpec((1, GLA_VW), const2),
            pl.BlockSpec((n_sample, GLA_HEADS, GLA_DK, GLA_DV), s_idx),
        ],
        out_specs=[
            pl.BlockSpec((t, GLA_DV), lambda b, h, g: (b, h)),
            pl.BlockSpec((1, 1, GLA_DK, GLA_DV), lambda b, h, g: (b, h, 0, 0)),
            pl.BlockSpec((n, GLA_VW), const2),
            pl.BlockSpec((n_sample, GLA_HEADS, GLA_DK, GLA_DV), s_idx),
        ],
        out_shape=[
            jax.ShapeDtypeStruct((m, GLA_VW), BF16),
            jax.ShapeDtypeStruct((nb, GLA_HEADS, GLA_DK, GLA_DV), F32),
            jax.ShapeDtypeStruct((n, GLA_VW), F32),
            jax.ShapeDtypeStruct((n, GLA_HEADS, GLA_DK, GLA_DV), F32),
        ],
        scratch_shapes=[pltpu.VMEM((GLA_DV, GLA_DK), F32)],
        compiler_params=_cparams(3),
        name="gla",
    )(z_p, z_p, z_p, z_p, z_p, wg, bg, gn, z_s, z_s, z_s, z_s, z_s, wg, bg, gn, state)
    return og_p, s_new_p, og_s, s_new_s


def _gla_sample_units(n, si, q_ref, k_ref, v_ref, r_ref, glr_ref, wg_ref, bg_ref, gn_ref, s_ref,
                      og_ref, s_out_ref):
    row = pl.ds(n, 1)

    def rows8(ref):
        return jnp.broadcast_to(ref[row, :], (SUBLANES, ref.shape[1]))

    zg = jnp.dot(rows8(glr_ref).astype(BF16), wg_ref[...], preferred_element_type=F32) + bg_ref[...]
    g = _log_sigmoid(zg) * (1.0 / GATE_TEMP)
    eg = jnp.exp(g)
    eng = jnp.exp(-g)
    eg1 = eg.astype(BF16)
    rem = eg - eg1.astype(F32)
    eg2 = rem.astype(BF16)
    eg3 = (rem - eg2.astype(F32)).astype(BF16)
    first_row = lax.broadcasted_iota(jnp.int32, (SUBLANES, LANE), 0) == 0
    ones_row = jnp.where(first_row, 1.0, 0.0).astype(BF16)
    row0 = lax.broadcasted_iota(jnp.int32, (SUBLANES, GLA_DV), 0) == 0
    q_all = rows8(q_ref) * (GLA_DK ** -0.5)
    k_all = rows8(k_ref)
    v_all = rows8(v_ref)
    r_all = r_ref[row, :]
    gn_all = gn_ref[...]

    def head_unit(h):
        ks = slice(h * GLA_DK, (h + 1) * GLA_DK)
        vs = slice(h * GLA_DV, (h + 1) * GLA_DV)
        q, k, v = q_all[:, ks], k_all[:, ks], v_all[:, vs]
        s_old = s_ref[si, h]
        decay = (lax.dot_general(eg1[:, ks], ones_row, _TN, preferred_element_type=F32)
                 + lax.dot_general(eg2[:, ks], ones_row, _TN, preferred_element_type=F32)
                 + lax.dot_general(eg3[:, ks], ones_row, _TN, preferred_element_type=F32))
        v0 = jnp.where(row0, v, 0.0)
        ds = lax.dot_general(k.astype(BF16), v0.astype(BF16), _TN, preferred_element_type=F32)
        s_out_ref[si, h] = jnp.tile(decay, (1, GLA_DV // LANE)) * s_old + ds
        qb = q * eg[:, ks]
        kb = k * eng[:, ks]
        att = jnp.sum(qb * kb, axis=-1, keepdims=True)
        o = jnp.dot(qb.astype(BF16), s_old.astype(BF16), preferred_element_type=F32) + att * v
        o = o[0:1, :]
        ms = jnp.mean(o * o, axis=-1, keepdims=True)
        on = o * lax.rsqrt(ms + EPS) * gn_all[:, vs]
        og_ref[row, vs] = on * _silu(r_all[:, vs])

    return [functools.partial(head_unit, h) for h in range(GLA_HEADS)]


def _pool_group_out(mix, g, wp_ref, ps_ref):
    cs = slice(g * POOL_GW, (g + 1) * POOL_GW)
    pm = jnp.dot(mix.astype(BF16), wp_ref[g].astype(BF16), preferred_element_type=F32)
    return pm * ps_ref[:, cs]


POOL_LEVELS = tuple(w.bit_length() - 1 for w in POOL_WINDOWS)
assert all(1 << lv == w for lv, w in zip(POOL_LEVELS, POOL_WINDOWS)), "windows must be powers of two"
POOL_HIST = SUBLANES * max(POOL_LEVELS)
assert POOL_HIST > POOL_BUF + 1


def _pool_prompt_kernel(u_ref, wp_ref, ps_ref, o_ref, ext_ref, lvl_ref):
    i = pl.program_id(1)
    tt = u_ref.shape[0]
    hist = POOL_HIST
    n = hist + tt

    @pl.when(i == 0)
    def _():
        ext_ref[0:hist, :] = jnp.zeros((hist, POOL_W), F32)

    @pl.when(i > 0)
    def _():
        ext_ref[0:hist, :] = ext_ref[tt:n, :]

    ext_ref[hist:n, :] = u_ref[...].astype(F32)
    pos = i * tt + lax.broadcasted_iota(jnp.int32, (tt, 1), 0)
    for g, (w, levels) in enumerate(zip(POOL_WINDOWS, POOL_LEVELS)):
        cs = slice(g * POOL_GW, (g + 1) * POOL_GW)
        cur = None
        for lev in range(levels):
            sh = 1 << lev
            start = SUBLANES * (lev + 1)
            if lev == 0:
                cur = ext_ref[start:n, cs] + ext_ref[start - sh:n - sh, cs]
            else:
                prev = lvl_ref.at[(lev - 1) % 2]
                cur = prev[start:n, :] + prev[start - sh:n - sh, :]
            if lev + 1 < levels:
                lvl_ref[lev % 2, start:n, :] = cur
        s = cur[hist - SUBLANES * levels:]
        u = ext_ref[hist:n, cs]
        cnt = jnp.minimum(pos + 1, w).astype(F32)
        mix = s / cnt - u
        o_ref[:, cs] = _pool_group_out(mix, g, wp_ref, ps_ref).astype(o_ref.dtype)


def _pool_prompt(z, w_pool, pool_scale, nb, t, tt):
    m = z.shape[0]
    nt = t // tt
    return pl.pallas_call(
        _pool_prompt_kernel,
        grid=(nb, nt),
        in_specs=[
            pl.BlockSpec((tt, POOL_W), lambda b, i: (b * nt + i, Z_U // POOL_W)),
            pl.BlockSpec((4, POOL_GW, POOL_GW), lambda b, i: (0, 0, 0)),
            pl.BlockSpec((1, POOL_W), lambda b, i: (0, 0)),
        ],
        out_specs=pl.BlockSpec((tt, POOL_W), lambda b, i: (b * nt + i, 0)),
        out_shape=jax.ShapeDtypeStruct((m, POOL_W), BF16),
        scratch_shapes=[pltpu.VMEM((tt + POOL_HIST, POOL_W), F32), pltpu.VMEM((2, tt + POOL_HIST, POOL_GW), F32)],
        compiler_params=_cparams(2),
        name="pool_prompt",
    )(z, w_pool, pool_scale)


def _pool_sample_kernel(u_ref, buf_ref, wp_ref, ps_ref, o_ref, buf_out_ref):
    buf_out_ref[0:POOL_BUF - 1] = buf_ref[1:POOL_BUF]
    buf_out_ref[POOL_BUF - 1] = u_ref[...].astype(F32)
    for g, w in enumerate(POOL_WINDOWS):
        cs = slice(g * POOL_GW, (g + 1) * POOL_GW)
        u = u_ref[:, cs].astype(F32)
        s = u
        for d in range(1, w):
            s = s + buf_ref[POOL_BUF - d, :, cs]
        mix = s / float(w) - u
        o_ref[:, cs] = _pool_group_out(mix, g, wp_ref, ps_ref).astype(o_ref.dtype)


def _pool_sample(z, buf_t, w_pool, pool_scale):
    n = z.shape[0]
    return pl.pallas_call(
        _pool_sample_kernel,
        grid=(1,),
        in_specs=[
            pl.BlockSpec((n, POOL_W), lambda i: (0, Z_U // POOL_W)),
            pl.BlockSpec((POOL_BUF, n, POOL_W), lambda i: (0, 0, 0)),
            pl.BlockSpec((4, POOL_GW, POOL_GW), lambda i: (0, 0, 0)),
            pl.BlockSpec((1, POOL_W), lambda i: (0, 0)),
        ],
        out_specs=[pl.BlockSpec((n, POOL_W), lambda i: (0, 0)),
                   pl.BlockSpec((POOL_BUF, n, POOL_W), lambda i: (0, 0, 0))],
        out_shape=[jax.ShapeDtypeStruct((n, POOL_W), BF16), jax.ShapeDtypeStruct((POOL_BUF, n, POOL_W), F32)],
        compiler_params=_cparams(1),
        name="pool_sample",
    )(z, buf_t, w_pool, pool_scale)


MERGE_SUB = 256


def _merge_kernel(ogp_ref, ogs_ref, pmp_ref, pms_ref, wa_ref, wb_ref, gap_ref, gas_ref, gbp_ref, gbs_ref,
                  op_ref, os_ref, wabf_ref, wbbf_ref):
    i = pl.program_id(1)

    @pl.when(i == 0)
    def _():
        wabf_ref[...] = wa_ref[...].astype(BF16)
        wbbf_ref[...] = wb_ref[...].astype(BF16)

    def emit(og_ref, pm_ref, ga_ref, gb_ref, o_ref):
        og = og_ref[...].astype(BF16)
        pm = pm_ref[...]
        for part in range(o_ref.shape[1] // MERGE_SUB):
            cols = slice(part * MERGE_SUB, (part + 1) * MERGE_SUB)
            ya = jnp.dot(og, wabf_ref[:, cols], preferred_element_type=F32)
            yb = jnp.dot(pm, wbbf_ref[:, cols], preferred_element_type=F32)
            ga = ga_ref[:, cols].astype(F32)
            gb = gb_ref[:, cols].astype(F32)
            o_ref[:, cols] = (_sigmoid(ga) * ya + _sigmoid(gb) * yb).astype(o_ref.dtype)

    @pl.when(i == 0)
    def _():
        emit(ogs_ref, pms_ref, gas_ref, gbs_ref, os_ref)

    @pl.when(i > 0)
    def _():
        emit(ogp_ref, pmp_ref, gap_ref, gbp_ref, op_ref)


def _merge(og_p, og_s, pm_p, pm_s, wa, wb, z_p, z_s, *, tm, tn):
    mp = og_p.shape[0]
    ms = og_s.shape[0]
    return pl.pallas_call(
        _merge_kernel,
        grid=(D_MODEL // tn, mp // tm + 1),
        in_specs=[
            *_row_tiled_specs(tm, ms, GLA_VW),
            *_row_tiled_specs(tm, ms, POOL_W),
            pl.BlockSpec((GLA_VW, tn), lambda j, i: (0, j), pipeline_mode=pl.Buffered(1)),
            pl.BlockSpec((POOL_W, tn), lambda j, i: (0, j), pipeline_mode=pl.Buffered(1)),
            *_col_tiled_specs(tm, ms, tn, Z_GA // tn),
            *_col_tiled_specs(tm, ms, tn, Z_GB // tn),
        ],
        out_specs=_col_tiled_specs(tm, ms, tn),
        out_shape=[jax.ShapeDtypeStruct((mp, D_MODEL), BF16), jax.ShapeDtypeStruct((ms, D_MODEL), BF16)],
        scratch_shapes=[pltpu.VMEM((GLA_VW, tn), BF16), pltpu.VMEM((POOL_W, tn), BF16)],
        compiler_params=_cparams(2),
        name="merge",
    )(og_p, og_s, pm_p, pm_s, wa, wb, z_p, z_s, z_p, z_s)


UP_TN = 512
UP_SUB = 256
UP_ROWS = 1024
UP_PARTS = UP_TN // LANE


def _up_kernel(*refs, tiles_per_seq):
    cp_ref, cs_ref, wg_ref = refs[:3]
    wu_refs = refs[3:3 + UP_PARTS]
    (cw_ref, cb_ref, h0_ref, h1_ref, actp_ref, acts_ref, tailp_ref, gates_ref,
     wgbf_ref, wubf_ref, carry_ref) = refs[3 + UP_PARTS:]
    i = pl.program_id(1)

    @pl.when(i == 0)
    def _():
        wgbf_ref[...] = wg_ref[...].astype(BF16)
        for part, wu_ref in enumerate(wu_refs):
            wubf_ref[:, part * LANE:(part + 1) * LANE] = wu_ref[...].astype(BF16)

    def conv_act(gate, up, g1, g2, cols):
        cw = cw_ref[:, cols]
        gconv = cb_ref[:, cols] + cw[0:1] * g2 + cw[1:2] * g1 + cw[2:3] * gate
        return _silu(gconv) * up

    @pl.when(i > 0)
    def _():
        @pl.when((i - 1) % tiles_per_seq == 0)
        def _():
            carry_ref[...] = jnp.zeros_like(carry_ref)

        tm = cp_ref.shape[0]
        rows = min(tm, UP_ROWS)
        sub = lax.broadcasted_iota(jnp.int32, (SUBLANES, UP_SUB), 0)
        n_cols = UP_TN // UP_SUB
        prev = [carry_ref[:, h * UP_SUB:(h + 1) * UP_SUB] for h in range(n_cols)]
        for rt in range(tm // rows):
            rsl = slice(rt * rows, (rt + 1) * rows)
            c = cp_ref[rsl, :]
            for h in range(n_cols):
                cols = slice(h * UP_SUB, (h + 1) * UP_SUB)
                gate = jnp.dot(c, wgbf_ref[:, cols], preferred_element_type=F32)
                up = jnp.dot(c, wubf_ref[:, cols], preferred_element_type=F32)
                r1 = pltpu.roll(gate, 1, 0)
                r2 = pltpu.roll(gate, 2, 0)
                head1 = jnp.where(sub < 1, pltpu.roll(prev[h], 1, 0), r1[0:SUBLANES])
                head2 = jnp.where(sub < 2, pltpu.roll(prev[h], 2, 0), r2[0:SUBLANES])
                g1 = jnp.concatenate([head1, r1[SUBLANES:]], axis=0)
                g2 = jnp.concatenate([head2, r2[SUBLANES:]], axis=0)
                prev[h] = gate[rows - SUBLANES:rows]
                actp_ref[rsl, cols] = conv_act(gate, up, g1, g2, cols).astype(actp_ref.dtype)
        for h in range(n_cols):
            cols = slice(h * UP_SUB, (h + 1) * UP_SUB)
            carry_ref[:, cols] = prev[h]
            tailp_ref[0, :, cols] = prev[h][SUBLANES - (CONV_W - 1):SUBLANES]

    @pl.when(i == 0)
    def _():
        c = cs_ref[...]
        gate = jnp.dot(c, wgbf_ref[...], preferred_element_type=F32)
        up = jnp.dot(c, wubf_ref[...], preferred_element_type=F32)
        gates_ref[...] = gate
        acts_ref[...] = conv_act(gate, up, h1_ref[...], h0_ref[...], slice(None)).astype(acts_ref.dtype)


def _up_proj(c_p, c_s, w_up, conv_w, conv_b, hist0, hist1, *, tm, seq_len):
    mp, k = c_p.shape
    ms = c_s.shape[0]
    npt = mp // tm
    tiles_per_seq = seq_len // tm
    nseq = mp // seq_len
    tn = UP_TN
    nj = pl.cdiv(D_FF, tn)
    up_block0 = D_FF // LANE
    last_block = w_up.shape[1] // LANE - 1

    def row_p(j, i):
        return jnp.maximum(i - 1, 0)

    def wu_spec(part):
        return pl.BlockSpec(
            (k, LANE), lambda j, i: (0, jnp.minimum(up_block0 + j * UP_PARTS + part, last_block)))

    in_specs = [
        pl.BlockSpec((tm, k), lambda j, i: (row_p(j, i), 0)),
        pl.BlockSpec((ms, k), lambda j, i: (0, 0)),
        pl.BlockSpec((k, tn), lambda j, i: (0, j)),
        *[wu_spec(part) for part in range(UP_PARTS)],
        pl.BlockSpec((CONV_W, tn), lambda j, i: (0, j)),
        pl.BlockSpec((1, tn), lambda j, i: (0, j)),
        pl.BlockSpec((ms, tn), lambda j, i: (0, j)),
        pl.BlockSpec((ms, tn), lambda j, i: (0, j)),
    ]
    out_specs = [
        pl.BlockSpec((tm, tn), lambda j, i: (row_p(j, i), j)),
        pl.BlockSpec((ms, tn), lambda j, i: (0, j)),
        pl.BlockSpec((1, CONV_W - 1, tn), lambda j, i: (row_p(j, i) // tiles_per_seq, 0, j)),
        pl.BlockSpec((ms, tn), lambda j, i: (0, j)),
    ]
    out_shape = [
        jax.ShapeDtypeStruct((mp, D_FF), BF16),
        jax.ShapeDtypeStruct((ms, D_FF), BF16),
        jax.ShapeDtypeStruct((nseq, CONV_W - 1, D_FF), F32),
        jax.ShapeDtypeStruct((ms, D_FF), F32),
    ]
    return pl.pallas_call(
        functools.partial(_up_kernel, tiles_per_seq=tiles_per_seq),
        grid=(nj, npt + 1),
        in_specs=in_specs,
        out_specs=out_specs,
        out_shape=out_shape,
        scratch_shapes=[pltpu.VMEM((k, tn), BF16), pltpu.VMEM((k, tn), BF16), pltpu.VMEM((SUBLANES, tn), F32)],
        compiler_params=_cparams(2),
        name="up_proj",
    )(c_p, c_s, w_up, *([w_up] * UP_PARTS), conv_w, conv_b, hist0, hist1)


def _two_group_specs(tm, ms, widths):
    specs = []
    for w in widths:
        specs.append(pl.BlockSpec((tm, w), lambda i: (jnp.maximum(i - 1, 0), 0)))
        specs.append(pl.BlockSpec((ms, w), lambda i: (0, 0)))
    return specs


def _resident(shape):
    return pl.BlockSpec(shape, lambda i: (0,) * len(shape), pipeline_mode=pl.Buffered(1))


def _out_norm_kernel(mp_ref, ms_ref, xp_ref, xs_ref, w_ref, g_ref, hp_ref, hs_ref, cp_ref, cs_ref, wbf_ref):
    i = pl.program_id(0)

    @pl.when(i == 0)
    def _():
        wbf_ref[...] = w_ref[...].astype(BF16)

    def emit(m_ref, x_ref, h_ref, c_ref):
        h = x_ref[...] + jnp.dot(m_ref[...], wbf_ref[...], preferred_element_type=F32)
        h_ref[...] = h
        c_ref[...] = _rms(h, g_ref[...]).astype(c_ref.dtype)

    @pl.when(i == 0)
    def _():
        emit(ms_ref, xs_ref, hs_ref, cs_ref)

    @pl.when(i > 0)
    def _():
        emit(mp_ref, xp_ref, hp_ref, cp_ref)


def _out_norm(m_p, m_s, x_p, x_s, w_out, g, *, tm):
    mp, d = x_p.shape
    ms = x_s.shape[0]
    return pl.pallas_call(
        _out_norm_kernel,
        grid=(mp // tm + 1,),
        in_specs=[*_two_group_specs(tm, ms, (d, d)), _resident((d, d)), _resident((1, d))],
        out_specs=_two_group_specs(tm, ms, (d, d)),
        out_shape=[
            jax.ShapeDtypeStruct((mp, d), F32), jax.ShapeDtypeStruct((ms, d), F32),
            jax.ShapeDtypeStruct((mp, d), BF16), jax.ShapeDtypeStruct((ms, d), BF16),
        ],
        scratch_shapes=[pltpu.VMEM((d, d), BF16)],
        compiler_params=_cparams(1),
        name="out_norm",
    )(m_p, m_s, x_p, x_s, w_out, g.reshape(1, d))


def _ple_kernel(hp_ref, hs_ref, pp_ref, ps_ref, wpg_ref, wple_ref, gple_ref, gfin_ref, yp_ref, ys_ref,
                wpgbf_ref, wplebf_ref, *, final_norm):
    i = pl.program_id(0)

    @pl.when(i == 0)
    def _():
        wpgbf_ref[...] = wpg_ref[...].astype(BF16)
        wplebf_ref[...] = wple_ref[...].astype(BF16)

    def emit(h_ref, p_ref, y_ref):
        h2 = h_ref[...]
        cn = _rms(h2, gple_ref[...]).astype(BF16)
        pg = _sigmoid(jnp.dot(cn, wpgbf_ref[...], preferred_element_type=F32))
        e = jnp.dot(p_ref[...].astype(BF16), wplebf_ref[...], preferred_element_type=F32)
        h3 = h2 + pg * e
        y_ref[...] = _rms(h3, gfin_ref[...]) if final_norm else h3

    @pl.when(i == 0)
    def _():
        emit(hs_ref, ps_ref, ys_ref)

    @pl.when(i > 0)
    def _():
        emit(hp_ref, pp_ref, yp_ref)


def _ple(h_p, h_s, p_p, p_s, wpg, wple, g_ple, g_final, *, tm, final_norm):
    mp, d = h_p.shape
    ms = h_s.shape[0]
    pd = p_p.shape[1]
    return pl.pallas_call(
        functools.partial(_ple_kernel, final_norm=final_norm),
        grid=(mp // tm + 1,),
        in_specs=[*_two_group_specs(tm, ms, (d, pd)), _resident((d, d)), _resident((pd, d)),
                  _resident((1, d)), _resident((1, d))],
        out_specs=_two_group_specs(tm, ms, (d,)),
        out_shape=[jax.ShapeDtypeStruct((mp, d), F32), jax.ShapeDtypeStruct((ms, d), F32)],
        scratch_shapes=[pltpu.VMEM((d, d), BF16), pltpu.VMEM((pd, d), BF16)],
        compiler_params=_cparams(1),
        name="ple",
    )(h_p, h_s, p_p, p_s, wpg, wple, g_ple.reshape(1, d), g_final.reshape(1, d))


def _pool_mixers(z_p, z_s, lw, s_pool, nb, t):
    pm_s, pool_new_t = _pool_sample(z_s, jnp.transpose(s_pool, (1, 0, 2)), lw["w_pool"], lw["pool_scale"])
    pool_new_s = jnp.transpose(pool_new_t, (1, 0, 2))
    pm_p = _pool_prompt(z_p, lw["w_pool"], lw["pool_scale"], nb, t, 1024)
    pool_new_p = z_p.reshape(nb, t, Z_WIDTH)[:, t - POOL_BUF:, Z_U:Z_U + POOL_W].astype(F32)
    return pm_p, pm_s, pool_new_p, pool_new_s


def kernel(x_prompt, x_sample, state_gla, state_pool, state_conv, p_prompt, p_sample, norm_mix, w_in, w_gate_up, b_gate, gla_norm, w_branch_a, w_pool, pool_scale, w_branch_b, w_out, norm_ffn, w_up, conv_w, conv_b, w_down, norm_ple, w_ple_gate, w_ple, norm_final):
    depth = w_in.shape[0]
    nb, t, d = x_prompt.shape
    ns = x_sample.shape[0]
    hp = x_prompt.reshape(nb * t, d)
    hs = x_sample.reshape(ns, d)
    outs = {k: [] for k in ("gp", "pp", "cp", "gs", "ps", "cs")}
    for li in range(depth):
        lw = {
            "wg": jnp.pad(w_gate_up[li], ((0, LANE - GATE_RANK), (0, 0))).astype(BF16),
            "bg": b_gate[li].reshape(1, GLA_KW),
            "gn": gla_norm[li].reshape(1, GLA_VW),
            "w_pool": w_pool[li],
            "pool_scale": pool_scale[li].reshape(1, POOL_W),
        }
        a_p = _rmsnorm(hp, norm_mix[li], BF16, 1024)
        a_s = _rmsnorm(hs, norm_mix[li], BF16, ns)
        z_p, z_s = _in_proj(a_p, a_s, w_in[li].T, tm=2048)
        og_p, g1, og_s, g2 = _gla(z_p, z_s, lw["wg"], lw["bg"], lw["gn"], state_gla[li], nb, t)
        pm_p, pm_s, p1, p2 = _pool_mixers(z_p, z_s, lw, state_pool[li], nb, t)
        mg_p, mg_s = _merge(og_p, og_s, pm_p, pm_s, w_branch_a[li], w_branch_b[li], z_p, z_s, tm=1024, tn=1024)
        h1_p, h1_s, c_p, c_s = _out_norm(mg_p, mg_s, hp, hs, w_out[li], norm_ffn[li], tm=512)
        s_conv = state_conv[li]
        act_p, act_s, c1, gate_s = _up_proj(c_p, c_s, w_up[li], conv_w[li], conv_b[li].reshape(1, D_FF),
                                            s_conv[:, 0], s_conv[:, 1], tm=2048, seq_len=t)
        c2 = jnp.stack([s_conv[:, 1], gate_s], axis=1)
        h2_p, h2_s = _down_proj(act_p, act_s, w_down[li], h1_p, h1_s, tm=512, tn=1024)
        hp, hs = _ple(h2_p, h2_s, p_prompt[li].reshape(nb * t, PLE_DIM), p_sample[li].reshape(ns, PLE_DIM),
                      w_ple_gate[li], w_ple[li], norm_ple[li], norm_final, tm=512,
                      final_norm=li == depth - 1)
        for key, val in zip(("gp", "pp", "cp", "gs", "ps", "cs"), (g1, p1, c1, g2, p2, c2)):
            outs[key].append(val)
    st = {k: jnp.stack(v, axis=0) for k, v in outs.items()}
    return (hp.reshape(nb, t, d), hs.reshape(ns, 1, d),
            st["gp"], st["pp"], st["cp"], st["gs"], st["ps"], st["cs"])
```

```python
import functools

import jax
import jax.numpy as jnp
from jax import lax
from jax.experimental import pallas as pl
from jax.experimental.pallas import tpu as pltpu

F32 = jnp.float32
BF16 = jnp.bfloat16

D_MODEL = 2048
GLA_HEADS = 4
GLA_DK = 256
GLA_DV = 512
GLA_KW = GLA_HEADS * GLA_DK
GLA_VW = GLA_HEADS * GLA_DV
GATE_RANK = 16
GATE_TEMP = 16.0
GLA_CHUNK = 64
POOL_WINDOWS = (2, 4, 8, 16)
POOL_GW = 256
POOL_W = 1024
POOL_BUF = 15
D_FF = 5504
CONV_W = 3
PLE_DIM = 256
EPS = 1e-6

LANE = 128
SUBLANES = 8
Z_Q, Z_K, Z_V, Z_R, Z_U, Z_GA, Z_GB, Z_GLR, Z_WIDTH = 0, 1024, 2048, 4096, 6144, 7168, 9216, 11264, 11392
GLR_SRC = 6144
IN_TN = 1024
IN_SUB = 512

VMEM_LIMIT = 58 * 1024 * 1024


def _cparams(n_axes):
    return pltpu.CompilerParams(dimension_semantics=("arbitrary",) * n_axes, vmem_limit_bytes=VMEM_LIMIT)


def _sigmoid(x):
    return 1.0 / (1.0 + jnp.exp(-x))


def _silu(x):
    return x * _sigmoid(x)


def _log_sigmoid(x):
    return jnp.minimum(x, 0.0) - jnp.log(1.0 + jnp.exp(-jnp.abs(x)))


def _rms(x, g):
    return x * lax.rsqrt(jnp.mean(x * x, axis=-1, keepdims=True) + EPS) * g


def _rms_kernel(xp_ref, xs_ref, g_ref, o_ref):
    i = pl.program_id(0)
    ms = xs_ref.shape[0]

    @pl.when(i < pl.num_programs(0) - 1)
    def _():
        o_ref[...] = _rms(xp_ref[...], g_ref[...]).astype(o_ref.dtype)

    @pl.when(i == pl.num_programs(0) - 1)
    def _():
        o_ref[0:ms, :] = _rms(xs_ref[...], g_ref[...]).astype(o_ref.dtype)


def _rmsnorm(x_p, x_s, g, out_dtype, tm):
    mp, d = x_p.shape
    ms = x_s.shape[0]
    npt = mp // tm
    return pl.pallas_call(
        _rms_kernel,
        grid=(npt + 1,),
        in_specs=[
            pl.BlockSpec((tm, d), lambda i: (jnp.minimum(i, npt - 1), 0)),
            pl.BlockSpec((ms, d), lambda i: (0, 0)),
            pl.BlockSpec((1, d), lambda i: (0, 0)),
        ],
        out_specs=pl.BlockSpec((tm, d), lambda i: (i, 0)),
        out_shape=jax.ShapeDtypeStruct((mp + ms, d), out_dtype),
        compiler_params=_cparams(1),
        name="rmsnorm",
    )(x_p, x_s, g.reshape(1, d))


def _col_tiled_specs(tm, ms, width, col0=0):
    return [
        pl.BlockSpec((tm, width), lambda j, i: (jnp.maximum(i - 1, 0), col0 + j)),
        pl.BlockSpec((ms, width), lambda j, i: (0, col0 + j)),
    ]


def _row_tiled_specs(tm, ms, width):
    return [
        pl.BlockSpec((tm, width), lambda j, i: (jnp.maximum(i - 1, 0), 0)),
        pl.BlockSpec((ms, width), lambda j, i: (0, 0)),
    ]


def _down_kernel(ap_ref, as_ref, w_ref, rp_ref, rs_ref, op_ref, os_ref, wbf_ref):
    i = pl.program_id(1)

    @pl.when(i == 0)
    def _():
        wbf_ref[...] = w_ref[...].astype(BF16)
        os_ref[...] = rs_ref[...] + jnp.dot(as_ref[...], wbf_ref[...], preferred_element_type=F32)

    @pl.when(i > 0)
    def _():
        op_ref[...] = rp_ref[...] + jnp.dot(ap_ref[...], wbf_ref[...], preferred_element_type=F32)


def _down_proj(a_p, a_s, w, res_p, res_s, *, tm, tn):
    mp, k = a_p.shape
    ms = a_s.shape[0]
    n = w.shape[1]
    return pl.pallas_call(
        _down_kernel,
        grid=(n // tn, mp // tm + 1),
        in_specs=[
            *_row_tiled_specs(tm, ms, k),
            pl.BlockSpec((k, tn), lambda j, i: (0, j), pipeline_mode=pl.Buffered(1)),
            *_col_tiled_specs(tm, ms, tn),
        ],
        out_specs=_col_tiled_specs(tm, ms, tn),
        out_shape=[jax.ShapeDtypeStruct((mp, n), F32), jax.ShapeDtypeStruct((ms, n), F32)],
        scratch_shapes=[pltpu.VMEM((k, tn), BF16)],
        compiler_params=_cparams(2),
        name="down_proj",
    )(a_p, a_s, w, res_p, res_s)


_NT = (((1,), (1,)), ((), ()))
_TN = (((0,), (0,)), ((), ()))


def _in_proj_kernel(a_ref, w_ref, wx_ref, z_ref, zs_ref, wbf_ref):
    j = pl.program_id(0)
    i = pl.program_id(1)
    last = pl.num_programs(0) - 1
    last_rows = pl.num_programs(1) - 1
    first_shifted = GLR_SRC // IN_TN
    keep = IN_TN - GATE_RANK
    tm = a_ref.shape[0]
    ms = zs_ref.shape[0]

    @pl.when(i == 0)
    def _():
        @pl.when(j < first_shifted)
        def _():
            wbf_ref[...] = w_ref[...].astype(BF16)

        @pl.when(jnp.logical_and(j >= first_shifted, j < last))
        def _():
            wbf_ref[:keep, :] = w_ref[GATE_RANK:, :].astype(BF16)
            wbf_ref[keep:, :] = wx_ref[...].astype(BF16)

        @pl.when(j == last)
        def _():
            wbf_ref[:GATE_RANK, :] = wx_ref[...].astype(BF16)
            wbf_ref[GATE_RANK:LANE, :] = jnp.zeros((LANE - GATE_RANK, wbf_ref.shape[1]), BF16)

    def emit(cols):
        acc = lax.dot_general(a_ref[...], wbf_ref[cols, :], _NT, preferred_element_type=F32)
        z_ref[:, cols] = acc.astype(z_ref.dtype)

        @pl.when(i == last_rows)
        def _():
            zs_ref[:, cols] = acc[tm - ms:, :]

    @pl.when(j < last)
    def _():
        for part in range(IN_TN // IN_SUB):
            emit(slice(part * IN_SUB, (part + 1) * IN_SUB))

    @pl.when(j == last)
    def _():
        emit(slice(0, LANE))


def _in_proj(a, wt, *, tm, ms):
    m, k = a.shape
    nj = Z_GLR // IN_TN + 1
    aux_per_tile = IN_TN // GATE_RANK

    def aux_idx(j, i):
        return (jnp.where(j == nj - 1, GLR_SRC // GATE_RANK, (j + 1) * aux_per_tile), 0)

    return pl.pallas_call(
        _in_proj_kernel,
        grid=(nj, m // tm),
        in_specs=[
            pl.BlockSpec((tm, k), lambda j, i: (i, 0)),
            pl.BlockSpec((IN_TN, k), lambda j, i: (j, 0)),
            pl.BlockSpec((GATE_RANK, k), aux_idx),
        ],
        out_specs=[
            pl.BlockSpec((tm, IN_TN), lambda j, i: (i, j)),
            pl.BlockSpec((ms, IN_TN), lambda j, i: (0, j)),
        ],
        out_shape=[jax.ShapeDtypeStruct((m, Z_WIDTH), BF16), jax.ShapeDtypeStruct((ms, Z_WIDTH), F32)],
        scratch_shapes=[pltpu.VMEM((IN_TN, k), BF16)],
        compiler_params=_cparams(2),
        name="in_proj",
    )(a, wt, wt)


GLA_GROUP = 8


def _gla_kernel(q_ref, k_ref, v_ref, r_ref, glr_ref, wg_ref, bg_ref, gn_ref,
                sq_ref, sk_ref, sv_ref, sr_ref, sglr_ref, swg_ref, sbg_ref, sgn_ref, s_ref,
                og_ref, s_out_ref, sog_ref, ss_out_ref, st_ref):
    c = GLA_CHUNK
    rows = GLA_GROUP * c
    gi = pl.program_id(2)
    step = (pl.program_id(0) * pl.num_programs(1) + pl.program_id(1)) * pl.num_programs(2) + gi
    n_sample = s_ref.shape[0]

    sample_units = []
    for si in range(n_sample):
        sample_units += _gla_sample_units(step * n_sample + si, si, sq_ref, sk_ref, sv_ref, sr_ref, sglr_ref,
                                          swg_ref, sbg_ref, sgn_ref, s_ref, sog_ref, ss_out_ref)

    @pl.when(gi == 0)
    def _():
        st_ref[...] = jnp.zeros_like(st_ref)

    row = lax.broadcasted_iota(jnp.int32, (c, c), 0)
    col = lax.broadcasted_iota(jnp.int32, (c, c), 1)
    causal = row >= col
    tri = causal.astype(BF16)

    def group():
        r0 = pl.multiple_of(gi * rows, rows)
        rs = pl.ds(r0, rows)
        zg = jnp.dot(glr_ref[rs, :], wg_ref[...], preferred_element_type=F32) + bg_ref[...]
        v = v_ref[rs, :]
        vals = [dict() for _ in range(GLA_GROUP)]
        state = {"st": st_ref[...]}

        def s_gate(ci, d):
            gl = _log_sigmoid(zg[ci * c:(ci + 1) * c]) * (1.0 / GATE_TEMP)
            d["gl_hi"] = gl.astype(BF16)
            d["gl_lo"] = (gl - d["gl_hi"].astype(F32)).astype(BF16)

        def s_cumsum(ci, d):
            d["b"] = (jnp.dot(tri, d.pop("gl_hi"), preferred_element_type=F32)
                      + jnp.dot(tri, d.pop("gl_lo"), preferred_element_type=F32))

        def s_scale(ci, d):
            cr = pl.ds(r0 + ci * c, c)
            b = d.pop("b")
            bl = b[c - 1:c, :]
            q = q_ref[cr, :].astype(F32) * (GLA_DK ** -0.5)
            k = k_ref[cr, :].astype(F32)
            d["qb"] = (q * jnp.exp(b)).astype(BF16)
            d["kb"] = (k * jnp.exp(-b)).astype(BF16)
            d["kd"] = (k * jnp.exp(bl - b)).astype(BF16)
            d["decay"] = jnp.exp(bl)

        def s_att(ci, d):
            sl = slice(ci * c, (ci + 1) * c)
            d["att"] = lax.dot_general(d["qb"], d.pop("kb"), _NT, preferred_element_type=F32)
            d["dst"] = lax.dot_general(v[sl], d.pop("kd"), _TN, preferred_element_type=F32)

        def s_mask(ci, d):
            d["att"] = jnp.where(causal, d["att"], 0.0).astype(BF16)

        def s_intra(ci, d):
            sl = slice(ci * c, (ci + 1) * c)
            d["o"] = jnp.dot(d.pop("att"), v[sl], preferred_element_type=F32)

        def s_chain(ci, d):
            st = state["st"]
            d["st_in"] = st.astype(BF16)
            state["st"] = st * d.pop("decay") + d.pop("dst")

        def s_inter(ci, d):
            d["o"] = d["o"] + lax.dot_general(d.pop("qb"), d.pop("st_in"), _NT, preferred_element_type=F32)

        def s_finish(ci, d):
            cr = pl.ds(r0 + ci * c, c)
            on = _rms(d.pop("o"), gn_ref[...])
            og_ref[cr, :] = (on * _silu(r_ref[cr, :].astype(F32))).astype(og_ref.dtype)

        stages = (s_gate, s_cumsum, s_scale, s_att, s_mask, s_intra, s_chain, s_inter, s_finish)
        n_ticks = GLA_GROUP + len(stages) - 1
        pending = list(sample_units)
        for tick in range(n_ticks):
            for si, stage in enumerate(stages):
                ci = tick - si
                if 0 <= ci < GLA_GROUP:
                    stage(ci, vals[ci])
            if pending and (tick + 1) * len(sample_units) >= (len(sample_units) - len(pending) + 1) * n_ticks:
                pending.pop(0)()
        for unit in pending:
            unit()
        st_ref[...] = state["st"]

    group()

    @pl.when(gi == pl.num_programs(2) - 1)
    def _():
        s_out_ref[0, 0] = st_ref[...].T


def _gla(z_p, z_s, wg, bg, gn, state, nb, t):
    m = nb * t
    n = z_s.shape[0]
    n_groups = t // (GLA_GROUP * GLA_CHUNK)
    n_steps = nb * GLA_HEADS * n_groups
    n_sample = n // n_steps
    assert n_sample * n_steps == n

    def s_idx(b, h, g):
        return ((b * GLA_HEADS + h) * n_groups + g, 0, 0, 0)

    const2 = lambda b, h, g: (0, 0)
    og_p, s_new_p, og_s, s_new_s = pl.pallas_call(
        _gla_kernel,
        grid=(nb, GLA_HEADS, n_groups),
        in_specs=[
            pl.BlockSpec((t, GLA_DK), lambda b, h, g: (b, Z_Q // GLA_DK + h)),
            pl.BlockSpec((t, GLA_DK), lambda b, h, g: (b, Z_K // GLA_DK + h)),
            pl.BlockSpec((t, GLA_DV), lambda b, h, g: (b, Z_V // GLA_DV + h)),
            pl.BlockSpec((t, GLA_DV), lambda b, h, g: (b, Z_R // GLA_DV + h)),
            pl.BlockSpec((t, LANE), lambda b, h, g: (b, Z_GLR // LANE)),
            pl.BlockSpec((LANE, GLA_DK), lambda b, h, g: (0, h)),
            pl.BlockSpec((1, GLA_DK), lambda b, h, g: (0, h)),
            pl.BlockSpec((1, GLA_DV), lambda b, h, g: (0, h)),
            pl.BlockSpec((n, GLA_KW), lambda b, h, g: (0, Z_Q // GLA_KW)),
            pl.BlockSpec((n, GLA_KW), lambda b, h, g: (0, Z_K // GLA_KW)),
            pl.BlockSpec((n, GLA_VW), lambda b, h, g: (0, Z_V // GLA_VW)),
            pl.BlockSpec((n, GLA_VW), lambda b, h, g: (0, Z_R // GLA_VW)),
            pl.BlockSpec((n, LANE), lambda b, h, g: (0, Z_GLR // LANE)),
            pl.BlockSpec((LANE, GLA_KW), const2),
            pl.BlockSpec((1, GLA_KW), const2),
            pl.BlockSpec((1, GLA_VW), const2),
            pl.BlockSpec((n_sample, GLA_HEADS, GLA_DK, GLA_DV), s_idx),
        ],
        out_specs=[
            pl.BlockSpec((t, GLA_DV), lambda b, h, g: (b, h)),
            pl.BlockSpec((1, 1, GLA_DK, GLA_DV), lambda b, h, g: (b, h, 0, 0)),
            pl.BlockSpec((n, GLA_VW), const2),
            pl.BlockSpec((n_sample, GLA_HEADS, GLA_DK, GLA_DV), s_idx),
        ],
        out_shape=[
            jax.ShapeDtypeStruct((m, GLA_VW), BF16),
            jax.ShapeDtypeStruct((nb, GLA_HEADS, GLA_DK, GLA_DV), F32),
            jax.ShapeDtypeStruct((n, GLA_VW), F32),
            jax.ShapeDtypeStruct((n, GLA_HEADS, GLA_DK, GLA_DV), F32),
        ],
        scratch_shapes=[pltpu.VMEM((GLA_DV, GLA_DK), F32)],
        compiler_params=_cparams(3),
        name="gla",
    )(z_p, z_p, z_p, z_p, z_p, wg, bg, gn, z_s, z_s, z_s, z_s, z_s, wg, bg, gn, state)
    return og_p, s_new_p, og_s, s_new_s


def _gla_sample_units(n, si, q_ref, k_ref, v_ref, r_ref, glr_ref, wg_ref, bg_ref, gn_ref, s_ref,
                      og_ref, s_out_ref):
    row = pl.ds(n, 1)

    def rows8(ref):
        return jnp.broadcast_to(ref[row, :], (SUBLANES, ref.shape[1]))

    zg = jnp.dot(rows8(glr_ref).astype(BF16), wg_ref[...], preferred_element_type=F32) + bg_ref[...]
    g = _log_sigmoid(zg) * (1.0 / GATE_TEMP)
    eg = jnp.exp(g)
    eng = jnp.exp(-g)
    eg1 = eg.astype(BF16)
    rem = eg - eg1.astype(F32)
    eg2 = rem.astype(BF16)
    eg3 = (rem - eg2.astype(F32)).astype(BF16)
    first_row = lax.broadcasted_iota(jnp.int32, (SUBLANES, LANE), 0) == 0
    ones_row = jnp.where(first_row, 1.0, 0.0).astype(BF16)
    row0 = lax.broadcasted_iota(jnp.int32, (SUBLANES, GLA_DV), 0) == 0
    q_all = rows8(q_ref) * (GLA_DK ** -0.5)
    k_all = rows8(k_ref)
    v_all = rows8(v_ref)
    r_all = r_ref[row, :]
    gn_all = gn_ref[...]

    def head_unit(h):
        ks = slice(h * GLA_DK, (h + 1) * GLA_DK)
        vs = slice(h * GLA_DV, (h + 1) * GLA_DV)
        q, k, v = q_all[:, ks], k_all[:, ks], v_all[:, vs]
        s_old = s_ref[si, h]
        decay = (lax.dot_general(eg1[:, ks], ones_row, _TN, preferred_element_type=F32)
                 + lax.dot_general(eg2[:, ks], ones_row, _TN, preferred_element_type=F32)
                 + lax.dot_general(eg3[:, ks], ones_row, _TN, preferred_element_type=F32))
        v0 = jnp.where(row0, v, 0.0)
        ds = lax.dot_general(k.astype(BF16), v0.astype(BF16), _TN, preferred_element_type=F32)
        s_out_ref[si, h] = jnp.tile(decay, (1, GLA_DV // LANE)) * s_old + ds
        qb = q * eg[:, ks]
        kb = k * eng[:, ks]
        att = jnp.sum(qb * kb, axis=-1, keepdims=True)
        o = jnp.dot(qb.astype(BF16), s_old.astype(BF16), preferred_element_type=F32) + att * v
        o = o[0:1, :]
        ms = jnp.mean(o * o, axis=-1, keepdims=True)
        on = o * lax.rsqrt(ms + EPS) * gn_all[:, vs]
        og_ref[row, vs] = on * _silu(r_all[:, vs])

    return [functools.partial(head_unit, h) for h in range(GLA_HEADS)]


def _pool_group_out(mix, g, wp_ref, ps_ref):
    cs = slice(g * POOL_GW, (g + 1) * POOL_GW)
    pm = jnp.dot(mix.astype(BF16), wp_ref[g].astype(BF16), preferred_element_type=F32)
    return pm * ps_ref[:, cs]


POOL_LEVELS = tuple(w.bit_length() - 1 for w in POOL_WINDOWS)
assert all(1 << lv == w for lv, w in zip(POOL_LEVELS, POOL_WINDOWS)), "windows must be powers of two"
POOL_HIST = SUBLANES * max(POOL_LEVELS)
assert POOL_HIST > POOL_BUF + 1


def _pool_prompt_kernel(u_ref, wp_ref, ps_ref, o_ref, ext_ref, lvl_ref):
    i = pl.program_id(1)
    tt = u_ref.shape[0]
    hist = POOL_HIST
    n = hist + tt

    @pl.when(i == 0)
    def _():
        ext_ref[0:hist, :] = jnp.zeros((hist, POOL_W), F32)

    @pl.when(i > 0)
    def _():
        ext_ref[0:hist, :] = ext_ref[tt:n, :]

    ext_ref[hist:n, :] = u_ref[...].astype(F32)
    pos = i * tt + lax.broadcasted_iota(jnp.int32, (tt, 1), 0)
    for g, (w, levels) in enumerate(zip(POOL_WINDOWS, POOL_LEVELS)):
        cs = slice(g * POOL_GW, (g + 1) * POOL_GW)
        cur = None
        for lev in range(levels):
            sh = 1 << lev
            start = SUBLANES * (lev + 1)
            if lev == 0:
                cur = ext_ref[start:n, cs] + ext_ref[start - sh:n - sh, cs]
            else:
                prev = lvl_ref.at[(lev - 1) % 2]
                cur = prev[start:n, :] + prev[start - sh:n - sh, :]
            if lev + 1 < levels:
                lvl_ref[lev % 2, start:n, :] = cur
        s = cur[hist - SUBLANES * levels:]
        u = ext_ref[hist:n, cs]
        cnt = jnp.minimum(pos + 1, w).astype(F32)
        mix = s / cnt - u
        o_ref[:, cs] = _pool_group_out(mix, g, wp_ref, ps_ref).astype(o_ref.dtype)


def _pool_prompt(z, w_pool, pool_scale, nb, t, tt):
    m = nb * t
    nt = t // tt
    return pl.pallas_call(
        _pool_prompt_kernel,
        grid=(nb, nt),
        in_specs=[
            pl.BlockSpec((tt, POOL_W), lambda b, i: (b * nt + i, Z_U // POOL_W)),
            pl.BlockSpec((4, POOL_GW, POOL_GW), lambda b, i: (0, 0, 0)),
            pl.BlockSpec((1, POOL_W), lambda b, i: (0, 0)),
        ],
        out_specs=pl.BlockSpec((tt, POOL_W), lambda b, i: (b * nt + i, 0)),
        out_shape=jax.ShapeDtypeStruct((m, POOL_W), BF16),
        scratch_shapes=[pltpu.VMEM((tt + POOL_HIST, POOL_W), F32), pltpu.VMEM((2, tt + POOL_HIST, POOL_GW), F32)],
        compiler_params=_cparams(2),
        name="pool_prompt",
    )(z, w_pool, pool_scale)


def _pool_sample_kernel(u_ref, buf_ref, wp_ref, ps_ref, o_ref, buf_out_ref):
    buf_out_ref[0:POOL_BUF - 1] = buf_ref[1:POOL_BUF]
    buf_out_ref[POOL_BUF - 1] = u_ref[...].astype(F32)
    for g, w in enumerate(POOL_WINDOWS):
        cs = slice(g * POOL_GW, (g + 1) * POOL_GW)
        u = u_ref[:, cs].astype(F32)
        s = u
        for d in range(1, w):
            s = s + buf_ref[POOL_BUF - d, :, cs]
        mix = s / float(w) - u
        o_ref[:, cs] = _pool_group_out(mix, g, wp_ref, ps_ref).astype(o_ref.dtype)


def _pool_sample(z, buf_t, w_pool, pool_scale):
    n = z.shape[0]
    return pl.pallas_call(
        _pool_sample_kernel,
        grid=(1,),
        in_specs=[
            pl.BlockSpec((n, POOL_W), lambda i: (0, Z_U // POOL_W)),
            pl.BlockSpec((POOL_BUF, n, POOL_W), lambda i: (0, 0, 0)),
            pl.BlockSpec((4, POOL_GW, POOL_GW), lambda i: (0, 0, 0)),
            pl.BlockSpec((1, POOL_W), lambda i: (0, 0)),
        ],
        out_specs=[pl.BlockSpec((n, POOL_W), lambda i: (0, 0)),
                   pl.BlockSpec((POOL_BUF, n, POOL_W), lambda i: (0, 0, 0))],
        out_shape=[jax.ShapeDtypeStruct((n, POOL_W), BF16), jax.ShapeDtypeStruct((POOL_BUF, n, POOL_W), F32)],
        compiler_params=_cparams(1),
        name="pool_sample",
    )(z, buf_t, w_pool, pool_scale)


MERGE_SUB = 256


def _merge_kernel(ogp_ref, ogs_ref, pmp_ref, pms_ref, wa_ref, wb_ref, gap_ref, gas_ref, gbp_ref, gbs_ref,
                  op_ref, os_ref, wabf_ref, wbbf_ref):
    i = pl.program_id(1)

    @pl.when(i == 0)
    def _():
        wabf_ref[...] = wa_ref[...].astype(BF16)
        wbbf_ref[...] = wb_ref[...].astype(BF16)

    def emit(og_ref, pm_ref, ga_ref, gb_ref, o_ref):
        og = og_ref[...].astype(BF16)
        pm = pm_ref[...]
        for part in range(o_ref.shape[1] // MERGE_SUB):
            cols = slice(part * MERGE_SUB, (part + 1) * MERGE_SUB)
            ya = jnp.dot(og, wabf_ref[:, cols], preferred_element_type=F32)
            yb = jnp.dot(pm, wbbf_ref[:, cols], preferred_element_type=F32)
            ga = ga_ref[:, cols].astype(F32)
            gb = gb_ref[:, cols].astype(F32)
            o_ref[:, cols] = (_sigmoid(ga) * ya + _sigmoid(gb) * yb).astype(o_ref.dtype)

    @pl.when(i == 0)
    def _():
        emit(ogs_ref, pms_ref, gas_ref, gbs_ref, os_ref)

    @pl.when(i > 0)
    def _():
        emit(ogp_ref, pmp_ref, gap_ref, gbp_ref, op_ref)


def _merge(og_p, og_s, pm_p, pm_s, wa, wb, z_p, z_s, *, tm, tn):
    mp = og_p.shape[0]
    ms = og_s.shape[0]
    return pl.pallas_call(
        _merge_kernel,
        grid=(D_MODEL // tn, mp // tm + 1),
        in_specs=[
            *_row_tiled_specs(tm, ms, GLA_VW),
            *_row_tiled_specs(tm, ms, POOL_W),
            pl.BlockSpec((GLA_VW, tn), lambda j, i: (0, j), pipeline_mode=pl.Buffered(1)),
            pl.BlockSpec((POOL_W, tn), lambda j, i: (0, j), pipeline_mode=pl.Buffered(1)),
            *_col_tiled_specs(tm, ms, tn, Z_GA // tn),
            *_col_tiled_specs(tm, ms, tn, Z_GB // tn),
        ],
        out_specs=_col_tiled_specs(tm, ms, tn),
        out_shape=[jax.ShapeDtypeStruct((mp, D_MODEL), BF16), jax.ShapeDtypeStruct((ms, D_MODEL), BF16)],
        scratch_shapes=[pltpu.VMEM((GLA_VW, tn), BF16), pltpu.VMEM((POOL_W, tn), BF16)],
        compiler_params=_cparams(2),
        name="merge",
    )(og_p, og_s, pm_p, pm_s, wa, wb, z_p, z_s, z_p, z_s)


UP_TN = 512
UP_SUB = 256
UP_ROWS = 1024
UP_PARTS = UP_TN // LANE


def _up_kernel(*refs, tiles_per_seq):
    cp_ref, cs_ref, wg_ref = refs[:3]
    wu_refs = refs[3:3 + UP_PARTS]
    (cw_ref, cb_ref, h0_ref, h1_ref, actp_ref, acts_ref, tailp_ref, gates_ref,
     wgbf_ref, wubf_ref, carry_ref) = refs[3 + UP_PARTS:]
    i = pl.program_id(1)

    @pl.when(i == 0)
    def _():
        wgbf_ref[...] = wg_ref[...].astype(BF16)
        for part, wu_ref in enumerate(wu_refs):
            wubf_ref[:, part * LANE:(part + 1) * LANE] = wu_ref[...].astype(BF16)

    def conv_act(gate, up, g1, g2, cols):
        cw = cw_ref[:, cols]
        gconv = cb_ref[:, cols] + cw[0:1] * g2 + cw[1:2] * g1 + cw[2:3] * gate
        return _silu(gconv) * up

    @pl.when(i > 0)
    def _():
        @pl.when((i - 1) % tiles_per_seq == 0)
        def _():
            carry_ref[...] = jnp.zeros_like(carry_ref)

        tm = cp_ref.shape[0]
        rows = min(tm, UP_ROWS)
        sub = lax.broadcasted_iota(jnp.int32, (SUBLANES, UP_SUB), 0)
        n_cols = UP_TN // UP_SUB
        prev = [carry_ref[:, h * UP_SUB:(h + 1) * UP_SUB] for h in range(n_cols)]
        for rt in range(tm // rows):
            rsl = slice(rt * rows, (rt + 1) * rows)
            c = cp_ref[rsl, :]
            for h in range(n_cols):
                cols = slice(h * UP_SUB, (h + 1) * UP_SUB)
                gate = jnp.dot(c, wgbf_ref[:, cols], preferred_element_type=F32)
                up = jnp.dot(c, wubf_ref[:, cols], preferred_element_type=F32)
                r1 = pltpu.roll(gate, 1, 0)
                r2 = pltpu.roll(gate, 2, 0)
                head1 = jnp.where(sub < 1, pltpu.roll(prev[h], 1, 0), r1[0:SUBLANES])
                head2 = jnp.where(sub < 2, pltpu.roll(prev[h], 2, 0), r2[0:SUBLANES])
                g1 = jnp.concatenate([head1, r1[SUBLANES:]], axis=0)
                g2 = jnp.concatenate([head2, r2[SUBLANES:]], axis=0)
                prev[h] = gate[rows - SUBLANES:rows]
                actp_ref[rsl, cols] = conv_act(gate, up, g1, g2, cols).astype(actp_ref.dtype)
        for h in range(n_cols):
            cols = slice(h * UP_SUB, (h + 1) * UP_SUB)
            carry_ref[:, cols] = prev[h]
            tailp_ref[0, :, cols] = prev[h][SUBLANES - (CONV_W - 1):SUBLANES]

    @pl.when(i == 0)
    def _():
        c = cs_ref[...]
        gate = jnp.dot(c, wgbf_ref[...], preferred_element_type=F32)
        up = jnp.dot(c, wubf_ref[...], preferred_element_type=F32)
        gates_ref[...] = gate
        acts_ref[...] = conv_act(gate, up, h1_ref[...], h0_ref[...], slice(None)).astype(acts_ref.dtype)


def _up_proj(c_p, c_s, w_up, conv_w, conv_b, hist0, hist1, *, tm, seq_len):
    mp, k = c_p.shape
    ms = c_s.shape[0]
    npt = mp // tm
    tiles_per_seq = seq_len // tm
    nseq = mp // seq_len
    tn = UP_TN
    nj = pl.cdiv(D_FF, tn)
    up_block0 = D_FF // LANE
    last_block = w_up.shape[1] // LANE - 1

    def row_p(j, i):
        return jnp.maximum(i - 1, 0)

    def wu_spec(part):
        return pl.BlockSpec(
            (k, LANE), lambda j, i: (0, jnp.minimum(up_block0 + j * UP_PARTS + part, last_block)))

    in_specs = [
        pl.BlockSpec((tm, k), lambda j, i: (row_p(j, i), 0)),
        pl.BlockSpec((ms, k), lambda j, i: (0, 0)),
        pl.BlockSpec((k, tn), lambda j, i: (0, j)),
        *[wu_spec(part) for part in range(UP_PARTS)],
        pl.BlockSpec((CONV_W, tn), lambda j, i: (0, j)),
        pl.BlockSpec((1, tn), lambda j, i: (0, j)),
        pl.BlockSpec((ms, tn), lambda j, i: (0, j)),
        pl.BlockSpec((ms, tn), lambda j, i: (0, j)),
    ]
    out_specs = [
        pl.BlockSpec((tm, tn), lambda j, i: (row_p(j, i), j)),
        pl.BlockSpec((ms, tn), lambda j, i: (0, j)),
        pl.BlockSpec((1, CONV_W - 1, tn), lambda j, i: (row_p(j, i) // tiles_per_seq, 0, j)),
        pl.BlockSpec((ms, tn), lambda j, i: (0, j)),
    ]
    out_shape = [
        jax.ShapeDtypeStruct((mp, D_FF), BF16),
        jax.ShapeDtypeStruct((ms, D_FF), BF16),
        jax.ShapeDtypeStruct((nseq, CONV_W - 1, D_FF), F32),
        jax.ShapeDtypeStruct((ms, D_FF), F32),
    ]
    return pl.pallas_call(
        functools.partial(_up_kernel, tiles_per_seq=tiles_per_seq),
        grid=(nj, npt + 1),
        in_specs=in_specs,
        out_specs=out_specs,
        out_shape=out_shape,
        scratch_shapes=[pltpu.VMEM((k, tn), BF16), pltpu.VMEM((k, tn), BF16), pltpu.VMEM((SUBLANES, tn), F32)],
        compiler_params=_cparams(2),
        name="up_proj",
    )(c_p, c_s, w_up, *([w_up] * UP_PARTS), conv_w, conv_b, hist0, hist1)


def _two_group_specs(tm, ms, widths):
    specs = []
    for w in widths:
        specs.append(pl.BlockSpec((tm, w), lambda i: (jnp.maximum(i - 1, 0), 0)))
        specs.append(pl.BlockSpec((ms, w), lambda i: (0, 0)))
    return specs


def _resident(shape):
    return pl.BlockSpec(shape, lambda i: (0,) * len(shape), pipeline_mode=pl.Buffered(1))


def _out_norm_kernel(mp_ref, ms_ref, xp_ref, xs_ref, w_ref, g_ref, hp_ref, hs_ref, cp_ref, cs_ref, wbf_ref):
    i = pl.program_id(0)

    @pl.when(i == 0)
    def _():
        wbf_ref[...] = w_ref[...].astype(BF16)

    def emit(m_ref, x_ref, h_ref, c_ref):
        h = x_ref[...] + jnp.dot(m_ref[...], wbf_ref[...], preferred_element_type=F32)
        h_ref[...] = h
        c_ref[...] = _rms(h, g_ref[...]).astype(c_ref.dtype)

    @pl.when(i == 0)
    def _():
        emit(ms_ref, xs_ref, hs_ref, cs_ref)

    @pl.when(i > 0)
    def _():
        emit(mp_ref, xp_ref, hp_ref, cp_ref)


def _out_norm(m_p, m_s, x_p, x_s, w_out, g, *, tm):
    mp, d = x_p.shape
    ms = x_s.shape[0]
    return pl.pallas_call(
        _out_norm_kernel,
        grid=(mp // tm + 1,),
        in_specs=[*_two_group_specs(tm, ms, (d, d)), _resident((d, d)), _resident((1, d))],
        out_specs=_two_group_specs(tm, ms, (d, d)),
        out_shape=[
            jax.ShapeDtypeStruct((mp, d), F32), jax.ShapeDtypeStruct((ms, d), F32),
            jax.ShapeDtypeStruct((mp, d), BF16), jax.ShapeDtypeStruct((ms, d), BF16),
        ],
        scratch_shapes=[pltpu.VMEM((d, d), BF16)],
        compiler_params=_cparams(1),
        name="out_norm",
    )(m_p, m_s, x_p, x_s, w_out, g.reshape(1, d))


def _ple_kernel(hp_ref, hs_ref, pp_ref, ps_ref, wpg_ref, wple_ref, gple_ref, gfin_ref, yp_ref, ys_ref,
                wpgbf_ref, wplebf_ref, *, final_norm):
    i = pl.program_id(0)

    @pl.when(i == 0)
    def _():
        wpgbf_ref[...] = wpg_ref[...].astype(BF16)
        wplebf_ref[...] = wple_ref[...].astype(BF16)

    def emit(h_ref, p_ref, y_ref):
        h2 = h_ref[...]
        cn = _rms(h2, gple_ref[...]).astype(BF16)
        pg = _sigmoid(jnp.dot(cn, wpgbf_ref[...], preferred_element_type=F32))
        e = jnp.dot(p_ref[...].astype(BF16), wplebf_ref[...], preferred_element_type=F32)
        h3 = h2 + pg * e
        y_ref[...] = _rms(h3, gfin_ref[...]) if final_norm else h3

    @pl.when(i == 0)
    def _():
        emit(hs_ref, ps_ref, ys_ref)

    @pl.when(i > 0)
    def _():
        emit(hp_ref, pp_ref, yp_ref)


def _ple(h_p, h_s, p_p, p_s, wpg, wple, g_ple, g_final, *, tm, final_norm):
    mp, d = h_p.shape
    ms = h_s.shape[0]
    pd = p_p.shape[1]
    return pl.pallas_call(
        functools.partial(_ple_kernel, final_norm=final_norm),
        grid=(mp // tm + 1,),
        in_specs=[*_two_group_specs(tm, ms, (d, pd)), _resident((d, d)), _resident((pd, d)),
                  _resident((1, d)), _resident((1, d))],
        out_specs=_two_group_specs(tm, ms, (d,)),
        out_shape=[jax.ShapeDtypeStruct((mp, d), F32), jax.ShapeDtypeStruct((ms, d), F32)],
        scratch_shapes=[pltpu.VMEM((d, d), BF16), pltpu.VMEM((pd, d), BF16)],
        compiler_params=_cparams(1),
        name="ple",
    )(h_p, h_s, p_p, p_s, wpg, wple, g_ple.reshape(1, d), g_final.reshape(1, d))


def _pool_mixers(z_p, z_s, lw, s_pool, nb, t):
    pm_s, pool_new_t = _pool_sample(z_s, jnp.transpose(s_pool, (1, 0, 2)), lw["w_pool"], lw["pool_scale"])
    pool_new_s = jnp.transpose(pool_new_t, (1, 0, 2))
    pm_p = _pool_prompt(z_p, lw["w_pool"], lw["pool_scale"], nb, t, 1024)
    pool_new_p = z_p[:nb * t].reshape(nb, t, Z_WIDTH)[:, t - POOL_BUF:, Z_U:Z_U + POOL_W].astype(F32)
    return pm_p, pm_s, pool_new_p, pool_new_s


def kernel(x_prompt, x_sample, state_gla, state_pool, state_conv, p_prompt, p_sample, norm_mix, w_in, w_gate_up, b_gate, gla_norm, w_branch_a, w_pool, pool_scale, w_branch_b, w_out, norm_ffn, w_up, conv_w, conv_b, w_down, norm_ple, w_ple_gate, w_ple, norm_final):
    depth = w_in.shape[0]
    nb, t, d = x_prompt.shape
    ns = x_sample.shape[0]
    hp = x_prompt.reshape(nb * t, d)
    hs = x_sample.reshape(ns, d)
    outs = {k: [] for k in ("gp", "pp", "cp", "gs", "ps", "cs")}
    for li in range(depth):
        lw = {
            "wg": jnp.pad(w_gate_up[li], ((0, LANE - GATE_RANK), (0, 0))).astype(BF16),
            "bg": b_gate[li].reshape(1, GLA_KW),
            "gn": gla_norm[li].reshape(1, GLA_VW),
            "w_pool": w_pool[li],
            "pool_scale": pool_scale[li].reshape(1, POOL_W),
        }
        a = _rmsnorm(hp, hs, norm_mix[li], BF16, 1024)
        z_p, z_s = _in_proj(a, w_in[li].T, tm=(nb * t + ns) // nb, ms=ns)
        og_p, g1, og_s, g2 = _gla(z_p, z_s, lw["wg"], lw["bg"], lw["gn"], state_gla[li], nb, t)
        pm_p, pm_s, p1, p2 = _pool_mixers(z_p, z_s, lw, state_pool[li], nb, t)
        mg_p, mg_s = _merge(og_p, og_s, pm_p, pm_s, w_branch_a[li], w_branch_b[li], z_p, z_s, tm=1024, tn=1024)
        h1_p, h1_s, c_p, c_s = _out_norm(mg_p, mg_s, hp, hs, w_out[li], norm_ffn[li], tm=512)
        s_conv = state_conv[li]
        act_p, act_s, c1, gate_s = _up_proj(c_p, c_s, w_up[li], conv_w[li], conv_b[li].reshape(1, D_FF),
                                            s_conv[:, 0], s_conv[:, 1], tm=2048, seq_len=t)
        c2 = jnp.stack([s_conv[:, 1], gate_s], axis=1)
        h2_p, h2_s = _down_proj(act_p, act_s, w_down[li], h1_p, h1_s, tm=512, tn=1024)
        hp, hs = _ple(h2_p, h2_s, p_prompt[li].reshape(nb * t, PLE_DIM), p_sample[li].reshape(ns, PLE_DIM),
                      w_ple_gate[li], w_ple[li], norm_ple[li], norm_final, tm=512,
                      final_norm=li == depth - 1)
        for key, val in zip(("gp", "pp", "cp", "gs", "ps", "cs"), (g1, p1, c1, g2, p2, c2)):
            outs[key].append(val)
    st = {k: jnp.stack(v, axis=0) for k, v in outs.items()}
    return (hp.reshape(nb, t, d), hs.reshape(ns, 1, d),
            st["gp"], st["pp"], st["cp"], st["gs"], st["ps"], st["cs"])
```

```python
import functools

import jax
import jax.numpy as jnp
from jax import lax
from jax.experimental import pallas as pl
from jax.experimental.pallas import tpu as pltpu

F32 = jnp.float32
BF16 = jnp.bfloat16

D_MODEL = 2048
GLA_HEADS = 4
GLA_DK = 256
GLA_DV = 512
GLA_KW = GLA_HEADS * GLA_DK
GLA_VW = GLA_HEADS * GLA_DV
GATE_RANK = 16
GATE_TEMP = 16.0
GLA_CHUNK = 64
POOL_WINDOWS = (2, 4, 8, 16)
POOL_GW = 256
POOL_W = 1024
POOL_BUF = 15
D_FF = 5504
CONV_W = 3
PLE_DIM = 256
EPS = 1e-6

LANE = 128
SUBLANES = 8
Z_Q, Z_K, Z_V, Z_R, Z_U, Z_GA, Z_GB, Z_GLR, Z_WIDTH = 0, 1024, 2048, 4096, 6144, 7168, 9216, 11264, 11392
GLR_SRC = 6144
IN_TN = 1024
IN_SUB = 512

VMEM_LIMIT = 58 * 1024 * 1024


def _cparams(n_axes):
    return pltpu.CompilerParams(dimension_semantics=("arbitrary",) * n_axes, vmem_limit_bytes=VMEM_LIMIT)


def _sigmoid(x):
    return 1.0 / (1.0 + jnp.exp(-x))


def _silu(x):
    return x * _sigmoid(x)


def _log_sigmoid(x):
    return jnp.minimum(x, 0.0) - jnp.log(1.0 + jnp.exp(-jnp.abs(x)))


def _rms(x, g):
    return x * lax.rsqrt(jnp.mean(x * x, axis=-1, keepdims=True) + EPS) * g


def _rms_kernel(x_ref, g_ref, o_ref):
    o_ref[...] = _rms(x_ref[...], g_ref[...]).astype(o_ref.dtype)


def _rmsnorm(x, g, out_dtype, tm):
    m, d = x.shape
    return pl.pallas_call(
        _rms_kernel,
        grid=(m // tm,),
        in_specs=[pl.BlockSpec((tm, d), lambda i: (i, 0)), pl.BlockSpec((1, d), lambda i: (0, 0))],
        out_specs=pl.BlockSpec((tm, d), lambda i: (i, 0)),
        out_shape=jax.ShapeDtypeStruct((m, d), out_dtype),
        compiler_params=_cparams(1),
        name="rmsnorm",
    )(x, g.reshape(1, d))


def _col_tiled_specs(tm, ms, width, col0=0):
    return [
        pl.BlockSpec((tm, width), lambda j, i: (jnp.maximum(i - 1, 0), col0 + j)),
        pl.BlockSpec((ms, width), lambda j, i: (0, col0 + j)),
    ]


def _row_tiled_specs(tm, ms, width):
    return [
        pl.BlockSpec((tm, width), lambda j, i: (jnp.maximum(i - 1, 0), 0)),
        pl.BlockSpec((ms, width), lambda j, i: (0, 0)),
    ]


def _down_kernel(ap_ref, as_ref, w_ref, rp_ref, rs_ref, op_ref, os_ref, wbf_ref):
    i = pl.program_id(1)

    @pl.when(i == 0)
    def _():
        wbf_ref[...] = w_ref[...].astype(BF16)
        os_ref[...] = rs_ref[...] + jnp.dot(as_ref[...], wbf_ref[...], preferred_element_type=F32)

    @pl.when(i > 0)
    def _():
        op_ref[...] = rp_ref[...] + jnp.dot(ap_ref[...], wbf_ref[...], preferred_element_type=F32)


def _down_proj(a_p, a_s, w, res_p, res_s, *, tm, tn):
    mp, k = a_p.shape
    ms = a_s.shape[0]
    n = w.shape[1]
    return pl.pallas_call(
        _down_kernel,
        grid=(n // tn, mp // tm + 1),
        in_specs=[
            *_row_tiled_specs(tm, ms, k),
            pl.BlockSpec((k, tn), lambda j, i: (0, j), pipeline_mode=pl.Buffered(1)),
            *_col_tiled_specs(tm, ms, tn),
        ],
        out_specs=_col_tiled_specs(tm, ms, tn),
        out_shape=[jax.ShapeDtypeStruct((mp, n), F32), jax.ShapeDtypeStruct((ms, n), F32)],
        scratch_shapes=[pltpu.VMEM((k, tn), BF16)],
        compiler_params=_cparams(2),
        name="down_proj",
    )(a_p, a_s, w, res_p, res_s)


_NT = (((1,), (1,)), ((), ()))
_TN = (((0,), (0,)), ((), ()))


def _in_proj_kernel(ap_ref, as_ref, w_ref, wx_ref, zp_ref, zs_ref, wbf_ref):
    j = pl.program_id(0)
    i = pl.program_id(1)
    last = pl.num_programs(0) - 1
    first_shifted = GLR_SRC // IN_TN
    keep = IN_TN - GATE_RANK

    @pl.when(i == 0)
    def _():
        @pl.when(j < first_shifted)
        def _():
            wbf_ref[...] = w_ref[...].astype(BF16)

        @pl.when(jnp.logical_and(j >= first_shifted, j < last))
        def _():
            wbf_ref[:keep, :] = w_ref[GATE_RANK:, :].astype(BF16)
            wbf_ref[keep:, :] = wx_ref[...].astype(BF16)

        @pl.when(j == last)
        def _():
            wbf_ref[:GATE_RANK, :] = wx_ref[...].astype(BF16)
            wbf_ref[GATE_RANK:LANE, :] = jnp.zeros((LANE - GATE_RANK, wbf_ref.shape[1]), BF16)

    def emit(a_ref, z_ref):
        @pl.when(j < last)
        def _():
            a = a_ref[...]
            for part in range(IN_TN // IN_SUB):
                cols = slice(part * IN_SUB, (part + 1) * IN_SUB)
                z_ref[:, cols] = lax.dot_general(a, wbf_ref[cols, :], _NT,
                                                 preferred_element_type=F32).astype(z_ref.dtype)

        @pl.when(j == last)
        def _():
            z_ref[:, :LANE] = lax.dot_general(a_ref[...], wbf_ref[:LANE, :], _NT,
                                              preferred_element_type=F32).astype(z_ref.dtype)

    @pl.when(i == 0)
    def _():
        emit(as_ref, zs_ref)

    @pl.when(i > 0)
    def _():
        emit(ap_ref, zp_ref)


def _in_proj(a_p, a_s, wt, *, tm):
    mp, k = a_p.shape
    ms = a_s.shape[0]
    npt = mp // tm
    nj = Z_GLR // IN_TN + 1
    aux_per_tile = IN_TN // GATE_RANK

    def row_p(j, i):
        return jnp.maximum(i - 1, 0)

    def aux_idx(j, i):
        return (jnp.where(j == nj - 1, GLR_SRC // GATE_RANK, (j + 1) * aux_per_tile), 0)

    return pl.pallas_call(
        _in_proj_kernel,
        grid=(nj, npt + 1),
        in_specs=[
            pl.BlockSpec((tm, k), lambda j, i: (row_p(j, i), 0)),
            pl.BlockSpec((ms, k), lambda j, i: (0, 0)),
            pl.BlockSpec((IN_TN, k), lambda j, i: (j, 0)),
            pl.BlockSpec((GATE_RANK, k), aux_idx),
        ],
        out_specs=[
            pl.BlockSpec((tm, IN_TN), lambda j, i: (row_p(j, i), j)),
            pl.BlockSpec((ms, IN_TN), lambda j, i: (0, j)),
        ],
        out_shape=[jax.ShapeDtypeStruct((mp, Z_WIDTH), BF16), jax.ShapeDtypeStruct((ms, Z_WIDTH), F32)],
        scratch_shapes=[pltpu.VMEM((IN_TN, k), BF16)],
        compiler_params=_cparams(2),
        name="in_proj",
    )(a_p, a_s, wt, wt)


GLA_GROUP = 8
GLA_STEP_GROUPS = 2


def _gla_kernel(q_ref, k_ref, v_ref, r_ref, glr_ref, wg_ref, bg_ref, gn_ref,
                sq_ref, sk_ref, sv_ref, sr_ref, sglr_ref, swg_ref, sbg_ref, sgn_ref, s_ref,
                og_ref, s_out_ref, sog_ref, ss_out_ref, st_ref):
    c = GLA_CHUNK
    rows = GLA_GROUP * c
    gi = pl.program_id(2)
    step = (pl.program_id(0) * pl.num_programs(1) + pl.program_id(1)) * pl.num_programs(2) + gi
    n_sample = s_ref.shape[0]

    sample_units = []
    for si in range(n_sample):
        sample_units += _gla_sample_units(step * n_sample + si, si, sq_ref, sk_ref, sv_ref, sr_ref, sglr_ref,
                                          swg_ref, sbg_ref, sgn_ref, s_ref, sog_ref, ss_out_ref)

    @pl.when(gi == 0)
    def _():
        st_ref[...] = jnp.zeros_like(st_ref)

    row = lax.broadcasted_iota(jnp.int32, (c, c), 0)
    col = lax.broadcasted_iota(jnp.int32, (c, c), 1)
    causal = row >= col
    tri = causal.astype(BF16)

    def group(r0, sample_units):
        rs = pl.ds(r0, rows)
        zg = jnp.dot(glr_ref[rs, :], wg_ref[...], preferred_element_type=F32) + bg_ref[...]
        v = v_ref[rs, :]
        vals = [dict() for _ in range(GLA_GROUP)]
        state = {"st": st_ref[...]}

        def s_gate(ci, d):
            gl = _log_sigmoid(zg[ci * c:(ci + 1) * c]) * (1.0 / GATE_TEMP)
            d["gl_hi"] = gl.astype(BF16)
            d["gl_lo"] = (gl - d["gl_hi"].astype(F32)).astype(BF16)

        def s_cumsum(ci, d):
            d["b"] = (jnp.dot(tri, d.pop("gl_hi"), preferred_element_type=F32)
                      + jnp.dot(tri, d.pop("gl_lo"), preferred_element_type=F32))

        def s_scale(ci, d):
            cr = pl.ds(r0 + ci * c, c)
            b = d.pop("b")
            bl = b[c - 1:c, :]
            q = q_ref[cr, :].astype(F32) * (GLA_DK ** -0.5)
            k = k_ref[cr, :].astype(F32)
            d["qb"] = (q * jnp.exp(b)).astype(BF16)
            d["kb"] = (k * jnp.exp(-b)).astype(BF16)
            d["kd"] = (k * jnp.exp(bl - b)).astype(BF16)
            d["decay"] = jnp.exp(bl)

        def s_att(ci, d):
            sl = slice(ci * c, (ci + 1) * c)
            d["att"] = lax.dot_general(d["qb"], d.pop("kb"), _NT, preferred_element_type=F32)
            d["dst"] = lax.dot_general(v[sl], d.pop("kd"), _TN, preferred_element_type=F32)

        def s_mask(ci, d):
            d["att"] = jnp.where(causal, d["att"], 0.0).astype(BF16)

        def s_intra(ci, d):
            sl = slice(ci * c, (ci + 1) * c)
            d["o"] = jnp.dot(d.pop("att"), v[sl], preferred_element_type=F32)

        def s_chain(ci, d):
            st = state["st"]
            d["st_in"] = st.astype(BF16)
            state["st"] = st * d.pop("decay") + d.pop("dst")

        def s_inter(ci, d):
            d["o"] = d["o"] + lax.dot_general(d.pop("qb"), d.pop("st_in"), _NT, preferred_element_type=F32)

        def s_finish(ci, d):
            cr = pl.ds(r0 + ci * c, c)
            on = _rms(d.pop("o"), gn_ref[...])
            og_ref[cr, :] = (on * _silu(r_ref[cr, :].astype(F32))).astype(og_ref.dtype)

        stages = (s_gate, s_cumsum, s_scale, s_att, s_mask, s_intra, s_chain, s_inter, s_finish)
        n_ticks = GLA_GROUP + len(stages) - 1
        pending = list(sample_units)
        for tick in range(n_ticks):
            for si, stage in enumerate(stages):
                ci = tick - si
                if 0 <= ci < GLA_GROUP:
                    stage(ci, vals[ci])
            if pending and (tick + 1) * len(sample_units) >= (len(sample_units) - len(pending) + 1) * n_ticks:
                pending.pop(0)()
        for unit in pending:
            unit()
        st_ref[...] = state["st"]

    n_groups = q_ref.shape[0] // rows
    per_group = len(sample_units) // n_groups
    for gl in range(n_groups):
        units = sample_units[gl * per_group:] if gl == n_groups - 1 else sample_units[gl * per_group:(gl + 1) * per_group]
        group(gl * rows, units)

    @pl.when(gi == pl.num_programs(2) - 1)
    def _():
        s_out_ref[0, 0] = st_ref[...].T


def _gla(z_p, z_s, wg, bg, gn, state, nb, t):
    m = z_p.shape[0]
    n = z_s.shape[0]
    tr = GLA_STEP_GROUPS * GLA_GROUP * GLA_CHUNK
    n_groups = t // tr
    n_steps = nb * GLA_HEADS * n_groups
    n_sample = n // n_steps
    assert n_sample * n_steps == n

    def s_idx(b, h, g):
        return ((b * GLA_HEADS + h) * n_groups + g, 0, 0, 0)

    def rows_idx(col0):
        return lambda b, h, g: (b * n_groups + g, col0 + h)

    const2 = lambda b, h, g: (0, 0)
    og_p, s_new_p, og_s, s_new_s = pl.pallas_call(
        _gla_kernel,
        grid=(nb, GLA_HEADS, n_groups),
        in_specs=[
            pl.BlockSpec((tr, GLA_DK), rows_idx(Z_Q // GLA_DK)),
            pl.BlockSpec((tr, GLA_DK), rows_idx(Z_K // GLA_DK)),
            pl.BlockSpec((tr, GLA_DV), rows_idx(Z_V // GLA_DV)),
            pl.BlockSpec((tr, GLA_DV), rows_idx(Z_R // GLA_DV)),
            pl.BlockSpec((tr, LANE), lambda b, h, g: (b * n_groups + g, Z_GLR // LANE)),
            pl.BlockSpec((LANE, GLA_DK), lambda b, h, g: (0, h)),
            pl.BlockSpec((1, GLA_DK), lambda b, h, g: (0, h)),
            pl.BlockSpec((1, GLA_DV), lambda b, h, g: (0, h)),
            pl.BlockSpec((n, GLA_KW), lambda b, h, g: (0, Z_Q // GLA_KW)),
            pl.BlockSpec((n, GLA_KW), lambda b, h, g: (0, Z_K // GLA_KW)),
            pl.BlockSpec((n, GLA_VW), lambda b, h, g: (0, Z_V // GLA_VW)),
            pl.BlockSpec((n, GLA_VW), lambda b, h, g: (0, Z_R // GLA_VW)),
            pl.BlockSpec((n, LANE), lambda b, h, g: (0, Z_GLR // LANE)),
            pl.BlockSpec((LANE, GLA_KW), const2),
            pl.BlockSpec((1, GLA_KW), const2),
            pl.BlockSpec((1, GLA_VW), const2),
            pl.BlockSpec((n_sample, GLA_HEADS, GLA_DK, GLA_DV), s_idx),
        ],
        out_specs=[
            pl.BlockSpec((tr, GLA_DV), rows_idx(0)),
            pl.BlockSpec((1, 1, GLA_DK, GLA_DV), lambda b, h, g: (b, h, 0, 0)),
            pl.BlockSpec((n, GLA_VW), const2),
            pl.BlockSpec((n_sample, GLA_HEADS, GLA_DK, GLA_DV), s_idx),
        ],
        out_shape=[
            jax.ShapeDtypeStruct((m, GLA_VW), BF16),
            jax.ShapeDtypeStruct((nb, GLA_HEADS, GLA_DK, GLA_DV), F32),
            jax.ShapeDtypeStruct((n, GLA_VW), F32),
            jax.ShapeDtypeStruct((n, GLA_HEADS, GLA_DK, GLA_DV), F32),
        ],
        scratch_shapes=[pltpu.VMEM((GLA_DV, GLA_DK), F32)],
        compiler_params=_cparams(3),
        name="gla",
    )(z_p, z_p, z_p, z_p, z_p, wg, bg, gn, z_s, z_s, z_s, z_s, z_s, wg, bg, gn, state)
    return og_p, s_new_p, og_s, s_new_s


def _gla_sample_units(n, si, q_ref, k_ref, v_ref, r_ref, glr_ref, wg_ref, bg_ref, gn_ref, s_ref,
                      og_ref, s_out_ref):
    row = pl.ds(n, 1)

    def rows8(ref):
        return jnp.broadcast_to(ref[row, :], (SUBLANES, ref.shape[1]))

    zg = jnp.dot(rows8(glr_ref).astype(BF16), wg_ref[...], preferred_element_type=F32) + bg_ref[...]
    g = _log_sigmoid(zg) * (1.0 / GATE_TEMP)
    eg = jnp.exp(g)
    eng = jnp.exp(-g)
    eg1 = eg.astype(BF16)
    rem = eg - eg1.astype(F32)
    eg2 = rem.astype(BF16)
    eg3 = (rem - eg2.astype(F32)).astype(BF16)
    first_row = lax.broadcasted_iota(jnp.int32, (SUBLANES, LANE), 0) == 0
    ones_row = jnp.where(first_row, 1.0, 0.0).astype(BF16)
    row0 = lax.broadcasted_iota(jnp.int32, (SUBLANES, GLA_DV), 0) == 0
    q_all = rows8(q_ref) * (GLA_DK ** -0.5)
    k_all = rows8(k_ref)
    v_all = rows8(v_ref)
    r_all = r_ref[row, :]
    gn_all = gn_ref[...]

    def head_unit(h):
        ks = slice(h * GLA_DK, (h + 1) * GLA_DK)
        vs = slice(h * GLA_DV, (h + 1) * GLA_DV)
        q, k, v = q_all[:, ks], k_all[:, ks], v_all[:, vs]
        s_old = s_ref[si, h]
        decay = (lax.dot_general(eg1[:, ks], ones_row, _TN, preferred_element_type=F32)
                 + lax.dot_general(eg2[:, ks], ones_row, _TN, preferred_element_type=F32)
                 + lax.dot_general(eg3[:, ks], ones_row, _TN, preferred_element_type=F32))
        v0 = jnp.where(row0, v, 0.0)
        ds = lax.dot_general(k.astype(BF16), v0.astype(BF16), _TN, preferred_element_type=F32)
        s_out_ref[si, h] = jnp.tile(decay, (1, GLA_DV // LANE)) * s_old + ds
        qb = q * eg[:, ks]
        kb = k * eng[:, ks]
        att = jnp.sum(qb * kb, axis=-1, keepdims=True)
        o = jnp.dot(qb.astype(BF16), s_old.astype(BF16), preferred_element_type=F32) + att * v
        o = o[0:1, :]
        ms = jnp.mean(o * o, axis=-1, keepdims=True)
        on = o * lax.rsqrt(ms + EPS) * gn_all[:, vs]
        og_ref[row, vs] = on * _silu(r_all[:, vs])

    return [functools.partial(head_unit, h) for h in range(GLA_HEADS)]


def _pool_group_out(mix, g, wp_ref, ps_ref):
    cs = slice(g * POOL_GW, (g + 1) * POOL_GW)
    pm = jnp.dot(mix.astype(BF16), wp_ref[g].astype(BF16), preferred_element_type=F32)
    return pm * ps_ref[:, cs]


POOL_LEVELS = tuple(w.bit_length() - 1 for w in POOL_WINDOWS)
assert all(1 << lv == w for lv, w in zip(POOL_LEVELS, POOL_WINDOWS)), "windows must be powers of two"
POOL_HIST = SUBLANES * max(POOL_LEVELS)
assert POOL_HIST > POOL_BUF + 1


def _pool_prompt_kernel(u_ref, wp_ref, ps_ref, o_ref, ext_ref, lvl_ref):
    i = pl.program_id(1)
    tt = u_ref.shape[0]
    hist = POOL_HIST
    n = hist + tt

    @pl.when(i == 0)
    def _():
        ext_ref[0:hist, :] = jnp.zeros((hist, POOL_W), F32)

    @pl.when(i > 0)
    def _():
        ext_ref[0:hist, :] = ext_ref[tt:n, :]

    ext_ref[hist:n, :] = u_ref[...].astype(F32)
    pos = i * tt + lax.broadcasted_iota(jnp.int32, (tt, 1), 0)
    for g, (w, levels) in enumerate(zip(POOL_WINDOWS, POOL_LEVELS)):
        cs = slice(g * POOL_GW, (g + 1) * POOL_GW)
        cur = None
        for lev in range(levels):
            sh = 1 << lev
            start = SUBLANES * (lev + 1)
            if lev == 0:
                cur = ext_ref[start:n, cs] + ext_ref[start - sh:n - sh, cs]
            else:
                prev = lvl_ref.at[(lev - 1) % 2]
                cur = prev[start:n, :] + prev[start - sh:n - sh, :]
            if lev + 1 < levels:
                lvl_ref[lev % 2, start:n, :] = cur
        s = cur[hist - SUBLANES * levels:]
        u = ext_ref[hist:n, cs]
        cnt = jnp.minimum(pos + 1, w).astype(F32)
        mix = s / cnt - u
        o_ref[:, cs] = _pool_group_out(mix, g, wp_ref, ps_ref).astype(o_ref.dtype)


def _pool_prompt(z, w_pool, pool_scale, nb, t, tt):
    m = z.shape[0]
    nt = t // tt
    return pl.pallas_call(
        _pool_prompt_kernel,
        grid=(nb, nt),
        in_specs=[
            pl.BlockSpec((tt, POOL_W), lambda b, i: (b * nt + i, Z_U // POOL_W)),
            pl.BlockSpec((4, POOL_GW, POOL_GW), lambda b, i: (0, 0, 0)),
            pl.BlockSpec((1, POOL_W), lambda b, i: (0, 0)),
        ],
        out_specs=pl.BlockSpec((tt, POOL_W), lambda b, i: (b * nt + i, 0)),
        out_shape=jax.ShapeDtypeStruct((m, POOL_W), BF16),
        scratch_shapes=[pltpu.VMEM((tt + POOL_HIST, POOL_W), F32), pltpu.VMEM((2, tt + POOL_HIST, POOL_GW), F32)],
        compiler_params=_cparams(2),
        name="pool_prompt",
    )(z, w_pool, pool_scale)


def _pool_sample_kernel(u_ref, buf_ref, wp_ref, ps_ref, o_ref, buf_out_ref):
    buf_out_ref[0:POOL_BUF - 1] = buf_ref[1:POOL_BUF]
    buf_out_ref[POOL_BUF - 1] = u_ref[...].astype(F32)
    for g, w in enumerate(POOL_WINDOWS):
        cs = slice(g * POOL_GW, (g + 1) * POOL_GW)
        u = u_ref[:, cs].astype(F32)
        s = u
        for d in range(1, w):
            s = s + buf_ref[POOL_BUF - d, :, cs]
        mix = s / float(w) - u
        o_ref[:, cs] = _pool_group_out(mix, g, wp_ref, ps_ref).astype(o_ref.dtype)


def _pool_sample(z, buf_t, w_pool, pool_scale):
    n = z.shape[0]
    return pl.pallas_call(
        _pool_sample_kernel,
        grid=(1,),
        in_specs=[
            pl.BlockSpec((n, POOL_W), lambda i: (0, Z_U // POOL_W)),
            pl.BlockSpec((POOL_BUF, n, POOL_W), lambda i: (0, 0, 0)),
            pl.BlockSpec((4, POOL_GW, POOL_GW), lambda i: (0, 0, 0)),
            pl.BlockSpec((1, POOL_W), lambda i: (0, 0)),
        ],
        out_specs=[pl.BlockSpec((n, POOL_W), lambda i: (0, 0)),
                   pl.BlockSpec((POOL_BUF, n, POOL_W), lambda i: (0, 0, 0))],
        out_shape=[jax.ShapeDtypeStruct((n, POOL_W), BF16), jax.ShapeDtypeStruct((POOL_BUF, n, POOL_W), F32)],
        compiler_params=_cparams(1),
        name="pool_sample",
    )(z, buf_t, w_pool, pool_scale)


MERGE_SUB = 256


def _merge_kernel(ogp_ref, ogs_ref, pmp_ref, pms_ref, wa_ref, wb_ref, gap_ref, gas_ref, gbp_ref, gbs_ref,
                  op_ref, os_ref, wabf_ref, wbbf_ref):
    i = pl.program_id(1)

    @pl.when(i == 0)
    def _():
        wabf_ref[...] = wa_ref[...].astype(BF16)
        wbbf_ref[...] = wb_ref[...].astype(BF16)

    def emit(og_ref, pm_ref, ga_ref, gb_ref, o_ref):
        og = og_ref[...].astype(BF16)
        pm = pm_ref[...]
        for part in range(o_ref.shape[1] // MERGE_SUB):
            cols = slice(part * MERGE_SUB, (part + 1) * MERGE_SUB)
            ya = jnp.dot(og, wabf_ref[:, cols], preferred_element_type=F32)
            yb = jnp.dot(pm, wbbf_ref[:, cols], preferred_element_type=F32)
            ga = ga_ref[:, cols].astype(F32)
            gb = gb_ref[:, cols].astype(F32)
            o_ref[:, cols] = (_sigmoid(ga) * ya + _sigmoid(gb) * yb).astype(o_ref.dtype)

    @pl.when(i == 0)
    def _():
        emit(ogs_ref, pms_ref, gas_ref, gbs_ref, os_ref)

    @pl.when(i > 0)
    def _():
        emit(ogp_ref, pmp_ref, gap_ref, gbp_ref, op_ref)


def _merge(og_p, og_s, pm_p, pm_s, wa, wb, z_p, z_s, *, tm, tn):
    mp = og_p.shape[0]
    ms = og_s.shape[0]
    return pl.pallas_call(
        _merge_kernel,
        grid=(D_MODEL // tn, mp // tm + 1),
        in_specs=[
            *_row_tiled_specs(tm, ms, GLA_VW),
            *_row_tiled_specs(tm, ms, POOL_W),
            pl.BlockSpec((GLA_VW, tn), lambda j, i: (0, j), pipeline_mode=pl.Buffered(1)),
            pl.BlockSpec((POOL_W, tn), lambda j, i: (0, j), pipeline_mode=pl.Buffered(1)),
            *_col_tiled_specs(tm, ms, tn, Z_GA // tn),
            *_col_tiled_specs(tm, ms, tn, Z_GB // tn),
        ],
        out_specs=_col_tiled_specs(tm, ms, tn),
        out_shape=[jax.ShapeDtypeStruct((mp, D_MODEL), BF16), jax.ShapeDtypeStruct((ms, D_MODEL), BF16)],
        scratch_shapes=[pltpu.VMEM((GLA_VW, tn), BF16), pltpu.VMEM((POOL_W, tn), BF16)],
        compiler_params=_cparams(2),
        name="merge",
    )(og_p, og_s, pm_p, pm_s, wa, wb, z_p, z_s, z_p, z_s)


UP_TN = 512
UP_SUB = 256
UP_ROWS = 1024
UP_PARTS = UP_TN // LANE


def _up_kernel(*refs, tiles_per_seq):
    cp_ref, cs_ref, wg_ref = refs[:3]
    wu_refs = refs[3:3 + UP_PARTS]
    (cw_ref, cb_ref, h0_ref, h1_ref, actp_ref, acts_ref, tailp_ref, gates_ref,
     wgbf_ref, wubf_ref, carry_ref) = refs[3 + UP_PARTS:]
    i = pl.program_id(1)

    @pl.when(i == 0)
    def _():
        wgbf_ref[...] = wg_ref[...].astype(BF16)
        for part, wu_ref in enumerate(wu_refs):
            wubf_ref[:, part * LANE:(part + 1) * LANE] = wu_ref[...].astype(BF16)

    def conv_act(gate, up, g1, g2, cols):
        cw = cw_ref[:, cols]
        gconv = cb_ref[:, cols] + cw[0:1] * g2 + cw[1:2] * g1 + cw[2:3] * gate
        return _silu(gconv) * up

    @pl.when(i > 0)
    def _():
        @pl.when((i - 1) % tiles_per_seq == 0)
        def _():
            carry_ref[...] = jnp.zeros_like(carry_ref)

        tm = cp_ref.shape[0]
        rows = min(tm, UP_ROWS)
        sub = lax.broadcasted_iota(jnp.int32, (SUBLANES, UP_SUB), 0)
        n_cols = UP_TN // UP_SUB
        prev = [carry_ref[:, h * UP_SUB:(h + 1) * UP_SUB] for h in range(n_cols)]
        for rt in range(tm // rows):
            rsl = slice(rt * rows, (rt + 1) * rows)
            c = cp_ref[rsl, :]
            for h in range(n_cols):
                cols = slice(h * UP_SUB, (h + 1) * UP_SUB)
                gate = jnp.dot(c, wgbf_ref[:, cols], preferred_element_type=F32)
                up = jnp.dot(c, wubf_ref[:, cols], preferred_element_type=F32)
                r1 = pltpu.roll(gate, 1, 0)
                r2 = pltpu.roll(gate, 2, 0)
                head1 = jnp.where(sub < 1, pltpu.roll(prev[h], 1, 0), r1[0:SUBLANES])
                head2 = jnp.where(sub < 2, pltpu.roll(prev[h], 2, 0), r2[0:SUBLANES])
                g1 = jnp.concatenate([head1, r1[SUBLANES:]], axis=0)
                g2 = jnp.concatenate([head2, r2[SUBLANES:]], axis=0)
                prev[h] = gate[rows - SUBLANES:rows]
                actp_ref[rsl, cols] = conv_act(gate, up, g1, g2, cols).astype(actp_ref.dtype)
        for h in range(n_cols):
            cols = slice(h * UP_SUB, (h + 1) * UP_SUB)
            carry_ref[:, cols] = prev[h]
            tailp_ref[0, :, cols] = prev[h][SUBLANES - (CONV_W - 1):SUBLANES]

    @pl.when(i == 0)
    def _():
        c = cs_ref[...]
        gate = jnp.dot(c, wgbf_ref[...], preferred_element_type=F32)
        up = jnp.dot(c, wubf_ref[...], preferred_element_type=F32)
        gates_ref[...] = gate
        acts_ref[...] = conv_act(gate, up, h1_ref[...], h0_ref[...], slice(None)).astype(acts_ref.dtype)


def _up_proj(c_p, c_s, w_up, conv_w, conv_b, hist0, hist1, *, tm, seq_len):
    mp, k = c_p.shape
    ms = c_s.shape[0]
    npt = mp // tm
    tiles_per_seq = seq_len // tm
    nseq = mp // seq_len
    tn = UP_TN
    nj = pl.cdiv(D_FF, tn)
    up_block0 = D_FF // LANE
    last_block = w_up.shape[1] // LANE - 1

    def row_p(j, i):
        return jnp.maximum(i - 1, 0)

    def wu_spec(part):
        return pl.BlockSpec(
            (k, LANE), lambda j, i: (0, jnp.minimum(up_block0 + j * UP_PARTS + part, last_block)))

    in_specs = [
        pl.BlockSpec((tm, k), lambda j, i: (row_p(j, i), 0)),
        pl.BlockSpec((ms, k), lambda j, i: (0, 0)),
        pl.BlockSpec((k, tn), lambda j, i: (0, j)),
        *[wu_spec(part) for part in range(UP_PARTS)],
        pl.BlockSpec((CONV_W, tn), lambda j, i: (0, j)),
        pl.BlockSpec((1, tn), lambda j, i: (0, j)),
        pl.BlockSpec((ms, tn), lambda j, i: (0, j)),
        pl.BlockSpec((ms, tn), lambda j, i: (0, j)),
    ]
    out_specs = [
        pl.BlockSpec((tm, tn), lambda j, i: (row_p(j, i), j)),
        pl.BlockSpec((ms, tn), lambda j, i: (0, j)),
        pl.BlockSpec((1, CONV_W - 1, tn), lambda j, i: (row_p(j, i) // tiles_per_seq, 0, j)),
        pl.BlockSpec((ms, tn), lambda j, i: (0, j)),
    ]
    out_shape = [
        jax.ShapeDtypeStruct((mp, D_FF), BF16),
        jax.ShapeDtypeStruct((ms, D_FF), BF16),
        jax.ShapeDtypeStruct((nseq, CONV_W - 1, D_FF), F32),
        jax.ShapeDtypeStruct((ms, D_FF), F32),
    ]
    return pl.pallas_call(
        functools.partial(_up_kernel, tiles_per_seq=tiles_per_seq),
        grid=(nj, npt + 1),
        in_specs=in_specs,
        out_specs=out_specs,
        out_shape=out_shape,
        scratch_shapes=[pltpu.VMEM((k, tn), BF16), pltpu.VMEM((k, tn), BF16), pltpu.VMEM((SUBLANES, tn), F32)],
        compiler_params=_cparams(2),
        name="up_proj",
    )(c_p, c_s, w_up, *([w_up] * UP_PARTS), conv_w, conv_b, hist0, hist1)


def _two_group_specs(tm, ms, widths):
    specs = []
    for w in widths:
        specs.append(pl.BlockSpec((tm, w), lambda i: (jnp.maximum(i - 1, 0), 0)))
        specs.append(pl.BlockSpec((ms, w), lambda i: (0, 0)))
    return specs


def _resident(shape):
    return pl.BlockSpec(shape, lambda i: (0,) * len(shape), pipeline_mode=pl.Buffered(1))


def _out_norm_kernel(mp_ref, ms_ref, xp_ref, xs_ref, w_ref, g_ref, hp_ref, hs_ref, cp_ref, cs_ref, wbf_ref):
    i = pl.program_id(0)

    @pl.when(i == 0)
    def _():
        wbf_ref[...] = w_ref[...].astype(BF16)

    def emit(m_ref, x_ref, h_ref, c_ref):
        h = x_ref[...] + jnp.dot(m_ref[...], wbf_ref[...], preferred_element_type=F32)
        h_ref[...] = h
        c_ref[...] = _rms(h, g_ref[...]).astype(c_ref.dtype)

    @pl.when(i == 0)
    def _():
        emit(ms_ref, xs_ref, hs_ref, cs_ref)

    @pl.when(i > 0)
    def _():
        emit(mp_ref, xp_ref, hp_ref, cp_ref)


def _out_norm(m_p, m_s, x_p, x_s, w_out, g, *, tm):
    mp, d = x_p.shape
    ms = x_s.shape[0]
    return pl.pallas_call(
        _out_norm_kernel,
        grid=(mp // tm + 1,),
        in_specs=[*_two_group_specs(tm, ms, (d, d)), _resident((d, d)), _resident((1, d))],
        out_specs=_two_group_specs(tm, ms, (d, d)),
        out_shape=[
            jax.ShapeDtypeStruct((mp, d), F32), jax.ShapeDtypeStruct((ms, d), F32),
            jax.ShapeDtypeStruct((mp, d), BF16), jax.ShapeDtypeStruct((ms, d), BF16),
        ],
        scratch_shapes=[pltpu.VMEM((d, d), BF16)],
        compiler_params=_cparams(1),
        name="out_norm",
    )(m_p, m_s, x_p, x_s, w_out, g.reshape(1, d))


def _ple_kernel(hp_ref, hs_ref, pp_ref, ps_ref, wpg_ref, wple_ref, gple_ref, gfin_ref, yp_ref, ys_ref,
                wpgbf_ref, wplebf_ref, *, final_norm):
    i = pl.program_id(0)

    @pl.when(i == 0)
    def _():
        wpgbf_ref[...] = wpg_ref[...].astype(BF16)
        wplebf_ref[...] = wple_ref[...].astype(BF16)

    def emit(h_ref, p_ref, y_ref):
        h2 = h_ref[...]
        cn = _rms(h2, gple_ref[...]).astype(BF16)
        pg = _sigmoid(jnp.dot(cn, wpgbf_ref[...], preferred_element_type=F32))
        e = jnp.dot(p_ref[...].astype(BF16), wplebf_ref[...], preferred_element_type=F32)
        h3 = h2 + pg * e
        y_ref[...] = _rms(h3, gfin_ref[...]) if final_norm else h3

    @pl.when(i == 0)
    def _():
        emit(hs_ref, ps_ref, ys_ref)

    @pl.when(i > 0)
    def _():
        emit(hp_ref, pp_ref, yp_ref)


def _ple(h_p, h_s, p_p, p_s, wpg, wple, g_ple, g_final, *, tm, final_norm):
    mp, d = h_p.shape
    ms = h_s.shape[0]
    pd = p_p.shape[1]
    return pl.pallas_call(
        functools.partial(_ple_kernel, final_norm=final_norm),
        grid=(mp // tm + 1,),
        in_specs=[*_two_group_specs(tm, ms, (d, pd)), _resident((d, d)), _resident((pd, d)),
                  _resident((1, d)), _resident((1, d))],
        out_specs=_two_group_specs(tm, ms, (d,)),
        out_shape=[jax.ShapeDtypeStruct((mp, d), F32), jax.ShapeDtypeStruct((ms, d), F32)],
        scratch_shapes=[pltpu.VMEM((d, d), BF16), pltpu.VMEM((pd, d), BF16)],
        compiler_params=_cparams(1),
        name="ple",
    )(h_p, h_s, p_p, p_s, wpg, wple, g_ple.reshape(1, d), g_final.reshape(1, d))


def _pool_mixers(z_p, z_s, lw, s_pool, nb, t):
    pm_s, pool_new_t = _pool_sample(z_s, jnp.transpose(s_pool, (1, 0, 2)), lw["w_pool"], lw["pool_scale"])
    pool_new_s = jnp.transpose(pool_new_t, (1, 0, 2))
    pm_p = _pool_prompt(z_p, lw["w_pool"], lw["pool_scale"], nb, t, 1024)
    pool_new_p = z_p.reshape(nb, t, Z_WIDTH)[:, t - POOL_BUF:, Z_U:Z_U + POOL_W].astype(F32)
    return pm_p, pm_s, pool_new_p, pool_new_s


def kernel(x_prompt, x_sample, state_gla, state_pool, state_conv, p_prompt, p_sample, norm_mix, w_in, w_gate_up, b_gate, gla_norm, w_branch_a, w_pool, pool_scale, w_branch_b, w_out, norm_ffn, w_up, conv_w, conv_b, w_down, norm_ple, w_ple_gate, w_ple, norm_final):
    depth = w_in.shape[0]
    nb, t, d = x_prompt.shape
    ns = x_sample.shape[0]
    hp = x_prompt.reshape(nb * t, d)
    hs = x_sample.reshape(ns, d)
    outs = {k: [] for k in ("gp", "pp", "cp", "gs", "ps", "cs")}
    for li in range(depth):
        lw = {
            "wg": jnp.pad(w_gate_up[li], ((0, LANE - GATE_RANK), (0, 0))).astype(BF16),
            "bg": b_gate[li].reshape(1, GLA_KW),
            "gn": gla_norm[li].reshape(1, GLA_VW),
            "w_pool": w_pool[li],
            "pool_scale": pool_scale[li].reshape(1, POOL_W),
        }
        a_p = _rmsnorm(hp, norm_mix[li], BF16, 1024)
        a_s = _rmsnorm(hs, norm_mix[li], BF16, ns)
        z_p, z_s = _in_proj(a_p, a_s, w_in[li].T, tm=2048)
        og_p, g1, og_s, g2 = _gla(z_p, z_s, lw["wg"], lw["bg"], lw["gn"], state_gla[li], nb, t)
        pm_p, pm_s, p1, p2 = _pool_mixers(z_p, z_s, lw, state_pool[li], nb, t)
        mg_p, mg_s = _merge(og_p, og_s, pm_p, pm_s, w_branch_a[li], w_branch_b[li], z_p, z_s, tm=1024, tn=1024)
        h1_p, h1_s, c_p, c_s = _out_norm(mg_p, mg_s, hp, hs, w_out[li], norm_ffn[li], tm=512)
        s_conv = state_conv[li]
        act_p, act_s, c1, gate_s = _up_proj(c_p, c_s, w_up[li], conv_w[li], conv_b[li].reshape(1, D_FF),
                                            s_conv[:, 0], s_conv[:, 1], tm=2048, seq_len=t)
        c2 = jnp.stack([s_conv[:, 1], gate_s], axis=1)
        h2_p, h2_s = _down_proj(act_p, act_s, w_down[li], h1_p, h1_s, tm=512, tn=1024)
        hp, hs = _ple(h2_p, h2_s, p_prompt[li].reshape(nb * t, PLE_DIM), p_sample[li].reshape(ns, PLE_DIM),
                      w_ple_gate[li], w_ple[li], norm_ple[li], norm_final, tm=512,
                      final_norm=li == depth - 1)
        for key, val in zip(("gp", "pp", "cp", "gs", "ps", "cs"), (g1, p1, c1, g2, p2, c2)):
            outs[key].append(val)
    st = {k: jnp.stack(v, axis=0) for k, v in outs.items()}
    return (hp.reshape(nb, t, d), hs.reshape(ns, 1, d),
            st["gp"], st["pp"], st["cp"], st["gs"], st["ps"], st["cs"])
```

```python
import functools

import jax
import jax.numpy as jnp
from jax import lax
from jax.experimental import pallas as pl
from jax.experimental.pallas import tpu as pltpu

F32 = jnp.float32
BF16 = jnp.bfloat16

D_MODEL = 2048
GLA_HEADS = 4
GLA_DK = 256
GLA_DV = 512
GLA_KW = GLA_HEADS * GLA_DK
GLA_VW = GLA_HEADS * GLA_DV
GATE_RANK = 16
GATE_TEMP = 16.0
GLA_CHUNK = 64
POOL_WINDOWS = (2, 4, 8, 16)
POOL_GW = 256
POOL_W = 1024
POOL_BUF = 15
D_FF = 5504
CONV_W = 3
PLE_DIM = 256
EPS = 1e-6

LANE = 128
SUBLANES = 8
Z_Q, Z_K, Z_V, Z_R, Z_U, Z_GA, Z_GB, Z_GLR, Z_WIDTH = 0, 1024, 2048, 4096, 6144, 7168, 9216, 11264, 11392
GLR_SRC = 6144
IN_TN = 1024
IN_SUB = 512

VMEM_LIMIT = 58 * 1024 * 1024
TM_NORM = 1024
TM_IN = 2048
TM_MERGE, TN_MERGE = 1024, 1024
TM_ROW = 512
TM_UP = 2048
TM_DOWN, TN_DOWN = 512, 1024
TT_POOL = 1024


def _cparams(n_axes):
    return pltpu.CompilerParams(dimension_semantics=("arbitrary",) * n_axes, vmem_limit_bytes=VMEM_LIMIT)


def _sigmoid(x):
    return 1.0 / (1.0 + jnp.exp(-x))


def _silu(x):
    return x * _sigmoid(x)


def _log_sigmoid(x):
    return jnp.minimum(x, 0.0) - jnp.log(1.0 + jnp.exp(-jnp.abs(x)))


def _rms(x, g):
    return x * lax.rsqrt(jnp.mean(x * x, axis=-1, keepdims=True) + EPS) * g


def _rms_kernel(x_ref, g_ref, o_ref):
    o_ref[...] = _rms(x_ref[...], g_ref[...]).astype(o_ref.dtype)


def _rmsnorm(x, g, out_dtype, tm):
    m, d = x.shape
    return pl.pallas_call(
        _rms_kernel,
        grid=(m // tm,),
        in_specs=[pl.BlockSpec((tm, d), lambda i: (i, 0)), pl.BlockSpec((1, d), lambda i: (0, 0))],
        out_specs=pl.BlockSpec((tm, d), lambda i: (i, 0)),
        out_shape=jax.ShapeDtypeStruct((m, d), out_dtype),
        compiler_params=_cparams(1),
        name="rmsnorm",
    )(x, g.reshape(1, d))


def _col_tiled_specs(tm, ms, width, col0=0):
    return [
        pl.BlockSpec((tm, width), lambda j, i: (jnp.maximum(i - 1, 0), col0 + j)),
        pl.BlockSpec((ms, width), lambda j, i: (0, col0 + j)),
    ]


def _row_tiled_specs(tm, ms, width):
    return [
        pl.BlockSpec((tm, width), lambda j, i: (jnp.maximum(i - 1, 0), 0)),
        pl.BlockSpec((ms, width), lambda j, i: (0, 0)),
    ]


def _down_kernel(ap_ref, as_ref, w_ref, rp_ref, rs_ref, op_ref, os_ref, wbf_ref):
    i = pl.program_id(1)

    @pl.when(i == 0)
    def _():
        wbf_ref[...] = w_ref[...].astype(BF16)
        os_ref[...] = rs_ref[...] + jnp.dot(as_ref[...], wbf_ref[...], preferred_element_type=F32)

    @pl.when(i > 0)
    def _():
        op_ref[...] = rp_ref[...] + jnp.dot(ap_ref[...], wbf_ref[...], preferred_element_type=F32)


def _down_proj(a_p, a_s, w, res_p, res_s, *, tm, tn):
    mp, k = a_p.shape
    ms = a_s.shape[0]
    n = w.shape[1]
    return pl.pallas_call(
        _down_kernel,
        grid=(n // tn, mp // tm + 1),
        in_specs=[
            *_row_tiled_specs(tm, ms, k),
            pl.BlockSpec((k, tn), lambda j, i: (0, j), pipeline_mode=pl.Buffered(1)),
            *_col_tiled_specs(tm, ms, tn),
        ],
        out_specs=_col_tiled_specs(tm, ms, tn),
        out_shape=[jax.ShapeDtypeStruct((mp, n), F32), jax.ShapeDtypeStruct((ms, n), F32)],
        scratch_shapes=[pltpu.VMEM((k, tn), BF16)],
        compiler_params=_cparams(2),
        name="down_proj",
    )(a_p, a_s, w, res_p, res_s)


_NT = (((1,), (1,)), ((), ()))
_TN = (((0,), (0,)), ((), ()))


def _in_proj_kernel(ap_ref, as_ref, w_ref, wx_ref, zp_ref, zs_ref, wbf_ref):
    j = pl.program_id(0)
    i = pl.program_id(1)
    last = pl.num_programs(0) - 1
    first_shifted = GLR_SRC // IN_TN
    keep = IN_TN - GATE_RANK

    @pl.when(i == 0)
    def _():
        @pl.when(j < first_shifted)
        def _():
            wbf_ref[...] = w_ref[...].astype(BF16)

        @pl.when(jnp.logical_and(j >= first_shifted, j < last))
        def _():
            wbf_ref[:keep, :] = w_ref[GATE_RANK:, :].astype(BF16)
            wbf_ref[keep:, :] = wx_ref[...].astype(BF16)

        @pl.when(j == last)
        def _():
            wbf_ref[:GATE_RANK, :] = wx_ref[...].astype(BF16)
            wbf_ref[GATE_RANK:LANE, :] = jnp.zeros((LANE - GATE_RANK, wbf_ref.shape[1]), BF16)

    def emit(a_ref, z_ref):
        @pl.when(j < last)
        def _():
            a = a_ref[...]
            for part in range(IN_TN // IN_SUB):
                cols = slice(part * IN_SUB, (part + 1) * IN_SUB)
                z_ref[:, cols] = lax.dot_general(a, wbf_ref[cols, :], _NT,
                                                 preferred_element_type=F32).astype(z_ref.dtype)

        @pl.when(j == last)
        def _():
            z_ref[:, :LANE] = lax.dot_general(a_ref[...], wbf_ref[:LANE, :], _NT,
                                              preferred_element_type=F32).astype(z_ref.dtype)

    @pl.when(i == 0)
    def _():
        emit(as_ref, zs_ref)

    @pl.when(i > 0)
    def _():
        emit(ap_ref, zp_ref)


def _in_proj(a_p, a_s, wt, *, tm):
    mp, k = a_p.shape
    ms = a_s.shape[0]
    npt = mp // tm
    nj = Z_GLR // IN_TN + 1
    aux_per_tile = IN_TN // GATE_RANK

    def row_p(j, i):
        return jnp.maximum(i - 1, 0)

    def aux_idx(j, i):
        return (jnp.where(j == nj - 1, GLR_SRC // GATE_RANK, (j + 1) * aux_per_tile), 0)

    return pl.pallas_call(
        _in_proj_kernel,
        grid=(nj, npt + 1),
        in_specs=[
            pl.BlockSpec((tm, k), lambda j, i: (row_p(j, i), 0)),
            pl.BlockSpec((ms, k), lambda j, i: (0, 0)),
            pl.BlockSpec((IN_TN, k), lambda j, i: (j, 0)),
            pl.BlockSpec((GATE_RANK, k), aux_idx),
        ],
        out_specs=[
            pl.BlockSpec((tm, IN_TN), lambda j, i: (row_p(j, i), j)),
            pl.BlockSpec((ms, IN_TN), lambda j, i: (0, j)),
        ],
        out_shape=[jax.ShapeDtypeStruct((mp, Z_WIDTH), BF16), jax.ShapeDtypeStruct((ms, Z_WIDTH), F32)],
        scratch_shapes=[pltpu.VMEM((IN_TN, k), BF16)],
        compiler_params=_cparams(2),
        name="in_proj",
    )(a_p, a_s, wt, wt)


GLA_GROUP = 8
GLA_STEP_GROUPS = 2


def _gla_kernel(q_ref, k_ref, v_ref, r_ref, glr_ref, wg_ref, bg_ref, gn_ref,
                sq_ref, sk_ref, sv_ref, sr_ref, sglr_ref, swg_ref, sbg_ref, sgn_ref, s_ref,
                og_ref, s_out_ref, sog_ref, ss_out_ref, st_ref):
    c = GLA_CHUNK
    rows = GLA_GROUP * c
    gi = pl.program_id(2)
    step = (pl.program_id(0) * pl.num_programs(1) + pl.program_id(1)) * pl.num_programs(2) + gi
    n_sample = s_ref.shape[0]

    sample_units = []
    for si in range(n_sample):
        sample_units += _gla_sample_units(step * n_sample + si, si, sq_ref, sk_ref, sv_ref, sr_ref, sglr_ref,
                                          swg_ref, sbg_ref, sgn_ref, s_ref, sog_ref, ss_out_ref)

    @pl.when(gi == 0)
    def _():
        st_ref[...] = jnp.zeros_like(st_ref)

    row = lax.broadcasted_iota(jnp.int32, (c, c), 0)
    col = lax.broadcasted_iota(jnp.int32, (c, c), 1)
    causal = row >= col
    tri = causal.astype(BF16)

    def group(r0, sample_units):
        rs = pl.ds(r0, rows)
        zg = jnp.dot(glr_ref[rs, :], wg_ref[...], preferred_element_type=F32) + bg_ref[...]
        v = v_ref[rs, :]
        vals = [dict() for _ in range(GLA_GROUP)]
        state = {"st": st_ref[...]}

        def s_gate(ci, d):
            gl = _log_sigmoid(zg[ci * c:(ci + 1) * c]) * (1.0 / GATE_TEMP)
            d["gl_hi"] = gl.astype(BF16)
            d["gl_lo"] = (gl - d["gl_hi"].astype(F32)).astype(BF16)

        def s_cumsum(ci, d):
            d["b"] = (jnp.dot(tri, d.pop("gl_hi"), preferred_element_type=F32)
                      + jnp.dot(tri, d.pop("gl_lo"), preferred_element_type=F32))

        def s_scale(ci, d):
            cr = pl.ds(r0 + ci * c, c)
            b = d.pop("b")
            bl = b[c - 1:c, :]
            q = q_ref[cr, :].astype(F32) * (GLA_DK ** -0.5)
            k = k_ref[cr, :].astype(F32)
            d["qb"] = (q * jnp.exp(b)).astype(BF16)
            d["kb"] = (k * jnp.exp(-b)).astype(BF16)
            d["kd"] = (k * jnp.exp(bl - b)).astype(BF16)
            d["decay"] = jnp.exp(bl)

        def s_att(ci, d):
            sl = slice(ci * c, (ci + 1) * c)
            d["att"] = lax.dot_general(d["qb"], d.pop("kb"), _NT, preferred_element_type=F32)
            d["dst"] = lax.dot_general(v[sl], d.pop("kd"), _TN, preferred_element_type=F32)

        def s_mask(ci, d):
            d["att"] = jnp.where(causal, d["att"], 0.0).astype(BF16)

        def s_intra(ci, d):
            sl = slice(ci * c, (ci + 1) * c)
            d["o"] = jnp.dot(d.pop("att"), v[sl], preferred_element_type=F32)

        def s_chain(ci, d):
            st = state["st"]
            d["st_in"] = st.astype(BF16)
            state["st"] = st * d.pop("decay") + d.pop("dst")

        def s_inter(ci, d):
            d["o"] = d["o"] + lax.dot_general(d.pop("qb"), d.pop("st_in"), _NT, preferred_element_type=F32)

        def s_finish(ci, d):
            cr = pl.ds(r0 + ci * c, c)
            on = _rms(d.pop("o"), gn_ref[...])
            og_ref[cr, :] = (on * _silu(r_ref[cr, :].astype(F32))).astype(og_ref.dtype)

        stages = (s_gate, s_cumsum, s_scale, s_att, s_mask, s_intra, s_chain, s_inter, s_finish)
        n_ticks = GLA_GROUP + len(stages) - 1
        pending = list(sample_units)
        for tick in range(n_ticks):
            for si, stage in enumerate(stages):
                ci = tick - si
                if 0 <= ci < GLA_GROUP:
                    stage(ci, vals[ci])
            if pending and (tick + 1) * len(sample_units) >= (len(sample_units) - len(pending) + 1) * n_ticks:
                pending.pop(0)()
        for unit in pending:
            unit()
        st_ref[...] = state["st"]

    n_groups = q_ref.shape[0] // rows
    per_group = len(sample_units) // n_groups
    for gl in range(n_groups):
        units = sample_units[gl * per_group:] if gl == n_groups - 1 else sample_units[gl * per_group:(gl + 1) * per_group]
        group(gl * rows, units)

    @pl.when(gi == pl.num_programs(2) - 1)
    def _():
        s_out_ref[0, 0] = st_ref[...].T


def _gla(z_p, z_s, wg, bg, gn, state, nb, t):
    m = z_p.shape[0]
    n = z_s.shape[0]
    tr = GLA_STEP_GROUPS * GLA_GROUP * GLA_CHUNK
    n_groups = t // tr
    n_steps = nb * GLA_HEADS * n_groups
    n_sample = n // n_steps
    assert n_sample * n_steps == n

    def s_idx(b, h, g):
        return ((b * GLA_HEADS + h) * n_groups + g, 0, 0, 0)

    def rows_idx(col0):
        return lambda b, h, g: (b * n_groups + g, col0 + h)

    const2 = lambda b, h, g: (0, 0)
    og_p, s_new_p, og_s, s_new_s = pl.pallas_call(
        _gla_kernel,
        grid=(nb, GLA_HEADS, n_groups),
        in_specs=[
            pl.BlockSpec((tr, GLA_DK), rows_idx(Z_Q // GLA_DK)),
            pl.BlockSpec((tr, GLA_DK), rows_idx(Z_K // GLA_DK)),
            pl.BlockSpec((tr, GLA_DV), rows_idx(Z_V // GLA_DV)),
            pl.BlockSpec((tr, GLA_DV), rows_idx(Z_R // GLA_DV)),
            pl.BlockSpec((tr, LANE), lambda b, h, g: (b * n_groups + g, Z_GLR // LANE)),
            pl.BlockSpec((LANE, GLA_DK), lambda b, h, g: (0, h)),
            pl.BlockSpec((1, GLA_DK), lambda b, h, g: (0, h)),
            pl.BlockSpec((1, GLA_DV), lambda b, h, g: (0, h)),
            pl.BlockSpec((n, GLA_KW), lambda b, h, g: (0, Z_Q // GLA_KW)),
            pl.BlockSpec((n, GLA_KW), lambda b, h, g: (0, Z_K // GLA_KW)),
            pl.BlockSpec((n, GLA_VW), lambda b, h, g: (0, Z_V // GLA_VW)),
            pl.BlockSpec((n, GLA_VW), lambda b, h, g: (0, Z_R // GLA_VW)),
            pl.BlockSpec((n, LANE), lambda b, h, g: (0, Z_GLR // LANE)),
            pl.BlockSpec((LANE, GLA_KW), const2),
            pl.BlockSpec((1, GLA_KW), const2),
            pl.BlockSpec((1, GLA_VW), const2),
            pl.BlockSpec((n_sample, GLA_HEADS, GLA_DK, GLA_DV), s_idx),
        ],
        out_specs=[
            pl.BlockSpec((tr, GLA_DV), rows_idx(0)),
            pl.BlockSpec((1, 1, GLA_DK, GLA_DV), lambda b, h, g: (b, h, 0, 0)),
            pl.BlockSpec((n, GLA_VW), const2),
            pl.BlockSpec((n_sample, GLA_HEADS, GLA_DK, GLA_DV), s_idx),
        ],
        out_shape=[
            jax.ShapeDtypeStruct((m, GLA_VW), BF16),
            jax.ShapeDtypeStruct((nb, GLA_HEADS, GLA_DK, GLA_DV), F32),
            jax.ShapeDtypeStruct((n, GLA_VW), F32),
            jax.ShapeDtypeStruct((n, GLA_HEADS, GLA_DK, GLA_DV), F32),
        ],
        scratch_shapes=[pltpu.VMEM((GLA_DV, GLA_DK), F32)],
        compiler_params=_cparams(3),
        name="gla",
    )(z_p, z_p, z_p, z_p, z_p, wg, bg, gn, z_s, z_s, z_s, z_s, z_s, wg, bg, gn, state)
    return og_p, s_new_p, og_s, s_new_s


def _gla_sample_units(n, si, q_ref, k_ref, v_ref, r_ref, glr_ref, wg_ref, bg_ref, gn_ref, s_ref,
                      og_ref, s_out_ref):
    row = pl.ds(n, 1)

    def rows8(ref):
        return jnp.broadcast_to(ref[row, :], (SUBLANES, ref.shape[1]))

    zg = jnp.dot(rows8(glr_ref).astype(BF16), wg_ref[...], preferred_element_type=F32) + bg_ref[...]
    g = _log_sigmoid(zg) * (1.0 / GATE_TEMP)
    eg = jnp.exp(g)
    eng = jnp.exp(-g)
    eg1 = eg.astype(BF16)
    rem = eg - eg1.astype(F32)
    eg2 = rem.astype(BF16)
    eg3 = (rem - eg2.astype(F32)).astype(BF16)
    first_row = lax.broadcasted_iota(jnp.int32, (SUBLANES, LANE), 0) == 0
    ones_row = jnp.where(first_row, 1.0, 0.0).astype(BF16)
    row0 = lax.broadcasted_iota(jnp.int32, (SUBLANES, GLA_DV), 0) == 0
    q_all = rows8(q_ref) * (GLA_DK ** -0.5)
    k_all = rows8(k_ref)
    v_all = rows8(v_ref)
    r_all = r_ref[row, :]
    gn_all = gn_ref[...]

    def head_unit(h):
        ks = slice(h * GLA_DK, (h + 1) * GLA_DK)
        vs = slice(h * GLA_DV, (h + 1) * GLA_DV)
        q, k, v = q_all[:, ks], k_all[:, ks], v_all[:, vs]
        s_old = s_ref[si, h]
        decay = (lax.dot_general(eg1[:, ks], ones_row, _TN, preferred_element_type=F32)
                 + lax.dot_general(eg2[:, ks], ones_row, _TN, preferred_element_type=F32)
                 + lax.dot_general(eg3[:, ks], ones_row, _TN, preferred_element_type=F32))
        v0 = jnp.where(row0, v, 0.0)
        ds = lax.dot_general(k.astype(BF16), v0.astype(BF16), _TN, preferred_element_type=F32)
        s_out_ref[si, h] = jnp.tile(decay, (1, GLA_DV // LANE)) * s_old + ds
        qb = q * eg[:, ks]
        kb = k * eng[:, ks]
        att = jnp.sum(qb * kb, axis=-1, keepdims=True)
        o = jnp.dot(qb.astype(BF16), s_old.astype(BF16), preferred_element_type=F32) + att * v
        o = o[0:1, :]
        ms = jnp.mean(o * o, axis=-1, keepdims=True)
        on = o * lax.rsqrt(ms + EPS) * gn_all[:, vs]
        og_ref[row, vs] = on * _silu(r_all[:, vs])

    return [functools.partial(head_unit, h) for h in range(GLA_HEADS)]


def _pool_group_out(mix, g, wp_ref, ps_ref):
    cs = slice(g * POOL_GW, (g + 1) * POOL_GW)
    pm = jnp.dot(mix.astype(BF16), wp_ref[g].astype(BF16), preferred_element_type=F32)
    return pm * ps_ref[:, cs]


POOL_LEVELS = tuple(w.bit_length() - 1 for w in POOL_WINDOWS)
assert all(1 << lv == w for lv, w in zip(POOL_LEVELS, POOL_WINDOWS)), "windows must be powers of two"
POOL_HIST = SUBLANES * max(POOL_LEVELS)
assert POOL_HIST > POOL_BUF + 1


def _pool_prompt_kernel(u_ref, wp_ref, ps_ref, o_ref, ext_ref, lvl_ref):
    i = pl.program_id(1)
    tt = u_ref.shape[0]
    hist = POOL_HIST
    n = hist + tt

    @pl.when(i == 0)
    def _():
        ext_ref[0:hist, :] = jnp.zeros((hist, POOL_W), F32)

    @pl.when(i > 0)
    def _():
        ext_ref[0:hist, :] = ext_ref[tt:n, :]

    ext_ref[hist:n, :] = u_ref[...].astype(F32)
    pos = i * tt + lax.broadcasted_iota(jnp.int32, (tt, 1), 0)
    for g, (w, levels) in enumerate(zip(POOL_WINDOWS, POOL_LEVELS)):
        cs = slice(g * POOL_GW, (g + 1) * POOL_GW)
        cur = None
        for lev in range(levels):
            sh = 1 << lev
            start = SUBLANES * (lev + 1)
            if lev == 0:
                cur = ext_ref[start:n, cs] + ext_ref[start - sh:n - sh, cs]
            else:
                prev = lvl_ref.at[(lev - 1) % 2]
                cur = prev[start:n, :] + prev[start - sh:n - sh, :]
            if lev + 1 < levels:
                lvl_ref[lev % 2, start:n, :] = cur
        s = cur[hist - SUBLANES * levels:]
        u = ext_ref[hist:n, cs]
        cnt = jnp.minimum(pos + 1, w).astype(F32)
        mix = s / cnt - u
        o_ref[:, cs] = _pool_group_out(mix, g, wp_ref, ps_ref).astype(o_ref.dtype)


def _pool_prompt(z, w_pool, pool_scale, nb, t, tt):
    m = z.shape[0]
    nt = t // tt
    return pl.pallas_call(
        _pool_prompt_kernel,
        grid=(nb, nt),
        in_specs=[
            pl.BlockSpec((tt, POOL_W), lambda b, i: (b * nt + i, Z_U // POOL_W)),
            pl.BlockSpec((4, POOL_GW, POOL_GW), lambda b, i: (0, 0, 0)),
            pl.BlockSpec((1, POOL_W), lambda b, i: (0, 0)),
        ],
        out_specs=pl.BlockSpec((tt, POOL_W), lambda b, i: (b * nt + i, 0)),
        out_shape=jax.ShapeDtypeStruct((m, POOL_W), BF16),
        scratch_shapes=[pltpu.VMEM((tt + POOL_HIST, POOL_W), F32), pltpu.VMEM((2, tt + POOL_HIST, POOL_GW), F32)],
        compiler_params=_cparams(2),
        name="pool_prompt",
    )(z, w_pool, pool_scale)


def _pool_sample_kernel(u_ref, buf_ref, wp_ref, ps_ref, o_ref, buf_out_ref):
    buf_out_ref[0:POOL_BUF - 1] = buf_ref[1:POOL_BUF]
    buf_out_ref[POOL_BUF - 1] = u_ref[...].astype(F32)
    for g, w in enumerate(POOL_WINDOWS):
        cs = slice(g * POOL_GW, (g + 1) * POOL_GW)
        u = u_ref[:, cs].astype(F32)
        s = u
        for d in range(1, w):
            s = s + buf_ref[POOL_BUF - d, :, cs]
        mix = s / float(w) - u
        o_ref[:, cs] = _pool_group_out(mix, g, wp_ref, ps_ref).astype(o_ref.dtype)


def _pool_sample(z, buf_t, w_pool, pool_scale):
    n = z.shape[0]
    return pl.pallas_call(
        _pool_sample_kernel,
        grid=(1,),
        in_specs=[
            pl.BlockSpec((n, POOL_W), lambda i: (0, Z_U // POOL_W)),
            pl.BlockSpec((POOL_BUF, n, POOL_W), lambda i: (0, 0, 0)),
            pl.BlockSpec((4, POOL_GW, POOL_GW), lambda i: (0, 0, 0)),
            pl.BlockSpec((1, POOL_W), lambda i: (0, 0)),
        ],
        out_specs=[pl.BlockSpec((n, POOL_W), lambda i: (0, 0)),
                   pl.BlockSpec((POOL_BUF, n, POOL_W), lambda i: (0, 0, 0))],
        out_shape=[jax.ShapeDtypeStruct((n, POOL_W), BF16), jax.ShapeDtypeStruct((POOL_BUF, n, POOL_W), F32)],
        compiler_params=_cparams(1),
        name="pool_sample",
    )(z, buf_t, w_pool, pool_scale)


MERGE_SUB = 256


def _merge_kernel(ogp_ref, ogs_ref, pmp_ref, pms_ref, wa_ref, wb_ref, gap_ref, gas_ref, gbp_ref, gbs_ref,
                  op_ref, os_ref, wabf_ref, wbbf_ref):
    i = pl.program_id(1)

    @pl.when(i == 0)
    def _():
        wabf_ref[...] = wa_ref[...].astype(BF16)
        wbbf_ref[...] = wb_ref[...].astype(BF16)

    def emit(og_ref, pm_ref, ga_ref, gb_ref, o_ref):
        og = og_ref[...].astype(BF16)
        pm = pm_ref[...]
        for part in range(o_ref.shape[1] // MERGE_SUB):
            cols = slice(part * MERGE_SUB, (part + 1) * MERGE_SUB)
            ya = jnp.dot(og, wabf_ref[:, cols], preferred_element_type=F32)
            yb = jnp.dot(pm, wbbf_ref[:, cols], preferred_element_type=F32)
            ga = ga_ref[:, cols].astype(F32)
            gb = gb_ref[:, cols].astype(F32)
            o_ref[:, cols] = (_sigmoid(ga) * ya + _sigmoid(gb) * yb).astype(o_ref.dtype)

    @pl.when(i == 0)
    def _():
        emit(ogs_ref, pms_ref, gas_ref, gbs_ref, os_ref)

    @pl.when(i > 0)
    def _():
        emit(ogp_ref, pmp_ref, gap_ref, gbp_ref, op_ref)


def _merge(og_p, og_s, pm_p, pm_s, wa, wb, z_p, z_s, *, tm, tn):
    mp = og_p.shape[0]
    ms = og_s.shape[0]
    return pl.pallas_call(
        _merge_kernel,
        grid=(D_MODEL // tn, mp // tm + 1),
        in_specs=[
            *_row_tiled_specs(tm, ms, GLA_VW),
            *_row_tiled_specs(tm, ms, POOL_W),
            pl.BlockSpec((GLA_VW, tn), lambda j, i: (0, j), pipeline_mode=pl.Buffered(1)),
            pl.BlockSpec((POOL_W, tn), lambda j, i: (0, j), pipeline_mode=pl.Buffered(1)),
            *_col_tiled_specs(tm, ms, tn, Z_GA // tn),
            *_col_tiled_specs(tm, ms, tn, Z_GB // tn),
        ],
        out_specs=_col_tiled_specs(tm, ms, tn),
        out_shape=[jax.ShapeDtypeStruct((mp, D_MODEL), BF16), jax.ShapeDtypeStruct((ms, D_MODEL), BF16)],
        scratch_shapes=[pltpu.VMEM((GLA_VW, tn), BF16), pltpu.VMEM((POOL_W, tn), BF16)],
        compiler_params=_cparams(2),
        name="merge",
    )(og_p, og_s, pm_p, pm_s, wa, wb, z_p, z_s, z_p, z_s)


UP_TN = 512
UP_SUB = 256
UP_ROWS = 1024
UP_PARTS = UP_TN // LANE


def _up_kernel(*refs, tiles_per_seq):
    cp_ref, cs_ref, wg_ref = refs[:3]
    wu_refs = refs[3:3 + UP_PARTS]
    (cw_ref, cb_ref, h0_ref, h1_ref, actp_ref, acts_ref, tailp_ref, gates_ref,
     wgbf_ref, wubf_ref, carry_ref) = refs[3 + UP_PARTS:]
    i = pl.program_id(1)

    @pl.when(i == 0)
    def _():
        wgbf_ref[...] = wg_ref[...].astype(BF16)
        for part, wu_ref in enumerate(wu_refs):
            wubf_ref[:, part * LANE:(part + 1) * LANE] = wu_ref[...].astype(BF16)

    def conv_act(gate, up, g1, g2, cols):
        cw = cw_ref[:, cols]
        gconv = cb_ref[:, cols] + cw[0:1] * g2 + cw[1:2] * g1 + cw[2:3] * gate
        return _silu(gconv) * up

    @pl.when(i > 0)
    def _():
        @pl.when((i - 1) % tiles_per_seq == 0)
        def _():
            carry_ref[...] = jnp.zeros_like(carry_ref)

        tm = cp_ref.shape[0]
        rows = min(tm, UP_ROWS)
        sub = lax.broadcasted_iota(jnp.int32, (SUBLANES, UP_SUB), 0)
        n_cols = UP_TN // UP_SUB
        prev = [carry_ref[:, h * UP_SUB:(h + 1) * UP_SUB] for h in range(n_cols)]
        for rt in range(tm // rows):
            rsl = slice(rt * rows, (rt + 1) * rows)
            c = cp_ref[rsl, :]
            for h in range(n_cols):
                cols = slice(h * UP_SUB, (h + 1) * UP_SUB)
                gate = jnp.dot(c, wgbf_ref[:, cols], preferred_element_type=F32)
                up = jnp.dot(c, wubf_ref[:, cols], preferred_element_type=F32)
                r1 = pltpu.roll(gate, 1, 0)
                r2 = pltpu.roll(gate, 2, 0)
                head1 = jnp.where(sub < 1, pltpu.roll(prev[h], 1, 0), r1[0:SUBLANES])
                head2 = jnp.where(sub < 2, pltpu.roll(prev[h], 2, 0), r2[0:SUBLANES])
                g1 = jnp.concatenate([head1, r1[SUBLANES:]], axis=0)
                g2 = jnp.concatenate([head2, r2[SUBLANES:]], axis=0)
                prev[h] = gate[rows - SUBLANES:rows]
                actp_ref[rsl, cols] = conv_act(gate, up, g1, g2, cols).astype(actp_ref.dtype)
        for h in range(n_cols):
            cols = slice(h * UP_SUB, (h + 1) * UP_SUB)
            carry_ref[:, cols] = prev[h]
            tailp_ref[0, :, cols] = prev[h][SUBLANES - (CONV_W - 1):SUBLANES]

    @pl.when(i == 0)
    def _():
        c = cs_ref[...]
        gate = jnp.dot(c, wgbf_ref[...], preferred_element_type=F32)
        up = jnp.dot(c, wubf_ref[...], preferred_element_type=F32)
        gates_ref[...] = gate
        acts_ref[...] = conv_act(gate, up, h1_ref[...], h0_ref[...], slice(None)).astype(acts_ref.dtype)


def _up_proj(c_p, c_s, w_up, conv_w, conv_b, hist0, hist1, *, tm, seq_len):
    mp, k = c_p.shape
    ms = c_s.shape[0]
    npt = mp // tm
    tiles_per_seq = seq_len // tm
    nseq = mp // seq_len
    tn = UP_TN
    nj = pl.cdiv(D_FF, tn)
    up_block0 = D_FF // LANE
    last_block = w_up.shape[1] // LANE - 1

    def row_p(j, i):
        return jnp.maximum(i - 1, 0)

    def wu_spec(part):
        return pl.BlockSpec(
            (k, LANE), lambda j, i: (0, jnp.minimum(up_block0 + j * UP_PARTS + part, last_block)))

    in_specs = [
        pl.BlockSpec((tm, k), lambda j, i: (row_p(j, i), 0)),
        pl.BlockSpec((ms, k), lambda j, i: (0, 0)),
        pl.BlockSpec((k, tn), lambda j, i: (0, j)),
        *[wu_spec(part) for part in range(UP_PARTS)],
        pl.BlockSpec((CONV_W, tn), lambda j, i: (0, j)),
        pl.BlockSpec((1, tn), lambda j, i: (0, j)),
        pl.BlockSpec((ms, tn), lambda j, i: (0, j)),
        pl.BlockSpec((ms, tn), lambda j, i: (0, j)),
    ]
    out_specs = [
        pl.BlockSpec((tm, tn), lambda j, i: (row_p(j, i), j)),
        pl.BlockSpec((ms, tn), lambda j, i: (0, j)),
        pl.BlockSpec((1, CONV_W - 1, tn), lambda j, i: (row_p(j, i) // tiles_per_seq, 0, j)),
        pl.BlockSpec((ms, tn), lambda j, i: (0, j)),
    ]
    out_shape = [
        jax.ShapeDtypeStruct((mp, D_FF), BF16),
        jax.ShapeDtypeStruct((ms, D_FF), BF16),
        jax.ShapeDtypeStruct((nseq, CONV_W - 1, D_FF), F32),
        jax.ShapeDtypeStruct((ms, D_FF), F32),
    ]
    return pl.pallas_call(
        functools.partial(_up_kernel, tiles_per_seq=tiles_per_seq),
        grid=(nj, npt + 1),
        in_specs=in_specs,
        out_specs=out_specs,
        out_shape=out_shape,
        scratch_shapes=[pltpu.VMEM((k, tn), BF16), pltpu.VMEM((k, tn), BF16), pltpu.VMEM((SUBLANES, tn), F32)],
        compiler_params=_cparams(2),
        name="up_proj",
    )(c_p, c_s, w_up, *([w_up] * UP_PARTS), conv_w, conv_b, hist0, hist1)


def _two_group_specs(tm, ms, widths):
    specs = []
    for w in widths:
        specs.append(pl.BlockSpec((tm, w), lambda i: (jnp.maximum(i - 1, 0), 0)))
        specs.append(pl.BlockSpec((ms, w), lambda i: (0, 0)))
    return specs


def _resident(shape):
    return pl.BlockSpec(shape, lambda i: (0,) * len(shape), pipeline_mode=pl.Buffered(1))


def _out_norm_kernel(mp_ref, ms_ref, xp_ref, xs_ref, w_ref, g_ref, hp_ref, hs_ref, cp_ref, cs_ref, wbf_ref):
    i = pl.program_id(0)

    @pl.when(i == 0)
    def _():
        wbf_ref[...] = w_ref[...].astype(BF16)

    def emit(m_ref, x_ref, h_ref, c_ref):
        h = x_ref[...] + jnp.dot(m_ref[...], wbf_ref[...], preferred_element_type=F32)
        h_ref[...] = h
        c_ref[...] = _rms(h, g_ref[...]).astype(c_ref.dtype)

    @pl.when(i == 0)
    def _():
        emit(ms_ref, xs_ref, hs_ref, cs_ref)

    @pl.when(i > 0)
    def _():
        emit(mp_ref, xp_ref, hp_ref, cp_ref)


def _out_norm(m_p, m_s, x_p, x_s, w_out, g, *, tm):
    mp, d = x_p.shape
    ms = x_s.shape[0]
    return pl.pallas_call(
        _out_norm_kernel,
        grid=(mp // tm + 1,),
        in_specs=[*_two_group_specs(tm, ms, (d, d)), _resident((d, d)), _resident((1, d))],
        out_specs=_two_group_specs(tm, ms, (d, d)),
        out_shape=[
            jax.ShapeDtypeStruct((mp, d), F32), jax.ShapeDtypeStruct((ms, d), F32),
            jax.ShapeDtypeStruct((mp, d), BF16), jax.ShapeDtypeStruct((ms, d), BF16),
        ],
        scratch_shapes=[pltpu.VMEM((d, d), BF16)],
        compiler_params=_cparams(1),
        name="out_norm",
    )(m_p, m_s, x_p, x_s, w_out, g.reshape(1, d))


def _ple_kernel(hp_ref, hs_ref, pp_ref, ps_ref, wpg_ref, wple_ref, gple_ref, gfin_ref, yp_ref, ys_ref,
                wpgbf_ref, wplebf_ref, *, final_norm):
    i = pl.program_id(0)

    @pl.when(i == 0)
    def _():
        wpgbf_ref[...] = wpg_ref[...].astype(BF16)
        wplebf_ref[...] = wple_ref[...].astype(BF16)

    def emit(h_ref, p_ref, y_ref):
        h2 = h_ref[...]
        cn = _rms(h2, gple_ref[...]).astype(BF16)
        pg = _sigmoid(jnp.dot(cn, wpgbf_ref[...], preferred_element_type=F32))
        e = jnp.dot(p_ref[...].astype(BF16), wplebf_ref[...], preferred_element_type=F32)
        h3 = h2 + pg * e
        y_ref[...] = _rms(h3, gfin_ref[...]) if final_norm else h3

    @pl.when(i == 0)
    def _():
        emit(hs_ref, ps_ref, ys_ref)

    @pl.when(i > 0)
    def _():
        emit(hp_ref, pp_ref, yp_ref)


def _ple(h_p, h_s, p_p, p_s, wpg, wple, g_ple, g_final, *, tm, final_norm):
    mp, d = h_p.shape
    ms = h_s.shape[0]
    pd = p_p.shape[1]
    return pl.pallas_call(
        functools.partial(_ple_kernel, final_norm=final_norm),
        grid=(mp // tm + 1,),
        in_specs=[*_two_group_specs(tm, ms, (d, pd)), _resident((d, d)), _resident((pd, d)),
                  _resident((1, d)), _resident((1, d))],
        out_specs=_two_group_specs(tm, ms, (d,)),
        out_shape=[jax.ShapeDtypeStruct((mp, d), F32), jax.ShapeDtypeStruct((ms, d), F32)],
        scratch_shapes=[pltpu.VMEM((d, d), BF16), pltpu.VMEM((pd, d), BF16)],
        compiler_params=_cparams(1),
        name="ple",
    )(h_p, h_s, p_p, p_s, wpg, wple, g_ple.reshape(1, d), g_final.reshape(1, d))


def _pool_mixers(z_p, z_s, lw, s_pool, nb, t):
    pm_s, pool_new_t = _pool_sample(z_s, jnp.transpose(s_pool, (1, 0, 2)), lw["w_pool"], lw["pool_scale"])
    pool_new_s = jnp.transpose(pool_new_t, (1, 0, 2))
    pm_p = _pool_prompt(z_p, lw["w_pool"], lw["pool_scale"], nb, t, TT_POOL)
    pool_new_p = z_p.reshape(nb, t, Z_WIDTH)[:, t - POOL_BUF:, Z_U:Z_U + POOL_W].astype(F32)
    return pm_p, pm_s, pool_new_p, pool_new_s


def kernel(x_prompt, x_sample, state_gla, state_pool, state_conv, p_prompt, p_sample, norm_mix, w_in, w_gate_up, b_gate, gla_norm, w_branch_a, w_pool, pool_scale, w_branch_b, w_out, norm_ffn, w_up, conv_w, conv_b, w_down, norm_ple, w_ple_gate, w_ple, norm_final):
    depth = w_in.shape[0]
    nb, t, d = x_prompt.shape
    ns = x_sample.shape[0]
    hp = x_prompt.reshape(nb * t, d)
    hs = x_sample.reshape(ns, d)
    outs = {k: [] for k in ("gp", "pp", "cp", "gs", "ps", "cs")}
    for li in range(depth):
        lw = {
            "wg": jnp.pad(w_gate_up[li], ((0, LANE - GATE_RANK), (0, 0))).astype(BF16),
            "bg": b_gate[li].reshape(1, GLA_KW),
            "gn": gla_norm[li].reshape(1, GLA_VW),
            "w_pool": w_pool[li],
            "pool_scale": pool_scale[li].reshape(1, POOL_W),
        }
        a_p = _rmsnorm(hp, norm_mix[li], BF16, TM_NORM)
        a_s = _rmsnorm(hs, norm_mix[li], BF16, ns)
        z_p, z_s = _in_proj(a_p, a_s, w_in[li].T, tm=TM_IN)
        og_p, g1, og_s, g2 = _gla(z_p, z_s, lw["wg"], lw["bg"], lw["gn"], state_gla[li], nb, t)
        pm_p, pm_s, p1, p2 = _pool_mixers(z_p, z_s, lw, state_pool[li], nb, t)
        mg_p, mg_s = _merge(og_p, og_s, pm_p, pm_s, w_branch_a[li], w_branch_b[li], z_p, z_s,
                            tm=TM_MERGE, tn=TN_MERGE)
        h1_p, h1_s, c_p, c_s = _out_norm(mg_p, mg_s, hp, hs, w_out[li], norm_ffn[li], tm=TM_ROW)
        s_conv = state_conv[li]
        act_p, act_s, c1, gate_s = _up_proj(c_p, c_s, w_up[li], conv_w[li], conv_b[li].reshape(1, D_FF),
                                            s_conv[:, 0], s_conv[:, 1], tm=TM_UP, seq_len=t)
        c2 = jnp.stack([s_conv[:, 1], gate_s], axis=1)
        h2_p, h2_s = _down_proj(act_p, act_s, w_down[li], h1_p, h1_s, tm=TM_DOWN, tn=TN_DOWN)
        hp, hs = _ple(h2_p, h2_s, p_prompt[li].reshape(nb * t, PLE_DIM), p_sample[li].reshape(ns, PLE_DIM),
                      w_ple_gate[li], w_ple[li], norm_ple[li], norm_final, tm=TM_ROW,
                      final_norm=li == depth - 1)
        for key, val in zip(("gp", "pp", "cp", "gs", "ps", "cs"), (g1, p1, c1, g2, p2, c2)):
            outs[key].append(val)
    st = {k: jnp.stack(v, axis=0) for k, v in outs.items()}
    return (hp.reshape(nb, t, d), hs.reshape(ns, 1, d),
            st["gp"], st["pp"], st["cp"], st["gs"], st["ps"], st["cs"])
```

```python
import functools

import jax
import jax.numpy as jnp
from jax import lax
from jax.experimental import pallas as pl
from jax.experimental.pallas import tpu as pltpu

F32 = jnp.float32
BF16 = jnp.bfloat16

D_MODEL = 2048
GLA_HEADS = 4
GLA_DK = 256
GLA_DV = 512
GLA_KW = GLA_HEADS * GLA_DK
GLA_VW = GLA_HEADS * GLA_DV
GATE_RANK = 16
GATE_TEMP = 16.0
GLA_CHUNK = 64
POOL_WINDOWS = (2, 4, 8, 16)
POOL_GW = 256
POOL_W = 1024
POOL_BUF = 15
D_FF = 5504
CONV_W = 3
PLE_DIM = 256
EPS = 1e-6

LANE = 128
SUBLANES = 8
Z_Q, Z_K, Z_V, Z_R, Z_U, Z_GA, Z_GB, Z_GLR, Z_WIDTH = 0, 1024, 2048, 4096, 6144, 7168, 9216, 11264, 11392
GLR_SRC = 6144
IN_TN = 1024
IN_SUB = 512

VMEM_LIMIT = 58 * 1024 * 1024
TM_NORM = 1024
TM_IN = 2048
TM_MERGE, TN_MERGE = 1024, 1024
TM_ROW = 512
TM_UP = 2048
TM_DOWN, TN_DOWN = 512, 1024
TT_POOL = 1024


def _cparams(n_axes):
    return pltpu.CompilerParams(dimension_semantics=("arbitrary",) * n_axes, vmem_limit_bytes=VMEM_LIMIT)


def _sigmoid(x):
    return 1.0 / (1.0 + jnp.exp(-x))


def _silu(x):
    return x * _sigmoid(x)


def _log_sigmoid(x):
    return jnp.minimum(x, 0.0) - jnp.log(1.0 + jnp.exp(-jnp.abs(x)))


def _rms(x, g):
    return x * lax.rsqrt(jnp.mean(x * x, axis=-1, keepdims=True) + EPS) * g


def _rms_kernel(x_ref, g_ref, o_ref):
    o_ref[...] = _rms(x_ref[...], g_ref[...]).astype(o_ref.dtype)


def _rmsnorm(x, g, out_dtype, tm):
    m, d = x.shape
    return pl.pallas_call(
        _rms_kernel,
        grid=(m // tm,),
        in_specs=[pl.BlockSpec((tm, d), lambda i: (i, 0)), pl.BlockSpec((1, d), lambda i: (0, 0))],
        out_specs=pl.BlockSpec((tm, d), lambda i: (i, 0)),
        out_shape=jax.ShapeDtypeStruct((m, d), out_dtype),
        compiler_params=_cparams(1),
        name="rmsnorm",
    )(x, g.reshape(1, d))


def _col_tiled_specs(tm, ms, width, col0=0):
    return [
        pl.BlockSpec((tm, width), lambda j, i: (jnp.maximum(i - 1, 0), col0 + j)),
        pl.BlockSpec((ms, width), lambda j, i: (0, col0 + j)),
    ]


def _row_tiled_specs(tm, ms, width):
    return [
        pl.BlockSpec((tm, width), lambda j, i: (jnp.maximum(i - 1, 0), 0)),
        pl.BlockSpec((ms, width), lambda j, i: (0, 0)),
    ]


def _down_kernel(ap_ref, as_ref, w_ref, rp_ref, rs_ref, op_ref, os_ref, wbf_ref):
    i = pl.program_id(1)

    @pl.when(i == 0)
    def _():
        wbf_ref[...] = w_ref[...].astype(BF16)
        os_ref[...] = rs_ref[...] + jnp.dot(as_ref[...], wbf_ref[...], preferred_element_type=F32)

    @pl.when(i > 0)
    def _():
        op_ref[...] = rp_ref[...] + jnp.dot(ap_ref[...], wbf_ref[...], preferred_element_type=F32)


def _down_proj(a_p, a_s, w, res_p, res_s, *, tm, tn):
    mp, k = a_p.shape
    ms = a_s.shape[0]
    n = w.shape[1]
    return pl.pallas_call(
        _down_kernel,
        grid=(n // tn, mp // tm + 1),
        in_specs=[
            *_row_tiled_specs(tm, ms, k),
            pl.BlockSpec((k, tn), lambda j, i: (0, j), pipeline_mode=pl.Buffered(1)),
            *_col_tiled_specs(tm, ms, tn),
        ],
        out_specs=_col_tiled_specs(tm, ms, tn),
        out_shape=[jax.ShapeDtypeStruct((mp, n), F32), jax.ShapeDtypeStruct((ms, n), F32)],
        scratch_shapes=[pltpu.VMEM((k, tn), BF16)],
        compiler_params=_cparams(2),
        name="down_proj",
    )(a_p, a_s, w, res_p, res_s)


_NT = (((1,), (1,)), ((), ()))
_TN = (((0,), (0,)), ((), ()))


def _in_proj_kernel(ap_ref, as_ref, w_ref, wx_ref, zp_ref, zs_ref, wbf_ref):
    j = pl.program_id(0)
    i = pl.program_id(1)
    last = pl.num_programs(0) - 1
    first_shifted = GLR_SRC // IN_TN
    keep = IN_TN - GATE_RANK

    @pl.when(i == 0)
    def _():
        @pl.when(j < first_shifted)
        def _():
            wbf_ref[...] = w_ref[...].astype(BF16)

        @pl.when(jnp.logical_and(j >= first_shifted, j < last))
        def _():
            wbf_ref[:keep, :] = w_ref[GATE_RANK:, :].astype(BF16)
            wbf_ref[keep:, :] = wx_ref[...].astype(BF16)

        @pl.when(j == last)
        def _():
            wbf_ref[:GATE_RANK, :] = wx_ref[...].astype(BF16)
            wbf_ref[GATE_RANK:LANE, :] = jnp.zeros((LANE - GATE_RANK, wbf_ref.shape[1]), BF16)

    def emit(a_ref, z_ref):
        @pl.when(j < last)
        def _():
            a = a_ref[...]
            for part in range(IN_TN // IN_SUB):
                cols = slice(part * IN_SUB, (part + 1) * IN_SUB)
                z_ref[:, cols] = lax.dot_general(a, wbf_ref[cols, :], _NT,
                                                 preferred_element_type=F32).astype(z_ref.dtype)

        @pl.when(j == last)
        def _():
            z_ref[:, :LANE] = lax.dot_general(a_ref[...], wbf_ref[:LANE, :], _NT,
                                              preferred_element_type=F32).astype(z_ref.dtype)

    @pl.when(i == 0)
    def _():
        emit(as_ref, zs_ref)

    @pl.when(i > 0)
    def _():
        emit(ap_ref, zp_ref)


def _in_proj(a_p, a_s, wt, *, tm):
    mp, k = a_p.shape
    ms = a_s.shape[0]
    npt = mp // tm
    nj = Z_GLR // IN_TN + 1
    aux_per_tile = IN_TN // GATE_RANK

    def row_p(j, i):
        return jnp.maximum(i - 1, 0)

    def aux_idx(j, i):
        return (jnp.where(j == nj - 1, GLR_SRC // GATE_RANK, (j + 1) * aux_per_tile), 0)

    return pl.pallas_call(
        _in_proj_kernel,
        grid=(nj, npt + 1),
        in_specs=[
            pl.BlockSpec((tm, k), lambda j, i: (row_p(j, i), 0)),
            pl.BlockSpec((ms, k), lambda j, i: (0, 0)),
            pl.BlockSpec((IN_TN, k), lambda j, i: (j, 0)),
            pl.BlockSpec((GATE_RANK, k), aux_idx),
        ],
        out_specs=[
            pl.BlockSpec((tm, IN_TN), lambda j, i: (row_p(j, i), j)),
            pl.BlockSpec((ms, IN_TN), lambda j, i: (0, j)),
        ],
        out_shape=[jax.ShapeDtypeStruct((mp, Z_WIDTH), BF16), jax.ShapeDtypeStruct((ms, Z_WIDTH), F32)],
        scratch_shapes=[pltpu.VMEM((IN_TN, k), BF16)],
        compiler_params=_cparams(2),
        name="in_proj",
    )(a_p, a_s, wt, wt)


GLA_GROUP = 8
GLA_STEP_GROUPS = 2


def _gla_kernel(q_ref, k_ref, v_ref, r_ref, glr_ref, wg_ref, bg_ref, gn_ref,
                sq_ref, sk_ref, sv_ref, sr_ref, sglr_ref, swg_ref, sbg_ref, sgn_ref, s_ref,
                og_ref, s_out_ref, sog_ref, ss_out_ref, st_ref):
    c = GLA_CHUNK
    rows = GLA_GROUP * c
    gi = pl.program_id(2)
    step = (pl.program_id(0) * pl.num_programs(1) + pl.program_id(1)) * pl.num_programs(2) + gi
    n_sample = s_ref.shape[0]

    sample_units = []
    for si in range(n_sample):
        sample_units += _gla_sample_units(step * n_sample + si, si, sq_ref, sk_ref, sv_ref, sr_ref, sglr_ref,
                                          swg_ref, sbg_ref, sgn_ref, s_ref, sog_ref, ss_out_ref)

    @pl.when(gi == 0)
    def _():
        st_ref[...] = jnp.zeros_like(st_ref)

    row = lax.broadcasted_iota(jnp.int32, (c, c), 0)
    col = lax.broadcasted_iota(jnp.int32, (c, c), 1)
    causal = row >= col
    tri = causal.astype(BF16)

    def group(r0, sample_units):
        rs = pl.ds(r0, rows)
        zg = jnp.dot(glr_ref[rs, :], wg_ref[...], preferred_element_type=F32) + bg_ref[...]
        v = v_ref[rs, :]
        vals = [dict() for _ in range(GLA_GROUP)]
        state = {"st": st_ref[...]}

        def s_gate(ci, d):
            gl = _log_sigmoid(zg[ci * c:(ci + 1) * c]) * (1.0 / GATE_TEMP)
            d["gl_hi"] = gl.astype(BF16)
            d["gl_lo"] = (gl - d["gl_hi"].astype(F32)).astype(BF16)

        def s_cumsum(ci, d):
            d["b"] = (jnp.dot(tri, d.pop("gl_hi"), preferred_element_type=F32)
                      + jnp.dot(tri, d.pop("gl_lo"), preferred_element_type=F32))

        def s_scale(ci, d):
            cr = pl.ds(r0 + ci * c, c)
            b = d.pop("b")
            bl = b[c - 1:c, :]
            q = q_ref[cr, :].astype(F32) * (GLA_DK ** -0.5)
            k = k_ref[cr, :].astype(F32)
            d["qb"] = (q * jnp.exp(b)).astype(BF16)
            d["kb"] = (k * jnp.exp(-b)).astype(BF16)
            d["kd"] = (k * jnp.exp(bl - b)).astype(BF16)
            d["decay"] = jnp.exp(bl)

        def s_att(ci, d):
            sl = slice(ci * c, (ci + 1) * c)
            d["att"] = lax.dot_general(d["qb"], d.pop("kb"), _NT, preferred_element_type=F32)
            d["dst"] = lax.dot_general(v[sl], d.pop("kd"), _TN, preferred_element_type=F32)

        def s_mask(ci, d):
            d["att"] = jnp.where(causal, d["att"], 0.0).astype(BF16)

        def s_intra(ci, d):
            sl = slice(ci * c, (ci + 1) * c)
            d["o"] = jnp.dot(d.pop("att"), v[sl], preferred_element_type=F32)

        def s_chain(ci, d):
            st = state["st"]
            d["st_in"] = st.astype(BF16)
            state["st"] = st * d.pop("decay") + d.pop("dst")

        def s_inter(ci, d):
            d["o"] = d["o"] + lax.dot_general(d.pop("qb"), d.pop("st_in"), _NT, preferred_element_type=F32)

        def s_finish(ci, d):
            cr = pl.ds(r0 + ci * c, c)
            on = _rms(d.pop("o"), gn_ref[...])
            og_ref[cr, :] = (on * _silu(r_ref[cr, :].astype(F32))).astype(og_ref.dtype)

        stages = (s_gate, s_cumsum, s_scale, s_att, s_mask, s_intra, s_chain, s_inter, s_finish)
        n_ticks = GLA_GROUP + len(stages) - 1
        pending = list(sample_units)
        for tick in range(n_ticks):
            for si, stage in enumerate(stages):
                ci = tick - si
                if 0 <= ci < GLA_GROUP:
                    stage(ci, vals[ci])
            n_front = 3 * len(sample_units) // 4
            if pending and (tick < n_front or tick >= n_ticks - (len(sample_units) - n_front)):
                pending.pop(0)()
        for unit in pending:
            unit()
        st_ref[...] = state["st"]

    n_groups = q_ref.shape[0] // rows
    per_group = len(sample_units) // n_groups
    for gl in range(n_groups):
        units = sample_units[gl * per_group:] if gl == n_groups - 1 else sample_units[gl * per_group:(gl + 1) * per_group]
        group(gl * rows, units)

    @pl.when(gi == pl.num_programs(2) - 1)
    def _():
        s_out_ref[0, 0] = st_ref[...].T


def _gla(z_p, z_s, wg, bg, gn, state, nb, t):
    m = z_p.shape[0]
    n = z_s.shape[0]
    tr = GLA_STEP_GROUPS * GLA_GROUP * GLA_CHUNK
    n_groups = t // tr
    n_steps = nb * GLA_HEADS * n_groups
    n_sample = n // n_steps
    assert n_sample * n_steps == n

    def s_idx(b, h, g):
        return ((b * GLA_HEADS + h) * n_groups + g, 0, 0, 0)

    def rows_idx(col0):
        return lambda b, h, g: (b * n_groups + g, col0 + h)

    const2 = lambda b, h, g: (0, 0)
    og_p, s_new_p, og_s, s_new_s = pl.pallas_call(
        _gla_kernel,
        grid=(nb, GLA_HEADS, n_groups),
        in_specs=[
            pl.BlockSpec((tr, GLA_DK), rows_idx(Z_Q // GLA_DK)),
            pl.BlockSpec((tr, GLA_DK), rows_idx(Z_K // GLA_DK)),
            pl.BlockSpec((tr, GLA_DV), rows_idx(Z_V // GLA_DV)),
            pl.BlockSpec((tr, GLA_DV), rows_idx(Z_R // GLA_DV)),
            pl.BlockSpec((tr, LANE), lambda b, h, g: (b * n_groups + g, Z_GLR // LANE)),
            pl.BlockSpec((LANE, GLA_DK), lambda b, h, g: (0, h)),
            pl.BlockSpec((1, GLA_DK), lambda b, h, g: (0, h)),
            pl.BlockSpec((1, GLA_DV), lambda b, h, g: (0, h)),
            pl.BlockSpec((n, GLA_KW), lambda b, h, g: (0, Z_Q // GLA_KW)),
            pl.BlockSpec((n, GLA_KW), lambda b, h, g: (0, Z_K // GLA_KW)),
            pl.BlockSpec((n, GLA_VW), lambda b, h, g: (0, Z_V // GLA_VW)),
            pl.BlockSpec((n, GLA_VW), lambda b, h, g: (0, Z_R // GLA_VW)),
            pl.BlockSpec((n, LANE), lambda b, h, g: (0, Z_GLR // LANE)),
            pl.BlockSpec((LANE, GLA_KW), const2),
            pl.BlockSpec((1, GLA_KW), const2),
            pl.BlockSpec((1, GLA_VW), const2),
            pl.BlockSpec((n_sample, GLA_HEADS, GLA_DK, GLA_DV), s_idx),
        ],
        out_specs=[
            pl.BlockSpec((tr, GLA_DV), rows_idx(0)),
            pl.BlockSpec((1, 1, GLA_DK, GLA_DV), lambda b, h, g: (b, h, 0, 0)),
            pl.BlockSpec((n, GLA_VW), const2),
            pl.BlockSpec((n_sample, GLA_HEADS, GLA_DK, GLA_DV), s_idx),
        ],
        out_shape=[
            jax.ShapeDtypeStruct((m, GLA_VW), BF16),
            jax.ShapeDtypeStruct((nb, GLA_HEADS, GLA_DK, GLA_DV), F32),
            jax.ShapeDtypeStruct((n, GLA_VW), F32),
            jax.ShapeDtypeStruct((n, GLA_HEADS, GLA_DK, GLA_DV), F32),
        ],
        scratch_shapes=[pltpu.VMEM((GLA_DV, GLA_DK), F32)],
        compiler_params=_cparams(3),
        name="gla",
    )(z_p, z_p, z_p, z_p, z_p, wg, bg, gn, z_s, z_s, z_s, z_s, z_s, wg, bg, gn, state)
    return og_p, s_new_p, og_s, s_new_s


def _gla_sample_units(n, si, q_ref, k_ref, v_ref, r_ref, glr_ref, wg_ref, bg_ref, gn_ref, s_ref,
                      og_ref, s_out_ref):
    row = pl.ds(n, 1)

    def rows8(ref):
        return jnp.broadcast_to(ref[row, :], (SUBLANES, ref.shape[1]))

    zg = jnp.dot(rows8(glr_ref).astype(BF16), wg_ref[...], preferred_element_type=F32) + bg_ref[...]
    g = _log_sigmoid(zg) * (1.0 / GATE_TEMP)
    eg = jnp.exp(g)
    eng = jnp.exp(-g)
    eg1 = eg.astype(BF16)
    rem = eg - eg1.astype(F32)
    eg2 = rem.astype(BF16)
    eg3 = (rem - eg2.astype(F32)).astype(BF16)
    first_row = lax.broadcasted_iota(jnp.int32, (SUBLANES, LANE), 0) == 0
    ones_row = jnp.where(first_row, 1.0, 0.0).astype(BF16)
    row0 = lax.broadcasted_iota(jnp.int32, (SUBLANES, GLA_DV), 0) == 0
    q_all = rows8(q_ref) * (GLA_DK ** -0.5)
    k_all = rows8(k_ref)
    v_all = rows8(v_ref)
    r_all = r_ref[row, :]
    gn_all = gn_ref[...]

    def head_unit(h):
        ks = slice(h * GLA_DK, (h + 1) * GLA_DK)
        vs = slice(h * GLA_DV, (h + 1) * GLA_DV)
        q, k, v = q_all[:, ks], k_all[:, ks], v_all[:, vs]
        s_old = s_ref[si, h]
        decay = (lax.dot_general(eg1[:, ks], ones_row, _TN, preferred_element_type=F32)
                 + lax.dot_general(eg2[:, ks], ones_row, _TN, preferred_element_type=F32)
                 + lax.dot_general(eg3[:, ks], ones_row, _TN, preferred_element_type=F32))
        v0 = jnp.where(row0, v, 0.0)
        ds = lax.dot_general(k.astype(BF16), v0.astype(BF16), _TN, preferred_element_type=F32)
        s_out_ref[si, h] = jnp.tile(decay, (1, GLA_DV // LANE)) * s_old + ds
        qb = q * eg[:, ks]
        kb = k * eng[:, ks]
        att = jnp.sum(qb * kb, axis=-1, keepdims=True)
        o = jnp.dot(qb.astype(BF16), s_old.astype(BF16), preferred_element_type=F32) + att * v
        o = o[0:1, :]
        ms = jnp.mean(o * o, axis=-1, keepdims=True)
        on = o * lax.rsqrt(ms + EPS) * gn_all[:, vs]
        og_ref[row, vs] = on * _silu(r_all[:, vs])

    return [functools.partial(head_unit, h) for h in range(GLA_HEADS)]


def _pool_group_out(mix, g, wp_ref, ps_ref):
    cs = slice(g * POOL_GW, (g + 1) * POOL_GW)
    pm = jnp.dot(mix.astype(BF16), wp_ref[g].astype(BF16), preferred_element_type=F32)
    return pm * ps_ref[:, cs]


POOL_LEVELS = tuple(w.bit_length() - 1 for w in POOL_WINDOWS)
assert all(1 << lv == w for lv, w in zip(POOL_LEVELS, POOL_WINDOWS)), "windows must be powers of two"
POOL_HIST = SUBLANES * max(POOL_LEVELS)
assert POOL_HIST > POOL_BUF + 1


def _pool_prompt_kernel(u_ref, wp_ref, ps_ref, o_ref, ext_ref, lvl_ref):
    i = pl.program_id(1)
    tt = u_ref.shape[0]
    hist = POOL_HIST
    n = hist + tt

    @pl.when(i == 0)
    def _():
        ext_ref[0:hist, :] = jnp.zeros((hist, POOL_W), F32)

    @pl.when(i > 0)
    def _():
        ext_ref[0:hist, :] = ext_ref[tt:n, :]

    ext_ref[hist:n, :] = u_ref[...].astype(F32)
    pos = i * tt + lax.broadcasted_iota(jnp.int32, (tt, 1), 0)
    for g, (w, levels) in enumerate(zip(POOL_WINDOWS, POOL_LEVELS)):
        cs = slice(g * POOL_GW, (g + 1) * POOL_GW)
        cur = None
        for lev in range(levels):
            sh = 1 << lev
            start = SUBLANES * (lev + 1)
            if lev == 0:
                cur = ext_ref[start:n, cs] + ext_ref[start - sh:n - sh, cs]
            else:
                prev = lvl_ref.at[(lev - 1) % 2]
                cur = prev[start:n, :] + prev[start - sh:n - sh, :]
            if lev + 1 < levels:
                lvl_ref[lev % 2, start:n, :] = cur
        s = cur[hist - SUBLANES * levels:]
        u = ext_ref[hist:n, cs]
        cnt = jnp.minimum(pos + 1, w).astype(F32)
        mix = s / cnt - u
        o_ref[:, cs] = _pool_group_out(mix, g, wp_ref, ps_ref).astype(o_ref.dtype)


def _pool_prompt(z, w_pool, pool_scale, nb, t, tt):
    m = z.shape[0]
    nt = t // tt
    return pl.pallas_call(
        _pool_prompt_kernel,
        grid=(nb, nt),
        in_specs=[
            pl.BlockSpec((tt, POOL_W), lambda b, i: (b * nt + i, Z_U // POOL_W)),
            pl.BlockSpec((4, POOL_GW, POOL_GW), lambda b, i: (0, 0, 0)),
            pl.BlockSpec((1, POOL_W), lambda b, i: (0, 0)),
        ],
        out_specs=pl.BlockSpec((tt, POOL_W), lambda b, i: (b * nt + i, 0)),
        out_shape=jax.ShapeDtypeStruct((m, POOL_W), BF16),
        scratch_shapes=[pltpu.VMEM((tt + POOL_HIST, POOL_W), F32), pltpu.VMEM((2, tt + POOL_HIST, POOL_GW), F32)],
        compiler_params=_cparams(2),
        name="pool_prompt",
    )(z, w_pool, pool_scale)


def _pool_sample_kernel(u_ref, buf_ref, wp_ref, ps_ref, o_ref, buf_out_ref):
    buf_out_ref[0:POOL_BUF - 1] = buf_ref[1:POOL_BUF]
    buf_out_ref[POOL_BUF - 1] = u_ref[...].astype(F32)
    for g, w in enumerate(POOL_WINDOWS):
        cs = slice(g * POOL_GW, (g + 1) * POOL_GW)
        u = u_ref[:, cs].astype(F32)
        s = u
        for d in range(1, w):
            s = s + buf_ref[POOL_BUF - d, :, cs]
        mix = s / float(w) - u
        o_ref[:, cs] = _pool_group_out(mix, g, wp_ref, ps_ref).astype(o_ref.dtype)


def _pool_sample(z, buf_t, w_pool, pool_scale):
    n = z.shape[0]
    return pl.pallas_call(
        _pool_sample_kernel,
        grid=(1,),
        in_specs=[
            pl.BlockSpec((n, POOL_W), lambda i: (0, Z_U // POOL_W)),
            pl.BlockSpec((POOL_BUF, n, POOL_W), lambda i: (0, 0, 0)),
            pl.BlockSpec((4, POOL_GW, POOL_GW), lambda i: (0, 0, 0)),
            pl.BlockSpec((1, POOL_W), lambda i: (0, 0)),
        ],
        out_specs=[pl.BlockSpec((n, POOL_W), lambda i: (0, 0)),
                   pl.BlockSpec((POOL_BUF, n, POOL_W), lambda i: (0, 0, 0))],
        out_shape=[jax.ShapeDtypeStruct((n, POOL_W), BF16), jax.ShapeDtypeStruct((POOL_BUF, n, POOL_W), F32)],
        compiler_params=_cparams(1),
        name="pool_sample",
    )(z, buf_t, w_pool, pool_scale)


MERGE_SUB = 256


def _merge_kernel(ogp_ref, ogs_ref, pmp_ref, pms_ref, wa_ref, wb_ref, gap_ref, gas_ref, gbp_ref, gbs_ref,
                  op_ref, os_ref, wabf_ref, wbbf_ref):
    i = pl.program_id(1)

    @pl.when(i == 0)
    def _():
        wabf_ref[...] = wa_ref[...].astype(BF16)
        wbbf_ref[...] = wb_ref[...].astype(BF16)

    def emit(og_ref, pm_ref, ga_ref, gb_ref, o_ref):
        og = og_ref[...].astype(BF16)
        pm = pm_ref[...]
        for part in range(o_ref.shape[1] // MERGE_SUB):
            cols = slice(part * MERGE_SUB, (part + 1) * MERGE_SUB)
            ya = jnp.dot(og, wabf_ref[:, cols], preferred_element_type=F32)
            yb = jnp.dot(pm, wbbf_ref[:, cols], preferred_element_type=F32)
            ga = ga_ref[:, cols].astype(F32)
            gb = gb_ref[:, cols].astype(F32)
            o_ref[:, cols] = (_sigmoid(ga) * ya + _sigmoid(gb) * yb).astype(o_ref.dtype)

    @pl.when(i == 0)
    def _():
        emit(ogs_ref, pms_ref, gas_ref, gbs_ref, os_ref)

    @pl.when(i > 0)
    def _():
        emit(ogp_ref, pmp_ref, gap_ref, gbp_ref, op_ref)


def _merge(og_p, og_s, pm_p, pm_s, wa, wb, z_p, z_s, *, tm, tn):
    mp = og_p.shape[0]
    ms = og_s.shape[0]
    return pl.pallas_call(
        _merge_kernel,
        grid=(D_MODEL // tn, mp // tm + 1),
        in_specs=[
            *_row_tiled_specs(tm, ms, GLA_VW),
            *_row_tiled_specs(tm, ms, POOL_W),
            pl.BlockSpec((GLA_VW, tn), lambda j, i: (0, j), pipeline_mode=pl.Buffered(1)),
            pl.BlockSpec((POOL_W, tn), lambda j, i: (0, j), pipeline_mode=pl.Buffered(1)),
            *_col_tiled_specs(tm, ms, tn, Z_GA // tn),
            *_col_tiled_specs(tm, ms, tn, Z_GB // tn),
        ],
        out_specs=_col_tiled_specs(tm, ms, tn),
        out_shape=[jax.ShapeDtypeStruct((mp, D_MODEL), BF16), jax.ShapeDtypeStruct((ms, D_MODEL), BF16)],
        scratch_shapes=[pltpu.VMEM((GLA_VW, tn), BF16), pltpu.VMEM((POOL_W, tn), BF16)],
        compiler_params=_cparams(2),
        name="merge",
    )(og_p, og_s, pm_p, pm_s, wa, wb, z_p, z_s, z_p, z_s)


UP_TN = 512
UP_SUB = 256
UP_ROWS = 1024
UP_PARTS = UP_TN // LANE


def _up_kernel(*refs, tiles_per_seq):
    cp_ref, cs_ref, wg_ref = refs[:3]
    wu_refs = refs[3:3 + UP_PARTS]
    (cw_ref, cb_ref, h0_ref, h1_ref, actp_ref, acts_ref, tailp_ref, gates_ref,
     wgbf_ref, wubf_ref, carry_ref) = refs[3 + UP_PARTS:]
    i = pl.program_id(1)

    @pl.when(i == 0)
    def _():
        wgbf_ref[...] = wg_ref[...].astype(BF16)
        for part, wu_ref in enumerate(wu_refs):
            wubf_ref[:, part * LANE:(part + 1) * LANE] = wu_ref[...].astype(BF16)

    def conv_act(gate, up, g1, g2, cols):
        cw = cw_ref[:, cols]
        gconv = cb_ref[:, cols] + cw[0:1] * g2 + cw[1:2] * g1 + cw[2:3] * gate
        return _silu(gconv) * up

    @pl.when(i > 0)
    def _():
        @pl.when((i - 1) % tiles_per_seq == 0)
        def _():
            carry_ref[...] = jnp.zeros_like(carry_ref)

        tm = cp_ref.shape[0]
        rows = min(tm, UP_ROWS)
        sub = lax.broadcasted_iota(jnp.int32, (SUBLANES, UP_SUB), 0)
        n_cols = UP_TN // UP_SUB
        prev = [carry_ref[:, h * UP_SUB:(h + 1) * UP_SUB] for h in range(n_cols)]
        for rt in range(tm // rows):
            rsl = slice(rt * rows, (rt + 1) * rows)
            c = cp_ref[rsl, :]
            for h in range(n_cols):
                cols = slice(h * UP_SUB, (h + 1) * UP_SUB)
                gate = jnp.dot(c, wgbf_ref[:, cols], preferred_element_type=F32)
                up = jnp.dot(c, wubf_ref[:, cols], preferred_element_type=F32)
                r1 = pltpu.roll(gate, 1, 0)
                r2 = pltpu.roll(gate, 2, 0)
                head1 = jnp.where(sub < 1, pltpu.roll(prev[h], 1, 0), r1[0:SUBLANES])
                head2 = jnp.where(sub < 2, pltpu.roll(prev[h], 2, 0), r2[0:SUBLANES])
                g1 = jnp.concatenate([head1, r1[SUBLANES:]], axis=0)
                g2 = jnp.concatenate([head2, r2[SUBLANES:]], axis=0)
                prev[h] = gate[rows - SUBLANES:rows]
                actp_ref[rsl, cols] = conv_act(gate, up, g1, g2, cols).astype(actp_ref.dtype)
        for h in range(n_cols):
            cols = slice(h * UP_SUB, (h + 1) * UP_SUB)
            carry_ref[:, cols] = prev[h]
            tailp_ref[0, :, cols] = prev[h][SUBLANES - (CONV_W - 1):SUBLANES]

    @pl.when(i == 0)
    def _():
        c = cs_ref[...]
        gate = jnp.dot(c, wgbf_ref[...], preferred_element_type=F32)
        up = jnp.dot(c, wubf_ref[...], preferred_element_type=F32)
        gates_ref[...] = gate
        acts_ref[...] = conv_act(gate, up, h1_ref[...], h0_ref[...], slice(None)).astype(acts_ref.dtype)


def _up_proj(c_p, c_s, w_up, conv_w, conv_b, hist0, hist1, *, tm, seq_len):
    mp, k = c_p.shape
    ms = c_s.shape[0]
    npt = mp // tm
    tiles_per_seq = seq_len // tm
    nseq = mp // seq_len
    tn = UP_TN
    nj = pl.cdiv(D_FF, tn)
    up_block0 = D_FF // LANE
    last_block = w_up.shape[1] // LANE - 1

    def row_p(j, i):
        return jnp.maximum(i - 1, 0)

    def wu_spec(part):
        return pl.BlockSpec(
            (k, LANE), lambda j, i: (0, jnp.minimum(up_block0 + j * UP_PARTS + part, last_block)))

    in_specs = [
        pl.BlockSpec((tm, k), lambda j, i: (row_p(j, i), 0)),
        pl.BlockSpec((ms, k), lambda j, i: (0, 0)),
        pl.BlockSpec((k, tn), lambda j, i: (0, j)),
        *[wu_spec(part) for part in range(UP_PARTS)],
        pl.BlockSpec((CONV_W, tn), lambda j, i: (0, j)),
        pl.BlockSpec((1, tn), lambda j, i: (0, j)),
        pl.BlockSpec((ms, tn), lambda j, i: (0, j)),
        pl.BlockSpec((ms, tn), lambda j, i: (0, j)),
    ]
    out_specs = [
        pl.BlockSpec((tm, tn), lambda j, i: (row_p(j, i), j)),
        pl.BlockSpec((ms, tn), lambda j, i: (0, j)),
        pl.BlockSpec((1, CONV_W - 1, tn), lambda j, i: (row_p(j, i) // tiles_per_seq, 0, j)),
        pl.BlockSpec((ms, tn), lambda j, i: (0, j)),
    ]
    out_shape = [
        jax.ShapeDtypeStruct((mp, D_FF), BF16),
        jax.ShapeDtypeStruct((ms, D_FF), BF16),
        jax.ShapeDtypeStruct((nseq, CONV_W - 1, D_FF), F32),
        jax.ShapeDtypeStruct((ms, D_FF), F32),
    ]
    return pl.pallas_call(
        functools.partial(_up_kernel, tiles_per_seq=tiles_per_seq),
        grid=(nj, npt + 1),
        in_specs=in_specs,
        out_specs=out_specs,
        out_shape=out_shape,
        scratch_shapes=[pltpu.VMEM((k, tn), BF16), pltpu.VMEM((k, tn), BF16), pltpu.VMEM((SUBLANES, tn), F32)],
        compiler_params=_cparams(2),
        name="up_proj",
    )(c_p, c_s, w_up, *([w_up] * UP_PARTS), conv_w, conv_b, hist0, hist1)


def _two_group_specs(tm, ms, widths):
    specs = []
    for w in widths:
        specs.append(pl.BlockSpec((tm, w), lambda i: (jnp.maximum(i - 1, 0), 0)))
        specs.append(pl.BlockSpec((ms, w), lambda i: (0, 0)))
    return specs


def _resident(shape):
    return pl.BlockSpec(shape, lambda i: (0,) * len(shape), pipeline_mode=pl.Buffered(1))


def _out_norm_kernel(mp_ref, ms_ref, xp_ref, xs_ref, w_ref, g_ref, hp_ref, hs_ref, cp_ref, cs_ref, wbf_ref):
    i = pl.program_id(0)

    @pl.when(i == 0)
    def _():
        wbf_ref[...] = w_ref[...].astype(BF16)

    def emit(m_ref, x_ref, h_ref, c_ref):
        h = x_ref[...] + jnp.dot(m_ref[...], wbf_ref[...], preferred_element_type=F32)
        h_ref[...] = h
        c_ref[...] = _rms(h, g_ref[...]).astype(c_ref.dtype)

    @pl.when(i == 0)
    def _():
        emit(ms_ref, xs_ref, hs_ref, cs_ref)

    @pl.when(i > 0)
    def _():
        emit(mp_ref, xp_ref, hp_ref, cp_ref)


def _out_norm(m_p, m_s, x_p, x_s, w_out, g, *, tm):
    mp, d = x_p.shape
    ms = x_s.shape[0]
    return pl.pallas_call(
        _out_norm_kernel,
        grid=(mp // tm + 1,),
        in_specs=[*_two_group_specs(tm, ms, (d, d)), _resident((d, d)), _resident((1, d))],
        out_specs=_two_group_specs(tm, ms, (d, d)),
        out_shape=[
            jax.ShapeDtypeStruct((mp, d), F32), jax.ShapeDtypeStruct((ms, d), F32),
            jax.ShapeDtypeStruct((mp, d), BF16), jax.ShapeDtypeStruct((ms, d), BF16),
        ],
        scratch_shapes=[pltpu.VMEM((d, d), BF16)],
        compiler_params=_cparams(1),
        name="out_norm",
    )(m_p, m_s, x_p, x_s, w_out, g.reshape(1, d))


def _ple_kernel(hp_ref, hs_ref, pp_ref, ps_ref, wpg_ref, wple_ref, gple_ref, gfin_ref, yp_ref, ys_ref,
                wpgbf_ref, wplebf_ref, *, final_norm):
    i = pl.program_id(0)

    @pl.when(i == 0)
    def _():
        wpgbf_ref[...] = wpg_ref[...].astype(BF16)
        wplebf_ref[...] = wple_ref[...].astype(BF16)

    def emit(h_ref, p_ref, y_ref):
        h2 = h_ref[...]
        cn = _rms(h2, gple_ref[...]).astype(BF16)
        pg = _sigmoid(jnp.dot(cn, wpgbf_ref[...], preferred_element_type=F32))
        e = jnp.dot(p_ref[...].astype(BF16), wplebf_ref[...], preferred_element_type=F32)
        h3 = h2 + pg * e
        y_ref[...] = _rms(h3, gfin_ref[...]) if final_norm else h3

    @pl.when(i == 0)
    def _():
        emit(hs_ref, ps_ref, ys_ref)

    @pl.when(i > 0)
    def _():
        emit(hp_ref, pp_ref, yp_ref)


def _ple(h_p, h_s, p_p, p_s, wpg, wple, g_ple, g_final, *, tm, final_norm):
    mp, d = h_p.shape
    ms = h_s.shape[0]
    pd = p_p.shape[1]
    return pl.pallas_call(
        functools.partial(_ple_kernel, final_norm=final_norm),
        grid=(mp // tm + 1,),
        in_specs=[*_two_group_specs(tm, ms, (d, pd)), _resident((d, d)), _resident((pd, d)),
                  _resident((1, d)), _resident((1, d))],
        out_specs=_two_group_specs(tm, ms, (d,)),
        out_shape=[jax.ShapeDtypeStruct((mp, d), F32), jax.ShapeDtypeStruct((ms, d), F32)],
        scratch_shapes=[pltpu.VMEM((d, d), BF16), pltpu.VMEM((pd, d), BF16)],
        compiler_params=_cparams(1),
        name="ple",
    )(h_p, h_s, p_p, p_s, wpg, wple, g_ple.reshape(1, d), g_final.reshape(1, d))


def _pool_mixers(z_p, z_s, lw, s_pool, nb, t):
    pm_s, pool_new_t = _pool_sample(z_s, jnp.transpose(s_pool, (1, 0, 2)), lw["w_pool"], lw["pool_scale"])
    pool_new_s = jnp.transpose(pool_new_t, (1, 0, 2))
    pm_p = _pool_prompt(z_p, lw["w_pool"], lw["pool_scale"], nb, t, TT_POOL)
    pool_new_p = z_p.reshape(nb, t, Z_WIDTH)[:, t - POOL_BUF:, Z_U:Z_U + POOL_W].astype(F32)
    return pm_p, pm_s, pool_new_p, pool_new_s


def kernel(x_prompt, x_sample, state_gla, state_pool, state_conv, p_prompt, p_sample, norm_mix, w_in, w_gate_up, b_gate, gla_norm, w_branch_a, w_pool, pool_scale, w_branch_b, w_out, norm_ffn, w_up, conv_w, conv_b, w_down, norm_ple, w_ple_gate, w_ple, norm_final):
    depth = w_in.shape[0]
    nb, t, d = x_prompt.shape
    ns = x_sample.shape[0]
    hp = x_prompt.reshape(nb * t, d)
    hs = x_sample.reshape(ns, d)
    outs = {k: [] for k in ("gp", "pp", "cp", "gs", "ps", "cs")}
    for li in range(depth):
        lw = {
            "wg": jnp.pad(w_gate_up[li], ((0, LANE - GATE_RANK), (0, 0))).astype(BF16),
            "bg": b_gate[li].reshape(1, GLA_KW),
            "gn": gla_norm[li].reshape(1, GLA_VW),
            "w_pool": w_pool[li],
            "pool_scale": pool_scale[li].reshape(1, POOL_W),
        }
        a_p = _rmsnorm(hp, norm_mix[li], BF16, TM_NORM)
        a_s = _rmsnorm(hs, norm_mix[li], BF16, ns)
        z_p, z_s = _in_proj(a_p, a_s, w_in[li].T, tm=TM_IN)
        og_p, g1, og_s, g2 = _gla(z_p, z_s, lw["wg"], lw["bg"], lw["gn"], state_gla[li], nb, t)
        pm_p, pm_s, p1, p2 = _pool_mixers(z_p, z_s, lw, state_pool[li], nb, t)
        mg_p, mg_s = _merge(og_p, og_s, pm_p, pm_s, w_branch_a[li], w_branch_b[li], z_p, z_s,
                            tm=TM_MERGE, tn=TN_MERGE)
        h1_p, h1_s, c_p, c_s = _out_norm(mg_p, mg_s, hp, hs, w_out[li], norm_ffn[li], tm=TM_ROW)
        s_conv = state_conv[li]
        act_p, act_s, c1, gate_s = _up_proj(c_p, c_s, w_up[li], conv_w[li], conv_b[li].reshape(1, D_FF),
                                            s_conv[:, 0], s_conv[:, 1], tm=TM_UP, seq_len=t)
        c2 = jnp.stack([s_conv[:, 1], gate_s], axis=1)
        h2_p, h2_s = _down_proj(act_p, act_s, w_down[li], h1_p, h1_s, tm=TM_DOWN, tn=TN_DOWN)
        hp, hs = _ple(h2_p, h2_s, p_prompt[li].reshape(nb * t, PLE_DIM), p_sample[li].reshape(ns, PLE_DIM),
                      w_ple_gate[li], w_ple[li], norm_ple[li], norm_final, tm=TM_ROW,
                      final_norm=li == depth - 1)
        for key, val in zip(("gp", "pp", "cp", "gs", "ps", "cs"), (g1, p1, c1, g2, p2, c2)):
            outs[key].append(val)
    st = {k: jnp.stack(v, axis=0) for k, v in outs.items()}
    return (hp.reshape(nb, t, d), hs.reshape(ns, 1, d),
            st["gp"], st["pp"], st["cp"], st["gs"], st["ps"], st["cs"])
```

```python
import functools

import jax
import jax.numpy as jnp
from jax import lax
from jax.experimental import pallas as pl
from jax.experimental.pallas import tpu as pltpu

F32 = jnp.float32
BF16 = jnp.bfloat16

D_MODEL = 2048
GLA_HEADS = 4
GLA_DK = 256
GLA_DV = 512
GLA_KW = GLA_HEADS * GLA_DK
GLA_VW = GLA_HEADS * GLA_DV
GATE_RANK = 16
GATE_TEMP = 16.0
GLA_CHUNK = 64
POOL_WINDOWS = (2, 4, 8, 16)
POOL_GW = 256
POOL_W = 1024
POOL_BUF = 15
D_FF = 5504
CONV_W = 3
PLE_DIM = 256
EPS = 1e-6

LANE = 128
SUBLANES = 8
Z_Q, Z_K, Z_V, Z_R, Z_U, Z_GA, Z_GB, Z_GLR, Z_WIDTH = 0, 1024, 2048, 4096, 6144, 7168, 9216, 11264, 11392
GLR_SRC = 6144
IN_TN = 1024
IN_SUB = 512

VMEM_LIMIT = 58 * 1024 * 1024
TM_NORM = 1024
TM_IN = 2048
TM_MERGE, TN_MERGE = 1024, 1024
TM_ROW = 512
TM_UP = 2048
TM_DOWN, TN_DOWN = 512, 1024
TT_POOL = 1024


def _cparams(n_axes):
    return pltpu.CompilerParams(dimension_semantics=("arbitrary",) * n_axes, vmem_limit_bytes=VMEM_LIMIT)


def _sigmoid(x):
    return 1.0 / (1.0 + jnp.exp(-x))


def _silu(x):
    return x * _sigmoid(x)


def _log_sigmoid(x):
    return jnp.minimum(x, 0.0) - jnp.log(1.0 + jnp.exp(-jnp.abs(x)))


def _rms(x, g):
    return x * lax.rsqrt(jnp.mean(x * x, axis=-1, keepdims=True) + EPS) * g


def _rms_kernel(x_ref, g_ref, o_ref):
    o_ref[...] = _rms(x_ref[...], g_ref[...]).astype(o_ref.dtype)


def _rmsnorm(x, g, out_dtype, tm):
    m, d = x.shape
    return pl.pallas_call(
        _rms_kernel,
        grid=(m // tm,),
        in_specs=[pl.BlockSpec((tm, d), lambda i: (i, 0)), pl.BlockSpec((1, d), lambda i: (0, 0))],
        out_specs=pl.BlockSpec((tm, d), lambda i: (i, 0)),
        out_shape=jax.ShapeDtypeStruct((m, d), out_dtype),
        compiler_params=_cparams(1),
        name="rmsnorm",
    )(x, g.reshape(1, d))


def _col_tiled_specs(tm, ms, width, col0=0):
    return [
        pl.BlockSpec((tm, width), lambda j, i: (jnp.maximum(i - 1, 0), col0 + j)),
        pl.BlockSpec((ms, width), lambda j, i: (0, col0 + j)),
    ]


def _row_tiled_specs(tm, ms, width):
    return [
        pl.BlockSpec((tm, width), lambda j, i: (jnp.maximum(i - 1, 0), 0)),
        pl.BlockSpec((ms, width), lambda j, i: (0, 0)),
    ]


def _down_kernel(ap_ref, as_ref, w_ref, rp_ref, rs_ref, op_ref, os_ref, wbf_ref):
    i = pl.program_id(1)

    @pl.when(i == 0)
    def _():
        wbf_ref[...] = w_ref[...].astype(BF16)
        os_ref[...] = rs_ref[...] + jnp.dot(as_ref[...], wbf_ref[...], preferred_element_type=F32)

    @pl.when(i > 0)
    def _():
        op_ref[...] = rp_ref[...] + jnp.dot(ap_ref[...], wbf_ref[...], preferred_element_type=F32)


def _down_proj(a_p, a_s, w, res_p, res_s, *, tm, tn):
    mp, k = a_p.shape
    ms = a_s.shape[0]
    n = w.shape[1]
    return pl.pallas_call(
        _down_kernel,
        grid=(n // tn, mp // tm + 1),
        in_specs=[
            *_row_tiled_specs(tm, ms, k),
            pl.BlockSpec((k, tn), lambda j, i: (0, j), pipeline_mode=pl.Buffered(1)),
            *_col_tiled_specs(tm, ms, tn),
        ],
        out_specs=_col_tiled_specs(tm, ms, tn),
        out_shape=[jax.ShapeDtypeStruct((mp, n), F32), jax.ShapeDtypeStruct((ms, n), F32)],
        scratch_shapes=[pltpu.VMEM((k, tn), BF16)],
        compiler_params=_cparams(2),
        name="down_proj",
    )(a_p, a_s, w, res_p, res_s)


_NT = (((1,), (1,)), ((), ()))
_TN = (((0,), (0,)), ((), ()))


def _in_proj_kernel(ap_ref, xs_ref, g_ref, w_ref, wx_ref, zp_ref, zs_ref, wbf_ref):
    j = pl.program_id(0)
    i = pl.program_id(1)
    last = pl.num_programs(0) - 1
    first_shifted = GLR_SRC // IN_TN
    keep = IN_TN - GATE_RANK

    @pl.when(i == 0)
    def _():
        @pl.when(j < first_shifted)
        def _():
            wbf_ref[...] = w_ref[...].astype(BF16)

        @pl.when(jnp.logical_and(j >= first_shifted, j < last))
        def _():
            wbf_ref[:keep, :] = w_ref[GATE_RANK:, :].astype(BF16)
            wbf_ref[keep:, :] = wx_ref[...].astype(BF16)

        @pl.when(j == last)
        def _():
            wbf_ref[:GATE_RANK, :] = wx_ref[...].astype(BF16)
            wbf_ref[GATE_RANK:LANE, :] = jnp.zeros((LANE - GATE_RANK, wbf_ref.shape[1]), BF16)

    def emit(load_a, z_ref):
        @pl.when(j < last)
        def _():
            a = load_a()
            for part in range(IN_TN // IN_SUB):
                cols = slice(part * IN_SUB, (part + 1) * IN_SUB)
                z_ref[:, cols] = lax.dot_general(a, wbf_ref[cols, :], _NT,
                                                 preferred_element_type=F32).astype(z_ref.dtype)

        @pl.when(j == last)
        def _():
            z_ref[:, :LANE] = lax.dot_general(load_a(), wbf_ref[:LANE, :], _NT,
                                              preferred_element_type=F32).astype(z_ref.dtype)

    @pl.when(i == 0)
    def _():
        emit(lambda: _rms(xs_ref[...], g_ref[...]).astype(BF16), zs_ref)

    @pl.when(i > 0)
    def _():
        emit(lambda: ap_ref[...], zp_ref)


def _in_proj(a_p, x_s, g, wt, *, tm):
    mp, k = a_p.shape
    ms = x_s.shape[0]
    npt = mp // tm
    nj = Z_GLR // IN_TN + 1
    aux_per_tile = IN_TN // GATE_RANK

    def row_p(j, i):
        return jnp.maximum(i - 1, 0)

    def aux_idx(j, i):
        return (jnp.where(j == nj - 1, GLR_SRC // GATE_RANK, (j + 1) * aux_per_tile), 0)

    return pl.pallas_call(
        _in_proj_kernel,
        grid=(nj, npt + 1),
        in_specs=[
            pl.BlockSpec((tm, k), lambda j, i: (row_p(j, i), 0)),
            pl.BlockSpec((ms, k), lambda j, i: (0, 0)),
            pl.BlockSpec((1, k), lambda j, i: (0, 0)),
            pl.BlockSpec((IN_TN, k), lambda j, i: (j, 0)),
            pl.BlockSpec((GATE_RANK, k), aux_idx),
        ],
        out_specs=[
            pl.BlockSpec((tm, IN_TN), lambda j, i: (row_p(j, i), j)),
            pl.BlockSpec((ms, IN_TN), lambda j, i: (0, j)),
        ],
        out_shape=[jax.ShapeDtypeStruct((mp, Z_WIDTH), BF16), jax.ShapeDtypeStruct((ms, Z_WIDTH), F32)],
        scratch_shapes=[pltpu.VMEM((IN_TN, k), BF16)],
        compiler_params=_cparams(2),
        name="in_proj",
    )(a_p, x_s, g.reshape(1, k), wt, wt)


GLA_GROUP = 8
GLA_STEP_GROUPS = 2


def _gla_kernel(q_ref, k_ref, v_ref, r_ref, glr_ref, wg_ref, bg_ref, gn_ref,
                sq_ref, sk_ref, sv_ref, sr_ref, sglr_ref, swg_ref, sbg_ref, sgn_ref, s_ref,
                og_ref, s_out_ref, sog_ref, ss_out_ref, st_ref):
    c = GLA_CHUNK
    rows = GLA_GROUP * c
    gi = pl.program_id(2)
    step = (pl.program_id(0) * pl.num_programs(1) + pl.program_id(1)) * pl.num_programs(2) + gi
    n_sample = s_ref.shape[0]

    sample_units = []
    for si in range(n_sample):
        sample_units += _gla_sample_units(step * n_sample + si, si, sq_ref, sk_ref, sv_ref, sr_ref, sglr_ref,
                                          swg_ref, sbg_ref, sgn_ref, s_ref, sog_ref, ss_out_ref)

    @pl.when(gi == 0)
    def _():
        st_ref[...] = jnp.zeros_like(st_ref)

    row = lax.broadcasted_iota(jnp.int32, (c, c), 0)
    col = lax.broadcasted_iota(jnp.int32, (c, c), 1)
    causal = row >= col
    tri = causal.astype(BF16)

    def group(r0, sample_units):
        rs = pl.ds(r0, rows)
        zg = jnp.dot(glr_ref[rs, :], wg_ref[...], preferred_element_type=F32) + bg_ref[...]
        v = v_ref[rs, :]
        vals = [dict() for _ in range(GLA_GROUP)]
        state = {"st": st_ref[...]}

        def s_gate(ci, d):
            gl = _log_sigmoid(zg[ci * c:(ci + 1) * c]) * (1.0 / GATE_TEMP)
            d["gl_hi"] = gl.astype(BF16)
            d["gl_lo"] = (gl - d["gl_hi"].astype(F32)).astype(BF16)

        def s_cumsum(ci, d):
            d["b"] = (jnp.dot(tri, d.pop("gl_hi"), preferred_element_type=F32)
                      + jnp.dot(tri, d.pop("gl_lo"), preferred_element_type=F32))

        def s_scale(ci, d):
            cr = pl.ds(r0 + ci * c, c)
            b = d.pop("b")
            bl = b[c - 1:c, :]
            q = q_ref[cr, :].astype(F32) * (GLA_DK ** -0.5)
            k = k_ref[cr, :].astype(F32)
            d["qb"] = (q * jnp.exp(b)).astype(BF16)
            d["kb"] = (k * jnp.exp(-b)).astype(BF16)
            d["kd"] = (k * jnp.exp(bl - b)).astype(BF16)
            d["decay"] = jnp.exp(bl)

        def s_att(ci, d):
            sl = slice(ci * c, (ci + 1) * c)
            d["att"] = lax.dot_general(d["qb"], d.pop("kb"), _NT, preferred_element_type=F32)
            d["dst"] = lax.dot_general(v[sl], d.pop("kd"), _TN, preferred_element_type=F32)

        def s_mask(ci, d):
            d["att"] = jnp.where(causal, d["att"], 0.0).astype(BF16)

        def s_intra(ci, d):
            sl = slice(ci * c, (ci + 1) * c)
            d["o"] = jnp.dot(d.pop("att"), v[sl], preferred_element_type=F32)

        def s_chain(ci, d):
            st = state["st"]
            d["st_in"] = st.astype(BF16)
            state["st"] = st * d.pop("decay") + d.pop("dst")

        def s_inter(ci, d):
            d["o"] = d["o"] + lax.dot_general(d.pop("qb"), d.pop("st_in"), _NT, preferred_element_type=F32)

        def s_finish(ci, d):
            cr = pl.ds(r0 + ci * c, c)
            on = _rms(d.pop("o"), gn_ref[...])
            og_ref[cr, :] = (on * _silu(r_ref[cr, :].astype(F32))).astype(og_ref.dtype)

        stages = (s_gate, s_cumsum, s_scale, s_att, s_mask, s_intra, s_chain, s_inter, s_finish)
        n_ticks = GLA_GROUP + len(stages) - 1
        pending = list(sample_units)
        for tick in range(n_ticks):
            for si, stage in enumerate(stages):
                ci = tick - si
                if 0 <= ci < GLA_GROUP:
                    stage(ci, vals[ci])
            n_front = 3 * len(sample_units) // 4
            if pending and (tick < n_front or tick >= n_ticks - (len(sample_units) - n_front)):
                pending.pop(0)()
        for unit in pending:
            unit()
        st_ref[...] = state["st"]

    n_groups = q_ref.shape[0] // rows
    per_group = len(sample_units) // n_groups
    for gl in range(n_groups):
        units = sample_units[gl * per_group:] if gl == n_groups - 1 else sample_units[gl * per_group:(gl + 1) * per_group]
        group(gl * rows, units)

    @pl.when(gi == pl.num_programs(2) - 1)
    def _():
        s_out_ref[0, 0] = st_ref[...].T


def _gla(z_p, z_s, wg, bg, gn, state, nb, t):
    m = z_p.shape[0]
    n = z_s.shape[0]
    tr = GLA_STEP_GROUPS * GLA_GROUP * GLA_CHUNK
    n_groups = t // tr
    n_steps = nb * GLA_HEADS * n_groups
    n_sample = n // n_steps
    assert n_sample * n_steps == n

    def s_idx(b, h, g):
        return ((b * GLA_HEADS + h) * n_groups + g, 0, 0, 0)

    def rows_idx(col0):
        return lambda b, h, g: (b * n_groups + g, col0 + h)

    const2 = lambda b, h, g: (0, 0)
    og_p, s_new_p, og_s, s_new_s = pl.pallas_call(
        _gla_kernel,
        grid=(nb, GLA_HEADS, n_groups),
        in_specs=[
            pl.BlockSpec((tr, GLA_DK), rows_idx(Z_Q // GLA_DK)),
            pl.BlockSpec((tr, GLA_DK), rows_idx(Z_K // GLA_DK)),
            pl.BlockSpec((tr, GLA_DV), rows_idx(Z_V // GLA_DV)),
            pl.BlockSpec((tr, GLA_DV), rows_idx(Z_R // GLA_DV)),
            pl.BlockSpec((tr, LANE), lambda b, h, g: (b * n_groups + g, Z_GLR // LANE)),
            pl.BlockSpec((LANE, GLA_DK), lambda b, h, g: (0, h)),
            pl.BlockSpec((1, GLA_DK), lambda b, h, g: (0, h)),
            pl.BlockSpec((1, GLA_DV), lambda b, h, g: (0, h)),
            pl.BlockSpec((n, GLA_KW), lambda b, h, g: (0, Z_Q // GLA_KW)),
            pl.BlockSpec((n, GLA_KW), lambda b, h, g: (0, Z_K // GLA_KW)),
            pl.BlockSpec((n, GLA_VW), lambda b, h, g: (0, Z_V // GLA_VW)),
            pl.BlockSpec((n, GLA_VW), lambda b, h, g: (0, Z_R // GLA_VW)),
            pl.BlockSpec((n, LANE), lambda b, h, g: (0, Z_GLR // LANE)),
            pl.BlockSpec((LANE, GLA_KW), const2),
            pl.BlockSpec((1, GLA_KW), const2),
            pl.BlockSpec((1, GLA_VW), const2),
            pl.BlockSpec((n_sample, GLA_HEADS, GLA_DK, GLA_DV), s_idx),
        ],
        out_specs=[
            pl.BlockSpec((tr, GLA_DV), rows_idx(0)),
            pl.BlockSpec((1, 1, GLA_DK, GLA_DV), lambda b, h, g: (b, h, 0, 0)),
            pl.BlockSpec((n, GLA_VW), const2),
            pl.BlockSpec((n_sample, GLA_HEADS, GLA_DK, GLA_DV), s_idx),
        ],
        out_shape=[
            jax.ShapeDtypeStruct((m, GLA_VW), BF16),
            jax.ShapeDtypeStruct((nb, GLA_HEADS, GLA_DK, GLA_DV), F32),
            jax.ShapeDtypeStruct((n, GLA_VW), F32),
            jax.ShapeDtypeStruct((n, GLA_HEADS, GLA_DK, GLA_DV), F32),
        ],
        scratch_shapes=[pltpu.VMEM((GLA_DV, GLA_DK), F32)],
        compiler_params=_cparams(3),
        name="gla",
    )(z_p, z_p, z_p, z_p, z_p, wg, bg, gn, z_s, z_s, z_s, z_s, z_s, wg, bg, gn, state)
    return og_p, s_new_p, og_s, s_new_s


def _gla_sample_units(n, si, q_ref, k_ref, v_ref, r_ref, glr_ref, wg_ref, bg_ref, gn_ref, s_ref,
                      og_ref, s_out_ref):
    row = pl.ds(n, 1)

    def rows8(ref):
        return jnp.broadcast_to(ref[row, :], (SUBLANES, ref.shape[1]))

    zg = jnp.dot(rows8(glr_ref).astype(BF16), wg_ref[...], preferred_element_type=F32) + bg_ref[...]
    g = _log_sigmoid(zg) * (1.0 / GATE_TEMP)
    eg = jnp.exp(g)
    eng = jnp.exp(-g)
    eg1 = eg.astype(BF16)
    rem = eg - eg1.astype(F32)
    eg2 = rem.astype(BF16)
    eg3 = (rem - eg2.astype(F32)).astype(BF16)
    first_row = lax.broadcasted_iota(jnp.int32, (SUBLANES, LANE), 0) == 0
    ones_row = jnp.where(first_row, 1.0, 0.0).astype(BF16)
    row0 = lax.broadcasted_iota(jnp.int32, (SUBLANES, GLA_DV), 0) == 0
    q_all = rows8(q_ref) * (GLA_DK ** -0.5)
    k_all = rows8(k_ref)
    v_all = rows8(v_ref)
    r_all = r_ref[row, :]
    gn_all = gn_ref[...]

    def head_unit(h):
        ks = slice(h * GLA_DK, (h + 1) * GLA_DK)
        vs = slice(h * GLA_DV, (h + 1) * GLA_DV)
        q, k, v = q_all[:, ks], k_all[:, ks], v_all[:, vs]
        s_old = s_ref[si, h]
        decay = (lax.dot_general(eg1[:, ks], ones_row, _TN, preferred_element_type=F32)
                 + lax.dot_general(eg2[:, ks], ones_row, _TN, preferred_element_type=F32)
                 + lax.dot_general(eg3[:, ks], ones_row, _TN, preferred_element_type=F32))
        v0 = jnp.where(row0, v, 0.0)
        ds = lax.dot_general(k.astype(BF16), v0.astype(BF16), _TN, preferred_element_type=F32)
        s_out_ref[si, h] = jnp.tile(decay, (1, GLA_DV // LANE)) * s_old + ds
        qb = q * eg[:, ks]
        kb = k * eng[:, ks]
        att = jnp.sum(qb * kb, axis=-1, keepdims=True)
        o = jnp.dot(qb.astype(BF16), s_old.astype(BF16), preferred_element_type=F32) + att * v
        o = o[0:1, :]
        ms = jnp.mean(o * o, axis=-1, keepdims=True)
        on = o * lax.rsqrt(ms + EPS) * gn_all[:, vs]
        og_ref[row, vs] = on * _silu(r_all[:, vs])

    return [functools.partial(head_unit, h) for h in range(GLA_HEADS)]


def _pool_group_out(mix, g, wp_ref, ps_ref):
    cs = slice(g * POOL_GW, (g + 1) * POOL_GW)
    pm = jnp.dot(mix.astype(BF16), wp_ref[g].astype(BF16), preferred_element_type=F32)
    return pm * ps_ref[:, cs]


POOL_LEVELS = tuple(w.bit_length() - 1 for w in POOL_WINDOWS)
assert all(1 << lv == w for lv, w in zip(POOL_LEVELS, POOL_WINDOWS)), "windows must be powers of two"
POOL_HIST = SUBLANES * max(POOL_LEVELS)
assert POOL_HIST > POOL_BUF + 1


def _pool_prompt_kernel(u_ref, wp_ref, ps_ref, o_ref, ext_ref, lvl_ref):
    i = pl.program_id(1)
    tt = u_ref.shape[0]
    hist = POOL_HIST
    n = hist + tt

    @pl.when(i == 0)
    def _():
        ext_ref[0:hist, :] = jnp.zeros((hist, POOL_W), F32)

    @pl.when(i > 0)
    def _():
        ext_ref[0:hist, :] = ext_ref[tt:n, :]

    ext_ref[hist:n, :] = u_ref[...].astype(F32)
    pos = i * tt + lax.broadcasted_iota(jnp.int32, (tt, 1), 0)
    for g, (w, levels) in enumerate(zip(POOL_WINDOWS, POOL_LEVELS)):
        cs = slice(g * POOL_GW, (g + 1) * POOL_GW)
        cur = None
        for lev in range(levels):
            sh = 1 << lev
            start = SUBLANES * (lev + 1)
            if lev == 0:
                cur = ext_ref[start:n, cs] + ext_ref[start - sh:n - sh, cs]
            else:
                prev = lvl_ref.at[(lev - 1) % 2]
                cur = prev[start:n, :] + prev[start - sh:n - sh, :]
            if lev + 1 < levels:
                lvl_ref[lev % 2, start:n, :] = cur
        s = cur[hist - SUBLANES * levels:]
        u = ext_ref[hist:n, cs]
        cnt = jnp.minimum(pos + 1, w).astype(F32)
        mix = s / cnt - u
        o_ref[:, cs] = _pool_group_out(mix, g, wp_ref, ps_ref).astype(o_ref.dtype)


def _pool_prompt(z, w_pool, pool_scale, nb, t, tt):
    m = z.shape[0]
    nt = t // tt
    return pl.pallas_call(
        _pool_prompt_kernel,
        grid=(nb, nt),
        in_specs=[
            pl.BlockSpec((tt, POOL_W), lambda b, i: (b * nt + i, Z_U // POOL_W)),
            pl.BlockSpec((4, POOL_GW, POOL_GW), lambda b, i: (0, 0, 0)),
            pl.BlockSpec((1, POOL_W), lambda b, i: (0, 0)),
        ],
        out_specs=pl.BlockSpec((tt, POOL_W), lambda b, i: (b * nt + i, 0)),
        out_shape=jax.ShapeDtypeStruct((m, POOL_W), BF16),
        scratch_shapes=[pltpu.VMEM((tt + POOL_HIST, POOL_W), F32), pltpu.VMEM((2, tt + POOL_HIST, POOL_GW), F32)],
        compiler_params=_cparams(2),
        name="pool_prompt",
    )(z, w_pool, pool_scale)


def _pool_sample_kernel(u_ref, buf_ref, wp_ref, ps_ref, o_ref, buf_out_ref):
    buf_out_ref[0:POOL_BUF - 1] = buf_ref[1:POOL_BUF]
    buf_out_ref[POOL_BUF - 1] = u_ref[...].astype(F32)
    for g, w in enumerate(POOL_WINDOWS):
        cs = slice(g * POOL_GW, (g + 1) * POOL_GW)
        u = u_ref[:, cs].astype(F32)
        s = u
        for d in range(1, w):
            s = s + buf_ref[POOL_BUF - d, :, cs]
        mix = s / float(w) - u
        o_ref[:, cs] = _pool_group_out(mix, g, wp_ref, ps_ref).astype(o_ref.dtype)


def _pool_sample(z, buf_t, w_pool, pool_scale):
    n = z.shape[0]
    return pl.pallas_call(
        _pool_sample_kernel,
        grid=(1,),
        in_specs=[
            pl.BlockSpec((n, POOL_W), lambda i: (0, Z_U // POOL_W)),
            pl.BlockSpec((POOL_BUF, n, POOL_W), lambda i: (0, 0, 0)),
            pl.BlockSpec((4, POOL_GW, POOL_GW), lambda i: (0, 0, 0)),
            pl.BlockSpec((1, POOL_W), lambda i: (0, 0)),
        ],
        out_specs=[pl.BlockSpec((n, POOL_W), lambda i: (0, 0)),
                   pl.BlockSpec((POOL_BUF, n, POOL_W), lambda i: (0, 0, 0))],
        out_shape=[jax.ShapeDtypeStruct((n, POOL_W), BF16), jax.ShapeDtypeStruct((POOL_BUF, n, POOL_W), F32)],
        compiler_params=_cparams(1),
        name="pool_sample",
    )(z, buf_t, w_pool, pool_scale)


MERGE_SUB = 256


def _merge_kernel(ogp_ref, ogs_ref, pmp_ref, pms_ref, wa_ref, wb_ref, gap_ref, gas_ref, gbp_ref, gbs_ref,
                  op_ref, os_ref, wabf_ref, wbbf_ref):
    i = pl.program_id(1)

    @pl.when(i == 0)
    def _():
        wabf_ref[...] = wa_ref[...].astype(BF16)
        wbbf_ref[...] = wb_ref[...].astype(BF16)

    def emit(og_ref, pm_ref, ga_ref, gb_ref, o_ref):
        og = og_ref[...].astype(BF16)
        pm = pm_ref[...]
        for part in range(o_ref.shape[1] // MERGE_SUB):
            cols = slice(part * MERGE_SUB, (part + 1) * MERGE_SUB)
            ya = jnp.dot(og, wabf_ref[:, cols], preferred_element_type=F32)
            yb = jnp.dot(pm, wbbf_ref[:, cols], preferred_element_type=F32)
            ga = ga_ref[:, cols].astype(F32)
            gb = gb_ref[:, cols].astype(F32)
            o_ref[:, cols] = (_sigmoid(ga) * ya + _sigmoid(gb) * yb).astype(o_ref.dtype)

    @pl.when(i == 0)
    def _():
        emit(ogs_ref, pms_ref, gas_ref, gbs_ref, os_ref)

    @pl.when(i > 0)
    def _():
        emit(ogp_ref, pmp_ref, gap_ref, gbp_ref, op_ref)


def _merge(og_p, og_s, pm_p, pm_s, wa, wb, z_p, z_s, *, tm, tn):
    mp = og_p.shape[0]
    ms = og_s.shape[0]
    return pl.pallas_call(
        _merge_kernel,
        grid=(D_MODEL // tn, mp // tm + 1),
        in_specs=[
            *_row_tiled_specs(tm, ms, GLA_VW),
            *_row_tiled_specs(tm, ms, POOL_W),
            pl.BlockSpec((GLA_VW, tn), lambda j, i: (0, j), pipeline_mode=pl.Buffered(1)),
            pl.BlockSpec((POOL_W, tn), lambda j, i: (0, j), pipeline_mode=pl.Buffered(1)),
            *_col_tiled_specs(tm, ms, tn, Z_GA // tn),
            *_col_tiled_specs(tm, ms, tn, Z_GB // tn),
        ],
        out_specs=_col_tiled_specs(tm, ms, tn),
        out_shape=[jax.ShapeDtypeStruct((mp, D_MODEL), BF16), jax.ShapeDtypeStruct((ms, D_MODEL), BF16)],
        scratch_shapes=[pltpu.VMEM((GLA_VW, tn), BF16), pltpu.VMEM((POOL_W, tn), BF16)],
        compiler_params=_cparams(2),
        name="merge",
    )(og_p, og_s, pm_p, pm_s, wa, wb, z_p, z_s, z_p, z_s)


UP_TN = 512
UP_SUB = 256
UP_ROWS = 1024
UP_PARTS = UP_TN // LANE


def _up_kernel(*refs, tiles_per_seq):
    cp_ref, cs_ref, wg_ref = refs[:3]
    wu_refs = refs[3:3 + UP_PARTS]
    (cw_ref, cb_ref, h0_ref, h1_ref, actp_ref, acts_ref, tailp_ref, gates_ref,
     wgbf_ref, wubf_ref, carry_ref) = refs[3 + UP_PARTS:]
    i = pl.program_id(1)

    @pl.when(i == 0)
    def _():
        wgbf_ref[...] = wg_ref[...].astype(BF16)
        for part, wu_ref in enumerate(wu_refs):
            wubf_ref[:, part * LANE:(part + 1) * LANE] = wu_ref[...].astype(BF16)

    def conv_act(gate, up, g1, g2, cols):
        cw = cw_ref[:, cols]
        gconv = cb_ref[:, cols] + cw[0:1] * g2 + cw[1:2] * g1 + cw[2:3] * gate
        return _silu(gconv) * up

    @pl.when(i > 0)
    def _():
        @pl.when((i - 1) % tiles_per_seq == 0)
        def _():
            carry_ref[...] = jnp.zeros_like(carry_ref)

        tm = cp_ref.shape[0]
        rows = min(tm, UP_ROWS)
        sub = lax.broadcasted_iota(jnp.int32, (SUBLANES, UP_SUB), 0)
        n_cols = UP_TN // UP_SUB
        prev = [carry_ref[:, h * UP_SUB:(h + 1) * UP_SUB] for h in range(n_cols)]
        for rt in range(tm // rows):
            rsl = slice(rt * rows, (rt + 1) * rows)
            c = cp_ref[rsl, :]
            for h in range(n_cols):
                cols = slice(h * UP_SUB, (h + 1) * UP_SUB)
                gate = jnp.dot(c, wgbf_ref[:, cols], preferred_element_type=F32)
                up = jnp.dot(c, wubf_ref[:, cols], preferred_element_type=F32)
                r1 = pltpu.roll(gate, 1, 0)
                r2 = pltpu.roll(gate, 2, 0)
                head1 = jnp.where(sub < 1, pltpu.roll(prev[h], 1, 0), r1[0:SUBLANES])
                head2 = jnp.where(sub < 2, pltpu.roll(prev[h], 2, 0), r2[0:SUBLANES])
                g1 = jnp.concatenate([head1, r1[SUBLANES:]], axis=0)
                g2 = jnp.concatenate([head2, r2[SUBLANES:]], axis=0)
                prev[h] = gate[rows - SUBLANES:rows]
                actp_ref[rsl, cols] = conv_act(gate, up, g1, g2, cols).astype(actp_ref.dtype)
        for h in range(n_cols):
            cols = slice(h * UP_SUB, (h + 1) * UP_SUB)
            carry_ref[:, cols] = prev[h]
            tailp_ref[0, :, cols] = prev[h][SUBLANES - (CONV_W - 1):SUBLANES]

    @pl.when(i == 0)
    def _():
        c = cs_ref[...]
        gate = jnp.dot(c, wgbf_ref[...], preferred_element_type=F32)
        up = jnp.dot(c, wubf_ref[...], preferred_element_type=F32)
        gates_ref[...] = gate
        acts_ref[...] = conv_act(gate, up, h1_ref[...], h0_ref[...], slice(None)).astype(acts_ref.dtype)


def _up_proj(c_p, c_s, w_up, conv_w, conv_b, hist0, hist1, *, tm, seq_len):
    mp, k = c_p.shape
    ms = c_s.shape[0]
    npt = mp // tm
    tiles_per_seq = seq_len // tm
    nseq = mp // seq_len
    tn = UP_TN
    nj = pl.cdiv(D_FF, tn)
    up_block0 = D_FF // LANE
    last_block = w_up.shape[1] // LANE - 1

    def row_p(j, i):
        return jnp.maximum(i - 1, 0)

    def wu_spec(part):
        return pl.BlockSpec(
            (k, LANE), lambda j, i: (0, jnp.minimum(up_block0 + j * UP_PARTS + part, last_block)))

    in_specs = [
        pl.BlockSpec((tm, k), lambda j, i: (row_p(j, i), 0)),
        pl.BlockSpec((ms, k), lambda j, i: (0, 0)),
        pl.BlockSpec((k, tn), lambda j, i: (0, j)),
        *[wu_spec(part) for part in range(UP_PARTS)],
        pl.BlockSpec((CONV_W, tn), lambda j, i: (0, j)),
        pl.BlockSpec((1, tn), lambda j, i: (0, j)),
        pl.BlockSpec((ms, tn), lambda j, i: (0, j)),
        pl.BlockSpec((ms, tn), lambda j, i: (0, j)),
    ]
    out_specs = [
        pl.BlockSpec((tm, tn), lambda j, i: (row_p(j, i), j)),
        pl.BlockSpec((ms, tn), lambda j, i: (0, j)),
        pl.BlockSpec((1, CONV_W - 1, tn), lambda j, i: (row_p(j, i) // tiles_per_seq, 0, j)),
        pl.BlockSpec((ms, tn), lambda j, i: (0, j)),
    ]
    out_shape = [
        jax.ShapeDtypeStruct((mp, D_FF), BF16),
        jax.ShapeDtypeStruct((ms, D_FF), BF16),
        jax.ShapeDtypeStruct((nseq, CONV_W - 1, D_FF), F32),
        jax.ShapeDtypeStruct((ms, D_FF), F32),
    ]
    return pl.pallas_call(
        functools.partial(_up_kernel, tiles_per_seq=tiles_per_seq),
        grid=(nj, npt + 1),
        in_specs=in_specs,
        out_specs=out_specs,
        out_shape=out_shape,
        scratch_shapes=[pltpu.VMEM((k, tn), BF16), pltpu.VMEM((k, tn), BF16), pltpu.VMEM((SUBLANES, tn), F32)],
        compiler_params=_cparams(2),
        name="up_proj",
    )(c_p, c_s, w_up, *([w_up] * UP_PARTS), conv_w, conv_b, hist0, hist1)


def _two_group_specs(tm, ms, widths):
    specs = []
    for w in widths:
        specs.append(pl.BlockSpec((tm, w), lambda i: (jnp.maximum(i - 1, 0), 0)))
        specs.append(pl.BlockSpec((ms, w), lambda i: (0, 0)))
    return specs


def _resident(shape):
    return pl.BlockSpec(shape, lambda i: (0,) * len(shape), pipeline_mode=pl.Buffered(1))


def _out_norm_kernel(mp_ref, ms_ref, xp_ref, xs_ref, w_ref, g_ref, hp_ref, hs_ref, cp_ref, cs_ref, wbf_ref):
    i = pl.program_id(0)

    @pl.when(i == 0)
    def _():
        wbf_ref[...] = w_ref[...].astype(BF16)

    def emit(m_ref, x_ref, h_ref, c_ref):
        h = x_ref[...] + jnp.dot(m_ref[...], wbf_ref[...], preferred_element_type=F32)
        h_ref[...] = h
        c_ref[...] = _rms(h, g_ref[...]).astype(c_ref.dtype)

    @pl.when(i == 0)
    def _():
        emit(ms_ref, xs_ref, hs_ref, cs_ref)

    @pl.when(i > 0)
    def _():
        emit(mp_ref, xp_ref, hp_ref, cp_ref)


def _out_norm(m_p, m_s, x_p, x_s, w_out, g, *, tm):
    mp, d = x_p.shape
    ms = x_s.shape[0]
    return pl.pallas_call(
        _out_norm_kernel,
        grid=(mp // tm + 1,),
        in_specs=[*_two_group_specs(tm, ms, (d, d)), _resident((d, d)), _resident((1, d))],
        out_specs=_two_group_specs(tm, ms, (d, d)),
        out_shape=[
            jax.ShapeDtypeStruct((mp, d), F32), jax.ShapeDtypeStruct((ms, d), F32),
            jax.ShapeDtypeStruct((mp, d), BF16), jax.ShapeDtypeStruct((ms, d), BF16),
        ],
        scratch_shapes=[pltpu.VMEM((d, d), BF16)],
        compiler_params=_cparams(1),
        name="out_norm",
    )(m_p, m_s, x_p, x_s, w_out, g.reshape(1, d))


def _ple_kernel(hp_ref, hs_ref, pp_ref, ps_ref, wpg_ref, wple_ref, gple_ref, gfin_ref, yp_ref, ys_ref,
                wpgbf_ref, wplebf_ref, *, final_norm):
    i = pl.program_id(0)

    @pl.when(i == 0)
    def _():
        wpgbf_ref[...] = wpg_ref[...].astype(BF16)
        wplebf_ref[...] = wple_ref[...].astype(BF16)

    def emit(h_ref, p_ref, y_ref):
        h2 = h_ref[...]
        cn = _rms(h2, gple_ref[...]).astype(BF16)
        pg = _sigmoid(jnp.dot(cn, wpgbf_ref[...], preferred_element_type=F32))
        e = jnp.dot(p_ref[...].astype(BF16), wplebf_ref[...], preferred_element_type=F32)
        h3 = h2 + pg * e
        y_ref[...] = _rms(h3, gfin_ref[...]) if final_norm else h3

    @pl.when(i == 0)
    def _():
        emit(hs_ref, ps_ref, ys_ref)

    @pl.when(i > 0)
    def _():
        emit(hp_ref, pp_ref, yp_ref)


def _ple(h_p, h_s, p_p, p_s, wpg, wple, g_ple, g_final, *, tm, final_norm):
    mp, d = h_p.shape
    ms = h_s.shape[0]
    pd = p_p.shape[1]
    return pl.pallas_call(
        functools.partial(_ple_kernel, final_norm=final_norm),
        grid=(mp // tm + 1,),
        in_specs=[*_two_group_specs(tm, ms, (d, pd)), _resident((d, d)), _resident((pd, d)),
                  _resident((1, d)), _resident((1, d))],
        out_specs=_two_group_specs(tm, ms, (d,)),
        out_shape=[jax.ShapeDtypeStruct((mp, d), F32), jax.ShapeDtypeStruct((ms, d), F32)],
        scratch_shapes=[pltpu.VMEM((d, d), BF16), pltpu.VMEM((pd, d), BF16)],
        compiler_params=_cparams(1),
        name="ple",
    )(h_p, h_s, p_p, p_s, wpg, wple, g_ple.reshape(1, d), g_final.reshape(1, d))


def _pool_mixers(z_p, z_s, lw, s_pool, nb, t):
    pm_s, pool_new_t = _pool_sample(z_s, jnp.transpose(s_pool, (1, 0, 2)), lw["w_pool"], lw["pool_scale"])
    pool_new_s = jnp.transpose(pool_new_t, (1, 0, 2))
    pm_p = _pool_prompt(z_p, lw["w_pool"], lw["pool_scale"], nb, t, TT_POOL)
    pool_new_p = z_p.reshape(nb, t, Z_WIDTH)[:, t - POOL_BUF:, Z_U:Z_U + POOL_W].astype(F32)
    return pm_p, pm_s, pool_new_p, pool_new_s


def kernel(x_prompt, x_sample, state_gla, state_pool, state_conv, p_prompt, p_sample, norm_mix, w_in, w_gate_up, b_gate, gla_norm, w_branch_a, w_pool, pool_scale, w_branch_b, w_out, norm_ffn, w_up, conv_w, conv_b, w_down, norm_ple, w_ple_gate, w_ple, norm_final):
    depth = w_in.shape[0]
    nb, t, d = x_prompt.shape
    ns = x_sample.shape[0]
    hp = x_prompt.reshape(nb * t, d)
    hs = x_sample.reshape(ns, d)
    outs = {k: [] for k in ("gp", "pp", "cp", "gs", "ps", "cs")}
    for li in range(depth):
        lw = {
            "wg": jnp.pad(w_gate_up[li], ((0, LANE - GATE_RANK), (0, 0))).astype(BF16),
            "bg": b_gate[li].reshape(1, GLA_KW),
            "gn": gla_norm[li].reshape(1, GLA_VW),
            "w_pool": w_pool[li],
            "pool_scale": pool_scale[li].reshape(1, POOL_W),
        }
        a_p = _rmsnorm(hp, norm_mix[li], BF16, TM_NORM)
        z_p, z_s = _in_proj(a_p, hs, norm_mix[li], w_in[li].T, tm=TM_IN)
        og_p, g1, og_s, g2 = _gla(z_p, z_s, lw["wg"], lw["bg"], lw["gn"], state_gla[li], nb, t)
        pm_p, pm_s, p1, p2 = _pool_mixers(z_p, z_s, lw, state_pool[li], nb, t)
        mg_p, mg_s = _merge(og_p, og_s, pm_p, pm_s, w_branch_a[li], w_branch_b[li], z_p, z_s,
                            tm=TM_MERGE, tn=TN_MERGE)
        h1_p, h1_s, c_p, c_s = _out_norm(mg_p, mg_s, hp, hs, w_out[li], norm_ffn[li], tm=TM_ROW)
        s_conv = state_conv[li]
        act_p, act_s, c1, gate_s = _up_proj(c_p, c_s, w_up[li], conv_w[li], conv_b[li].reshape(1, D_FF),
                                            s_conv[:, 0], s_conv[:, 1], tm=TM_UP, seq_len=t)
        c2 = jnp.stack([s_conv[:, 1], gate_s], axis=1)
        h2_p, h2_s = _down_proj(act_p, act_s, w_down[li], h1_p, h1_s, tm=TM_DOWN, tn=TN_DOWN)
        hp, hs = _ple(h2_p, h2_s, p_prompt[li].reshape(nb * t, PLE_DIM), p_sample[li].reshape(ns, PLE_DIM),
                      w_ple_gate[li], w_ple[li], norm_ple[li], norm_final, tm=TM_ROW,
                      final_norm=li == depth - 1)
        for key, val in zip(("gp", "pp", "cp", "gs", "ps", "cs"), (g1, p1, c1, g2, p2, c2)):
            outs[key].append(val)
    st = {k: jnp.stack(v, axis=0) for k, v in outs.items()}
    return (hp.reshape(nb, t, d), hs.reshape(ns, 1, d),
            st["gp"], st["pp"], st["cp"], st["gs"], st["ps"], st["cs"])
```
